```python
import jax, jax.numpy as jnp
from jax import lax
import numpy as np

D_MODEL = 1024
BATCH = 4
SEQ = 4096
DEPTH = 1
DEC_BATCH = 32
DEC_SEQ = 1
PAST_LEN = 8192
PAGE_SIZE = 128

HEAD_DIM = 64
HEADS_PER_GROUP = 4
ATTN_GROUPS = ((128, 1), (512, 4), (2048, 16))
N_GROUPS = 3
N_ATTN_HEADS = HEADS_PER_GROUP * N_GROUPS
ATTN_WIDTH = N_ATTN_HEADS * HEAD_DIM
ATTN_OUT_WIDTH = HEADS_PER_GROUP * HEAD_DIM
BAND_BLOCK = 128
CONV_WIDTH = D_MODEL // 2
CONV_K = 3
N_EXPERT_GROUPS = 4
EXPERTS_PER_GROUP = 8
N_EXPERTS = N_EXPERT_GROUPS * EXPERTS_PER_GROUP
TOP_K_IN_GROUP = 2
EXPERT_FF = D_MODEL // 4
IN_COLS = 3 * ATTN_WIDTH + 3 * CONV_WIDTH + 2 * D_MODEL
RMS_EPS = 1e-6

kernel_name = 'hybrid_dilated_shortconv_hmoe_step'


def _rmsnorm(x, g):
    x32 = x.astype(jnp.float32)
    y = x32 * lax.rsqrt(jnp.mean(x32 * x32, axis=-1, keepdims=True) + RMS_EPS)
    return (y * g.astype(jnp.float32)).astype(x.dtype)


def _alibi_slopes():
    i = jnp.arange(1, N_ATTN_HEADS + 1, dtype=jnp.float32)
    return (2.0 ** (-8.0 * i / N_ATTN_HEADS)).reshape(N_GROUPS, HEADS_PER_GROUP)


def _dilated_band_attention(q, k, v, slopes, window, dilation):
    b, s, h, dh = q.shape
    n_back = window // dilation
    sub_len = s // dilation
    nblk = -(-sub_len // BAND_BLOCK)
    pad_len = nblk * BAND_BLOCK - sub_len

    def blocks(a):
        a = a.astype(jnp.float32).reshape(b, sub_len, dilation, h, dh)
        a = jnp.pad(a, ((0, 0), (0, pad_len), (0, 0), (0, 0), (0, 0)))
        return a.reshape(b, nblk, BAND_BLOCK, dilation, h, dh)

    def band(a):
        a = jnp.pad(blocks(a), ((0, 0), (1, 0), (0, 0), (0, 0), (0, 0), (0, 0)))
        return jnp.concatenate([a[:, :-1], a[:, 1:]], axis=2)

    qb, kb, vb = blocks(q), band(k), band(v)
    qi = jnp.arange(BAND_BLOCK)
    ki = jnp.arange(2 * BAND_BLOCK) - BAND_BLOCK
    delta = qi[:, None] - ki[None, :]
    key_idx = jnp.arange(nblk)[:, None] * BAND_BLOCK + ki[None, :]
    mask = ((delta >= 0) & (delta <= n_back))[None] & (key_idx >= 0)[:, None, :]
    bias = -slopes[:, None, None] * (delta * dilation).astype(jnp.float32)[None]
    sc = jnp.einsum('bnqrhd,bnkrhd->bnrhqk', qb, kb) * (dh ** -0.5) + bias[None, None, None]
    sc = jnp.where(mask[None, :, None, None], sc, -jnp.inf)
    m = jnp.max(sc, axis=-1, keepdims=True)
    e = jnp.exp(sc - m)
    den = jnp.sum(e, axis=-1, keepdims=True)
    o = jnp.einsum('bnrhqk,bnkrhd->bnqrhd', e / den, vb)
    o = o.reshape(b, nblk * BAND_BLOCK, dilation, h, dh)[:, :sub_len].reshape(b, s, h, dh)
    lse = (m + jnp.log(den))[..., 0].transpose(0, 1, 4, 2, 3)
    lse = lse.reshape(b, nblk * BAND_BLOCK, dilation, h)[:, :sub_len].reshape(b, s, h)
    return o, lse


def _dilated_cached_attention(q, k_new, v_new, kv_buf, slopes, window, dilation):
    buf_len = kv_buf.shape[1]
    t = q.shape[1]
    dh = q.shape[-1]
    offs = dilation * jnp.arange(window // dilation + 1)
    rows = buf_len + jnp.arange(t)[:, None] - offs[None, :]
    valid = rows >= 0
    rows = jnp.maximum(rows, 0)
    k_ext = jnp.concatenate([kv_buf[:, :, 0], k_new], axis=1).astype(jnp.float32)
    v_ext = jnp.concatenate([kv_buf[:, :, 1], v_new], axis=1).astype(jnp.float32)
    kg = k_ext[:, rows]
    vg = v_ext[:, rows]
    sc = (jnp.einsum('bthd,btjhd->bhtj', q.astype(jnp.float32), kg) * (dh ** -0.5)
          - slopes[:, None, None] * offs.astype(jnp.float32)[None, None, :])
    sc = jnp.where(valid[None, None], sc, -jnp.inf)
    m = jnp.max(sc, axis=-1, keepdims=True)
    e = jnp.exp(sc - m)
    den = jnp.sum(e, axis=-1, keepdims=True)
    o = jnp.einsum('bhtj,btjhd->bthd', e / den, vg)
    lse = (m + jnp.log(den))[..., 0].transpose(0, 2, 1)
    return o, lse


def _mix_groups(outs, lses, dtype):
    alpha = jax.nn.softmax(jnp.stack(lses, axis=-1), axis=-1)
    o = jnp.einsum('bthg,bthgd->bthd', alpha, jnp.stack(outs, axis=3))
    return o.reshape(o.shape[0], o.shape[1], ATTN_OUT_WIDTH).astype(dtype)


def _causal_conv(u, prefix, conv_w):
    t = u.shape[1]
    ext = jnp.concatenate([prefix, u], axis=1)
    y = sum(conv_w[j] * ext[:, j:j + t] for j in range(CONV_K))
    return y, ext[:, t:]


def _hier_moe(h, w_rg, b_rg, w_re, b_re, w_gate_e, w_up_e, w_down_e):
    bsz, t, _ = h.shape
    lg = (h @ w_rg).astype(jnp.float32) + b_rg.astype(jnp.float32)
    pg = jax.nn.softmax(lg, axis=-1)
    gidx = jnp.argmax(lg, axis=-1)
    p_top = jnp.take_along_axis(pg, gidx[..., None], axis=-1)
    le = ((h @ w_re).astype(jnp.float32) + b_re.astype(jnp.float32)).reshape(bsz, t, N_EXPERT_GROUPS, EXPERTS_PER_GROUP)
    le_sel = jnp.take_along_axis(le, gidx[..., None, None], axis=2)[..., 0, :]
    top_v, top_i = lax.top_k(le_sel, TOP_K_IN_GROUP)
    w_top = jax.nn.softmax(top_v, axis=-1) * p_top
    inner = jnp.einsum('btk,btke->bte', w_top, jax.nn.one_hot(top_i, EXPERTS_PER_GROUP, dtype=jnp.float32))
    combine = jax.nn.one_hot(gidx, N_EXPERT_GROUPS, dtype=jnp.float32)[..., :, None] * inner[..., None, :]
    combine = combine.reshape(bsz, t, N_EXPERTS).astype(h.dtype)
    a = jnp.einsum('btd,edf->btef', h, w_gate_e)
    b = jnp.einsum('btd,edf->btef', h, w_up_e)
    return jnp.einsum('btef,efd->btd', jax.nn.silu(a) * b * combine[..., None], w_down_e)


def _front(x, c, g_mix, w_ada, b_ada, w_in):
    mod = (jax.nn.silu(c) @ w_ada + b_ada)[:, None, :]
    shift1, scale1, gate1, shift2, scale2, gate2 = jnp.split(mod, 6, axis=-1)
    h = _rmsnorm(x, g_mix) * (1 + scale1) + shift1
    aw, cw = ATTN_WIDTH, CONV_WIDTH
    pts = [aw, 2 * aw, 3 * aw, 3 * aw + cw, 3 * aw + 2 * cw, 3 * aw + 3 * cw, 3 * aw + 3 * cw + D_MODEL]
    q, k, v, b_gate, c_gate, u_in, gate_a, gate_b = jnp.split(h @ w_in, pts, axis=-1)
    bsz, t = x.shape[:2]
    q, k, v = (z.reshape(bsz, t, N_GROUPS, HEADS_PER_GROUP, HEAD_DIM) for z in (q, k, v))
    return (q, k, v), c_gate * u_in, b_gate, gate_a, gate_b, (gate1, shift2, scale2, gate2)


def _back(x, mods, o_attn, o_conv, gate_a, gate_b, w_pa, w_pb, w_o, g_ffn,
          w_rg, b_rg, w_re, b_re, w_gate_e, w_up_e, w_down_e):
    gate1, shift2, scale2, gate2 = mods
    merged = jax.nn.sigmoid(gate_a) * (o_attn @ w_pa) + jax.nn.sigmoid(gate_b) * (o_conv @ w_pb)
    x = x + gate1 * (merged @ w_o)
    h = _rmsnorm(x, g_ffn) * (1 + scale2) + shift2
    return x + gate2 * _hier_moe(h, w_rg, b_rg, w_re, b_re, w_gate_e, w_up_e, w_down_e)


def _prompt_layer(x, c, lw):
    g_mix, w_ada, b_ada, w_in, conv_w, *back = lw
    (q, k, v), u, b_gate, gate_a, gate_b, mods = _front(x, c, g_mix, w_ada, b_ada, w_in)
    slopes = _alibi_slopes()
    s = x.shape[1]
    outs, lses, kv_rows = [], [], []
    for g, (window, dilation) in enumerate(ATTN_GROUPS):
        o, lse = _dilated_band_attention(q[:, :, g], k[:, :, g], v[:, :, g], slopes[g], window, dilation)
        outs.append(o)
        lses.append(lse)
        kv_rows.append(jnp.stack([k[:, :, g], v[:, :, g]], axis=2)[:, s - min(window, s):])
    o_attn = _mix_groups(outs, lses, x.dtype)
    y_conv, conv_state = _causal_conv(u, jnp.zeros_like(u[:, :CONV_K - 1]), conv_w)
    y = _back(x, mods, o_attn, b_gate * y_conv, gate_a, gate_b, *back)
    return y, (*kv_rows, conv_state)


def _sample_layer(x, c, kv_bufs, conv_buf, lw):
    g_mix, w_ada, b_ada, w_in, conv_w, *back = lw
    (q, k, v), u, b_gate, gate_a, gate_b, mods = _front(x, c, g_mix, w_ada, b_ada, w_in)
    slopes = _alibi_slopes()
    outs, lses, kv_rows = [], [], []
    for g, ((window, dilation), kv_buf) in enumerate(zip(ATTN_GROUPS, kv_bufs)):
        o, lse = _dilated_cached_attention(q[:, :, g], k[:, :, g], v[:, :, g], kv_buf, slopes[g], window, dilation)
        outs.append(o)
        lses.append(lse)
        kv_rows.append(jnp.stack([k[:, :, g], v[:, :, g]], axis=2))
    o_attn = _mix_groups(outs, lses, x.dtype)
    y_conv, conv_state = _causal_conv(u, conv_buf, conv_w)
    y = _back(x, mods, o_attn, b_gate * y_conv, gate_a, gate_b, *back)
    return y, (*kv_rows, conv_state)


def setup_inputs(seed: int = 0) -> dict:
    key = jax.random.key(seed)
    ks = jax.random.split(key, 28)

    def nrm(k, shape, scale):
        return scale * jax.random.normal(k, shape, jnp.float32)

    def kv_shape(window):
        return (DEPTH, DEC_BATCH, min(window, PAST_LEN), 2, HEADS_PER_GROUP, HEAD_DIM)

    return {
        'x_prompt': nrm(ks[0], (BATCH, SEQ, D_MODEL), 1.0),
        'x_sample': nrm(ks[1], (DEC_BATCH, DEC_SEQ, D_MODEL), 1.0),
        'cache_kv_w128': nrm(ks[2], kv_shape(ATTN_GROUPS[0][0]), 1.0),
        'cache_kv_w512': nrm(ks[3], kv_shape(ATTN_GROUPS[1][0]), 1.0),
        'cache_kv_w2048': nrm(ks[4], kv_shape(ATTN_GROUPS[2][0]), 1.0),
        'cache_conv': nrm(ks[5], (DEPTH, DEC_BATCH, CONV_K - 1, CONV_WIDTH), 1.0),
        'c_prompt': nrm(ks[6], (BATCH, D_MODEL), 1.0),
        'c_sample': nrm(ks[7], (DEC_BATCH, D_MODEL), 1.0),
        'g_mix': 1.0 + nrm(ks[8], (DEPTH, D_MODEL), 0.02),
        'w_ada': nrm(ks[9], (DEPTH, D_MODEL, 6 * D_MODEL), 0.5 * D_MODEL ** -0.5),
        'b_ada': nrm(ks[10], (DEPTH, 6 * D_MODEL), 0.02),
        'w_in': nrm(ks[11], (DEPTH, D_MODEL, IN_COLS), D_MODEL ** -0.5),
        'conv_w': nrm(ks[12], (DEPTH, CONV_K, CONV_WIDTH), CONV_K ** -0.5),
        'w_pa': nrm(ks[13], (DEPTH, ATTN_OUT_WIDTH, D_MODEL), ATTN_OUT_WIDTH ** -0.5),
        'w_pb': nrm(ks[14], (DEPTH, CONV_WIDTH, D_MODEL), CONV_WIDTH ** -0.5),
        'w_o': nrm(ks[15], (DEPTH, D_MODEL, D_MODEL), D_MODEL ** -0.5),
        'g_ffn': 1.0 + nrm(ks[16], (DEPTH, D_MODEL), 0.02),
        'w_router_group': nrm(ks[17], (DEPTH, D_MODEL, N_EXPERT_GROUPS), D_MODEL ** -0.5),
        'b_router_group': nrm(ks[18], (DEPTH, N_EXPERT_GROUPS), 0.01),
        'w_router_expert': nrm(ks[19], (DEPTH, D_MODEL, N_EXPERTS), D_MODEL ** -0.5),
        'b_router_expert': nrm(ks[20], (DEPTH, N_EXPERTS), 0.01),
        'w_gate_e': nrm(ks[21], (DEPTH, N_EXPERTS, D_MODEL, EXPERT_FF), D_MODEL ** -0.5),
        'w_up_e': nrm(ks[22], (DEPTH, N_EXPERTS, D_MODEL, EXPERT_FF), D_MODEL ** -0.5),
        'w_down_e': nrm(ks[23], (DEPTH, N_EXPERTS, EXPERT_FF, D_MODEL), EXPERT_FF ** -0.5),
        'g_final': 1.0 + nrm(ks[24], (D_MODEL,), 0.02),
    }


def reference(x_prompt, x_sample, cache_kv_w128, cache_kv_w512, cache_kv_w2048, cache_conv,
              c_prompt, c_sample, g_mix, w_ada, b_ada, w_in, conv_w, w_pa, w_pb, w_o, g_ffn,
              w_router_group, b_router_group, w_router_expert, b_router_expert,
              w_gate_e, w_up_e, w_down_e, g_final):
    xp, xs = x_prompt, x_sample
    st_p, st_s = [], []
    for l in range(DEPTH):
        lw = (g_mix[l], w_ada[l], b_ada[l], w_in[l], conv_w[l], w_pa[l], w_pb[l], w_o[l], g_ffn[l],
              w_router_group[l], b_router_group[l], w_router_expert[l], b_router_expert[l],
              w_gate_e[l], w_up_e[l], w_down_e[l])
        xp, sp = _prompt_layer(xp, c_prompt, lw)
        xs, ss = _sample_layer(xs, c_sample, (cache_kv_w128[l], cache_kv_w512[l], cache_kv_w2048[l]),
                               cache_conv[l], lw)
        st_p.append(sp)
        st_s.append(ss)
    y_prompt = _rmsnorm(xp, g_final)
    y_sample = _rmsnorm(xs, g_final)
    p_kv128, p_kv512, p_kv2048, p_conv = (jnp.stack(z) for z in zip(*st_p))
    s_kv128, s_kv512, s_kv2048, s_conv = (jnp.stack(z) for z in zip(*st_s))
    return (y_prompt, y_sample, p_kv128, p_kv512, p_kv2048, p_conv, s_kv128, s_kv512, s_kv2048, s_conv)
```

```python
import functools
import math

import jax
import jax.numpy as jnp
from jax import lax
from jax.experimental import pallas as pl
from jax.experimental.pallas import tpu as pltpu

F32 = jnp.float32
BF16 = jnp.bfloat16
HIGHEST = lax.Precision.HIGHEST

D_MODEL = 1024
HEAD_DIM = 64
HEADS = 4
N_GROUPS = 3
GROUP_W = HEADS * HEAD_DIM
ATTN_W = N_GROUPS * GROUP_W
CONV_W = 512
DILATIONS = (1, 4, 16)
BAND = 128
N_EXPERTS = 32
EXPERTS_PER_GROUP = 8
N_EXPERT_GROUPS = 4
EXPERT_FF = 256
RMS_EPS = 1e-6
MASKED = -1e30

Q_OFF, K_OFF, V_OFF = 0, ATTN_W, 2 * ATTN_W
BG_OFF = 3 * ATTN_W
CG_OFF = BG_OFF + CONV_W
UI_OFF = CG_OFF + CONV_W
GA_OFF = UI_OFF + CONV_W
GB_OFF = GA_OFF + D_MODEL
IN_COLS = GB_OFF + D_MODEL

LANES = 128
ROUTER_LANES = 128
EXPERT_LANE0 = N_EXPERT_GROUPS

TM = 512
ATT_TILE = 2048
MOE_TM = 1024
VMEM_LIMIT = 56 * 1024 * 1024


def _sigmoid(x):
    return 1.0 / (1.0 + jnp.exp(-x))


def _rmsnorm(x, g):
    return x * lax.rsqrt(jnp.mean(x * x, axis=-1, keepdims=True) + RMS_EPS) * g


def _alibi_slope(g, h):
    return 2.0 ** (-8.0 * (g * HEADS + h + 1) / (N_GROUPS * HEADS))


def _resident(shape):
    nd = len(shape)
    return pl.BlockSpec(shape, lambda *_: (0,) * nd, pipeline_mode=pl.Buffered(1))


def _mod_body(c_ref, w_ref, b_ref, o_ref):
    c = c_ref[...]
    s = c * _sigmoid(c)
    o_ref[...] = jnp.dot(s, w_ref[...], precision=HIGHEST, preferred_element_type=F32) + b_ref[...]


def _modulation(c_all, w_ada, b_ada):
    rows = c_all.shape[0]
    tn = 1024
    return pl.pallas_call(
        _mod_body,
        grid=(6 * D_MODEL // tn,),
        in_specs=[pl.BlockSpec((rows, D_MODEL), lambda j: (0, 0)),
                  pl.BlockSpec((D_MODEL, tn), lambda j: (0, j)),
                  pl.BlockSpec((1, tn), lambda j: (0, j))],
        out_specs=pl.BlockSpec((rows, tn), lambda j: (0, j)),
        out_shape=jax.ShapeDtypeStruct((rows, 6 * D_MODEL), F32),
        compiler_params=pltpu.CompilerParams(dimension_semantics=("arbitrary",), vmem_limit_bytes=VMEM_LIMIT),
        name="modulation",
    )(c_all, w_ada, b_ada.reshape(1, -1))


def _front_body(tiles_per_seq, x_ref, mod_ref, g_ref, w_ref, cw_ref,
                a0_ref, a1_ref, a2_ref, oconv_ref, sga_ref, sgb_ref,
                kv0_ref, kv1_ref, kv2_ref, pconv_ref, res_ref, uprev_ref):
    t_in_seq = pl.program_id(0) % tiles_per_seq
    x = x_ref[...]
    shift1 = mod_ref[:, 0:D_MODEL]
    scale1 = mod_ref[:, D_MODEL:2 * D_MODEL]
    h = (_rmsnorm(x, g_ref[...]) * (1.0 + scale1) + shift1).astype(BF16)

    def proj(c0, n):
        return jnp.dot(h, w_ref[:, c0:c0 + n], preferred_element_type=F32)

    a_refs = (a0_ref, a1_ref, a2_ref)
    kv_refs = (kv0_ref, kv1_ref, kv2_ref)
    for g in range(N_GROUPS):
        d = DILATIONS[g]
        n = TM // d
        for part, base in enumerate((Q_OFF, K_OFF, V_OFF)):
            r = proj(base + g * GROUP_W, GROUP_W)
            cols = slice(part * GROUP_W, (part + 1) * GROUP_W)
            if part == 0:
                r = r * (HEAD_DIM ** -0.5)
            else:
                kvc = slice((part - 1) * GROUP_W, part * GROUP_W)
                if g == 0:
                    kv_refs[g][:, kvc] = r[TM - BAND:, :]
                else:
                    kv_refs[g][:, kvc] = r
            if g == 0:
                a_refs[g][:, cols] = r.astype(BF16)
            else:
                for c in range(GROUP_W // LANES):
                    res_ref[c] = r[:, c * LANES:(c + 1) * LANES]
                for rr in range(d):
                    for c in range(GROUP_W // LANES):
                        c0 = part * GROUP_W + c * LANES
                        a_refs[g][0, rr, :, c0:c0 + LANES] = res_ref[c, pl.ds(rr, n, stride=d), :].astype(BF16)

    bg = proj(BG_OFF, CONV_W)
    u = proj(CG_OFF, CONV_W) * proj(UI_OFF, CONV_W)
    tail = u[TM - 8:, :]
    pconv_ref[...] = tail
    prev = jnp.where(t_in_seq == 0, 0.0, uprev_ref[...])
    row = lax.broadcasted_iota(jnp.int32, (TM, 1), 0)
    u1 = jnp.where(row == 0, prev[7:8, :], pltpu.roll(u, 1, axis=0))
    u2 = jnp.where(row == 0, prev[6:7, :], jnp.where(row == 1, prev[7:8, :], pltpu.roll(u, 2, axis=0)))
    yconv = cw_ref[0:1, :] * u2 + cw_ref[1:2, :] * u1 + cw_ref[2:3, :] * u
    oconv_ref[...] = (bg * yconv).astype(BF16)
    uprev_ref[...] = tail

    sga_ref[...] = _sigmoid(proj(GA_OFF, D_MODEL)).astype(BF16)
    sgb_ref[...] = _sigmoid(proj(GB_OFF, D_MODEL)).astype(BF16)


def _front(x2d, mod_p, g_mix, w_in_bf16, conv_w, batch, seq):
    n_tok = x2d.shape[0]
    n_tiles = n_tok // TM
    tps = seq // TM
    kv2_blocks = ATT_TILE // TM
    out_shape = (
        jax.ShapeDtypeStruct((n_tok, ATTN_W), BF16),
        jax.ShapeDtypeStruct((n_tiles, 4, TM // 4, ATTN_W), BF16),
        jax.ShapeDtypeStruct((n_tiles, 16, TM // 16, ATTN_W), BF16),
        jax.ShapeDtypeStruct((n_tok, CONV_W), BF16),
        jax.ShapeDtypeStruct((n_tok, D_MODEL), BF16),
        jax.ShapeDtypeStruct((n_tok, D_MODEL), BF16),
        jax.ShapeDtypeStruct((batch, 128, 2 * GROUP_W), F32),
        jax.ShapeDtypeStruct((batch, 512, 2 * GROUP_W), F32),
        jax.ShapeDtypeStruct((batch, 2048, 2 * GROUP_W), F32),
        jax.ShapeDtypeStruct((batch, 8, CONV_W), F32),
    )
    out_specs = (
        pl.BlockSpec((TM, ATTN_W), lambda i: (i, 0)),
        pl.BlockSpec((1, 4, TM // 4, ATTN_W), lambda i: (i, 0, 0, 0)),
        pl.BlockSpec((1, 16, TM // 16, ATTN_W), lambda i: (i, 0, 0, 0)),
        pl.BlockSpec((TM, CONV_W), lambda i: (i, 0)),
        pl.BlockSpec((TM, D_MODEL), lambda i: (i, 0)),
        pl.BlockSpec((TM, D_MODEL), lambda i: (i, 0)),
        pl.BlockSpec((None, 128, 2 * GROUP_W), lambda i: (i // tps, 0, 0)),
        pl.BlockSpec((None, TM, 2 * GROUP_W), lambda i: (i // tps, 0, 0)),
        pl.BlockSpec((None, TM, 2 * GROUP_W),
                     lambda i: (i // tps, jnp.maximum(i % tps - (tps - kv2_blocks), 0), 0)),
        pl.BlockSpec((None, 8, CONV_W), lambda i: (i // tps, 0, 0)),
    )
    in_specs = [
        pl.BlockSpec((TM, D_MODEL), lambda i: (i, 0)),
        pl.BlockSpec((None, 1, 6 * D_MODEL), lambda i: (i // tps, 0, 0)),
        _resident((1, D_MODEL)),
        _resident((D_MODEL, IN_COLS)),
        _resident((3, CONV_W)),
    ]
    return pl.pallas_call(
        functools.partial(_front_body, tps),
        grid=(n_tiles,),
        in_specs=in_specs,
        out_specs=out_specs,
        out_shape=out_shape,
        scratch_shapes=[pltpu.VMEM((GROUP_W // LANES, TM, LANES), F32), pltpu.VMEM((8, CONV_W), F32)],
        compiler_params=pltpu.CompilerParams(dimension_semantics=("arbitrary",), vmem_limit_bytes=VMEM_LIMIT),
        name="prompt_front",
    )(x2d, mod_p, g_mix.reshape(1, -1), w_in_bf16, conv_w)


def _band_bias_table():
    qi = jnp.arange(BAND)[:, None]
    kc = jnp.arange(2 * BAND)[None, :]
    delta = qi - (kc - BAND)
    valid = (delta >= 0) & (delta <= BAND)
    tabs = []
    for first in (False, True):
        ok = valid & (kc >= BAND) if first else valid
        for g in range(N_GROUPS):
            for h in range(HEADS):
                b = -_alibi_slope(g, h) * (delta * DILATIONS[g]).astype(F32)
                tabs.append(jnp.where(ok, b, MASKED))
    return jnp.stack(tabs).astype(F32)


def _head_lane_mask(h, dtype=None):
    lane = lax.broadcasted_iota(jnp.int32, (1, GROUP_W), 1)
    return (lane >= h * HEAD_DIM) & (lane < (h + 1) * HEAD_DIM)


def _attn_unit(q, k, v, bias_ref, bias_base):
    o = jnp.zeros((BAND, GROUP_W), F32)
    lse_b = jnp.zeros((BAND, GROUP_W), F32)
    for h in range(HEADS):
        hm = _head_lane_mask(h)
        qm = jnp.where(hm, q, jnp.zeros_like(q))
        s = lax.dot_general(qm, k, (((1,), (1,)), ((), ())), preferred_element_type=F32)
        s = s + bias_ref[bias_base + h]
        m = jnp.max(s, axis=-1, keepdims=True)
        e = jnp.exp(s - m)
        l = jnp.sum(e, axis=-1, keepdims=True)
        p = (e * (1.0 / l)).astype(BF16)
        vm = jnp.where(hm, v, jnp.zeros_like(v))
        o = o + jnp.dot(p, vm, preferred_element_type=F32)
        lse_b = lse_b + jnp.where(hm, m + jnp.log(l), 0.0)
    return o, lse_b


def _attn_body(a0c_ref, a0p_ref, a1c_ref, a1p_ref, a2c_ref, a2p_ref, bias_ref, o_ref, og_ref, lg_ref):
    first_tile = pl.program_id(1) == 0
    first_off = jnp.where(first_tile, N_GROUPS * HEADS, 0)
    qs, ks, vs = (slice(0, GROUP_W), slice(GROUP_W, 2 * GROUP_W), slice(2 * GROUP_W, 3 * GROUP_W))

    def put(g, rows, o, lse_b):
        for c in range(GROUP_W // LANES):
            og_ref[g, c, rows, :] = o[:, c * LANES:(c + 1) * LANES]
            lg_ref[g, c, rows, :] = lse_b[:, c * LANES:(c + 1) * LANES]

    def g0_store(n0, o, lse_b):
        put(0, pl.ds(n0, BAND), o, lse_b)

    q = a0c_ref[0:BAND, qs]
    k = jnp.concatenate([a0p_ref[:, ks], a0c_ref[0:BAND, ks]], axis=0)
    v = jnp.concatenate([a0p_ref[:, vs], a0c_ref[0:BAND, vs]], axis=0)
    g0_store(0, *_attn_unit(q, k, v, bias_ref, first_off))

    def g0_loop(n, carry):
        n0 = pl.multiple_of(n * BAND, BAND)
        q = a0c_ref[pl.ds(n0, BAND), qs]
        k = a0c_ref[pl.ds(n0 - BAND, 2 * BAND), ks]
        v = a0c_ref[pl.ds(n0 - BAND, 2 * BAND), vs]
        g0_store(n0, *_attn_unit(q, k, v, bias_ref, 0))
        return carry

    lax.fori_loop(1, ATT_TILE // BAND, g0_loop, 0)

    def g1_unit(jj, r, kp, vp, bias_base):
        q = a1c_ref[jj, r, :, qs]
        k = jnp.concatenate([kp, a1c_ref[jj, r, :, ks]], axis=0)
        v = jnp.concatenate([vp, a1c_ref[jj, r, :, vs]], axis=0)
        o, lse_b = _attn_unit(q, k, v, bias_ref, bias_base)
        put(1, pl.ds(jj * (4 * BAND) + r, BAND, stride=4), o, lse_b)

    def g1_first(r, carry):
        g1_unit(0, r, a1p_ref[0, r, :, ks], a1p_ref[0, r, :, vs], first_off + HEADS)
        return carry

    lax.fori_loop(0, 4, g1_first, 0)

    def g1_rest(t, carry):
        jj = 1 + t // 4
        r = t % 4
        g1_unit(jj, r, a1c_ref[jj - 1, r, :, ks], a1c_ref[jj - 1, r, :, vs], HEADS)
        return carry

    lax.fori_loop(0, 12, g1_rest, 0)

    n_sub = a2c_ref.shape[0]

    def g2_rows(ref, r, cols):
        return jnp.concatenate([ref[t, r, :, cols] for t in range(n_sub)], axis=0)

    def g2_loop(r, carry):
        q = g2_rows(a2c_ref, r, qs)
        k = jnp.concatenate([g2_rows(a2p_ref, r, ks), g2_rows(a2c_ref, r, ks)], axis=0)
        v = jnp.concatenate([g2_rows(a2p_ref, r, vs), g2_rows(a2c_ref, r, vs)], axis=0)
        o, lse_b = _attn_unit(q, k, v, bias_ref, first_off + 2 * HEADS)
        put(2, pl.ds(r, BAND, stride=16), o, lse_b)
        return carry

    lax.fori_loop(0, 16, g2_loop, 0)

    def mix(c, carry):
        rows = pl.ds(pl.multiple_of(c * BAND, BAND), BAND)
        for half in range(GROUP_W // LANES):
            l0, l1, l2 = lg_ref[0, half, rows, :], lg_ref[1, half, rows, :], lg_ref[2, half, rows, :]
            mx = jnp.maximum(jnp.maximum(l0, l1), l2)
            w0, w1, w2 = jnp.exp(l0 - mx), jnp.exp(l1 - mx), jnp.exp(l2 - mx)
            num = w0 * og_ref[0, half, rows, :] + w1 * og_ref[1, half, rows, :] + w2 * og_ref[2, half, rows, :]
            o_ref[rows, half * LANES:(half + 1) * LANES] = (num / (w0 + w1 + w2)).astype(o_ref.dtype)
        return carry

    lax.fori_loop(0, ATT_TILE // BAND, mix, 0)


def _attention(a0, a1, a2, bias, batch, seq):
    steps = seq // ATT_TILE
    sub = ATT_TILE // TM
    a0 = a0.reshape(batch, seq, ATTN_W)
    a1 = a1.reshape(batch, seq // TM, 4, TM // 4, ATTN_W)
    a2 = a2.reshape(batch, seq // TM, 16, TM // 16, ATTN_W)
    in_specs = [
        pl.BlockSpec((None, ATT_TILE, ATTN_W), lambda b, j: (b, j, 0)),
        pl.BlockSpec((None, BAND, ATTN_W), lambda b, j: (b, jnp.maximum(j * (ATT_TILE // BAND) - 1, 0), 0)),
        pl.BlockSpec((None, sub, 4, TM // 4, ATTN_W), lambda b, j: (b, j, 0, 0, 0)),
        pl.BlockSpec((None, 1, 4, TM // 4, ATTN_W), lambda b, j: (b, jnp.maximum(j * sub - 1, 0), 0, 0, 0)),
        pl.BlockSpec((None, sub, 16, TM // 16, ATTN_W), lambda b, j: (b, j, 0, 0, 0)),
        pl.BlockSpec((None, sub, 16, TM // 16, ATTN_W), lambda b, j: (b, jnp.maximum(j - 1, 0), 0, 0, 0)),
        _resident(bias.shape),
    ]
    return pl.pallas_call(
        _attn_body,
        grid=(batch, steps),
        in_specs=in_specs,
        out_specs=pl.BlockSpec((None, ATT_TILE, GROUP_W), lambda b, j: (b, j, 0)),
        out_shape=jax.ShapeDtypeStruct((batch, seq, GROUP_W), BF16),
        scratch_shapes=[pltpu.VMEM((N_GROUPS, GROUP_W // LANES, ATT_TILE, LANES), F32)] * 2,
        compiler_params=pltpu.CompilerParams(dimension_semantics=("arbitrary", "arbitrary"),
                                             vmem_limit_bytes=VMEM_LIMIT),
        name="prompt_attention",
    )(a0, a0, a1, a1, a2, a2, bias)


def _route(logits):
    lane = lax.broadcasted_iota(jnp.int32, logits.shape, 1)
    big = jnp.int32(1 << 20)
    gmask = lane < N_EXPERT_GROUPS
    lg = jnp.where(gmask, logits, MASKED)
    gmax = jnp.max(lg, axis=-1, keepdims=True)
    gidx = jnp.min(jnp.where(gmask & (lg == gmax), lane, big), axis=-1, keepdims=True)
    p_top = 1.0 / jnp.sum(jnp.where(gmask, jnp.exp(lg - gmax), 0.0), axis=-1, keepdims=True)
    lo = EXPERT_LANE0 + EXPERTS_PER_GROUP * gidx
    emask = (lane >= lo) & (lane < lo + EXPERTS_PER_GROUP)
    le = jnp.where(emask, logits, MASKED)
    v1 = jnp.max(le, axis=-1, keepdims=True)
    i1 = jnp.min(jnp.where(emask & (le == v1), lane, big), axis=-1, keepdims=True)
    emask2 = emask & (lane != i1)
    le2 = jnp.where(emask2, logits, MASKED)
    v2 = jnp.max(le2, axis=-1, keepdims=True)
    i2 = jnp.min(jnp.where(emask2 & (le2 == v2), lane, big), axis=-1, keepdims=True)
    e2 = jnp.exp(v2 - v1)
    den = 1.0 + e2
    w1 = (1.0 / den) * p_top
    w2 = (e2 / den) * p_top
    return jnp.where(lane == i1, w1, 0.0) + jnp.where(lane == i2, w2, 0.0)


def _back_body(oa_ref, oc_ref, sga_ref, sgb_ref, x_ref, mod_ref, wpa_ref, wpb_ref, wo_ref, g_ref,
               wrh_ref, wrl_ref, br_ref, x1_ref, h2_ref, comb_ref):
    pa = jnp.dot(oa_ref[...], wpa_ref[...], preferred_element_type=F32)
    pb = jnp.dot(oc_ref[...], wpb_ref[...], preferred_element_type=F32)
    merged = sga_ref[...].astype(F32) * pa + sgb_ref[...].astype(F32) * pb
    gate1 = mod_ref[:, 2 * D_MODEL:3 * D_MODEL]
    shift2 = mod_ref[:, 3 * D_MODEL:4 * D_MODEL]
    scale2 = mod_ref[:, 4 * D_MODEL:5 * D_MODEL]
    x1 = x_ref[...] + gate1 * jnp.dot(merged.astype(BF16), wo_ref[...], preferred_element_type=F32)
    x1_ref[...] = x1
    h2 = _rmsnorm(x1, g_ref[...]) * (1.0 + scale2) + shift2
    hi = h2.astype(BF16)
    h2_ref[...] = hi
    lo = (h2 - hi.astype(F32)).astype(BF16)
    logits = (jnp.dot(hi, wrh_ref[...], preferred_element_type=F32)
              + jnp.dot(lo, wrh_ref[...], preferred_element_type=F32)
              + jnp.dot(hi, wrl_ref[...], preferred_element_type=F32)) + br_ref[...]
    comb_ref[...] = _route(logits)


def _back(o_attn, oconv, sga, sgb, x2d, mod_p, wpa, wpb, wo, g_ffn, wr_hi, wr_lo, b_r, seq):
    n_tok = x2d.shape[0]
    tps = seq // TM
    row = lambda w: pl.BlockSpec((TM, w), lambda i: (i, 0))
    return pl.pallas_call(
        _back_body,
        grid=(n_tok // TM,),
        in_specs=[row(GROUP_W), row(CONV_W), row(D_MODEL), row(D_MODEL), row(D_MODEL),
                  pl.BlockSpec((None, 1, 6 * D_MODEL), lambda i: (i // tps, 0, 0)),
                  _resident(wpa.shape), _resident(wpb.shape), _resident(wo.shape), _resident((1, D_MODEL)),
                  _resident(wr_hi.shape), _resident(wr_lo.shape), _resident(b_r.shape)],
        out_specs=(row(D_MODEL), row(D_MODEL), row(ROUTER_LANES)),
        out_shape=(jax.ShapeDtypeStruct((n_tok, D_MODEL), F32),
                   jax.ShapeDtypeStruct((n_tok, D_MODEL), BF16),
                   jax.ShapeDtypeStruct((n_tok, ROUTER_LANES), F32)),
        compiler_params=pltpu.CompilerParams(dimension_semantics=("arbitrary",), vmem_limit_bytes=VMEM_LIMIT),
        name="prompt_back",
    )(o_attn, oconv, sga, sgb, x2d, mod_p, wpa, wpb, wo, g_ffn.reshape(1, -1), wr_hi, wr_lo, b_r)


def _moe_body(h_ref, comb_ref, x1_ref, gate2_ref, wg_ref, wu_ref, wd_ref, gf_ref, y_ref, acc_ref):
    e = pl.program_id(1)

    @pl.when(e == 0)
    def _():
        acc_ref[...] = jnp.zeros_like(acc_ref)

    h = h_ref[...].astype(BF16)
    a = jnp.dot(h, wg_ref[0].astype(BF16), preferred_element_type=F32)
    b = jnp.dot(h, wu_ref[0].astype(BF16), preferred_element_type=F32)
    comb = comb_ref[...]
    lane = lax.broadcasted_iota(jnp.int32, comb.shape, 1)
    c = jnp.sum(jnp.where(lane == e + EXPERT_LANE0, comb, 0.0), axis=-1, keepdims=True)
    z = (a * _sigmoid(a)) * b * c
    acc_ref[...] += jnp.dot(z.astype(BF16), wd_ref[0].astype(BF16), preferred_element_type=F32)

    @pl.when(e == N_EXPERTS - 1)
    def _():
        x2 = x1_ref[...] + gate2_ref[0] * acc_ref[...]
        y_ref[...] = _rmsnorm(x2, gf_ref[...])


def _moe(h2, comb, x1, gate2, w_gate_e, w_up_e, w_down_e, g_final, tm):
    n_tok = x1.shape[0]
    n_tiles = n_tok // tm
    g_rows = gate2.shape[1]
    tiles_per_gate = n_tiles // gate2.shape[0]
    return pl.pallas_call(
        _moe_body,
        grid=(n_tiles, N_EXPERTS),
        in_specs=[pl.BlockSpec((tm, D_MODEL), lambda i, e: (i, 0)),
                  pl.BlockSpec((tm, ROUTER_LANES), lambda i, e: (i, 0)),
                  pl.BlockSpec((tm, D_MODEL), lambda i, e: (i, 0)),
                  pl.BlockSpec((1, g_rows, D_MODEL), lambda i, e: (i // tiles_per_gate, 0, 0)),
                  pl.BlockSpec((1, D_MODEL, EXPERT_FF), lambda i, e: (e, 0, 0)),
                  pl.BlockSpec((1, D_MODEL, EXPERT_FF), lambda i, e: (e, 0, 0)),
                  pl.BlockSpec((1, EXPERT_FF, D_MODEL), lambda i, e: (e, 0, 0)),
                  pl.BlockSpec((1, D_MODEL), lambda i, e: (0, 0))],
        out_specs=pl.BlockSpec((tm, D_MODEL), lambda i, e: (i, 0)),
        out_shape=jax.ShapeDtypeStruct((n_tok, D_MODEL), F32),
        scratch_shapes=[pltpu.VMEM((tm, D_MODEL), F32)],
        compiler_params=pltpu.CompilerParams(dimension_semantics=("arbitrary", "arbitrary"),
                                             vmem_limit_bytes=VMEM_LIMIT),
        name="experts_tm%d" % tm,
    )(h2, comb, x1, gate2, w_gate_e, w_up_e, w_down_e, g_final.reshape(1, -1))


def _sample_front_body(x_ref, mod_ref, g_ref, w_ref, o_ref):
    shift1 = mod_ref[:, 0:D_MODEL]
    scale1 = mod_ref[:, D_MODEL:2 * D_MODEL]
    h = _rmsnorm(x_ref[...], g_ref[...]) * (1.0 + scale1) + shift1
    o_ref[...] = jnp.dot(h, w_ref[...], precision=HIGHEST, preferred_element_type=F32)


def _sample_front(x_s, mod_s, g_mix, w_in):
    nb = x_s.shape[0]
    tn = 256
    return pl.pallas_call(
        _sample_front_body,
        grid=(IN_COLS // tn,),
        in_specs=[pl.BlockSpec((nb, D_MODEL), lambda j: (0, 0)),
                  pl.BlockSpec((nb, 2 * D_MODEL), lambda j: (0, 0)),
                  pl.BlockSpec((1, D_MODEL), lambda j: (0, 0)),
                  pl.BlockSpec((D_MODEL, tn), lambda j: (0, j))],
        out_specs=pl.BlockSpec((nb, tn), lambda j: (0, j)),
        out_shape=jax.ShapeDtypeStruct((nb, IN_COLS), F32),
        compiler_params=pltpu.CompilerParams(dimension_semantics=("arbitrary",), vmem_limit_bytes=VMEM_LIMIT),
        name="sample_front",
    )(x_s, mod_s, g_mix.reshape(1, -1), w_in)


def _head_indicator():
    col = jnp.arange(GROUP_W)[:, None] // HEAD_DIM
    return (col == jnp.arange(ROUTER_LANES)[None, :]).astype(F32)


def _sample_bias_table():
    dist = (BAND - jnp.arange(BAND))[:, None].astype(F32)
    lane = jnp.arange(ROUTER_LANES)[None, :]
    tabs = []
    for g in range(N_GROUPS):
        t = jnp.zeros((BAND, ROUTER_LANES), F32)
        for h in range(HEADS):
            t = jnp.where(lane == h, -_alibi_slope(g, h) * (dist * DILATIONS[g]), t)
        tabs.append(t)
    return jnp.stack(tabs)


def _sample_attn_body(p_ref, c0_ref, c1_ref, c2_ref, ind_ref, indt_ref, bias_ref,
                      oa_ref, kv0_ref, kv1_ref, kv2_ref):
    nb = p_ref.shape[0]
    ind = ind_ref[...]
    indt = indt_ref[...]
    hdot = functools.partial(jnp.dot, precision=HIGHEST, preferred_element_type=F32)
    outs, lses = [], []
    for g, (c_ref, kv_ref) in enumerate(((c0_ref, kv0_ref), (c1_ref, kv1_ref), (c2_ref, kv2_ref))):
        q = p_ref[:, Q_OFF + g * GROUP_W:Q_OFF + (g + 1) * GROUP_W] * (HEAD_DIM ** -0.5)
        k_new = p_ref[:, K_OFF + g * GROUP_W:K_OFF + (g + 1) * GROUP_W]
        v_new = p_ref[:, V_OFF + g * GROUP_W:V_OFF + (g + 1) * GROUP_W]
        kv_ref[:, 0:GROUP_W] = k_new
        kv_ref[:, GROUP_W:2 * GROUP_W] = v_new
        kc = c_ref[:, :, 0:GROUP_W]
        vc = c_ref[:, :, GROUP_W:2 * GROUP_W]
        prod = (kc * q[:, None, :]).reshape(nb * BAND, GROUP_W)
        s = hdot(prod, ind).reshape(nb, BAND, ROUTER_LANES) + bias_ref[g][None]
        s_self = hdot(q * k_new, ind)
        m = jnp.maximum(jnp.max(s, axis=1), s_self)
        e = jnp.exp(s - m[:, None, :])
        e_self = jnp.exp(s_self - m)
        l = jnp.sum(e, axis=1) + e_self
        pexp = hdot(e.reshape(nb * BAND, ROUTER_LANES), indt).reshape(nb, BAND, GROUP_W)
        o = jnp.sum(pexp * vc, axis=1) + hdot(e_self, indt) * v_new
        outs.append(o / hdot(l, indt))
        lses.append(hdot(m + jnp.log(l), indt))
    mx = jnp.maximum(jnp.maximum(lses[0], lses[1]), lses[2])
    w = [jnp.exp(x - mx) for x in lses]
    oa_ref[...] = (w[0] * outs[0] + w[1] * outs[1] + w[2] * outs[2]) / (w[0] + w[1] + w[2])


def _sample_attention(proj_s, cache0, cache1, cache2):
    nb = proj_s.shape[0]
    bb = 8
    kv_w = 2 * GROUP_W
    c0 = cache0.reshape(nb, BAND, kv_w)
    c1 = cache1.reshape(nb, BAND, 4 * kv_w)
    c2 = cache2.reshape(nb, BAND, 16 * kv_w)
    ind = _head_indicator()
    cache_spec = pl.BlockSpec((bb, BAND, kv_w), lambda i: (i, 0, 0))
    kv_shape = jax.ShapeDtypeStruct((nb, kv_w), F32)
    return pl.pallas_call(
        _sample_attn_body,
        grid=(nb // bb,),
        in_specs=[pl.BlockSpec((bb, IN_COLS), lambda i: (i, 0)), cache_spec, cache_spec, cache_spec,
                  _resident(ind.shape), _resident(ind.T.shape), _resident((N_GROUPS, BAND, ROUTER_LANES))],
        out_specs=(pl.BlockSpec((bb, GROUP_W), lambda i: (i, 0)),) + (pl.BlockSpec((bb, kv_w), lambda i: (i, 0)),) * 3,
        out_shape=(jax.ShapeDtypeStruct((nb, GROUP_W), F32), kv_shape, kv_shape, kv_shape),
        compiler_params=pltpu.CompilerParams(dimension_semantics=("arbitrary",), vmem_limit_bytes=VMEM_LIMIT),
        name="sample_attention",
    )(proj_s, c0, c1, c2, ind, ind.T, _sample_bias_table())


def _sample_back_body(p_ref, oa_ref, cc_ref, x_ref, mod_ref, cw_ref, wpa_ref, wpb_ref, wo_ref, g_ref,
                      wr_ref, br_ref, x1_ref, h2_ref, comb_ref, sconv_ref):
    hdot = functools.partial(jnp.dot, precision=HIGHEST, preferred_element_type=F32)
    bg = p_ref[:, BG_OFF:BG_OFF + CONV_W]
    u = p_ref[:, CG_OFF:CG_OFF + CONV_W] * p_ref[:, UI_OFF:UI_OFF + CONV_W]
    c_old, c_new = cc_ref[:, 0, :], cc_ref[:, 1, :]
    yconv = cw_ref[0:1, :] * c_old + cw_ref[1:2, :] * c_new + cw_ref[2:3, :] * u
    sconv_ref[:, 0, :] = c_new
    sconv_ref[:, 1, :] = u
    sga = _sigmoid(p_ref[:, GA_OFF:GA_OFF + D_MODEL])
    sgb = _sigmoid(p_ref[:, GB_OFF:GB_OFF + D_MODEL])
    merged = sga * hdot(oa_ref[...], wpa_ref[...]) + sgb * hdot(bg * yconv, wpb_ref[...])
    gate1 = mod_ref[:, 2 * D_MODEL:3 * D_MODEL]
    shift2 = mod_ref[:, 3 * D_MODEL:4 * D_MODEL]
    scale2 = mod_ref[:, 4 * D_MODEL:5 * D_MODEL]
    x1 = x_ref[...] + gate1 * hdot(merged, wo_ref[...])
    x1_ref[...] = x1
    h2 = _rmsnorm(x1, g_ref[...]) * (1.0 + scale2) + shift2
    h2_ref[...] = h2
    comb_ref[...] = _route(hdot(h2, wr_ref[...]) + br_ref[...])


def _sample_back(proj_s, oa_s, cache_conv, x_s, mod_s, conv_w, w_pa, w_pb, w_o, g_ffn, w_r, b_r):
    nb = x_s.shape[0]
    args = (proj_s, oa_s, cache_conv, x_s, mod_s, conv_w, w_pa, w_pb, w_o, g_ffn.reshape(1, -1), w_r, b_r)
    full = lambda shape: pl.BlockSpec(shape, lambda i: (0,) * len(shape))
    out_shape = (jax.ShapeDtypeStruct((nb, D_MODEL), F32), jax.ShapeDtypeStruct((nb, D_MODEL), F32),
                 jax.ShapeDtypeStruct((nb, ROUTER_LANES), F32), jax.ShapeDtypeStruct((nb, 2, CONV_W), F32))
    return pl.pallas_call(
        _sample_back_body,
        grid=(1,),
        in_specs=[full(a.shape) for a in args],
        out_specs=tuple(full(s.shape) for s in out_shape),
        out_shape=out_shape,
        compiler_params=pltpu.CompilerParams(dimension_semantics=("arbitrary",), vmem_limit_bytes=VMEM_LIMIT),
        name="sample_back",
    )(*args)


def kernel(x_prompt, x_sample, cache_kv_w128, cache_kv_w512, cache_kv_w2048, cache_conv, c_prompt, c_sample,
           g_mix, w_ada, b_ada, w_in, conv_w, w_pa, w_pb, w_o, g_ffn, w_router_group, b_router_group,
           w_router_expert, b_router_expert, w_gate_e, w_up_e, w_down_e, g_final):
    batch, seq, _ = x_prompt.shape
    nb = x_sample.shape[0]
    assert x_sample.shape[1] == 1 and g_mix.shape[0] == 1, "one layer, one new sample token per sequence"
    assert seq % ATT_TILE == 0 and cache_kv_w128.shape[2] == 128 and cache_kv_w512.shape[2] == 512 \
        and cache_kv_w2048.shape[2] == 2048
    (g_mix, w_ada, b_ada, w_in, conv_w, w_pa, w_pb, w_o, g_ffn, w_rg, b_rg, w_re, b_re, w_gate_e, w_up_e,
     w_down_e) = (a[0] for a in (g_mix, w_ada, b_ada, w_in, conv_w, w_pa, w_pb, w_o, g_ffn, w_router_group,
                                 b_router_group, w_router_expert, b_router_expert, w_gate_e, w_up_e, w_down_e))

    c_all = jnp.concatenate([c_prompt, jnp.zeros((8 - batch, D_MODEL), F32), c_sample], axis=0)
    mod = _modulation(c_all, w_ada, b_ada)
    mod_p = mod[:batch].reshape(batch, 1, 6 * D_MODEL)
    mod_s = mod[8:]

    w_r = jnp.zeros((D_MODEL, ROUTER_LANES), F32)
    w_r = w_r.at[:, :N_EXPERT_GROUPS].set(w_rg).at[:, EXPERT_LANE0:EXPERT_LANE0 + N_EXPERTS].set(w_re)
    b_r = jnp.full((1, ROUTER_LANES), MASKED, F32)
    b_r = b_r.at[0, :N_EXPERT_GROUPS].set(b_rg).at[0, EXPERT_LANE0:EXPERT_LANE0 + N_EXPERTS].set(b_re)
    wr_hi = w_r.astype(BF16)
    wr_lo = (w_r - wr_hi.astype(F32)).astype(BF16)

    x2d = x_prompt.reshape(batch * seq, D_MODEL)
    (a0, a1, a2, oconv, sga, sgb, kv0, kv1, kv2, pconv) = _front(
        x2d, mod_p, g_mix, w_in.astype(BF16), conv_w, batch, seq)
    o_attn = _attention(a0, a1, a2, _band_bias_table(), batch, seq).reshape(batch * seq, GROUP_W)
    x1, h2, comb = _back(o_attn, oconv, sga, sgb, x2d, mod_p, w_pa.astype(BF16), w_pb.astype(BF16),
                         w_o.astype(BF16), g_ffn, wr_hi, wr_lo, b_r, seq)
    gate2_p = mod_p[:, :, 5 * D_MODEL:]
    y_prompt = _moe(h2, comb, x1, gate2_p, w_gate_e, w_up_e, w_down_e, g_final, MOE_TM)

    x_s = x_sample.reshape(nb, D_MODEL)
    proj_s = _sample_front(x_s, mod_s[:, :2 * D_MODEL], g_mix, w_in)
    oa_s, skv0, skv1, skv2 = _sample_attention(proj_s, cache_kv_w128[0], cache_kv_w512[0], cache_kv_w2048[0])
    x1_s, h2_s, comb_s, sconv = _sample_back(proj_s, oa_s, cache_conv[0], x_s, mod_s, conv_w, w_pa, w_pb, w_o,
                                             g_ffn, w_r, b_r)
    gate2_s = mod_s[:, 5 * D_MODEL:].reshape(1, nb, D_MODEL)
    y_sample = _moe(h2_s, comb_s, x1_s, gate2_s, w_gate_e, w_up_e, w_down_e, g_final, nb)

    kv_state = lambda a, rows: a.reshape(1, a.shape[0], rows, 2, HEADS, HEAD_DIM)
    return (y_prompt.reshape(batch, seq, D_MODEL), y_sample.reshape(nb, 1, D_MODEL),
            kv_state(kv0, 128), kv_state(kv1, 512), kv_state(kv2, 2048),
            pconv[:, 6:8, :].reshape(1, batch, 2, CONV_W),
            kv_state(skv0, 1), kv_state(skv1, 1), kv_state(skv2, 1),
            sconv.reshape(1, nb, 2, CONV_W))
```

```python
import functools
import math

import jax
import jax.numpy as jnp
from jax import lax
from jax.experimental import pallas as pl
from jax.experimental.pallas import tpu as pltpu

F32 = jnp.float32
BF16 = jnp.bfloat16
HIGHEST = lax.Precision.HIGHEST

D_MODEL = 1024
HEAD_DIM = 64
HEADS = 4
N_GROUPS = 3
GROUP_W = HEADS * HEAD_DIM
ATTN_W = N_GROUPS * GROUP_W
CONV_W = 512
DILATIONS = (1, 4, 16)
BAND = 128
N_EXPERTS = 32
EXPERTS_PER_GROUP = 8
N_EXPERT_GROUPS = 4
EXPERT_FF = 256
RMS_EPS = 1e-6
MASKED = -1e30

Q_OFF, K_OFF, V_OFF = 0, ATTN_W, 2 * ATTN_W
BG_OFF = 3 * ATTN_W
CG_OFF = BG_OFF + CONV_W
UI_OFF = CG_OFF + CONV_W
GA_OFF = UI_OFF + CONV_W
GB_OFF = GA_OFF + D_MODEL
IN_COLS = GB_OFF + D_MODEL

LANES = 128
ROUTER_LANES = 128
EXPERT_LANE0 = N_EXPERT_GROUPS

TM = 512
ATT_TILE = 2048
MOE_TM = 1024
VMEM_LIMIT = 56 * 1024 * 1024


def _sigmoid(x):
    return 1.0 / (1.0 + jnp.exp(-x))


def _rmsnorm(x, g):
    return x * lax.rsqrt(jnp.mean(x * x, axis=-1, keepdims=True) + RMS_EPS) * g


def _alibi_slope(g, h):
    return 2.0 ** (-8.0 * (g * HEADS + h + 1) / (N_GROUPS * HEADS))


def _resident(shape):
    nd = len(shape)
    return pl.BlockSpec(shape, lambda *_: (0,) * nd, pipeline_mode=pl.Buffered(1))


def _mod_body(c_ref, w_ref, b_ref, o_ref):
    c = c_ref[...]
    s = c * _sigmoid(c)
    o_ref[...] = jnp.dot(s, w_ref[...], precision=HIGHEST, preferred_element_type=F32) + b_ref[...]


def _modulation(c_all, w_ada, b_ada):
    rows = c_all.shape[0]
    tn = 1024
    return pl.pallas_call(
        _mod_body,
        grid=(6 * D_MODEL // tn,),
        in_specs=[pl.BlockSpec((rows, D_MODEL), lambda j: (0, 0)),
                  pl.BlockSpec((D_MODEL, tn), lambda j: (0, j)),
                  pl.BlockSpec((1, tn), lambda j: (0, j))],
        out_specs=pl.BlockSpec((rows, tn), lambda j: (0, j)),
        out_shape=jax.ShapeDtypeStruct((rows, 6 * D_MODEL), F32),
        compiler_params=pltpu.CompilerParams(dimension_semantics=("arbitrary",), vmem_limit_bytes=VMEM_LIMIT),
        name="modulation",
    )(c_all, w_ada, b_ada.reshape(1, -1))


def _front_body(tiles_per_seq, x_ref, mod_ref, g_ref, w_ref, cw_ref,
                a0_ref, a1_ref, a2_ref, oconv_ref, sga_ref, sgb_ref,
                kv0_ref, kv1_ref, kv2_ref, pconv_ref, res_ref, uprev_ref):
    t_in_seq = pl.program_id(0) % tiles_per_seq
    x = x_ref[...]
    shift1 = mod_ref[:, 0:D_MODEL]
    scale1 = mod_ref[:, D_MODEL:2 * D_MODEL]
    h = (_rmsnorm(x, g_ref[...]) * (1.0 + scale1) + shift1).astype(BF16)

    def proj(c0, n):
        return jnp.dot(h, w_ref[:, c0:c0 + n], preferred_element_type=F32)

    a_refs = (a0_ref, a1_ref, a2_ref)
    kv_refs = (kv0_ref, kv1_ref, kv2_ref)
    for g in range(N_GROUPS):
        d = DILATIONS[g]
        n = TM // d
        for part, base in enumerate((Q_OFF, K_OFF, V_OFF)):
            r = proj(base + g * GROUP_W, GROUP_W)
            cols = slice(part * GROUP_W, (part + 1) * GROUP_W)
            if part == 0:
                r = r * (HEAD_DIM ** -0.5)
            else:
                kvc = slice((part - 1) * GROUP_W, part * GROUP_W)
                kv_refs[g][kvc, :] = (r[TM - BAND:, :] if g == 0 else r).T
            if g == 0:
                a_refs[g][:, cols] = r.astype(BF16)
            else:
                for c in range(GROUP_W // LANES):
                    res_ref[c] = r[:, c * LANES:(c + 1) * LANES]
                for rr in range(d):
                    for c in range(GROUP_W // LANES):
                        c0 = part * GROUP_W + c * LANES
                        a_refs[g][0, rr, :, c0:c0 + LANES] = res_ref[c, pl.ds(rr, n, stride=d), :].astype(BF16)

    bg = proj(BG_OFF, CONV_W)
    u = proj(CG_OFF, CONV_W) * proj(UI_OFF, CONV_W)
    tail = u[TM - 8:, :]
    pconv_ref[...] = tail
    prev = jnp.where(t_in_seq == 0, 0.0, uprev_ref[...])
    row = lax.broadcasted_iota(jnp.int32, (TM, 1), 0)
    u1 = jnp.where(row == 0, prev[7:8, :], pltpu.roll(u, 1, axis=0))
    u2 = jnp.where(row == 0, prev[6:7, :], jnp.where(row == 1, prev[7:8, :], pltpu.roll(u, 2, axis=0)))
    yconv = cw_ref[0:1, :] * u2 + cw_ref[1:2, :] * u1 + cw_ref[2:3, :] * u
    oconv_ref[...] = (bg * yconv).astype(BF16)
    uprev_ref[...] = tail

    sga_ref[...] = _sigmoid(proj(GA_OFF, D_MODEL)).astype(BF16)
    sgb_ref[...] = _sigmoid(proj(GB_OFF, D_MODEL)).astype(BF16)


def _front(x2d, mod_p, g_mix, w_in_bf16, conv_w, batch, seq):
    n_tok = x2d.shape[0]
    n_tiles = n_tok // TM
    tps = seq // TM
    kv2_blocks = ATT_TILE // TM
    out_shape = (
        jax.ShapeDtypeStruct((n_tok, ATTN_W), BF16),
        jax.ShapeDtypeStruct((n_tiles, 4, TM // 4, ATTN_W), BF16),
        jax.ShapeDtypeStruct((n_tiles, 16, TM // 16, ATTN_W), BF16),
        jax.ShapeDtypeStruct((n_tok, CONV_W), BF16),
        jax.ShapeDtypeStruct((n_tok, D_MODEL), BF16),
        jax.ShapeDtypeStruct((n_tok, D_MODEL), BF16),
        jax.ShapeDtypeStruct((batch, 2 * GROUP_W, 128), F32),
        jax.ShapeDtypeStruct((batch, 2 * GROUP_W, 512), F32),
        jax.ShapeDtypeStruct((batch, 2 * GROUP_W, 2048), F32),
        jax.ShapeDtypeStruct((batch, 8, CONV_W), F32),
    )
    out_specs = (
        pl.BlockSpec((TM, ATTN_W), lambda i: (i, 0)),
        pl.BlockSpec((1, 4, TM // 4, ATTN_W), lambda i: (i, 0, 0, 0)),
        pl.BlockSpec((1, 16, TM // 16, ATTN_W), lambda i: (i, 0, 0, 0)),
        pl.BlockSpec((TM, CONV_W), lambda i: (i, 0)),
        pl.BlockSpec((TM, D_MODEL), lambda i: (i, 0)),
        pl.BlockSpec((TM, D_MODEL), lambda i: (i, 0)),
        pl.BlockSpec((None, 2 * GROUP_W, 128), lambda i: (i // tps, 0, 0)),
        pl.BlockSpec((None, 2 * GROUP_W, TM), lambda i: (i // tps, 0, 0)),
        pl.BlockSpec((None, 2 * GROUP_W, TM),
                     lambda i: (i // tps, 0, jnp.maximum(i % tps - (tps - kv2_blocks), 0))),
        pl.BlockSpec((None, 8, CONV_W), lambda i: (i // tps, 0, 0)),
    )
    in_specs = [
        pl.BlockSpec((TM, D_MODEL), lambda i: (i, 0)),
        pl.BlockSpec((None, 1, 6 * D_MODEL), lambda i: (i // tps, 0, 0)),
        _resident((1, D_MODEL)),
        _resident((D_MODEL, IN_COLS)),
        _resident((3, CONV_W)),
    ]
    return pl.pallas_call(
        functools.partial(_front_body, tps),
        grid=(n_tiles,),
        in_specs=in_specs,
        out_specs=out_specs,
        out_shape=out_shape,
        scratch_shapes=[pltpu.VMEM((GROUP_W // LANES, TM, LANES), F32), pltpu.VMEM((8, CONV_W), F32)],
        compiler_params=pltpu.CompilerParams(dimension_semantics=("arbitrary",), vmem_limit_bytes=VMEM_LIMIT),
        name="prompt_front",
    )(x2d, mod_p, g_mix.reshape(1, -1), w_in_bf16, conv_w)


def _band_bias_table():
    qi = jnp.arange(BAND)[:, None]
    kc = jnp.arange(2 * BAND)[None, :]
    delta = qi - (kc - BAND)
    valid = (delta >= 0) & (delta <= BAND)
    tabs = []
    for first in (False, True):
        ok = valid & (kc >= BAND) if first else valid
        for g in range(N_GROUPS):
            for h in range(HEADS):
                b = -_alibi_slope(g, h) * (delta * DILATIONS[g]).astype(F32)
                tabs.append(jnp.where(ok, b, MASKED))
    return jnp.stack(tabs).astype(F32)


def _head_lane_mask(h, dtype=None):
    lane = lax.broadcasted_iota(jnp.int32, (1, GROUP_W), 1)
    return (lane >= h * HEAD_DIM) & (lane < (h + 1) * HEAD_DIM)


def _attn_unit(q, k, v, bias_ref, bias_base):
    o = jnp.zeros((BAND, GROUP_W), F32)
    lse_b = jnp.zeros((BAND, GROUP_W), F32)
    for h in range(HEADS):
        hm = _head_lane_mask(h)
        qm = jnp.where(hm, q, jnp.zeros_like(q))
        s = lax.dot_general(qm, k, (((1,), (1,)), ((), ())), preferred_element_type=F32)
        s = s + bias_ref[bias_base + h]
        m = jnp.max(s, axis=-1, keepdims=True)
        e = jnp.exp(s - m)
        l = jnp.sum(e, axis=-1, keepdims=True)
        p = (e * (1.0 / l)).astype(BF16)
        vm = jnp.where(hm, v, jnp.zeros_like(v))
        o = o + jnp.dot(p, vm, preferred_element_type=F32)
        lse_b = lse_b + jnp.where(hm, m + jnp.log(l), 0.0)
    return o, lse_b


def _attn_body(a0c_ref, a0p_ref, a1c_ref, a1p_ref, a2c_ref, a2p_ref, bias_ref, o_ref, og_ref, lg_ref):
    first_tile = pl.program_id(1) == 0
    first_off = jnp.where(first_tile, N_GROUPS * HEADS, 0)
    qs, ks, vs = (slice(0, GROUP_W), slice(GROUP_W, 2 * GROUP_W), slice(2 * GROUP_W, 3 * GROUP_W))

    def put(g, rows, o, lse_b):
        for c in range(GROUP_W // LANES):
            og_ref[g, c, rows, :] = o[:, c * LANES:(c + 1) * LANES]
            lg_ref[g, c, rows, :] = lse_b[:, c * LANES:(c + 1) * LANES]

    def g0_store(n0, o, lse_b):
        put(0, pl.ds(n0, BAND), o, lse_b)

    q = a0c_ref[0:BAND, qs]
    k = jnp.concatenate([a0p_ref[:, ks], a0c_ref[0:BAND, ks]], axis=0)
    v = jnp.concatenate([a0p_ref[:, vs], a0c_ref[0:BAND, vs]], axis=0)
    g0_store(0, *_attn_unit(q, k, v, bias_ref, first_off))

    def g0_loop(n, carry):
        n0 = pl.multiple_of(n * BAND, BAND)
        q = a0c_ref[pl.ds(n0, BAND), qs]
        k = a0c_ref[pl.ds(n0 - BAND, 2 * BAND), ks]
        v = a0c_ref[pl.ds(n0 - BAND, 2 * BAND), vs]
        g0_store(n0, *_attn_unit(q, k, v, bias_ref, 0))
        return carry

    lax.fori_loop(1, ATT_TILE // BAND, g0_loop, 0)

    def g1_unit(jj, r, kp, vp, bias_base):
        q = a1c_ref[jj, r, :, qs]
        k = jnp.concatenate([kp, a1c_ref[jj, r, :, ks]], axis=0)
        v = jnp.concatenate([vp, a1c_ref[jj, r, :, vs]], axis=0)
        o, lse_b = _attn_unit(q, k, v, bias_ref, bias_base)
        put(1, pl.ds(jj * (4 * BAND) + r, BAND, stride=4), o, lse_b)

    def g1_first(r, carry):
        g1_unit(0, r, a1p_ref[0, r, :, ks], a1p_ref[0, r, :, vs], first_off + HEADS)
        return carry

    lax.fori_loop(0, 4, g1_first, 0)

    def g1_rest(t, carry):
        jj = 1 + t // 4
        r = t % 4
        g1_unit(jj, r, a1c_ref[jj - 1, r, :, ks], a1c_ref[jj - 1, r, :, vs], HEADS)
        return carry

    lax.fori_loop(0, 12, g1_rest, 0)

    n_sub = a2c_ref.shape[0]

    def g2_rows(ref, r, cols):
        return jnp.concatenate([ref[t, r, :, cols] for t in range(n_sub)], axis=0)

    def g2_loop(r, carry):
        q = g2_rows(a2c_ref, r, qs)
        k = jnp.concatenate([g2_rows(a2p_ref, r, ks), g2_rows(a2c_ref, r, ks)], axis=0)
        v = jnp.concatenate([g2_rows(a2p_ref, r, vs), g2_rows(a2c_ref, r, vs)], axis=0)
        o, lse_b = _attn_unit(q, k, v, bias_ref, first_off + 2 * HEADS)
        put(2, pl.ds(r, BAND, stride=16), o, lse_b)
        return carry

    lax.fori_loop(0, 16, g2_loop, 0)

    def mix(c, carry):
        rows = pl.ds(pl.multiple_of(c * BAND, BAND), BAND)
        for half in range(GROUP_W // LANES):
            l0, l1, l2 = lg_ref[0, half, rows, :], lg_ref[1, half, rows, :], lg_ref[2, half, rows, :]
            mx = jnp.maximum(jnp.maximum(l0, l1), l2)
            w0, w1, w2 = jnp.exp(l0 - mx), jnp.exp(l1 - mx), jnp.exp(l2 - mx)
            num = w0 * og_ref[0, half, rows, :] + w1 * og_ref[1, half, rows, :] + w2 * og_ref[2, half, rows, :]
            o_ref[rows, half * LANES:(half + 1) * LANES] = (num / (w0 + w1 + w2)).astype(o_ref.dtype)
        return carry

    lax.fori_loop(0, ATT_TILE // BAND, mix, 0)


def _attention(a0, a1, a2, bias, batch, seq):
    steps = seq // ATT_TILE
    sub = ATT_TILE // TM
    a0 = a0.reshape(batch, seq, ATTN_W)
    a1 = a1.reshape(batch, seq // TM, 4, TM // 4, ATTN_W)
    a2 = a2.reshape(batch, seq // TM, 16, TM // 16, ATTN_W)
    in_specs = [
        pl.BlockSpec((None, ATT_TILE, ATTN_W), lambda b, j: (b, j, 0)),
        pl.BlockSpec((None, BAND, ATTN_W), lambda b, j: (b, jnp.maximum(j * (ATT_TILE // BAND) - 1, 0), 0)),
        pl.BlockSpec((None, sub, 4, TM // 4, ATTN_W), lambda b, j: (b, j, 0, 0, 0)),
        pl.BlockSpec((None, 1, 4, TM // 4, ATTN_W), lambda b, j: (b, jnp.maximum(j * sub - 1, 0), 0, 0, 0)),
        pl.BlockSpec((None, sub, 16, TM // 16, ATTN_W), lambda b, j: (b, j, 0, 0, 0)),
        pl.BlockSpec((None, sub, 16, TM // 16, ATTN_W), lambda b, j: (b, jnp.maximum(j - 1, 0), 0, 0, 0)),
        _resident(bias.shape),
    ]
    return pl.pallas_call(
        _attn_body,
        grid=(batch, steps),
        in_specs=in_specs,
        out_specs=pl.BlockSpec((None, ATT_TILE, GROUP_W), lambda b, j: (b, j, 0)),
        out_shape=jax.ShapeDtypeStruct((batch, seq, GROUP_W), BF16),
        scratch_shapes=[pltpu.VMEM((N_GROUPS, GROUP_W // LANES, ATT_TILE, LANES), F32)] * 2,
        compiler_params=pltpu.CompilerParams(dimension_semantics=("arbitrary", "arbitrary"),
                                             vmem_limit_bytes=VMEM_LIMIT),
        name="prompt_attention",
    )(a0, a0, a1, a1, a2, a2, bias)


def _route(logits):
    lane = lax.broadcasted_iota(jnp.int32, logits.shape, 1)
    big = jnp.int32(1 << 20)
    gmask = lane < N_EXPERT_GROUPS
    lg = jnp.where(gmask, logits, MASKED)
    gmax = jnp.max(lg, axis=-1, keepdims=True)
    gidx = jnp.min(jnp.where(gmask & (lg == gmax), lane, big), axis=-1, keepdims=True)
    p_top = 1.0 / jnp.sum(jnp.where(gmask, jnp.exp(lg - gmax), 0.0), axis=-1, keepdims=True)
    lo = EXPERT_LANE0 + EXPERTS_PER_GROUP * gidx
    emask = (lane >= lo) & (lane < lo + EXPERTS_PER_GROUP)
    le = jnp.where(emask, logits, MASKED)
    v1 = jnp.max(le, axis=-1, keepdims=True)
    i1 = jnp.min(jnp.where(emask & (le == v1), lane, big), axis=-1, keepdims=True)
    emask2 = emask & (lane != i1)
    le2 = jnp.where(emask2, logits, MASKED)
    v2 = jnp.max(le2, axis=-1, keepdims=True)
    i2 = jnp.min(jnp.where(emask2 & (le2 == v2), lane, big), axis=-1, keepdims=True)
    e2 = jnp.exp(v2 - v1)
    den = 1.0 + e2
    w1 = (1.0 / den) * p_top
    w2 = (e2 / den) * p_top
    return jnp.where(lane == i1, w1, 0.0) + jnp.where(lane == i2, w2, 0.0)


def _back_body(oa_ref, oc_ref, sga_ref, sgb_ref, x_ref, mod_ref, wpa_ref, wpb_ref, wo_ref, g_ref,
               wrh_ref, wrl_ref, br_ref, x1_ref, h2_ref, comb_ref):
    pa = jnp.dot(oa_ref[...], wpa_ref[...], preferred_element_type=F32)
    pb = jnp.dot(oc_ref[...], wpb_ref[...], preferred_element_type=F32)
    merged = sga_ref[...].astype(F32) * pa + sgb_ref[...].astype(F32) * pb
    gate1 = mod_ref[:, 2 * D_MODEL:3 * D_MODEL]
    shift2 = mod_ref[:, 3 * D_MODEL:4 * D_MODEL]
    scale2 = mod_ref[:, 4 * D_MODEL:5 * D_MODEL]
    x1 = x_ref[...] + gate1 * jnp.dot(merged.astype(BF16), wo_ref[...], preferred_element_type=F32)
    x1_ref[...] = x1
    h2 = _rmsnorm(x1, g_ref[...]) * (1.0 + scale2) + shift2
    hi = h2.astype(BF16)
    h2_ref[...] = hi
    lo = (h2 - hi.astype(F32)).astype(BF16)
    logits = (jnp.dot(hi, wrh_ref[...], preferred_element_type=F32)
              + jnp.dot(lo, wrh_ref[...], preferred_element_type=F32)
              + jnp.dot(hi, wrl_ref[...], preferred_element_type=F32)) + br_ref[...]
    comb_ref[...] = _route(logits)


def _back(o_attn, oconv, sga, sgb, x2d, mod_p, wpa, wpb, wo, g_ffn, wr_hi, wr_lo, b_r, seq):
    n_tok = x2d.shape[0]
    tps = seq // TM
    row = lambda w: pl.BlockSpec((TM, w), lambda i: (i, 0))
    return pl.pallas_call(
        _back_body,
        grid=(n_tok // TM,),
        in_specs=[row(GROUP_W), row(CONV_W), row(D_MODEL), row(D_MODEL), row(D_MODEL),
                  pl.BlockSpec((None, 1, 6 * D_MODEL), lambda i: (i // tps, 0, 0)),
                  _resident(wpa.shape), _resident(wpb.shape), _resident(wo.shape), _resident((1, D_MODEL)),
                  _resident(wr_hi.shape), _resident(wr_lo.shape), _resident(b_r.shape)],
        out_specs=(row(D_MODEL), row(D_MODEL), row(ROUTER_LANES)),
        out_shape=(jax.ShapeDtypeStruct((n_tok, D_MODEL), F32),
                   jax.ShapeDtypeStruct((n_tok, D_MODEL), BF16),
                   jax.ShapeDtypeStruct((n_tok, ROUTER_LANES), F32)),
        compiler_params=pltpu.CompilerParams(dimension_semantics=("arbitrary",), vmem_limit_bytes=VMEM_LIMIT),
        name="prompt_back",
    )(o_attn, oconv, sga, sgb, x2d, mod_p, wpa, wpb, wo, g_ffn.reshape(1, -1), wr_hi, wr_lo, b_r)


def _moe_body(h_ref, comb_ref, x1_ref, gate2_ref, wg_ref, wu_ref, wd_ref, gf_ref, y_ref, acc_ref):
    e = pl.program_id(1)

    @pl.when(e == 0)
    def _():
        acc_ref[...] = jnp.zeros_like(acc_ref)

    h = h_ref[...].astype(BF16)
    a = jnp.dot(h, wg_ref[0].astype(BF16), preferred_element_type=F32)
    b = jnp.dot(h, wu_ref[0].astype(BF16), preferred_element_type=F32)
    comb = comb_ref[...]
    lane = lax.broadcasted_iota(jnp.int32, comb.shape, 1)
    c = jnp.sum(jnp.where(lane == e + EXPERT_LANE0, comb, 0.0), axis=-1, keepdims=True)
    z = (a * _sigmoid(a)) * b * c
    acc_ref[...] += jnp.dot(z.astype(BF16), wd_ref[0].astype(BF16), preferred_element_type=F32)

    @pl.when(e == N_EXPERTS - 1)
    def _():
        x2 = x1_ref[...] + gate2_ref[0] * acc_ref[...]
        y_ref[...] = _rmsnorm(x2, gf_ref[...])


def _moe(h2, comb, x1, gate2, w_gate_e, w_up_e, w_down_e, g_final, tm):
    n_tok = x1.shape[0]
    n_tiles = n_tok // tm
    g_rows = gate2.shape[1]
    tiles_per_gate = n_tiles // gate2.shape[0]
    return pl.pallas_call(
        _moe_body,
        grid=(n_tiles, N_EXPERTS),
        in_specs=[pl.BlockSpec((tm, D_MODEL), lambda i, e: (i, 0)),
                  pl.BlockSpec((tm, ROUTER_LANES), lambda i, e: (i, 0)),
                  pl.BlockSpec((tm, D_MODEL), lambda i, e: (i, 0)),
                  pl.BlockSpec((1, g_rows, D_MODEL), lambda i, e: (i // tiles_per_gate, 0, 0)),
                  pl.BlockSpec((1, D_MODEL, EXPERT_FF), lambda i, e: (e, 0, 0)),
                  pl.BlockSpec((1, D_MODEL, EXPERT_FF), lambda i, e: (e, 0, 0)),
                  pl.BlockSpec((1, EXPERT_FF, D_MODEL), lambda i, e: (e, 0, 0)),
                  pl.BlockSpec((1, D_MODEL), lambda i, e: (0, 0))],
        out_specs=pl.BlockSpec((tm, D_MODEL), lambda i, e: (i, 0)),
        out_shape=jax.ShapeDtypeStruct((n_tok, D_MODEL), F32),
        scratch_shapes=[pltpu.VMEM((tm, D_MODEL), F32)],
        compiler_params=pltpu.CompilerParams(dimension_semantics=("arbitrary", "arbitrary"),
                                             vmem_limit_bytes=VMEM_LIMIT),
        name="experts_tm%d" % tm,
    )(h2, comb, x1, gate2, w_gate_e, w_up_e, w_down_e, g_final.reshape(1, -1))


def _sample_front_body(x_ref, mod_ref, g_ref, w_ref, o_ref):
    shift1 = mod_ref[:, 0:D_MODEL]
    scale1 = mod_ref[:, D_MODEL:2 * D_MODEL]
    h = _rmsnorm(x_ref[...], g_ref[...]) * (1.0 + scale1) + shift1
    o_ref[...] = jnp.dot(h, w_ref[...], precision=HIGHEST, preferred_element_type=F32)


def _sample_front(x_s, mod_s, g_mix, w_in):
    nb = x_s.shape[0]
    tn = 256
    return pl.pallas_call(
        _sample_front_body,
        grid=(IN_COLS // tn,),
        in_specs=[pl.BlockSpec((nb, D_MODEL), lambda j: (0, 0)),
                  pl.BlockSpec((nb, 2 * D_MODEL), lambda j: (0, 0)),
                  pl.BlockSpec((1, D_MODEL), lambda j: (0, 0)),
                  pl.BlockSpec((D_MODEL, tn), lambda j: (0, j))],
        out_specs=pl.BlockSpec((nb, tn), lambda j: (0, j)),
        out_shape=jax.ShapeDtypeStruct((nb, IN_COLS), F32),
        compiler_params=pltpu.CompilerParams(dimension_semantics=("arbitrary",), vmem_limit_bytes=VMEM_LIMIT),
        name="sample_front",
    )(x_s, mod_s, g_mix.reshape(1, -1), w_in)


def _sample_attn_body(qkv_ref, c0_ref, c1_ref, c2_ref, oa_ref):
    head = lax.broadcasted_iota(jnp.int32, (HEADS, 1, 1), 0)
    for b in range(qkv_ref.shape[0]):
        outs, lses = [], []
        for g, c_ref in enumerate((c0_ref, c1_ref, c2_ref)):
            window = c_ref.shape[-1]
            q = qkv_ref[b, 0, g] * (HEAD_DIM ** -0.5)
            k_new = qkv_ref[b, 1, g]
            v_new = qkv_ref[b, 2, g]
            s = jnp.sum(c_ref[b, 0] * q, axis=1, keepdims=True)
            pos = lax.broadcasted_iota(jnp.int32, (1, 1, window), 2)
            slope = jnp.full((HEADS, 1, 1), _alibi_slope(g, HEADS - 1), F32)
            for h in range(HEADS - 1):
                slope = jnp.where(head == h, _alibi_slope(g, h), slope)
            on_band = (pos & (DILATIONS[g] - 1)) == 0
            s = jnp.where(on_band, s - slope * (window - pos).astype(F32), MASKED)
            s_self = jnp.sum(q * k_new, axis=1, keepdims=True)
            m = jnp.maximum(jnp.max(s, axis=2, keepdims=True), s_self)
            e = jnp.exp(s - m)
            e_self = jnp.exp(s_self - m)
            l = jnp.sum(e, axis=2, keepdims=True) + e_self
            o = jnp.sum(c_ref[b, 1] * e, axis=2, keepdims=True) + e_self * v_new
            outs.append(o / l)
            lses.append(m + jnp.log(l))
        mx = jnp.maximum(jnp.maximum(lses[0], lses[1]), lses[2])
        w = [jnp.exp(x - mx) for x in lses]
        oa_ref[b] = (w[0] * outs[0] + w[1] * outs[1] + w[2] * outs[2]) / (w[0] + w[1] + w[2])


def _sample_attention(qkv_s, cache0, cache1, cache2):
    nb = qkv_s.shape[0]
    bb = 2
    hd = (HEADS, HEAD_DIM)
    caches = [jnp.transpose(c, (0, 1, 3, 4, 5, 2)) for c in (cache0, cache1, cache2)]
    cache_spec = lambda c: pl.BlockSpec((None, bb, 2, *hd, c.shape[-1]), lambda i: (0, i, 0, 0, 0, 0))
    return pl.pallas_call(
        _sample_attn_body,
        grid=(nb // bb,),
        in_specs=[pl.BlockSpec((bb, 3, N_GROUPS, *hd, 1), lambda i: (i, 0, 0, 0, 0, 0))]
        + [cache_spec(c) for c in caches],
        out_specs=pl.BlockSpec((bb, *hd, 1), lambda i: (i, 0, 0, 0)),
        out_shape=jax.ShapeDtypeStruct((nb, *hd, 1), F32),
        compiler_params=pltpu.CompilerParams(dimension_semantics=("arbitrary",), vmem_limit_bytes=VMEM_LIMIT),
        name="sample_attention",
    )(qkv_s, *caches)


def _sample_back_body(p_ref, oa_ref, cc_ref, x_ref, mod_ref, cw_ref, wpa_ref, wpb_ref, wo_ref, g_ref,
                      wr_ref, br_ref, x1_ref, h2_ref, comb_ref, sconv_ref):
    hdot = functools.partial(jnp.dot, precision=HIGHEST, preferred_element_type=F32)
    bg = p_ref[:, BG_OFF:BG_OFF + CONV_W]
    u = p_ref[:, CG_OFF:CG_OFF + CONV_W] * p_ref[:, UI_OFF:UI_OFF + CONV_W]
    c_old, c_new = cc_ref[:, 0, :], cc_ref[:, 1, :]
    yconv = cw_ref[0:1, :] * c_old + cw_ref[1:2, :] * c_new + cw_ref[2:3, :] * u
    sconv_ref[:, 0, :] = c_new
    sconv_ref[:, 1, :] = u
    sga = _sigmoid(p_ref[:, GA_OFF:GA_OFF + D_MODEL])
    sgb = _sigmoid(p_ref[:, GB_OFF:GB_OFF + D_MODEL])
    merged = sga * hdot(oa_ref[...], wpa_ref[...]) + sgb * hdot(bg * yconv, wpb_ref[...])
    gate1 = mod_ref[:, 2 * D_MODEL:3 * D_MODEL]
    shift2 = mod_ref[:, 3 * D_MODEL:4 * D_MODEL]
    scale2 = mod_ref[:, 4 * D_MODEL:5 * D_MODEL]
    x1 = x_ref[...] + gate1 * hdot(merged, wo_ref[...])
    x1_ref[...] = x1
    h2 = _rmsnorm(x1, g_ref[...]) * (1.0 + scale2) + shift2
    h2_ref[...] = h2
    comb_ref[...] = _route(hdot(h2, wr_ref[...]) + br_ref[...])


def _sample_back(proj_s, oa_s, cache_conv, x_s, mod_s, conv_w, w_pa, w_pb, w_o, g_ffn, w_r, b_r):
    nb = x_s.shape[0]
    args = (proj_s, oa_s, cache_conv, x_s, mod_s, conv_w, w_pa, w_pb, w_o, g_ffn.reshape(1, -1), w_r, b_r)
    full = lambda shape: pl.BlockSpec(shape, lambda i: (0,) * len(shape))
    out_shape = (jax.ShapeDtypeStruct((nb, D_MODEL), F32), jax.ShapeDtypeStruct((nb, D_MODEL), F32),
                 jax.ShapeDtypeStruct((nb, ROUTER_LANES), F32), jax.ShapeDtypeStruct((nb, 2, CONV_W), F32))
    return pl.pallas_call(
        _sample_back_body,
        grid=(1,),
        in_specs=[full(a.shape) for a in args],
        out_specs=tuple(full(s.shape) for s in out_shape),
        out_shape=out_shape,
        compiler_params=pltpu.CompilerParams(dimension_semantics=("arbitrary",), vmem_limit_bytes=VMEM_LIMIT),
        name="sample_back",
    )(*args)


def kernel(x_prompt, x_sample, cache_kv_w128, cache_kv_w512, cache_kv_w2048, cache_conv, c_prompt, c_sample,
           g_mix, w_ada, b_ada, w_in, conv_w, w_pa, w_pb, w_o, g_ffn, w_router_group, b_router_group,
           w_router_expert, b_router_expert, w_gate_e, w_up_e, w_down_e, g_final):
    batch, seq, _ = x_prompt.shape
    nb = x_sample.shape[0]
    assert x_sample.shape[1] == 1 and g_mix.shape[0] == 1, "one layer, one new sample token per sequence"
    assert seq % ATT_TILE == 0 and cache_kv_w128.shape[2] == 128 and cache_kv_w512.shape[2] == 512 \
        and cache_kv_w2048.shape[2] == 2048
    (g_mix, w_ada, b_ada, w_in, conv_w, w_pa, w_pb, w_o, g_ffn, w_rg, b_rg, w_re, b_re, w_gate_e, w_up_e,
     w_down_e) = (a[0] for a in (g_mix, w_ada, b_ada, w_in, conv_w, w_pa, w_pb, w_o, g_ffn, w_router_group,
                                 b_router_group, w_router_expert, b_router_expert, w_gate_e, w_up_e, w_down_e))

    c_all = jnp.concatenate([c_prompt, jnp.zeros((8 - batch, D_MODEL), F32), c_sample], axis=0)
    mod = _modulation(c_all, w_ada, b_ada)
    mod_p = mod[:batch].reshape(batch, 1, 6 * D_MODEL)
    mod_s = mod[8:]

    w_r = jnp.zeros((D_MODEL, ROUTER_LANES), F32)
    w_r = w_r.at[:, :N_EXPERT_GROUPS].set(w_rg).at[:, EXPERT_LANE0:EXPERT_LANE0 + N_EXPERTS].set(w_re)
    b_r = jnp.full((1, ROUTER_LANES), MASKED, F32)
    b_r = b_r.at[0, :N_EXPERT_GROUPS].set(b_rg).at[0, EXPERT_LANE0:EXPERT_LANE0 + N_EXPERTS].set(b_re)
    wr_hi = w_r.astype(BF16)
    wr_lo = (w_r - wr_hi.astype(F32)).astype(BF16)

    x2d = x_prompt.reshape(batch * seq, D_MODEL)
    (a0, a1, a2, oconv, sga, sgb, kv0, kv1, kv2, pconv) = _front(
        x2d, mod_p, g_mix, w_in.astype(BF16), conv_w, batch, seq)
    o_attn = _attention(a0, a1, a2, _band_bias_table(), batch, seq).reshape(batch * seq, GROUP_W)
    x1, h2, comb = _back(o_attn, oconv, sga, sgb, x2d, mod_p, w_pa.astype(BF16), w_pb.astype(BF16),
                         w_o.astype(BF16), g_ffn, wr_hi, wr_lo, b_r, seq)
    gate2_p = mod_p[:, :, 5 * D_MODEL:]
    y_prompt = _moe(h2, comb, x1, gate2_p, w_gate_e, w_up_e, w_down_e, g_final, MOE_TM)

    x_s = x_sample.reshape(nb, D_MODEL)
    proj_s = _sample_front(x_s, mod_s[:, :2 * D_MODEL], g_mix, w_in)
    qkv_s = proj_s[:, :3 * ATTN_W].reshape(nb, 3, N_GROUPS, HEADS, HEAD_DIM)
    oa_s = _sample_attention(qkv_s[..., None], cache_kv_w128, cache_kv_w512, cache_kv_w2048)
    skv0, skv1, skv2 = (jnp.stack([qkv_s[:, 1, g], qkv_s[:, 2, g]], axis=1) for g in range(N_GROUPS))
    x1_s, h2_s, comb_s, sconv = _sample_back(proj_s, oa_s.reshape(nb, GROUP_W), cache_conv[0], x_s, mod_s, conv_w,
                                             w_pa, w_pb, w_o, g_ffn, w_r, b_r)
    gate2_s = mod_s[:, 5 * D_MODEL:].reshape(1, nb, D_MODEL)
    y_sample = _moe(h2_s, comb_s, x1_s, gate2_s, w_gate_e, w_up_e, w_down_e, g_final, nb)

    def prompt_state(a):
        a = a.reshape(batch, 2, HEADS, HEAD_DIM, a.shape[-1])
        return jnp.transpose(a, (0, 4, 1, 2, 3))[None]

    sample_state = lambda a: a.reshape(1, nb, 1, 2, HEADS, HEAD_DIM)
    return (y_prompt.reshape(batch, seq, D_MODEL), y_sample.reshape(nb, 1, D_MODEL),
            prompt_state(kv0), prompt_state(kv1), prompt_state(kv2),
            pconv[:, 6:8, :].reshape(1, batch, 2, CONV_W),
            sample_state(skv0), sample_state(skv1), sample_state(skv2),
            sconv.reshape(1, nb, 2, CONV_W))
```

```python
import functools
import math

import jax
import jax.numpy as jnp
from jax import lax
from jax.experimental import pallas as pl
from jax.experimental.pallas import tpu as pltpu

F32 = jnp.float32
BF16 = jnp.bfloat16
HIGHEST = lax.Precision.HIGHEST

D_MODEL = 1024
HEAD_DIM = 64
HEADS = 4
N_GROUPS = 3
GROUP_W = HEADS * HEAD_DIM
ATTN_W = N_GROUPS * GROUP_W
CONV_W = 512
DILATIONS = (1, 4, 16)
BAND = 128
N_EXPERTS = 32
EXPERTS_PER_GROUP = 8
N_EXPERT_GROUPS = 4
EXPERT_FF = 256
RMS_EPS = 1e-6
MASKED = -1e30

Q_OFF, K_OFF, V_OFF = 0, ATTN_W, 2 * ATTN_W
BG_OFF = 3 * ATTN_W
CG_OFF = BG_OFF + CONV_W
UI_OFF = CG_OFF + CONV_W
GA_OFF = UI_OFF + CONV_W
GB_OFF = GA_OFF + D_MODEL
IN_COLS = GB_OFF + D_MODEL

LANES = 128
ROUTER_LANES = 128
EXPERT_LANE0 = N_EXPERT_GROUPS

TM = 512
ATT_TILE = 2048
EXPERT_TILE = 512
RT_W1, RT_W2, RT_E1, RT_E2 = 0, 1, 2, 3
F32_ROW = (D_MODEL // LANES, LANES)
DMA_CHUNK = 8
VMEM_LIMIT = 56 * 1024 * 1024


def _sigmoid(x):
    return 1.0 / (1.0 + jnp.exp(-x))


def _rmsnorm(x, g):
    return x * lax.rsqrt(jnp.mean(x * x, axis=-1, keepdims=True) + RMS_EPS) * g


def _alibi_slope(g, h):
    return 2.0 ** (-8.0 * (g * HEADS + h + 1) / (N_GROUPS * HEADS))


def _resident(shape):
    nd = len(shape)
    return pl.BlockSpec(shape, lambda *_: (0,) * nd, pipeline_mode=pl.Buffered(1))


def _mod_body(c_ref, w_ref, b_ref, o_ref):
    c = c_ref[...]
    s = c * _sigmoid(c)
    o_ref[...] = jnp.dot(s, w_ref[...], precision=HIGHEST, preferred_element_type=F32) + b_ref[...]


def _modulation(c_all, w_ada, b_ada):
    rows = c_all.shape[0]
    tn = 1024
    return pl.pallas_call(
        _mod_body,
        grid=(6 * D_MODEL // tn,),
        in_specs=[pl.BlockSpec((rows, D_MODEL), lambda j: (0, 0)),
                  pl.BlockSpec((D_MODEL, tn), lambda j: (0, j)),
                  pl.BlockSpec((1, tn), lambda j: (0, j))],
        out_specs=pl.BlockSpec((rows, tn), lambda j: (0, j)),
        out_shape=jax.ShapeDtypeStruct((rows, 6 * D_MODEL), F32),
        compiler_params=pltpu.CompilerParams(dimension_semantics=("arbitrary",), vmem_limit_bytes=VMEM_LIMIT),
        name="modulation",
    )(c_all, w_ada, b_ada.reshape(1, -1))


def _front_body(tiles_per_seq, x_ref, mod_ref, g_ref, w_ref, cw_ref,
                a0_ref, a1_ref, a2_ref, oconv_ref, sga_ref, sgb_ref,
                kv0_ref, kv1_ref, kv2_ref, pconv_ref, res_ref, uprev_ref):
    t_in_seq = pl.program_id(0) % tiles_per_seq
    x = x_ref[...]
    shift1 = mod_ref[:, 0:D_MODEL]
    scale1 = mod_ref[:, D_MODEL:2 * D_MODEL]
    h = (_rmsnorm(x, g_ref[...]) * (1.0 + scale1) + shift1).astype(BF16)

    def proj(c0, n):
        return jnp.dot(h, w_ref[:, c0:c0 + n], preferred_element_type=F32)

    a_refs = (a0_ref, a1_ref, a2_ref)
    kv_refs = (kv0_ref, kv1_ref, kv2_ref)
    for g in range(N_GROUPS):
        d = DILATIONS[g]
        n = TM // d
        for part, base in enumerate((Q_OFF, K_OFF, V_OFF)):
            r = proj(base + g * GROUP_W, GROUP_W)
            cols = slice(part * GROUP_W, (part + 1) * GROUP_W)
            if part == 0:
                r = r * (HEAD_DIM ** -0.5)
            else:
                kvc = slice((part - 1) * GROUP_W, part * GROUP_W)
                kv_refs[g][kvc, :] = (r[TM - BAND:, :] if g == 0 else r).T
            if g == 0:
                a_refs[g][:, cols] = r.astype(BF16)
            else:
                for c in range(GROUP_W // LANES):
                    res_ref[c] = r[:, c * LANES:(c + 1) * LANES]
                for rr in range(d):
                    for c in range(GROUP_W // LANES):
                        c0 = part * GROUP_W + c * LANES
                        a_refs[g][0, rr, :, c0:c0 + LANES] = res_ref[c, pl.ds(rr, n, stride=d), :].astype(BF16)

    bg = proj(BG_OFF, CONV_W)
    u = proj(CG_OFF, CONV_W) * proj(UI_OFF, CONV_W)
    tail = u[TM - 8:, :]
    pconv_ref[...] = tail
    prev = jnp.where(t_in_seq == 0, 0.0, uprev_ref[...])
    row = lax.broadcasted_iota(jnp.int32, (TM, 1), 0)
    u1 = jnp.where(row == 0, prev[7:8, :], pltpu.roll(u, 1, axis=0))
    u2 = jnp.where(row == 0, prev[6:7, :], jnp.where(row == 1, prev[7:8, :], pltpu.roll(u, 2, axis=0)))
    yconv = cw_ref[0:1, :] * u2 + cw_ref[1:2, :] * u1 + cw_ref[2:3, :] * u
    oconv_ref[...] = (bg * yconv).astype(BF16)
    uprev_ref[...] = tail

    sga_ref[...] = _sigmoid(proj(GA_OFF, D_MODEL)).astype(BF16)
    sgb_ref[...] = _sigmoid(proj(GB_OFF, D_MODEL)).astype(BF16)


def _front(x2d, mod_p, g_mix, w_in_bf16, conv_w, batch, seq):
    n_tok = x2d.shape[0]
    n_tiles = n_tok // TM
    tps = seq // TM
    kv2_blocks = ATT_TILE // TM
    out_shape = (
        jax.ShapeDtypeStruct((n_tok, ATTN_W), BF16),
        jax.ShapeDtypeStruct((n_tiles, 4, TM // 4, ATTN_W), BF16),
        jax.ShapeDtypeStruct((n_tiles, 16, TM // 16, ATTN_W), BF16),
        jax.ShapeDtypeStruct((n_tok, CONV_W), BF16),
        jax.ShapeDtypeStruct((n_tok, D_MODEL), BF16),
        jax.ShapeDtypeStruct((n_tok, D_MODEL), BF16),
        jax.ShapeDtypeStruct((batch, 2 * GROUP_W, 128), F32),
        jax.ShapeDtypeStruct((batch, 2 * GROUP_W, 512), F32),
        jax.ShapeDtypeStruct((batch, 2 * GROUP_W, 2048), F32),
        jax.ShapeDtypeStruct((batch, 8, CONV_W), F32),
    )
    out_specs = (
        pl.BlockSpec((TM, ATTN_W), lambda i: (i, 0)),
        pl.BlockSpec((1, 4, TM // 4, ATTN_W), lambda i: (i, 0, 0, 0)),
        pl.BlockSpec((1, 16, TM // 16, ATTN_W), lambda i: (i, 0, 0, 0)),
        pl.BlockSpec((TM, CONV_W), lambda i: (i, 0)),
        pl.BlockSpec((TM, D_MODEL), lambda i: (i, 0)),
        pl.BlockSpec((TM, D_MODEL), lambda i: (i, 0)),
        pl.BlockSpec((None, 2 * GROUP_W, 128), lambda i: (i // tps, 0, 0)),
        pl.BlockSpec((None, 2 * GROUP_W, TM), lambda i: (i // tps, 0, 0)),
        pl.BlockSpec((None, 2 * GROUP_W, TM),
                     lambda i: (i // tps, 0, jnp.maximum(i % tps - (tps - kv2_blocks), 0))),
        pl.BlockSpec((None, 8, CONV_W), lambda i: (i // tps, 0, 0)),
    )
    in_specs = [
        pl.BlockSpec((TM, D_MODEL), lambda i: (i, 0)),
        pl.BlockSpec((None, 1, 6 * D_MODEL), lambda i: (i // tps, 0, 0)),
        _resident((1, D_MODEL)),
        _resident((D_MODEL, IN_COLS)),
        _resident((3, CONV_W)),
    ]
    return pl.pallas_call(
        functools.partial(_front_body, tps),
        grid=(n_tiles,),
        in_specs=in_specs,
        out_specs=out_specs,
        out_shape=out_shape,
        scratch_shapes=[pltpu.VMEM((GROUP_W // LANES, TM, LANES), F32), pltpu.VMEM((8, CONV_W), F32)],
        compiler_params=pltpu.CompilerParams(dimension_semantics=("arbitrary",), vmem_limit_bytes=VMEM_LIMIT),
        name="prompt_front",
    )(x2d, mod_p, g_mix.reshape(1, -1), w_in_bf16, conv_w)


def _band_bias_table():
    qi = jnp.arange(BAND)[:, None]
    kc = jnp.arange(2 * BAND)[None, :]
    delta = qi - (kc - BAND)
    valid = (delta >= 0) & (delta <= BAND)
    tabs = []
    for first in (False, True):
        ok = valid & (kc >= BAND) if first else valid
        for g in range(N_GROUPS):
            for h in range(HEADS):
                b = -_alibi_slope(g, h) * (delta * DILATIONS[g]).astype(F32)
                tabs.append(jnp.where(ok, b, MASKED))
    return jnp.stack(tabs).astype(F32)


def _head_lane_mask(h, dtype=None):
    lane = lax.broadcasted_iota(jnp.int32, (1, GROUP_W), 1)
    return (lane >= h * HEAD_DIM) & (lane < (h + 1) * HEAD_DIM)


def _attn_unit(q, k, v, bias_ref, bias_base):
    o = jnp.zeros((BAND, GROUP_W), F32)
    lse_b = jnp.zeros((BAND, GROUP_W), F32)
    for h in range(HEADS):
        hm = _head_lane_mask(h)
        qm = jnp.where(hm, q, jnp.zeros_like(q))
        s = lax.dot_general(qm, k, (((1,), (1,)), ((), ())), preferred_element_type=F32)
        s = s + bias_ref[bias_base + h]
        m = jnp.max(s, axis=-1, keepdims=True)
        e = jnp.exp(s - m)
        l = jnp.sum(e, axis=-1, keepdims=True)
        p = (e * (1.0 / l)).astype(BF16)
        vm = jnp.where(hm, v, jnp.zeros_like(v))
        o = o + jnp.dot(p, vm, preferred_element_type=F32)
        lse_b = lse_b + jnp.where(hm, m + jnp.log(l), 0.0)
    return o, lse_b


def _attn_body(a0c_ref, a0p_ref, a1c_ref, a1p_ref, a2c_ref, a2p_ref, bias_ref, o_ref, og_ref, lg_ref):
    first_tile = pl.program_id(1) == 0
    first_off = jnp.where(first_tile, N_GROUPS * HEADS, 0)
    qs, ks, vs = (slice(0, GROUP_W), slice(GROUP_W, 2 * GROUP_W), slice(2 * GROUP_W, 3 * GROUP_W))

    def put(g, rows, o, lse_b):
        for c in range(GROUP_W // LANES):
            og_ref[g, c, rows, :] = o[:, c * LANES:(c + 1) * LANES]
            lg_ref[g, c, rows, :] = lse_b[:, c * LANES:(c + 1) * LANES]

    def g0_store(n0, o, lse_b):
        put(0, pl.ds(n0, BAND), o, lse_b)

    q = a0c_ref[0:BAND, qs]
    k = jnp.concatenate([a0p_ref[:, ks], a0c_ref[0:BAND, ks]], axis=0)
    v = jnp.concatenate([a0p_ref[:, vs], a0c_ref[0:BAND, vs]], axis=0)
    g0_store(0, *_attn_unit(q, k, v, bias_ref, first_off))

    def g0_loop(n, carry):
        n0 = pl.multiple_of(n * BAND, BAND)
        q = a0c_ref[pl.ds(n0, BAND), qs]
        k = a0c_ref[pl.ds(n0 - BAND, 2 * BAND), ks]
        v = a0c_ref[pl.ds(n0 - BAND, 2 * BAND), vs]
        g0_store(n0, *_attn_unit(q, k, v, bias_ref, 0))
        return carry

    lax.fori_loop(1, ATT_TILE // BAND, g0_loop, 0)

    def g1_unit(jj, r, kp, vp, bias_base):
        q = a1c_ref[jj, r, :, qs]
        k = jnp.concatenate([kp, a1c_ref[jj, r, :, ks]], axis=0)
        v = jnp.concatenate([vp, a1c_ref[jj, r, :, vs]], axis=0)
        o, lse_b = _attn_unit(q, k, v, bias_ref, bias_base)
        put(1, pl.ds(jj * (4 * BAND) + r, BAND, stride=4), o, lse_b)

    def g1_first(r, carry):
        g1_unit(0, r, a1p_ref[0, r, :, ks], a1p_ref[0, r, :, vs], first_off + HEADS)
        return carry

    lax.fori_loop(0, 4, g1_first, 0)

    def g1_rest(t, carry):
        jj = 1 + t // 4
        r = t % 4
        g1_unit(jj, r, a1c_ref[jj - 1, r, :, ks], a1c_ref[jj - 1, r, :, vs], HEADS)
        return carry

    lax.fori_loop(0, 12, g1_rest, 0)

    n_sub = a2c_ref.shape[0]

    def g2_rows(ref, r, cols):
        return jnp.concatenate([ref[t, r, :, cols] for t in range(n_sub)], axis=0)

    def g2_loop(r, carry):
        q = g2_rows(a2c_ref, r, qs)
        k = jnp.concatenate([g2_rows(a2p_ref, r, ks), g2_rows(a2c_ref, r, ks)], axis=0)
        v = jnp.concatenate([g2_rows(a2p_ref, r, vs), g2_rows(a2c_ref, r, vs)], axis=0)
        o, lse_b = _attn_unit(q, k, v, bias_ref, first_off + 2 * HEADS)
        put(2, pl.ds(r, BAND, stride=16), o, lse_b)
        return carry

    lax.fori_loop(0, 16, g2_loop, 0)

    def mix(c, carry):
        rows = pl.ds(pl.multiple_of(c * BAND, BAND), BAND)
        for half in range(GROUP_W // LANES):
            l0, l1, l2 = lg_ref[0, half, rows, :], lg_ref[1, half, rows, :], lg_ref[2, half, rows, :]
            mx = jnp.maximum(jnp.maximum(l0, l1), l2)
            w0, w1, w2 = jnp.exp(l0 - mx), jnp.exp(l1 - mx), jnp.exp(l2 - mx)
            num = w0 * og_ref[0, half, rows, :] + w1 * og_ref[1, half, rows, :] + w2 * og_ref[2, half, rows, :]
            o_ref[rows, half * LANES:(half + 1) * LANES] = (num / (w0 + w1 + w2)).astype(o_ref.dtype)
        return carry

    lax.fori_loop(0, ATT_TILE // BAND, mix, 0)


def _attention(a0, a1, a2, bias, batch, seq):
    steps = seq // ATT_TILE
    sub = ATT_TILE // TM
    a0 = a0.reshape(batch, seq, ATTN_W)
    a1 = a1.reshape(batch, seq // TM, 4, TM // 4, ATTN_W)
    a2 = a2.reshape(batch, seq // TM, 16, TM // 16, ATTN_W)
    in_specs = [
        pl.BlockSpec((None, ATT_TILE, ATTN_W), lambda b, j: (b, j, 0)),
        pl.BlockSpec((None, BAND, ATTN_W), lambda b, j: (b, jnp.maximum(j * (ATT_TILE // BAND) - 1, 0), 0)),
        pl.BlockSpec((None, sub, 4, TM // 4, ATTN_W), lambda b, j: (b, j, 0, 0, 0)),
        pl.BlockSpec((None, 1, 4, TM // 4, ATTN_W), lambda b, j: (b, jnp.maximum(j * sub - 1, 0), 0, 0, 0)),
        pl.BlockSpec((None, sub, 16, TM // 16, ATTN_W), lambda b, j: (b, j, 0, 0, 0)),
        pl.BlockSpec((None, sub, 16, TM // 16, ATTN_W), lambda b, j: (b, jnp.maximum(j - 1, 0), 0, 0, 0)),
        _resident(bias.shape),
    ]
    return pl.pallas_call(
        _attn_body,
        grid=(batch, steps),
        in_specs=in_specs,
        out_specs=pl.BlockSpec((None, ATT_TILE, GROUP_W), lambda b, j: (b, j, 0)),
        out_shape=jax.ShapeDtypeStruct((batch, seq, GROUP_W), BF16),
        scratch_shapes=[pltpu.VMEM((N_GROUPS, GROUP_W // LANES, ATT_TILE, LANES), F32)] * 2,
        compiler_params=pltpu.CompilerParams(dimension_semantics=("arbitrary", "arbitrary"),
                                             vmem_limit_bytes=VMEM_LIMIT),
        name="prompt_attention",
    )(a0, a0, a1, a1, a2, a2, bias)


def _route(logits):
    lane = lax.broadcasted_iota(jnp.int32, logits.shape, 1)
    big = jnp.int32(1 << 20)
    gmask = lane < N_EXPERT_GROUPS
    lg = jnp.where(gmask, logits, MASKED)
    gmax = jnp.max(lg, axis=-1, keepdims=True)
    gidx = jnp.min(jnp.where(gmask & (lg == gmax), lane, big), axis=-1, keepdims=True)
    p_top = 1.0 / jnp.sum(jnp.where(gmask, jnp.exp(lg - gmax), 0.0), axis=-1, keepdims=True)
    lo = EXPERT_LANE0 + EXPERTS_PER_GROUP * gidx
    emask = (lane >= lo) & (lane < lo + EXPERTS_PER_GROUP)
    le = jnp.where(emask, logits, MASKED)
    v1 = jnp.max(le, axis=-1, keepdims=True)
    i1 = jnp.min(jnp.where(emask & (le == v1), lane, big), axis=-1, keepdims=True)
    emask2 = emask & (lane != i1)
    le2 = jnp.where(emask2, logits, MASKED)
    v2 = jnp.max(le2, axis=-1, keepdims=True)
    i2 = jnp.min(jnp.where(emask2 & (le2 == v2), lane, big), axis=-1, keepdims=True)
    e2 = jnp.exp(v2 - v1)
    den = 1.0 + e2
    w1 = (1.0 / den) * p_top
    w2 = (e2 / den) * p_top
    id1 = (i1 - EXPERT_LANE0).astype(F32)
    id2 = (i2 - EXPERT_LANE0).astype(F32)
    return jnp.where(lane == RT_W1, w1, jnp.where(lane == RT_W2, w2,
                     jnp.where(lane == RT_E1, id1, jnp.where(lane == RT_E2, id2, 0.0))))


def _back_body(oa_ref, oc_ref, sga_ref, sgb_ref, x_ref, mod_ref, wpa_ref, wpb_ref, wo_ref, g_ref,
               wrh_ref, wrl_ref, br_ref, x1_ref, h2_ref, comb_ref):
    pa = jnp.dot(oa_ref[...], wpa_ref[...], preferred_element_type=F32)
    pb = jnp.dot(oc_ref[...], wpb_ref[...], preferred_element_type=F32)
    merged = sga_ref[...].astype(F32) * pa + sgb_ref[...].astype(F32) * pb
    gate1 = mod_ref[:, 2 * D_MODEL:3 * D_MODEL]
    shift2 = mod_ref[:, 3 * D_MODEL:4 * D_MODEL]
    scale2 = mod_ref[:, 4 * D_MODEL:5 * D_MODEL]
    x1 = x_ref[...] + gate1 * jnp.dot(merged.astype(BF16), wo_ref[...], preferred_element_type=F32)
    x1_ref[...] = x1
    h2 = _rmsnorm(x1, g_ref[...]) * (1.0 + scale2) + shift2
    hi = h2.astype(BF16)
    h2_ref[...] = hi
    lo = (h2 - hi.astype(F32)).astype(BF16)
    logits = (jnp.dot(hi, wrh_ref[...], preferred_element_type=F32)
              + jnp.dot(lo, wrh_ref[...], preferred_element_type=F32)
              + jnp.dot(hi, wrl_ref[...], preferred_element_type=F32)) + br_ref[...]
    comb_ref[...] = _route(logits)


def _back(o_attn, oconv, sga, sgb, x2d, mod_p, wpa, wpb, wo, g_ffn, wr_hi, wr_lo, b_r, seq):
    n_tok = x2d.shape[0]
    n_all = n_tok + TM
    tps = seq // TM
    row = lambda w: pl.BlockSpec((TM, w), lambda i: (i, 0))
    return pl.pallas_call(
        _back_body,
        grid=(n_tok // TM,),
        in_specs=[row(GROUP_W), row(CONV_W), row(D_MODEL), row(D_MODEL), row(D_MODEL),
                  pl.BlockSpec((None, 1, 6 * D_MODEL), lambda i: (i // tps, 0, 0)),
                  _resident(wpa.shape), _resident(wpb.shape), _resident(wo.shape), _resident((1, D_MODEL)),
                  _resident(wr_hi.shape), _resident(wr_lo.shape), _resident(b_r.shape)],
        out_specs=(row(D_MODEL), row(D_MODEL), row(ROUTER_LANES)),
        out_shape=(jax.ShapeDtypeStruct((n_all, D_MODEL), F32),
                   jax.ShapeDtypeStruct((n_all, D_MODEL), BF16),
                   jax.ShapeDtypeStruct((n_all, ROUTER_LANES), F32)),
        compiler_params=pltpu.CompilerParams(dimension_semantics=("arbitrary",), vmem_limit_bytes=VMEM_LIMIT),
        name="prompt_back",
    )(o_attn, oconv, sga, sgb, x2d, mod_p, wpa, wpb, wo, g_ffn.reshape(1, -1), wr_hi, wr_lo, b_r)


def _plan_body(rt_ref, rank_ref, cnt_ref, base_ref):
    @pl.when(pl.program_id(0) == 0)
    def _():
        base_ref[...] = jnp.zeros_like(base_ref)

    rt = rt_ref[...]
    lane = lax.broadcasted_iota(jnp.int32, (TM, LANES), 1)
    lane_f = lane.astype(F32)
    oh1 = jnp.where(lane_f == rt[:, RT_E1:RT_E1 + 1], 1.0, 0.0)
    oh2 = jnp.where(lane_f == rt[:, RT_E2:RT_E2 + 1], 1.0, 0.0)
    before = jnp.where(lax.broadcasted_iota(jnp.int32, (TM, TM), 1) < lax.broadcasted_iota(jnp.int32, (TM, TM), 0),
                       1.0, 0.0).astype(BF16)
    p1 = jnp.dot(before, oh1.astype(BF16), preferred_element_type=F32)
    p2 = jnp.dot(before, oh2.astype(BF16), preferred_element_type=F32)
    c1 = jnp.sum(oh1, axis=0, keepdims=True)
    c2 = jnp.sum(oh2, axis=0, keepdims=True)
    base = base_ref[...]
    rank1 = jnp.sum(oh1 * (base + p1), axis=-1, keepdims=True)
    rank2 = jnp.sum(oh2 * (base + c1 + p2), axis=-1, keepdims=True)
    rank_ref[...] = jnp.where(lane == 0, rank1, jnp.where(lane == 1, rank2, 0.0))
    total = base + c1 + c2
    base_ref[...] = total
    cnt_ref[...] = total


def _plan(rt):
    n_all = rt.shape[0]
    return pl.pallas_call(
        _plan_body,
        grid=(n_all // TM,),
        in_specs=[pl.BlockSpec((TM, ROUTER_LANES), lambda i: (i, 0))],
        out_specs=(pl.BlockSpec((TM, LANES), lambda i: (i, 0)), pl.BlockSpec((1, LANES), lambda i: (0, 0))),
        out_shape=(jax.ShapeDtypeStruct((n_all, LANES), F32), jax.ShapeDtypeStruct((1, LANES), F32)),
        scratch_shapes=[pltpu.VMEM((1, LANES), F32)],
        compiler_params=pltpu.CompilerParams(dimension_semantics=("arbitrary",), vmem_limit_bytes=VMEM_LIMIT),
        name="moe_plan",
    )(rt)


def _valid_chunks(tile, n_prompt_tiles, n_sample):
    return jnp.where(tile < n_prompt_tiles, TM // DMA_CHUNK, n_sample // DMA_CHUNK)


def _dispatch_body(n_prompt_tiles, n_sample, dest_ref, tail_ref, h_ref, xs_ref, stage_ref, zero_ref, sem):
    i = pl.program_id(0)

    def tail_copy(e):
        return pltpu.make_async_copy(zero_ref, xs_ref.at[pl.ds(tail_ref[e], EXPERT_TILE)], sem)

    @pl.when(i == 0)
    def _():
        zero_ref[...] = jnp.zeros_like(zero_ref)
        for e in range(N_EXPERTS):
            @pl.when(tail_ref[e] >= 0)
            def _():
                tail_copy(e).start()
        for e in range(N_EXPERTS):
            @pl.when(tail_ref[e] >= 0)
            def _():
                tail_copy(e).wait()

    for j in range(F32_ROW[0]):
        stage_ref[:, j, :] = h_ref[:, j * LANES:(j + 1) * LANES].astype(F32)
    n_chunks = _valid_chunks(i, n_prompt_tiles, n_sample)

    def issue(c, carry):
        for j in range(DMA_CHUNK):
            r = c * DMA_CHUNK + j
            for k in range(2):
                d = dest_ref[(i * TM + r) * 2 + k]
                pltpu.make_async_copy(stage_ref.at[pl.ds(r, 1)], xs_ref.at[pl.ds(d, 1)], sem).start()
        return carry

    lax.fori_loop(0, n_chunks, issue, 0)

    def drain(c, carry):
        pltpu.make_async_copy(stage_ref.at[pl.ds(0, 2 * DMA_CHUNK)], xs_ref.at[pl.ds(0, 2 * DMA_CHUNK)], sem).wait()
        return carry

    lax.fori_loop(0, n_chunks, drain, 0)


def _dispatch(dest, tail_start, h2, n_rows, n_prompt_tiles, n_sample):
    n_all = h2.shape[0]
    return pl.pallas_call(
        functools.partial(_dispatch_body, n_prompt_tiles, n_sample),
        grid_spec=pltpu.PrefetchScalarGridSpec(
            num_scalar_prefetch=2,
            grid=(n_all // TM,),
            in_specs=[pl.BlockSpec((TM, D_MODEL), lambda i, *_: (i, 0))],
            out_specs=pl.BlockSpec(memory_space=pl.ANY),
            scratch_shapes=[pltpu.VMEM((TM,) + F32_ROW, F32), pltpu.VMEM((EXPERT_TILE,) + F32_ROW, F32),
                            pltpu.SemaphoreType.DMA],
        ),
        out_shape=jax.ShapeDtypeStruct((n_rows,) + F32_ROW, F32),
        compiler_params=pltpu.CompilerParams(dimension_semantics=("arbitrary",), vmem_limit_bytes=VMEM_LIMIT,
                                             disable_bounds_checks=True),
        name="moe_dispatch",
    )(dest, tail_start, h2)


def _experts_body(te_ref, nt_ref, xs_ref, wg_ref, wu_ref, wd_ref, o_ref):
    @pl.when(pl.program_id(0) < nt_ref[0])
    def _():
        x = jnp.concatenate([xs_ref[:, j, :].astype(BF16) for j in range(F32_ROW[0])], axis=1)
        a = jnp.dot(x, wg_ref[0].astype(BF16), preferred_element_type=F32)
        z = (a * _sigmoid(a)) * jnp.dot(x, wu_ref[0].astype(BF16), preferred_element_type=F32)
        o = jnp.dot(z.astype(BF16), wd_ref[0].astype(BF16), preferred_element_type=F32)
        for j in range(F32_ROW[0]):
            o_ref[:, j, :] = o[:, j * LANES:(j + 1) * LANES]


def _experts(tile_expert, n_tiles, xs, w_gate_e, w_up_e, w_down_e):
    max_tiles = tile_expert.shape[0]
    rows = lambda s, te, nt: (jnp.minimum(s, nt[0] - 1), 0, 0)
    weight = lambda shape: pl.BlockSpec((1,) + shape, lambda s, te, nt: (te[jnp.minimum(s, nt[0] - 1)], 0, 0))
    return pl.pallas_call(
        _experts_body,
        grid_spec=pltpu.PrefetchScalarGridSpec(
            num_scalar_prefetch=2,
            grid=(max_tiles,),
            in_specs=[pl.BlockSpec((EXPERT_TILE,) + F32_ROW, rows),
                      weight((D_MODEL, EXPERT_FF)), weight((D_MODEL, EXPERT_FF)), weight((EXPERT_FF, D_MODEL))],
            out_specs=pl.BlockSpec((EXPERT_TILE,) + F32_ROW, rows),
        ),
        out_shape=jax.ShapeDtypeStruct((xs.shape[0],) + F32_ROW, F32),
        compiler_params=pltpu.CompilerParams(dimension_semantics=("arbitrary",), vmem_limit_bytes=VMEM_LIMIT),
        name="moe_experts",
    )(tile_expert, n_tiles, xs, w_gate_e, w_up_e, w_down_e)


def _combine_body(n_prompt_tiles, n_sample, dest_ref, x1_ref, rt_ref, gp_ref, gs_ref, gf_ref, eo_ref,
                  yp_ref, ys_ref, rows_ref, sem):
    i = pl.program_id(0)
    n_chunks = _valid_chunks(i, n_prompt_tiles, n_sample)

    def issue(c, carry):
        for j in range(DMA_CHUNK):
            r = c * DMA_CHUNK + j
            for k in range(2):
                d = dest_ref[(i * TM + r) * 2 + k]
                pltpu.make_async_copy(eo_ref.at[pl.ds(d, 1)], rows_ref.at[k, pl.ds(r, 1)], sem).start()
        return carry

    lax.fori_loop(0, n_chunks, issue, 0)

    def drain(c, carry):
        pltpu.make_async_copy(eo_ref.at[pl.ds(0, 2 * DMA_CHUNK)], rows_ref.at[0, pl.ds(0, 2 * DMA_CHUNK)], sem).wait()
        return carry

    lax.fori_loop(0, n_chunks, drain, 0)

    rt = rt_ref[...]
    w1, w2 = rt[:, RT_W1:RT_W1 + 1], rt[:, RT_W2:RT_W2 + 1]
    moe = jnp.concatenate([w1 * rows_ref[0, :, j, :] + w2 * rows_ref[1, :, j, :] for j in range(F32_ROW[0])], axis=1)
    is_prompt = i < n_prompt_tiles
    gate2 = jnp.where(is_prompt, gp_ref[...], gs_ref[...])
    y = _rmsnorm(x1_ref[...] + gate2 * moe, gf_ref[...])

    @pl.when(is_prompt)
    def _():
        yp_ref[...] = y

    @pl.when(jnp.logical_not(is_prompt))
    def _():
        ys_ref[...] = y[:n_sample, :]


def _combine(dest, x1, rt, gate2_p, gate2_s, g_final, eo, n_prompt_tiles, n_sample, tiles_per_seq):
    n_all = x1.shape[0]
    last_p = n_prompt_tiles - 1
    return pl.pallas_call(
        functools.partial(_combine_body, n_prompt_tiles, n_sample),
        grid_spec=pltpu.PrefetchScalarGridSpec(
            num_scalar_prefetch=1,
            grid=(n_all // TM,),
            in_specs=[pl.BlockSpec((TM, D_MODEL), lambda i, *_: (i, 0)),
                      pl.BlockSpec((TM, ROUTER_LANES), lambda i, *_: (i, 0)),
                      pl.BlockSpec((None, 1, D_MODEL), lambda i, *_: (jnp.minimum(i, last_p) // tiles_per_seq, 0, 0)),
                      pl.BlockSpec((TM, D_MODEL), lambda i, *_: (0, 0)),
                      pl.BlockSpec((1, D_MODEL), lambda i, *_: (0, 0)),
                      pl.BlockSpec(memory_space=pl.ANY)],
            out_specs=(pl.BlockSpec((TM, D_MODEL), lambda i, *_: (jnp.minimum(i, last_p), 0)),
                       pl.BlockSpec((n_sample, D_MODEL), lambda i, *_: (0, 0))),
            scratch_shapes=[pltpu.VMEM((2, TM) + F32_ROW, F32), pltpu.SemaphoreType.DMA],
        ),
        out_shape=(jax.ShapeDtypeStruct((n_prompt_tiles * TM, D_MODEL), F32),
                   jax.ShapeDtypeStruct((n_sample, D_MODEL), F32)),
        compiler_params=pltpu.CompilerParams(dimension_semantics=("arbitrary",), vmem_limit_bytes=VMEM_LIMIT,
                                             disable_bounds_checks=True),
        name="moe_combine",
    )(dest, x1, rt, gate2_p, gate2_s, g_final.reshape(1, -1), eo)


def _routing_tables(rt, rank, counts, max_tiles):
    cnt = counts[0, :N_EXPERTS].astype(jnp.int32)
    padded = (cnt + EXPERT_TILE - 1) // EXPERT_TILE * EXPERT_TILE
    ends = jnp.cumsum(padded)
    starts = ends - padded
    n_tiles = (ends[-1] // EXPERT_TILE).reshape(1)
    tile_row0 = jnp.arange(max_tiles, dtype=jnp.int32) * EXPERT_TILE
    tile_expert = jnp.minimum(jnp.sum(ends[None, :] <= tile_row0[:, None], axis=1), N_EXPERTS - 1).astype(jnp.int32)
    tail_start = jnp.where(cnt > 0, ends - EXPERT_TILE, -1).astype(jnp.int32)
    eid = rt[:, RT_E1:RT_E2 + 1].astype(jnp.int32)
    onehot = eid[..., None] == jnp.arange(N_EXPERTS, dtype=jnp.int32)
    dest = rank[:, 0:2].astype(jnp.int32) + jnp.sum(jnp.where(onehot, starts, 0), axis=-1)
    dest = jnp.where(eid >= 0, dest, 0).reshape(-1)
    return dest, tail_start, tile_expert, n_tiles


def _sample_front_body(x_ref, mod_ref, g_ref, w_ref, o_ref):
    shift1 = mod_ref[:, 0:D_MODEL]
    scale1 = mod_ref[:, D_MODEL:2 * D_MODEL]
    h = _rmsnorm(x_ref[...], g_ref[...]) * (1.0 + scale1) + shift1
    o_ref[...] = jnp.dot(h, w_ref[...], precision=HIGHEST, preferred_element_type=F32)


def _sample_front(x_s, mod_s, g_mix, w_in):
    nb = x_s.shape[0]
    tn = 256
    return pl.pallas_call(
        _sample_front_body,
        grid=(IN_COLS // tn,),
        in_specs=[pl.BlockSpec((nb, D_MODEL), lambda j: (0, 0)),
                  pl.BlockSpec((nb, 2 * D_MODEL), lambda j: (0, 0)),
                  pl.BlockSpec((1, D_MODEL), lambda j: (0, 0)),
                  pl.BlockSpec((D_MODEL, tn), lambda j: (0, j))],
        out_specs=pl.BlockSpec((nb, tn), lambda j: (0, j)),
        out_shape=jax.ShapeDtypeStruct((nb, IN_COLS), F32),
        compiler_params=pltpu.CompilerParams(dimension_semantics=("arbitrary",), vmem_limit_bytes=VMEM_LIMIT),
        name="sample_front",
    )(x_s, mod_s, g_mix.reshape(1, -1), w_in)


def _sample_attn_body(qkv_ref, c0_ref, c1_ref, c2_ref, oa_ref):
    head = lax.broadcasted_iota(jnp.int32, (HEADS, 1, 1), 0)
    for b in range(qkv_ref.shape[0]):
        outs, lses = [], []
        for g, c_ref in enumerate((c0_ref, c1_ref, c2_ref)):
            window = c_ref.shape[-1]
            q = qkv_ref[b, 0, g] * (HEAD_DIM ** -0.5)
            k_new = qkv_ref[b, 1, g]
            v_new = qkv_ref[b, 2, g]
            s = jnp.sum(c_ref[b, 0] * q, axis=1, keepdims=True)
            pos = lax.broadcasted_iota(jnp.int32, (1, 1, window), 2)
            slope = jnp.full((HEADS, 1, 1), _alibi_slope(g, HEADS - 1), F32)
            for h in range(HEADS - 1):
                slope = jnp.where(head == h, _alibi_slope(g, h), slope)
            on_band = (pos & (DILATIONS[g] - 1)) == 0
            s = jnp.where(on_band, s - slope * (window - pos).astype(F32), MASKED)
            s_self = jnp.sum(q * k_new, axis=1, keepdims=True)
            m = jnp.maximum(jnp.max(s, axis=2, keepdims=True), s_self)
            e = jnp.exp(s - m)
            e_self = jnp.exp(s_self - m)
            l = jnp.sum(e, axis=2, keepdims=True) + e_self
            o = jnp.sum(c_ref[b, 1] * e, axis=2, keepdims=True) + e_self * v_new
            outs.append(o / l)
            lses.append(m + jnp.log(l))
        mx = jnp.maximum(jnp.maximum(lses[0], lses[1]), lses[2])
        w = [jnp.exp(x - mx) for x in lses]
        oa_ref[b] = (w[0] * outs[0] + w[1] * outs[1] + w[2] * outs[2]) / (w[0] + w[1] + w[2])


def _sample_attention(qkv_s, cache0, cache1, cache2):
    nb = qkv_s.shape[0]
    bb = 2
    hd = (HEADS, HEAD_DIM)
    caches = [jnp.transpose(c, (0, 1, 3, 4, 5, 2)) for c in (cache0, cache1, cache2)]
    cache_spec = lambda c: pl.BlockSpec((None, bb, 2, *hd, c.shape[-1]), lambda i: (0, i, 0, 0, 0, 0))
    return pl.pallas_call(
        _sample_attn_body,
        grid=(nb // bb,),
        in_specs=[pl.BlockSpec((bb, 3, N_GROUPS, *hd, 1), lambda i: (i, 0, 0, 0, 0, 0))]
        + [cache_spec(c) for c in caches],
        out_specs=pl.BlockSpec((bb, *hd, 1), lambda i: (i, 0, 0, 0)),
        out_shape=jax.ShapeDtypeStruct((nb, *hd, 1), F32),
        compiler_params=pltpu.CompilerParams(dimension_semantics=("arbitrary",), vmem_limit_bytes=VMEM_LIMIT),
        name="sample_attention",
    )(qkv_s, *caches)


def _sample_back_body(p_ref, oa_ref, cc_ref, x_ref, mod_ref, cw_ref, wpa_ref, wpb_ref, wo_ref, g_ref,
                      wr_ref, br_ref, x1_any, h2_any, rt_any, x1_ref, h2_ref, rt_ref, sconv_ref):
    del x1_any, h2_any, rt_any
    nb = x_ref.shape[0]
    hdot = functools.partial(jnp.dot, precision=HIGHEST, preferred_element_type=F32)
    bg = p_ref[:, BG_OFF:BG_OFF + CONV_W]
    u = p_ref[:, CG_OFF:CG_OFF + CONV_W] * p_ref[:, UI_OFF:UI_OFF + CONV_W]
    c_old, c_new = cc_ref[:, 0, :], cc_ref[:, 1, :]
    yconv = cw_ref[0:1, :] * c_old + cw_ref[1:2, :] * c_new + cw_ref[2:3, :] * u
    sconv_ref[:, 0, :] = c_new
    sconv_ref[:, 1, :] = u
    sga = _sigmoid(p_ref[:, GA_OFF:GA_OFF + D_MODEL])
    sgb = _sigmoid(p_ref[:, GB_OFF:GB_OFF + D_MODEL])
    merged = sga * hdot(oa_ref[...], wpa_ref[...]) + sgb * hdot(bg * yconv, wpb_ref[...])
    gate1 = mod_ref[:, 2 * D_MODEL:3 * D_MODEL]
    shift2 = mod_ref[:, 3 * D_MODEL:4 * D_MODEL]
    scale2 = mod_ref[:, 4 * D_MODEL:5 * D_MODEL]
    x1 = x_ref[...] + gate1 * hdot(merged, wo_ref[...])
    h2 = _rmsnorm(x1, g_ref[...]) * (1.0 + scale2) + shift2
    rt = _route(hdot(h2, wr_ref[...]) + br_ref[...])
    lane = lax.broadcasted_iota(jnp.int32, (TM, ROUTER_LANES), 1)
    x1_ref[...] = jnp.zeros_like(x1_ref)
    h2_ref[...] = jnp.zeros_like(h2_ref)
    rt_ref[...] = jnp.where((lane == RT_E1) | (lane == RT_E2), -1.0, 0.0)
    x1_ref[0:nb, :] = x1
    h2_ref[0:nb, :] = h2.astype(BF16)
    rt_ref[0:nb, :] = rt


def _sample_back(proj_s, oa_s, cache_conv, x_s, mod_s, conv_w, w_pa, w_pb, w_o, g_ffn, w_r, b_r, x1, h2, rt):
    nb = x_s.shape[0]
    last = x1.shape[0] // TM - 1
    args = (proj_s, oa_s, cache_conv, x_s, mod_s, conv_w, w_pa, w_pb, w_o, g_ffn.reshape(1, -1), w_r, b_r)
    full = lambda shape: pl.BlockSpec(shape, lambda i: (0,) * len(shape))
    last_tile = lambda a: pl.BlockSpec((TM, a.shape[1]), lambda i: (last, 0))
    shape_of = lambda a: jax.ShapeDtypeStruct(a.shape, a.dtype)
    return pl.pallas_call(
        _sample_back_body,
        grid=(1,),
        in_specs=[full(a.shape) for a in args] + [pl.BlockSpec(memory_space=pl.ANY)] * 3,
        out_specs=(last_tile(x1), last_tile(h2), last_tile(rt), full((nb, 2, CONV_W))),
        out_shape=(shape_of(x1), shape_of(h2), shape_of(rt), jax.ShapeDtypeStruct((nb, 2, CONV_W), F32)),
        input_output_aliases={len(args): 0, len(args) + 1: 1, len(args) + 2: 2},
        compiler_params=pltpu.CompilerParams(dimension_semantics=("arbitrary",), vmem_limit_bytes=VMEM_LIMIT),
        name="sample_back",
    )(*args, x1, h2, rt)


def kernel(x_prompt, x_sample, cache_kv_w128, cache_kv_w512, cache_kv_w2048, cache_conv, c_prompt, c_sample,
           g_mix, w_ada, b_ada, w_in, conv_w, w_pa, w_pb, w_o, g_ffn, w_router_group, b_router_group,
           w_router_expert, b_router_expert, w_gate_e, w_up_e, w_down_e, g_final):
    batch, seq, _ = x_prompt.shape
    nb = x_sample.shape[0]
    assert x_sample.shape[1] == 1 and g_mix.shape[0] == 1, "one layer, one new sample token per sequence"
    assert seq % ATT_TILE == 0 and cache_kv_w128.shape[2] == 128 and cache_kv_w512.shape[2] == 512 \
        and cache_kv_w2048.shape[2] == 2048
    (g_mix, w_ada, b_ada, w_in, conv_w, w_pa, w_pb, w_o, g_ffn, w_rg, b_rg, w_re, b_re, w_gate_e, w_up_e,
     w_down_e) = (a[0] for a in (g_mix, w_ada, b_ada, w_in, conv_w, w_pa, w_pb, w_o, g_ffn, w_router_group,
                                 b_router_group, w_router_expert, b_router_expert, w_gate_e, w_up_e, w_down_e))

    c_all = jnp.concatenate([c_prompt, jnp.zeros((8 - batch, D_MODEL), F32), c_sample], axis=0)
    mod = _modulation(c_all, w_ada, b_ada)
    mod_p = mod[:batch].reshape(batch, 1, 6 * D_MODEL)
    mod_s = mod[8:]

    w_r = jnp.zeros((D_MODEL, ROUTER_LANES), F32)
    w_r = w_r.at[:, :N_EXPERT_GROUPS].set(w_rg).at[:, EXPERT_LANE0:EXPERT_LANE0 + N_EXPERTS].set(w_re)
    b_r = jnp.full((1, ROUTER_LANES), MASKED, F32)
    b_r = b_r.at[0, :N_EXPERT_GROUPS].set(b_rg).at[0, EXPERT_LANE0:EXPERT_LANE0 + N_EXPERTS].set(b_re)
    wr_hi = w_r.astype(BF16)
    wr_lo = (w_r - wr_hi.astype(F32)).astype(BF16)

    x2d = x_prompt.reshape(batch * seq, D_MODEL)
    (a0, a1, a2, oconv, sga, sgb, kv0, kv1, kv2, pconv) = _front(
        x2d, mod_p, g_mix, w_in.astype(BF16), conv_w, batch, seq)
    o_attn = _attention(a0, a1, a2, _band_bias_table(), batch, seq).reshape(batch * seq, GROUP_W)
    x1, h2, rt = _back(o_attn, oconv, sga, sgb, x2d, mod_p, w_pa.astype(BF16), w_pb.astype(BF16),
                       w_o.astype(BF16), g_ffn, wr_hi, wr_lo, b_r, seq)

    x_s = x_sample.reshape(nb, D_MODEL)
    proj_s = _sample_front(x_s, mod_s[:, :2 * D_MODEL], g_mix, w_in)
    qkv_s = proj_s[:, :3 * ATTN_W].reshape(nb, 3, N_GROUPS, HEADS, HEAD_DIM)
    oa_s = _sample_attention(qkv_s[..., None], cache_kv_w128, cache_kv_w512, cache_kv_w2048)
    skv0, skv1, skv2 = (jnp.stack([qkv_s[:, 1, g], qkv_s[:, 2, g]], axis=1) for g in range(N_GROUPS))
    x1, h2, rt, sconv = _sample_back(proj_s, oa_s.reshape(nb, GROUP_W), cache_conv[0], x_s, mod_s, conv_w,
                                     w_pa, w_pb, w_o, g_ffn, w_r, b_r, x1, h2, rt)

    n_prompt_tiles = batch * seq // TM
    max_tiles = -(-(2 * (batch * seq + nb) + N_EXPERTS * (EXPERT_TILE - 1)) // EXPERT_TILE)
    rank, counts = _plan(rt)
    dest, tail_start, tile_expert, n_tiles = _routing_tables(rt, rank, counts, max_tiles)
    xs = _dispatch(dest, tail_start, h2, max_tiles * EXPERT_TILE, n_prompt_tiles, nb)
    eo = _experts(tile_expert, n_tiles, xs, w_gate_e, w_up_e, w_down_e)
    gate2_s = jnp.zeros((TM, D_MODEL), F32).at[:nb].set(mod_s[:, 5 * D_MODEL:])
    y_prompt, y_sample = _combine(dest, x1, rt, mod_p[:, :, 5 * D_MODEL:], gate2_s, g_final, eo,
                                  n_prompt_tiles, nb, seq // TM)

    def prompt_state(a):
        a = a.reshape(batch, 2, HEADS, HEAD_DIM, a.shape[-1])
        return jnp.transpose(a, (0, 4, 1, 2, 3))[None]

    sample_state = lambda a: a.reshape(1, nb, 1, 2, HEADS, HEAD_DIM)
    return (y_prompt.reshape(batch, seq, D_MODEL), y_sample.reshape(nb, 1, D_MODEL),
            prompt_state(kv0), prompt_state(kv1), prompt_state(kv2),
            pconv[:, 6:8, :].reshape(1, batch, 2, CONV_W),
            sample_state(skv0), sample_state(skv1), sample_state(skv2),
            sconv.reshape(1, nb, 2, CONV_W))
```

```python
import functools
import math

import jax
import jax.numpy as jnp
from jax import lax
from jax.experimental import pallas as pl
from jax.experimental.pallas import tpu as pltpu

F32 = jnp.float32
BF16 = jnp.bfloat16
HIGHEST = lax.Precision.HIGHEST

D_MODEL = 1024
HEAD_DIM = 64
HEADS = 4
N_GROUPS = 3
GROUP_W = HEADS * HEAD_DIM
ATTN_W = N_GROUPS * GROUP_W
CONV_W = 512
DILATIONS = (1, 4, 16)
BAND = 128
N_EXPERTS = 32
EXPERTS_PER_GROUP = 8
N_EXPERT_GROUPS = 4
EXPERT_FF = 256
RMS_EPS = 1e-6
MASKED = -1e30

Q_OFF, K_OFF, V_OFF = 0, ATTN_W, 2 * ATTN_W
BG_OFF = 3 * ATTN_W
CG_OFF = BG_OFF + CONV_W
UI_OFF = CG_OFF + CONV_W
GA_OFF = UI_OFF + CONV_W
GB_OFF = GA_OFF + D_MODEL
IN_COLS = GB_OFF + D_MODEL

LANES = 128
ROUTER_LANES = 128
EXPERT_LANE0 = N_EXPERT_GROUPS

TM = 512
ATT_TILE = 2048
EXPERT_TILE = 512
RT_W1, RT_W2, RT_E1, RT_E2 = 0, 1, 2, 3
ROW_SUB = D_MODEL // LANES
DMA_CHUNK = 8
VMEM_LIMIT = 56 * 1024 * 1024


def _sigmoid(x):
    return 1.0 / (1.0 + jnp.exp(-x))


def _rmsnorm(x, g):
    return x * lax.rsqrt(jnp.mean(x * x, axis=-1, keepdims=True) + RMS_EPS) * g


def _alibi_slope(g, h):
    return 2.0 ** (-8.0 * (g * HEADS + h + 1) / (N_GROUPS * HEADS))


def _resident(shape):
    nd = len(shape)
    return pl.BlockSpec(shape, lambda *_: (0,) * nd, pipeline_mode=pl.Buffered(1))


def _mod_body(c_ref, w_ref, b_ref, o_ref):
    c = c_ref[...]
    s = c * _sigmoid(c)
    o_ref[...] = jnp.dot(s, w_ref[...], precision=HIGHEST, preferred_element_type=F32) + b_ref[...]


def _modulation(c_all, w_ada, b_ada):
    rows = c_all.shape[0]
    tn = 1024
    return pl.pallas_call(
        _mod_body,
        grid=(6 * D_MODEL // tn,),
        in_specs=[pl.BlockSpec((rows, D_MODEL), lambda j: (0, 0)),
                  pl.BlockSpec((D_MODEL, tn), lambda j: (0, j)),
                  pl.BlockSpec((1, tn), lambda j: (0, j))],
        out_specs=pl.BlockSpec((rows, tn), lambda j: (0, j)),
        out_shape=jax.ShapeDtypeStruct((rows, 6 * D_MODEL), F32),
        compiler_params=pltpu.CompilerParams(dimension_semantics=("arbitrary",), vmem_limit_bytes=VMEM_LIMIT),
        name="modulation",
    )(c_all, w_ada, b_ada.reshape(1, -1))


def _front_body(tiles_per_seq, x_ref, mod_ref, g_ref, w_ref, cw_ref,
                a0_ref, a1_ref, a2_ref, oconv_ref, sga_ref, sgb_ref,
                kv0_ref, kv1_ref, kv2_ref, pconv_ref, res_ref, uprev_ref):
    t_in_seq = pl.program_id(0) % tiles_per_seq
    x = x_ref[...]
    shift1 = mod_ref[:, 0:D_MODEL]
    scale1 = mod_ref[:, D_MODEL:2 * D_MODEL]
    h = (_rmsnorm(x, g_ref[...]) * (1.0 + scale1) + shift1).astype(BF16)

    def proj(c0, n):
        return jnp.dot(h, w_ref[:, c0:c0 + n], preferred_element_type=F32)

    a_refs = (a0_ref, a1_ref, a2_ref)
    kv_refs = (kv0_ref, kv1_ref, kv2_ref)
    for g in range(N_GROUPS):
        d = DILATIONS[g]
        n = TM // d
        for part, base in enumerate((Q_OFF, K_OFF, V_OFF)):
            r = proj(base + g * GROUP_W, GROUP_W)
            cols = slice(part * GROUP_W, (part + 1) * GROUP_W)
            if part == 0:
                r = r * (HEAD_DIM ** -0.5)
            else:
                kvc = slice((part - 1) * GROUP_W, part * GROUP_W)
                kv_refs[g][kvc, :] = (r[TM - BAND:, :] if g == 0 else r).T
            if g == 0:
                a_refs[g][:, cols] = r.astype(BF16)
            else:
                for c in range(GROUP_W // LANES):
                    res_ref[c] = r[:, c * LANES:(c + 1) * LANES]
                for rr in range(d):
                    for c in range(GROUP_W // LANES):
                        c0 = part * GROUP_W + c * LANES
                        a_refs[g][0, rr, :, c0:c0 + LANES] = res_ref[c, pl.ds(rr, n, stride=d), :].astype(BF16)

    bg = proj(BG_OFF, CONV_W)
    u = proj(CG_OFF, CONV_W) * proj(UI_OFF, CONV_W)
    tail = u[TM - 8:, :]
    pconv_ref[...] = tail
    prev = jnp.where(t_in_seq == 0, 0.0, uprev_ref[...])
    row = lax.broadcasted_iota(jnp.int32, (TM, 1), 0)
    u1 = jnp.where(row == 0, prev[7:8, :], pltpu.roll(u, 1, axis=0))
    u2 = jnp.where(row == 0, prev[6:7, :], jnp.where(row == 1, prev[7:8, :], pltpu.roll(u, 2, axis=0)))
    yconv = cw_ref[0:1, :] * u2 + cw_ref[1:2, :] * u1 + cw_ref[2:3, :] * u
    oconv_ref[...] = (bg * yconv).astype(BF16)
    uprev_ref[...] = tail

    sga_ref[...] = _sigmoid(proj(GA_OFF, D_MODEL)).astype(BF16)
    sgb_ref[...] = _sigmoid(proj(GB_OFF, D_MODEL)).astype(BF16)


def _front(x2d, mod_p, g_mix, w_in_bf16, conv_w, batch, seq):
    n_tok = x2d.shape[0]
    n_tiles = n_tok // TM
    tps = seq // TM
    kv2_blocks = ATT_TILE // TM
    out_shape = (
        jax.ShapeDtypeStruct((n_tok, ATTN_W), BF16),
        jax.ShapeDtypeStruct((n_tiles, 4, TM // 4, ATTN_W), BF16),
        jax.ShapeDtypeStruct((n_tiles, 16, TM // 16, ATTN_W), BF16),
        jax.ShapeDtypeStruct((n_tok, CONV_W), BF16),
        jax.ShapeDtypeStruct((n_tok, D_MODEL), BF16),
        jax.ShapeDtypeStruct((n_tok, D_MODEL), BF16),
        jax.ShapeDtypeStruct((batch, 2 * GROUP_W, 128), F32),
        jax.ShapeDtypeStruct((batch, 2 * GROUP_W, 512), F32),
        jax.ShapeDtypeStruct((batch, 2 * GROUP_W, 2048), F32),
        jax.ShapeDtypeStruct((batch, 8, CONV_W), F32),
    )
    out_specs = (
        pl.BlockSpec((TM, ATTN_W), lambda i: (i, 0)),
        pl.BlockSpec((1, 4, TM // 4, ATTN_W), lambda i: (i, 0, 0, 0)),
        pl.BlockSpec((1, 16, TM // 16, ATTN_W), lambda i: (i, 0, 0, 0)),
        pl.BlockSpec((TM, CONV_W), lambda i: (i, 0)),
        pl.BlockSpec((TM, D_MODEL), lambda i: (i, 0)),
        pl.BlockSpec((TM, D_MODEL), lambda i: (i, 0)),
        pl.BlockSpec((None, 2 * GROUP_W, 128), lambda i: (i // tps, 0, 0)),
        pl.BlockSpec((None, 2 * GROUP_W, TM), lambda i: (i // tps, 0, 0)),
        pl.BlockSpec((None, 2 * GROUP_W, TM),
                     lambda i: (i // tps, 0, jnp.maximum(i % tps - (tps - kv2_blocks), 0))),
        pl.BlockSpec((None, 8, CONV_W), lambda i: (i // tps, 0, 0)),
    )
    in_specs = [
        pl.BlockSpec((TM, D_MODEL), lambda i: (i, 0)),
        pl.BlockSpec((None, 1, 6 * D_MODEL), lambda i: (i // tps, 0, 0)),
        _resident((1, D_MODEL)),
        _resident((D_MODEL, IN_COLS)),
        _resident((3, CONV_W)),
    ]
    return pl.pallas_call(
        functools.partial(_front_body, tps),
        grid=(n_tiles,),
        in_specs=in_specs,
        out_specs=out_specs,
        out_shape=out_shape,
        scratch_shapes=[pltpu.VMEM((GROUP_W // LANES, TM, LANES), F32), pltpu.VMEM((8, CONV_W), F32)],
        compiler_params=pltpu.CompilerParams(dimension_semantics=("arbitrary",), vmem_limit_bytes=VMEM_LIMIT),
        name="prompt_front",
    )(x2d, mod_p, g_mix.reshape(1, -1), w_in_bf16, conv_w)


def _band_bias_table():
    qi = jnp.arange(BAND)[:, None]
    kc = jnp.arange(2 * BAND)[None, :]
    delta = qi - (kc - BAND)
    valid = (delta >= 0) & (delta <= BAND)
    tabs = []
    for first in (False, True):
        ok = valid & (kc >= BAND) if first else valid
        for g in range(N_GROUPS):
            for h in range(HEADS):
                b = -_alibi_slope(g, h) * (delta * DILATIONS[g]).astype(F32)
                tabs.append(jnp.where(ok, b, MASKED))
    return jnp.stack(tabs).astype(F32)


def _head_lane_mask(h, dtype=None):
    lane = lax.broadcasted_iota(jnp.int32, (1, GROUP_W), 1)
    return (lane >= h * HEAD_DIM) & (lane < (h + 1) * HEAD_DIM)


def _attn_unit(q, k, v, bias_ref, bias_base):
    o = jnp.zeros((BAND, GROUP_W), F32)
    lse_b = jnp.zeros((BAND, GROUP_W), F32)
    for h in range(HEADS):
        hm = _head_lane_mask(h)
        qm = jnp.where(hm, q, jnp.zeros_like(q))
        s = lax.dot_general(qm, k, (((1,), (1,)), ((), ())), preferred_element_type=F32)
        s = s + bias_ref[bias_base + h]
        m = jnp.max(s, axis=-1, keepdims=True)
        e = jnp.exp(s - m)
        l = jnp.sum(e, axis=-1, keepdims=True)
        p = (e * (1.0 / l)).astype(BF16)
        vm = jnp.where(hm, v, jnp.zeros_like(v))
        o = o + jnp.dot(p, vm, preferred_element_type=F32)
        lse_b = lse_b + jnp.where(hm, m + jnp.log(l), 0.0)
    return o, lse_b


def _attn_body(a0c_ref, a0p_ref, a1c_ref, a1p_ref, a2c_ref, a2p_ref, bias_ref, o_ref, og_ref, lg_ref):
    first_tile = pl.program_id(1) == 0
    first_off = jnp.where(first_tile, N_GROUPS * HEADS, 0)
    qs, ks, vs = (slice(0, GROUP_W), slice(GROUP_W, 2 * GROUP_W), slice(2 * GROUP_W, 3 * GROUP_W))

    def put(g, rows, o, lse_b):
        for c in range(GROUP_W // LANES):
            og_ref[g, c, rows, :] = o[:, c * LANES:(c + 1) * LANES]
            lg_ref[g, c, rows, :] = lse_b[:, c * LANES:(c + 1) * LANES]

    def g0_store(n0, o, lse_b):
        put(0, pl.ds(n0, BAND), o, lse_b)

    q = a0c_ref[0:BAND, qs]
    k = jnp.concatenate([a0p_ref[:, ks], a0c_ref[0:BAND, ks]], axis=0)
    v = jnp.concatenate([a0p_ref[:, vs], a0c_ref[0:BAND, vs]], axis=0)
    g0_store(0, *_attn_unit(q, k, v, bias_ref, first_off))

    def g0_loop(n, carry):
        n0 = pl.multiple_of(n * BAND, BAND)
        q = a0c_ref[pl.ds(n0, BAND), qs]
        k = a0c_ref[pl.ds(n0 - BAND, 2 * BAND), ks]
        v = a0c_ref[pl.ds(n0 - BAND, 2 * BAND), vs]
        g0_store(n0, *_attn_unit(q, k, v, bias_ref, 0))
        return carry

    lax.fori_loop(1, ATT_TILE // BAND, g0_loop, 0)

    def g1_unit(jj, r, kp, vp, bias_base):
        q = a1c_ref[jj, r, :, qs]
        k = jnp.concatenate([kp, a1c_ref[jj, r, :, ks]], axis=0)
        v = jnp.concatenate([vp, a1c_ref[jj, r, :, vs]], axis=0)
        o, lse_b = _attn_unit(q, k, v, bias_ref, bias_base)
        put(1, pl.ds(jj * (4 * BAND) + r, BAND, stride=4), o, lse_b)

    def g1_first(r, carry):
        g1_unit(0, r, a1p_ref[0, r, :, ks], a1p_ref[0, r, :, vs], first_off + HEADS)
        return carry

    lax.fori_loop(0, 4, g1_first, 0)

    def g1_rest(t, carry):
        jj = 1 + t // 4
        r = t % 4
        g1_unit(jj, r, a1c_ref[jj - 1, r, :, ks], a1c_ref[jj - 1, r, :, vs], HEADS)
        return carry

    lax.fori_loop(0, 12, g1_rest, 0)

    n_sub = a2c_ref.shape[0]

    def g2_rows(ref, r, cols):
        return jnp.concatenate([ref[t, r, :, cols] for t in range(n_sub)], axis=0)

    def g2_loop(r, carry):
        q = g2_rows(a2c_ref, r, qs)
        k = jnp.concatenate([g2_rows(a2p_ref, r, ks), g2_rows(a2c_ref, r, ks)], axis=0)
        v = jnp.concatenate([g2_rows(a2p_ref, r, vs), g2_rows(a2c_ref, r, vs)], axis=0)
        o, lse_b = _attn_unit(q, k, v, bias_ref, first_off + 2 * HEADS)
        put(2, pl.ds(r, BAND, stride=16), o, lse_b)
        return carry

    lax.fori_loop(0, 16, g2_loop, 0)

    def mix(c, carry):
        rows = pl.ds(pl.multiple_of(c * BAND, BAND), BAND)
        for half in range(GROUP_W // LANES):
            l0, l1, l2 = lg_ref[0, half, rows, :], lg_ref[1, half, rows, :], lg_ref[2, half, rows, :]
            mx = jnp.maximum(jnp.maximum(l0, l1), l2)
            w0, w1, w2 = jnp.exp(l0 - mx), jnp.exp(l1 - mx), jnp.exp(l2 - mx)
            num = w0 * og_ref[0, half, rows, :] + w1 * og_ref[1, half, rows, :] + w2 * og_ref[2, half, rows, :]
            o_ref[rows, half * LANES:(half + 1) * LANES] = (num / (w0 + w1 + w2)).astype(o_ref.dtype)
        return carry

    lax.fori_loop(0, ATT_TILE // BAND, mix, 0)


def _attention(a0, a1, a2, bias, batch, seq):
    steps = seq // ATT_TILE
    sub = ATT_TILE // TM
    a0 = a0.reshape(batch, seq, ATTN_W)
    a1 = a1.reshape(batch, seq // TM, 4, TM // 4, ATTN_W)
    a2 = a2.reshape(batch, seq // TM, 16, TM // 16, ATTN_W)
    in_specs = [
        pl.BlockSpec((None, ATT_TILE, ATTN_W), lambda b, j: (b, j, 0)),
        pl.BlockSpec((None, BAND, ATTN_W), lambda b, j: (b, jnp.maximum(j * (ATT_TILE // BAND) - 1, 0), 0)),
        pl.BlockSpec((None, sub, 4, TM // 4, ATTN_W), lambda b, j: (b, j, 0, 0, 0)),
        pl.BlockSpec((None, 1, 4, TM // 4, ATTN_W), lambda b, j: (b, jnp.maximum(j * sub - 1, 0), 0, 0, 0)),
        pl.BlockSpec((None, sub, 16, TM // 16, ATTN_W), lambda b, j: (b, j, 0, 0, 0)),
        pl.BlockSpec((None, sub, 16, TM // 16, ATTN_W), lambda b, j: (b, jnp.maximum(j - 1, 0), 0, 0, 0)),
        _resident(bias.shape),
    ]
    return pl.pallas_call(
        _attn_body,
        grid=(batch, steps),
        in_specs=in_specs,
        out_specs=pl.BlockSpec((None, ATT_TILE, GROUP_W), lambda b, j: (b, j, 0)),
        out_shape=jax.ShapeDtypeStruct((batch, seq, GROUP_W), BF16),
        scratch_shapes=[pltpu.VMEM((N_GROUPS, GROUP_W // LANES, ATT_TILE, LANES), F32)] * 2,
        compiler_params=pltpu.CompilerParams(dimension_semantics=("arbitrary", "arbitrary"),
                                             vmem_limit_bytes=VMEM_LIMIT),
        name="prompt_attention",
    )(a0, a0, a1, a1, a2, a2, bias)


def _route(logits):
    lane = lax.broadcasted_iota(jnp.int32, logits.shape, 1)
    big = jnp.int32(1 << 20)
    gmask = lane < N_EXPERT_GROUPS
    lg = jnp.where(gmask, logits, MASKED)
    gmax = jnp.max(lg, axis=-1, keepdims=True)
    gidx = jnp.min(jnp.where(gmask & (lg == gmax), lane, big), axis=-1, keepdims=True)
    p_top = 1.0 / jnp.sum(jnp.where(gmask, jnp.exp(lg - gmax), 0.0), axis=-1, keepdims=True)
    lo = EXPERT_LANE0 + EXPERTS_PER_GROUP * gidx
    emask = (lane >= lo) & (lane < lo + EXPERTS_PER_GROUP)
    le = jnp.where(emask, logits, MASKED)
    v1 = jnp.max(le, axis=-1, keepdims=True)
    i1 = jnp.min(jnp.where(emask & (le == v1), lane, big), axis=-1, keepdims=True)
    emask2 = emask & (lane != i1)
    le2 = jnp.where(emask2, logits, MASKED)
    v2 = jnp.max(le2, axis=-1, keepdims=True)
    i2 = jnp.min(jnp.where(emask2 & (le2 == v2), lane, big), axis=-1, keepdims=True)
    e2 = jnp.exp(v2 - v1)
    den = 1.0 + e2
    w1 = (1.0 / den) * p_top
    w2 = (e2 / den) * p_top
    id1 = (i1 - EXPERT_LANE0).astype(F32)
    id2 = (i2 - EXPERT_LANE0).astype(F32)
    return jnp.where(lane == RT_W1, w1, jnp.where(lane == RT_W2, w2,
                     jnp.where(lane == RT_E1, id1, jnp.where(lane == RT_E2, id2, 0.0))))


def _back_body(oa_ref, oc_ref, sga_ref, sgb_ref, x_ref, mod_ref, wpa_ref, wpb_ref, wo_ref, g_ref,
               wrh_ref, wrl_ref, br_ref, x1_ref, h2_ref, comb_ref):
    pa = jnp.dot(oa_ref[...], wpa_ref[...], preferred_element_type=F32)
    pb = jnp.dot(oc_ref[...], wpb_ref[...], preferred_element_type=F32)
    merged = sga_ref[...].astype(F32) * pa + sgb_ref[...].astype(F32) * pb
    gate1 = mod_ref[:, 2 * D_MODEL:3 * D_MODEL]
    shift2 = mod_ref[:, 3 * D_MODEL:4 * D_MODEL]
    scale2 = mod_ref[:, 4 * D_MODEL:5 * D_MODEL]
    x1 = x_ref[...] + gate1 * jnp.dot(merged.astype(BF16), wo_ref[...], preferred_element_type=F32)
    x1_ref[...] = x1
    h2 = _rmsnorm(x1, g_ref[...]) * (1.0 + scale2) + shift2
    hi = h2.astype(BF16)
    h2_ref[...] = hi
    lo = (h2 - hi.astype(F32)).astype(BF16)
    logits = (jnp.dot(hi, wrh_ref[...], preferred_element_type=F32)
              + jnp.dot(lo, wrh_ref[...], preferred_element_type=F32)
              + jnp.dot(hi, wrl_ref[...], preferred_element_type=F32)) + br_ref[...]
    comb_ref[...] = _route(logits)


def _back(o_attn, oconv, sga, sgb, x2d, mod_p, wpa, wpb, wo, g_ffn, wr_hi, wr_lo, b_r, seq):
    n_tok = x2d.shape[0]
    n_all = n_tok + TM
    tps = seq // TM
    row = lambda w: pl.BlockSpec((TM, w), lambda i: (i, 0))
    return pl.pallas_call(
        _back_body,
        grid=(n_tok // TM,),
        in_specs=[row(GROUP_W), row(CONV_W), row(D_MODEL), row(D_MODEL), row(D_MODEL),
                  pl.BlockSpec((None, 1, 6 * D_MODEL), lambda i: (i // tps, 0, 0)),
                  _resident(wpa.shape), _resident(wpb.shape), _resident(wo.shape), _resident((1, D_MODEL)),
                  _resident(wr_hi.shape), _resident(wr_lo.shape), _resident(b_r.shape)],
        out_specs=(row(D_MODEL), row(D_MODEL), row(ROUTER_LANES)),
        out_shape=(jax.ShapeDtypeStruct((n_all, D_MODEL), F32),
                   jax.ShapeDtypeStruct((n_all, D_MODEL), BF16),
                   jax.ShapeDtypeStruct((n_all, ROUTER_LANES), F32)),
        compiler_params=pltpu.CompilerParams(dimension_semantics=("arbitrary",), vmem_limit_bytes=VMEM_LIMIT),
        name="prompt_back",
    )(o_attn, oconv, sga, sgb, x2d, mod_p, wpa, wpb, wo, g_ffn.reshape(1, -1), wr_hi, wr_lo, b_r)


def _plan_body(rt_ref, rank_ref, cnt_ref, base_ref):
    @pl.when(pl.program_id(0) == 0)
    def _():
        base_ref[...] = jnp.zeros_like(base_ref)

    rt = rt_ref[...]
    lane = lax.broadcasted_iota(jnp.int32, (TM, LANES), 1)
    lane_f = lane.astype(F32)
    oh1 = jnp.where(lane_f == rt[:, RT_E1:RT_E1 + 1], 1.0, 0.0)
    oh2 = jnp.where(lane_f == rt[:, RT_E2:RT_E2 + 1], 1.0, 0.0)
    before = jnp.where(lax.broadcasted_iota(jnp.int32, (TM, TM), 1) < lax.broadcasted_iota(jnp.int32, (TM, TM), 0),
                       1.0, 0.0).astype(BF16)
    p1 = jnp.dot(before, oh1.astype(BF16), preferred_element_type=F32)
    p2 = jnp.dot(before, oh2.astype(BF16), preferred_element_type=F32)
    c1 = jnp.sum(oh1, axis=0, keepdims=True)
    c2 = jnp.sum(oh2, axis=0, keepdims=True)
    base = base_ref[...]
    rank1 = jnp.sum(oh1 * (base + p1), axis=-1, keepdims=True)
    rank2 = jnp.sum(oh2 * (base + c1 + p2), axis=-1, keepdims=True)
    rank_ref[...] = jnp.where(lane == 0, rank1, jnp.where(lane == 1, rank2, 0.0))
    total = base + c1 + c2
    base_ref[...] = total
    cnt_ref[...] = total


def _plan(rt):
    n_all = rt.shape[0]
    return pl.pallas_call(
        _plan_body,
        grid=(n_all // TM,),
        in_specs=[pl.BlockSpec((TM, ROUTER_LANES), lambda i: (i, 0))],
        out_specs=(pl.BlockSpec((TM, LANES), lambda i: (i, 0)), pl.BlockSpec((1, LANES), lambda i: (0, 0))),
        out_shape=(jax.ShapeDtypeStruct((n_all, LANES), F32), jax.ShapeDtypeStruct((1, LANES), F32)),
        scratch_shapes=[pltpu.VMEM((1, LANES), F32)],
        compiler_params=pltpu.CompilerParams(dimension_semantics=("arbitrary",), vmem_limit_bytes=VMEM_LIMIT),
        name="moe_plan",
    )(rt)


def _valid_chunks(tile, n_prompt_tiles, n_sample):
    return jnp.where(tile < n_prompt_tiles, TM // DMA_CHUNK, n_sample // DMA_CHUNK)


def _dispatch_body(n_prompt_tiles, n_sample, dest_ref, tail_ref, h_ref, xs_ref, stage_ref, zero_ref, sem):
    i = pl.program_id(0)

    def tail_copy(e):
        start = pl.multiple_of(tail_ref[e] * ROW_SUB, ROW_SUB)
        return pltpu.make_async_copy(zero_ref, xs_ref.at[pl.ds(start, EXPERT_TILE * ROW_SUB)], sem)

    @pl.when(i == 0)
    def _():
        zero_ref[...] = jnp.zeros_like(zero_ref)
        for e in range(N_EXPERTS):
            @pl.when(tail_ref[e] >= 0)
            def _():
                tail_copy(e).start()
        for e in range(N_EXPERTS):
            @pl.when(tail_ref[e] >= 0)
            def _():
                tail_copy(e).wait()

    for j in range(ROW_SUB):
        stage_ref[pl.ds(j, TM, stride=ROW_SUB), :] = h_ref[:, j * LANES:(j + 1) * LANES].astype(F32)
    n_chunks = _valid_chunks(i, n_prompt_tiles, n_sample)

    def issue(c, carry):
        for j in range(DMA_CHUNK):
            r = c * DMA_CHUNK + j
            src = stage_ref.at[pl.ds(pl.multiple_of(r * ROW_SUB, ROW_SUB), ROW_SUB)]
            for k in range(2):
                d = pl.multiple_of(dest_ref[(i * TM + r) * 2 + k], ROW_SUB)
                pltpu.make_async_copy(src, xs_ref.at[pl.ds(d, ROW_SUB)], sem).start(priority=k)
        return carry

    lax.fori_loop(0, n_chunks, issue, 0)

    def drain(c, carry):
        n = 2 * DMA_CHUNK * ROW_SUB
        pltpu.make_async_copy(stage_ref.at[pl.ds(0, n)], xs_ref.at[pl.ds(0, n)], sem).wait()
        return carry

    lax.fori_loop(0, n_chunks, drain, 0)


def _dispatch(dest, tail_start, h2, n_rows, n_prompt_tiles, n_sample):
    n_all = h2.shape[0]
    return pl.pallas_call(
        functools.partial(_dispatch_body, n_prompt_tiles, n_sample),
        grid_spec=pltpu.PrefetchScalarGridSpec(
            num_scalar_prefetch=2,
            grid=(n_all // TM,),
            in_specs=[pl.BlockSpec((TM, D_MODEL), lambda i, *_: (i, 0))],
            out_specs=pl.BlockSpec(memory_space=pl.ANY),
            scratch_shapes=[pltpu.VMEM((TM * ROW_SUB, LANES), F32), pltpu.VMEM((EXPERT_TILE * ROW_SUB, LANES), F32),
                            pltpu.SemaphoreType.DMA],
        ),
        out_shape=jax.ShapeDtypeStruct((n_rows * ROW_SUB, LANES), F32),
        compiler_params=pltpu.CompilerParams(dimension_semantics=("arbitrary",), vmem_limit_bytes=VMEM_LIMIT,
                                             disable_bounds_checks=True),
        name="moe_dispatch",
    )(dest, tail_start, h2)


def _experts_body(te_ref, nt_ref, xs_ref, wg_ref, wu_ref, wd_ref, o_ref):
    @pl.when(pl.program_id(0) < nt_ref[0])
    def _():
        x = jnp.concatenate([xs_ref[pl.ds(j, EXPERT_TILE, stride=ROW_SUB), :].astype(BF16) for j in range(ROW_SUB)],
                            axis=1)
        a = jnp.dot(x, wg_ref[0].astype(BF16), preferred_element_type=F32)
        z = (a * _sigmoid(a)) * jnp.dot(x, wu_ref[0].astype(BF16), preferred_element_type=F32)
        o = jnp.dot(z.astype(BF16), wd_ref[0].astype(BF16), preferred_element_type=F32)
        for j in range(ROW_SUB):
            o_ref[pl.ds(j, EXPERT_TILE, stride=ROW_SUB), :] = o[:, j * LANES:(j + 1) * LANES]


def _experts(tile_expert, n_tiles, xs, w_gate_e, w_up_e, w_down_e):
    max_tiles = tile_expert.shape[0]
    rows = lambda s, te, nt: (jnp.minimum(s, nt[0] - 1), 0)
    row_tile = pl.BlockSpec((EXPERT_TILE * ROW_SUB, LANES), rows)
    weight = lambda shape: pl.BlockSpec((1,) + shape, lambda s, te, nt: (te[jnp.minimum(s, nt[0] - 1)], 0, 0))
    return pl.pallas_call(
        _experts_body,
        grid_spec=pltpu.PrefetchScalarGridSpec(
            num_scalar_prefetch=2,
            grid=(max_tiles,),
            in_specs=[row_tile,
                      weight((D_MODEL, EXPERT_FF)), weight((D_MODEL, EXPERT_FF)), weight((EXPERT_FF, D_MODEL))],
            out_specs=row_tile,
        ),
        out_shape=jax.ShapeDtypeStruct(xs.shape, F32),
        compiler_params=pltpu.CompilerParams(dimension_semantics=("arbitrary",), vmem_limit_bytes=VMEM_LIMIT),
        name="moe_experts",
    )(tile_expert, n_tiles, xs, w_gate_e, w_up_e, w_down_e)


def _combine_body(n_prompt_tiles, n_sample, dest_ref, x1_ref, rt_ref, gp_ref, gs_ref, gf_ref, eo_ref,
                  yp_ref, ys_ref, rows_ref, sem):
    i = pl.program_id(0)
    n_chunks = _valid_chunks(i, n_prompt_tiles, n_sample)

    def issue(c, carry):
        for j in range(DMA_CHUNK):
            r = c * DMA_CHUNK + j
            row = pl.ds(pl.multiple_of(r * ROW_SUB, ROW_SUB), ROW_SUB)
            for k in range(2):
                d = pl.multiple_of(dest_ref[(i * TM + r) * 2 + k], ROW_SUB)
                pltpu.make_async_copy(eo_ref.at[pl.ds(d, ROW_SUB)], rows_ref.at[k, row], sem).start(priority=k)
        return carry

    lax.fori_loop(0, n_chunks, issue, 0)

    def drain(c, carry):
        n = 2 * DMA_CHUNK * ROW_SUB
        pltpu.make_async_copy(eo_ref.at[pl.ds(0, n)], rows_ref.at[0, pl.ds(0, n)], sem).wait()
        return carry

    lax.fori_loop(0, n_chunks, drain, 0)

    rt = rt_ref[...]
    w1, w2 = rt[:, RT_W1:RT_W1 + 1], rt[:, RT_W2:RT_W2 + 1]
    lane_tile = lambda k, j: rows_ref[k, pl.ds(j, TM, stride=ROW_SUB), :]
    moe = jnp.concatenate([w1 * lane_tile(0, j) + w2 * lane_tile(1, j) for j in range(ROW_SUB)], axis=1)
    is_prompt = i < n_prompt_tiles
    gate2 = jnp.where(is_prompt, gp_ref[...], gs_ref[...])
    y = _rmsnorm(x1_ref[...] + gate2 * moe, gf_ref[...])

    @pl.when(is_prompt)
    def _():
        yp_ref[...] = y

    @pl.when(jnp.logical_not(is_prompt))
    def _():
        ys_ref[...] = y[:n_sample, :]


def _combine(dest, x1, rt, gate2_p, gate2_s, g_final, eo, n_prompt_tiles, n_sample, tiles_per_seq):
    n_all = x1.shape[0]
    last_p = n_prompt_tiles - 1
    return pl.pallas_call(
        functools.partial(_combine_body, n_prompt_tiles, n_sample),
        grid_spec=pltpu.PrefetchScalarGridSpec(
            num_scalar_prefetch=1,
            grid=(n_all // TM,),
            in_specs=[pl.BlockSpec((TM, D_MODEL), lambda i, *_: (i, 0)),
                      pl.BlockSpec((TM, ROUTER_LANES), lambda i, *_: (i, 0)),
                      pl.BlockSpec((None, 1, D_MODEL), lambda i, *_: (jnp.minimum(i, last_p) // tiles_per_seq, 0, 0)),
                      pl.BlockSpec((TM, D_MODEL), lambda i, *_: (0, 0)),
                      pl.BlockSpec((1, D_MODEL), lambda i, *_: (0, 0)),
                      pl.BlockSpec(memory_space=pl.ANY)],
            out_specs=(pl.BlockSpec((TM, D_MODEL), lambda i, *_: (jnp.minimum(i, last_p), 0)),
                       pl.BlockSpec((n_sample, D_MODEL), lambda i, *_: (0, 0))),
            scratch_shapes=[pltpu.VMEM((2, TM * ROW_SUB, LANES), F32), pltpu.SemaphoreType.DMA],
        ),
        out_shape=(jax.ShapeDtypeStruct((n_prompt_tiles * TM, D_MODEL), F32),
                   jax.ShapeDtypeStruct((n_sample, D_MODEL), F32)),
        compiler_params=pltpu.CompilerParams(dimension_semantics=("arbitrary",), vmem_limit_bytes=VMEM_LIMIT,
                                             disable_bounds_checks=True),
        name="moe_combine",
    )(dest, x1, rt, gate2_p, gate2_s, g_final.reshape(1, -1), eo)


def _routing_tables(rt, rank, counts, max_tiles):
    cnt = counts[0, :N_EXPERTS].astype(jnp.int32)
    padded = (cnt + EXPERT_TILE - 1) // EXPERT_TILE * EXPERT_TILE
    ends = jnp.cumsum(padded)
    starts = ends - padded
    n_tiles = (ends[-1] // EXPERT_TILE).reshape(1)
    tile_row0 = jnp.arange(max_tiles, dtype=jnp.int32) * EXPERT_TILE
    tile_expert = jnp.minimum(jnp.sum(ends[None, :] <= tile_row0[:, None], axis=1), N_EXPERTS - 1).astype(jnp.int32)
    tail_start = jnp.where(cnt > 0, ends - EXPERT_TILE, -1).astype(jnp.int32)
    eid = rt[:, RT_E1:RT_E2 + 1].astype(jnp.int32)
    onehot = eid[..., None] == jnp.arange(N_EXPERTS, dtype=jnp.int32)
    dest = rank[:, 0:2].astype(jnp.int32) + jnp.sum(jnp.where(onehot, starts, 0), axis=-1)
    dest = (jnp.where(eid >= 0, dest, 0) * ROW_SUB).reshape(-1)
    return dest, tail_start, tile_expert, n_tiles


def _sample_front_body(x_ref, mod_ref, g_ref, w_ref, o_ref):
    shift1 = mod_ref[:, 0:D_MODEL]
    scale1 = mod_ref[:, D_MODEL:2 * D_MODEL]
    h = _rmsnorm(x_ref[...], g_ref[...]) * (1.0 + scale1) + shift1
    o_ref[...] = jnp.dot(h, w_ref[...], precision=HIGHEST, preferred_element_type=F32)


def _sample_front(x_s, mod_s, g_mix, w_in):
    nb = x_s.shape[0]
    tn = 256
    return pl.pallas_call(
        _sample_front_body,
        grid=(IN_COLS // tn,),
        in_specs=[pl.BlockSpec((nb, D_MODEL), lambda j: (0, 0)),
                  pl.BlockSpec((nb, 2 * D_MODEL), lambda j: (0, 0)),
                  pl.BlockSpec((1, D_MODEL), lambda j: (0, 0)),
                  pl.BlockSpec((D_MODEL, tn), lambda j: (0, j))],
        out_specs=pl.BlockSpec((nb, tn), lambda j: (0, j)),
        out_shape=jax.ShapeDtypeStruct((nb, IN_COLS), F32),
        compiler_params=pltpu.CompilerParams(dimension_semantics=("arbitrary",), vmem_limit_bytes=VMEM_LIMIT),
        name="sample_front",
    )(x_s, mod_s, g_mix.reshape(1, -1), w_in)


def _sample_attn_body(qkv_ref, c0_ref, c1_ref, c2_ref, oa_ref):
    head = lax.broadcasted_iota(jnp.int32, (HEADS, 1, 1), 0)
    for b in range(qkv_ref.shape[0]):
        outs, lses = [], []
        for g, c_ref in enumerate((c0_ref, c1_ref, c2_ref)):
            window = c_ref.shape[-1]
            q = qkv_ref[b, 0, g] * (HEAD_DIM ** -0.5)
            k_new = qkv_ref[b, 1, g]
            v_new = qkv_ref[b, 2, g]
            s = jnp.sum(c_ref[b, 0] * q, axis=1, keepdims=True)
            pos = lax.broadcasted_iota(jnp.int32, (1, 1, window), 2)
            slope = jnp.full((HEADS, 1, 1), _alibi_slope(g, HEADS - 1), F32)
            for h in range(HEADS - 1):
                slope = jnp.where(head == h, _alibi_slope(g, h), slope)
            on_band = (pos & (DILATIONS[g] - 1)) == 0
            s = jnp.where(on_band, s - slope * (window - pos).astype(F32), MASKED)
            s_self = jnp.sum(q * k_new, axis=1, keepdims=True)
            m = jnp.maximum(jnp.max(s, axis=2, keepdims=True), s_self)
            e = jnp.exp(s - m)
            e_self = jnp.exp(s_self - m)
            l = jnp.sum(e, axis=2, keepdims=True) + e_self
            o = jnp.sum(c_ref[b, 1] * e, axis=2, keepdims=True) + e_self * v_new
            outs.append(o / l)
            lses.append(m + jnp.log(l))
        mx = jnp.maximum(jnp.maximum(lses[0], lses[1]), lses[2])
        w = [jnp.exp(x - mx) for x in lses]
        oa_ref[b] = (w[0] * outs[0] + w[1] * outs[1] + w[2] * outs[2]) / (w[0] + w[1] + w[2])


def _sample_attention(qkv_s, cache0, cache1, cache2):
    nb = qkv_s.shape[0]
    bb = 2
    hd = (HEADS, HEAD_DIM)
    caches = [jnp.transpose(c, (0, 1, 3, 4, 5, 2)) for c in (cache0, cache1, cache2)]
    cache_spec = lambda c: pl.BlockSpec((None, bb, 2, *hd, c.shape[-1]), lambda i: (0, i, 0, 0, 0, 0))
    return pl.pallas_call(
        _sample_attn_body,
        grid=(nb // bb,),
        in_specs=[pl.BlockSpec((bb, 3, N_GROUPS, *hd, 1), lambda i: (i, 0, 0, 0, 0, 0))]
        + [cache_spec(c) for c in caches],
        out_specs=pl.BlockSpec((bb, *hd, 1), lambda i: (i, 0, 0, 0)),
        out_shape=jax.ShapeDtypeStruct((nb, *hd, 1), F32),
        compiler_params=pltpu.CompilerParams(dimension_semantics=("arbitrary",), vmem_limit_bytes=VMEM_LIMIT),
        name="sample_attention",
    )(qkv_s, *caches)


def _sample_back_body(p_ref, oa_ref, cc_ref, x_ref, mod_ref, cw_ref, wpa_ref, wpb_ref, wo_ref, g_ref,
                      wr_ref, br_ref, x1_any, h2_any, rt_any, x1_ref, h2_ref, rt_ref, sconv_ref):
    del x1_any, h2_any, rt_any
    nb = x_ref.shape[0]
    hdot = functools.partial(jnp.dot, precision=HIGHEST, preferred_element_type=F32)
    bg = p_ref[:, BG_OFF:BG_OFF + CONV_W]
    u = p_ref[:, CG_OFF:CG_OFF + CONV_W] * p_ref[:, UI_OFF:UI_OFF + CONV_W]
    c_old, c_new = cc_ref[:, 0, :], cc_ref[:, 1, :]
    yconv = cw_ref[0:1, :] * c_old + cw_ref[1:2, :] * c_new + cw_ref[2:3, :] * u
    sconv_ref[:, 0, :] = c_new
    sconv_ref[:, 1, :] = u
    sga = _sigmoid(p_ref[:, GA_OFF:GA_OFF + D_MODEL])
    sgb = _sigmoid(p_ref[:, GB_OFF:GB_OFF + D_MODEL])
    merged = sga * hdot(oa_ref[...], wpa_ref[...]) + sgb * hdot(bg * yconv, wpb_ref[...])
    gate1 = mod_ref[:, 2 * D_MODEL:3 * D_MODEL]
    shift2 = mod_ref[:, 3 * D_MODEL:4 * D_MODEL]
    scale2 = mod_ref[:, 4 * D_MODEL:5 * D_MODEL]
    x1 = x_ref[...] + gate1 * hdot(merged, wo_ref[...])
    h2 = _rmsnorm(x1, g_ref[...]) * (1.0 + scale2) + shift2
    rt = _route(hdot(h2, wr_ref[...]) + br_ref[...])
    lane = lax.broadcasted_iota(jnp.int32, (TM, ROUTER_LANES), 1)
    x1_ref[...] = jnp.zeros_like(x1_ref)
    h2_ref[...] = jnp.zeros_like(h2_ref)
    rt_ref[...] = jnp.where((lane == RT_E1) | (lane == RT_E2), -1.0, 0.0)
    x1_ref[0:nb, :] = x1
    h2_ref[0:nb, :] = h2.astype(BF16)
    rt_ref[0:nb, :] = rt


def _sample_back(proj_s, oa_s, cache_conv, x_s, mod_s, conv_w, w_pa, w_pb, w_o, g_ffn, w_r, b_r, x1, h2, rt):
    nb = x_s.shape[0]
    last = x1.shape[0] // TM - 1
    args = (proj_s, oa_s, cache_conv, x_s, mod_s, conv_w, w_pa, w_pb, w_o, g_ffn.reshape(1, -1), w_r, b_r)
    full = lambda shape: pl.BlockSpec(shape, lambda i: (0,) * len(shape))
    last_tile = lambda a: pl.BlockSpec((TM, a.shape[1]), lambda i: (last, 0))
    shape_of = lambda a: jax.ShapeDtypeStruct(a.shape, a.dtype)
    return pl.pallas_call(
        _sample_back_body,
        grid=(1,),
        in_specs=[full(a.shape) for a in args] + [pl.BlockSpec(memory_space=pl.ANY)] * 3,
        out_specs=(last_tile(x1), last_tile(h2), last_tile(rt), full((nb, 2, CONV_W))),
        out_shape=(shape_of(x1), shape_of(h2), shape_of(rt), jax.ShapeDtypeStruct((nb, 2, CONV_W), F32)),
        input_output_aliases={len(args): 0, len(args) + 1: 1, len(args) + 2: 2},
        compiler_params=pltpu.CompilerParams(dimension_semantics=("arbitrary",), vmem_limit_bytes=VMEM_LIMIT),
        name="sample_back",
    )(*args, x1, h2, rt)


def kernel(x_prompt, x_sample, cache_kv_w128, cache_kv_w512, cache_kv_w2048, cache_conv, c_prompt, c_sample,
           g_mix, w_ada, b_ada, w_in, conv_w, w_pa, w_pb, w_o, g_ffn, w_router_group, b_router_group,
           w_router_expert, b_router_expert, w_gate_e, w_up_e, w_down_e, g_final):
    batch, seq, _ = x_prompt.shape
    nb = x_sample.shape[0]
    assert x_sample.shape[1] == 1 and g_mix.shape[0] == 1, "one layer, one new sample token per sequence"
    assert seq % ATT_TILE == 0 and cache_kv_w128.shape[2] == 128 and cache_kv_w512.shape[2] == 512 \
        and cache_kv_w2048.shape[2] == 2048
    (g_mix, w_ada, b_ada, w_in, conv_w, w_pa, w_pb, w_o, g_ffn, w_rg, b_rg, w_re, b_re, w_gate_e, w_up_e,
     w_down_e) = (a[0] for a in (g_mix, w_ada, b_ada, w_in, conv_w, w_pa, w_pb, w_o, g_ffn, w_router_group,
                                 b_router_group, w_router_expert, b_router_expert, w_gate_e, w_up_e, w_down_e))

    c_all = jnp.concatenate([c_prompt, jnp.zeros((8 - batch, D_MODEL), F32), c_sample], axis=0)
    mod = _modulation(c_all, w_ada, b_ada)
    mod_p = mod[:batch].reshape(batch, 1, 6 * D_MODEL)
    mod_s = mod[8:]

    w_r = jnp.zeros((D_MODEL, ROUTER_LANES), F32)
    w_r = w_r.at[:, :N_EXPERT_GROUPS].set(w_rg).at[:, EXPERT_LANE0:EXPERT_LANE0 + N_EXPERTS].set(w_re)
    b_r = jnp.full((1, ROUTER_LANES), MASKED, F32)
    b_r = b_r.at[0, :N_EXPERT_GROUPS].set(b_rg).at[0, EXPERT_LANE0:EXPERT_LANE0 + N_EXPERTS].set(b_re)
    wr_hi = w_r.astype(BF16)
    wr_lo = (w_r - wr_hi.astype(F32)).astype(BF16)

    x2d = x_prompt.reshape(batch * seq, D_MODEL)
    (a0, a1, a2, oconv, sga, sgb, kv0, kv1, kv2, pconv) = _front(
        x2d, mod_p, g_mix, w_in.astype(BF16), conv_w, batch, seq)
    o_attn = _attention(a0, a1, a2, _band_bias_table(), batch, seq).reshape(batch * seq, GROUP_W)
    x1, h2, rt = _back(o_attn, oconv, sga, sgb, x2d, mod_p, w_pa.astype(BF16), w_pb.astype(BF16),
                       w_o.astype(BF16), g_ffn, wr_hi, wr_lo, b_r, seq)

    x_s = x_sample.reshape(nb, D_MODEL)
    proj_s = _sample_front(x_s, mod_s[:, :2 * D_MODEL], g_mix, w_in)
    qkv_s = proj_s[:, :3 * ATTN_W].reshape(nb, 3, N_GROUPS, HEADS, HEAD_DIM)
    oa_s = _sample_attention(qkv_s[..., None], cache_kv_w128, cache_kv_w512, cache_kv_w2048)
    skv0, skv1, skv2 = (jnp.stack([qkv_s[:, 1, g], qkv_s[:, 2, g]], axis=1) for g in range(N_GROUPS))
    x1, h2, rt, sconv = _sample_back(proj_s, oa_s.reshape(nb, GROUP_W), cache_conv[0], x_s, mod_s, conv_w,
                                     w_pa, w_pb, w_o, g_ffn, w_r, b_r, x1, h2, rt)

    n_prompt_tiles = batch * seq // TM
    max_tiles = -(-(2 * (batch * seq + nb) + N_EXPERTS * (EXPERT_TILE - 1)) // EXPERT_TILE)
    rank, counts = _plan(rt)
    dest, tail_start, tile_expert, n_tiles = _routing_tables(rt, rank, counts, max_tiles)
    xs = _dispatch(dest, tail_start, h2, max_tiles * EXPERT_TILE, n_prompt_tiles, nb)
    eo = _experts(tile_expert, n_tiles, xs, w_gate_e, w_up_e, w_down_e)
    gate2_s = jnp.zeros((TM, D_MODEL), F32).at[:nb].set(mod_s[:, 5 * D_MODEL:])
    y_prompt, y_sample = _combine(dest, x1, rt, mod_p[:, :, 5 * D_MODEL:], gate2_s, g_final, eo,
                                  n_prompt_tiles, nb, seq // TM)

    def prompt_state(a):
        a = a.reshape(batch, 2, HEADS, HEAD_DIM, a.shape[-1])
        return jnp.transpose(a, (0, 4, 1, 2, 3))[None]

    sample_state = lambda a: a.reshape(1, nb, 1, 2, HEADS, HEAD_DIM)
    return (y_prompt.reshape(batch, seq, D_MODEL), y_sample.reshape(nb, 1, D_MODEL),
            prompt_state(kv0), prompt_state(kv1), prompt_state(kv2),
            pconv[:, 6:8, :].reshape(1, batch, 2, CONV_W),
            sample_state(skv0), sample_state(skv1), sample_state(skv2),
            sconv.reshape(1, nb, 2, CONV_W))
```

```python
import functools
import math

import jax
import jax.numpy as jnp
from jax import lax
from jax.experimental import pallas as pl
from jax.experimental.pallas import tpu as pltpu

F32 = jnp.float32
BF16 = jnp.bfloat16
HIGHEST = lax.Precision.HIGHEST

D_MODEL = 1024
HEAD_DIM = 64
HEADS = 4
N_GROUPS = 3
GROUP_W = HEADS * HEAD_DIM
ATTN_W = N_GROUPS * GROUP_W
CONV_W = 512
DILATIONS = (1, 4, 16)
BAND = 128
N_EXPERTS = 32
EXPERTS_PER_GROUP = 8
N_EXPERT_GROUPS = 4
EXPERT_FF = 256
RMS_EPS = 1e-6
MASKED = -1e30

Q_OFF, K_OFF, V_OFF = 0, ATTN_W, 2 * ATTN_W
BG_OFF = 3 * ATTN_W
CG_OFF = BG_OFF + CONV_W
UI_OFF = CG_OFF + CONV_W
GA_OFF = UI_OFF + CONV_W
GB_OFF = GA_OFF + D_MODEL
IN_COLS = GB_OFF + D_MODEL

LANES = 128
ROUTER_LANES = 128
EXPERT_LANE0 = N_EXPERT_GROUPS

TM = 512
ATT_TILE = 2048
ATTN_PAIR = 2
EXPERT_TILE = 512
RT_W1, RT_W2, RT_E1, RT_E2 = 0, 1, 2, 3
ROW_SUB = D_MODEL // LANES
DMA_CHUNK = 8
VMEM_LIMIT = 56 * 1024 * 1024


def _sigmoid(x):
    return 1.0 / (1.0 + jnp.exp(-x))


def _rmsnorm(x, g):
    return x * lax.rsqrt(jnp.mean(x * x, axis=-1, keepdims=True) + RMS_EPS) * g


def _alibi_slope(g, h):
    return 2.0 ** (-8.0 * (g * HEADS + h + 1) / (N_GROUPS * HEADS))


def _resident(shape):
    nd = len(shape)
    return pl.BlockSpec(shape, lambda *_: (0,) * nd, pipeline_mode=pl.Buffered(1))


def _mod_body(c_ref, w_ref, b_ref, o_ref):
    c = c_ref[...]
    s = c * _sigmoid(c)
    o_ref[...] = jnp.dot(s, w_ref[...], precision=HIGHEST, preferred_element_type=F32) + b_ref[...]


def _modulation(c_all, w_ada, b_ada):
    rows = c_all.shape[0]
    tn = 1024
    return pl.pallas_call(
        _mod_body,
        grid=(6 * D_MODEL // tn,),
        in_specs=[pl.BlockSpec((rows, D_MODEL), lambda j: (0, 0)),
                  pl.BlockSpec((D_MODEL, tn), lambda j: (0, j)),
                  pl.BlockSpec((1, tn), lambda j: (0, j))],
        out_specs=pl.BlockSpec((rows, tn), lambda j: (0, j)),
        out_shape=jax.ShapeDtypeStruct((rows, 6 * D_MODEL), F32),
        compiler_params=pltpu.CompilerParams(dimension_semantics=("arbitrary",), vmem_limit_bytes=VMEM_LIMIT),
        name="modulation",
    )(c_all, w_ada, b_ada.reshape(1, -1))


def _front_body(tiles_per_seq, x_ref, mod_ref, g_ref, w_ref, cw_ref,
                a0_ref, a1_ref, a2_ref, oconv_ref, sga_ref, sgb_ref,
                kv0_ref, kv1_ref, kv2_ref, pconv_ref, res_ref, uprev_ref):
    t_in_seq = pl.program_id(0) % tiles_per_seq
    x = x_ref[...]
    shift1 = mod_ref[:, 0:D_MODEL]
    scale1 = mod_ref[:, D_MODEL:2 * D_MODEL]
    h = (_rmsnorm(x, g_ref[...]) * (1.0 + scale1) + shift1).astype(BF16)

    def proj(c0, n):
        return jnp.dot(h, w_ref[:, c0:c0 + n], preferred_element_type=F32)

    a_refs = (a0_ref, a1_ref, a2_ref)
    kv_refs = (kv0_ref, kv1_ref, kv2_ref)
    for g in range(N_GROUPS):
        d = DILATIONS[g]
        n = TM // d
        for part, base in enumerate((Q_OFF, K_OFF, V_OFF)):
            r = proj(base + g * GROUP_W, GROUP_W)
            cols = slice(part * GROUP_W, (part + 1) * GROUP_W)
            if part == 0:
                r = r * (HEAD_DIM ** -0.5)
            else:
                kvc = slice((part - 1) * GROUP_W, part * GROUP_W)
                kv_refs[g][kvc, :] = (r[TM - BAND:, :] if g == 0 else r).T
            if g == 0:
                a_refs[g][:, cols] = r.astype(BF16)
            else:
                for c in range(GROUP_W // LANES):
                    res_ref[c] = r[:, c * LANES:(c + 1) * LANES]
                for rr in range(d):
                    for c in range(GROUP_W // LANES):
                        c0 = part * GROUP_W + c * LANES
                        a_refs[g][0, rr, :, c0:c0 + LANES] = res_ref[c, pl.ds(rr, n, stride=d), :].astype(BF16)

    bg = proj(BG_OFF, CONV_W)
    u = proj(CG_OFF, CONV_W) * proj(UI_OFF, CONV_W)
    tail = u[TM - 8:, :]
    pconv_ref[...] = tail
    prev = jnp.where(t_in_seq == 0, 0.0, uprev_ref[...])
    row = lax.broadcasted_iota(jnp.int32, (TM, 1), 0)
    u1 = jnp.where(row == 0, prev[7:8, :], pltpu.roll(u, 1, axis=0))
    u2 = jnp.where(row == 0, prev[6:7, :], jnp.where(row == 1, prev[7:8, :], pltpu.roll(u, 2, axis=0)))
    yconv = cw_ref[0:1, :] * u2 + cw_ref[1:2, :] * u1 + cw_ref[2:3, :] * u
    oconv_ref[...] = (bg * yconv).astype(BF16)
    uprev_ref[...] = tail

    sga_ref[...] = _sigmoid(proj(GA_OFF, D_MODEL)).astype(BF16)
    sgb_ref[...] = _sigmoid(proj(GB_OFF, D_MODEL)).astype(BF16)


def _front(x2d, mod_p, g_mix, w_in_bf16, conv_w, batch, seq):
    n_tok = x2d.shape[0]
    n_tiles = n_tok // TM
    tps = seq // TM
    kv2_blocks = ATT_TILE // TM
    out_shape = (
        jax.ShapeDtypeStruct((n_tok, ATTN_W), BF16),
        jax.ShapeDtypeStruct((n_tiles, 4, TM // 4, ATTN_W), BF16),
        jax.ShapeDtypeStruct((n_tiles, 16, TM // 16, ATTN_W), BF16),
        jax.ShapeDtypeStruct((n_tok, CONV_W), BF16),
        jax.ShapeDtypeStruct((n_tok, D_MODEL), BF16),
        jax.ShapeDtypeStruct((n_tok, D_MODEL), BF16),
        jax.ShapeDtypeStruct((batch, 2 * GROUP_W, 128), F32),
        jax.ShapeDtypeStruct((batch, 2 * GROUP_W, 512), F32),
        jax.ShapeDtypeStruct((batch, 2 * GROUP_W, 2048), F32),
        jax.ShapeDtypeStruct((batch, 8, CONV_W), F32),
    )
    out_specs = (
        pl.BlockSpec((TM, ATTN_W), lambda i: (i, 0)),
        pl.BlockSpec((1, 4, TM // 4, ATTN_W), lambda i: (i, 0, 0, 0)),
        pl.BlockSpec((1, 16, TM // 16, ATTN_W), lambda i: (i, 0, 0, 0)),
        pl.BlockSpec((TM, CONV_W), lambda i: (i, 0)),
        pl.BlockSpec((TM, D_MODEL), lambda i: (i, 0)),
        pl.BlockSpec((TM, D_MODEL), lambda i: (i, 0)),
        pl.BlockSpec((None, 2 * GROUP_W, 128), lambda i: (i // tps, 0, 0)),
        pl.BlockSpec((None, 2 * GROUP_W, TM), lambda i: (i // tps, 0, 0)),
        pl.BlockSpec((None, 2 * GROUP_W, TM),
                     lambda i: (i // tps, 0, jnp.maximum(i % tps - (tps - kv2_blocks), 0))),
        pl.BlockSpec((None, 8, CONV_W), lambda i: (i // tps, 0, 0)),
    )
    in_specs = [
        pl.BlockSpec((TM, D_MODEL), lambda i: (i, 0)),
        pl.BlockSpec((None, 1, 6 * D_MODEL), lambda i: (i // tps, 0, 0)),
        _resident((1, D_MODEL)),
        _resident((D_MODEL, IN_COLS)),
        _resident((3, CONV_W)),
    ]
    return pl.pallas_call(
        functools.partial(_front_body, tps),
        grid=(n_tiles,),
        in_specs=in_specs,
        out_specs=out_specs,
        out_shape=out_shape,
        scratch_shapes=[pltpu.VMEM((GROUP_W // LANES, TM, LANES), F32), pltpu.VMEM((8, CONV_W), F32)],
        compiler_params=pltpu.CompilerParams(dimension_semantics=("arbitrary",), vmem_limit_bytes=VMEM_LIMIT),
        name="prompt_front",
    )(x2d, mod_p, g_mix.reshape(1, -1), w_in_bf16, conv_w)


def _band_bias_table():
    qi = jnp.arange(BAND)[:, None]
    kc = jnp.arange(2 * BAND)[None, :]
    delta = qi - (kc - BAND)
    valid = (delta >= 0) & (delta <= BAND)
    tabs = []
    for first in (False, True):
        ok = valid & (kc >= BAND) if first else valid
        for g in range(N_GROUPS):
            for h in range(HEADS):
                b = -_alibi_slope(g, h) * (delta * DILATIONS[g]).astype(F32)
                tabs.append(jnp.where(ok, b, MASKED))
    return jnp.stack(tabs).astype(F32)


def _head_lane_mask(h, dtype=None):
    lane = lax.broadcasted_iota(jnp.int32, (1, GROUP_W), 1)
    return (lane >= h * HEAD_DIM) & (lane < (h + 1) * HEAD_DIM)


def _attn_unit(q, k, v, bias_ref, bias_base):
    masks = [_head_lane_mask(h) for h in range(HEADS)]
    qs = jnp.concatenate([jnp.where(hm, q, jnp.zeros_like(q)) for hm in masks], axis=0)
    s = lax.dot_general(qs, k, (((1,), (1,)), ((), ())), preferred_element_type=F32)
    s = s + bias_ref[pl.ds(bias_base, HEADS)].reshape(HEADS * BAND, 2 * BAND)
    m = jnp.max(s, axis=-1, keepdims=True)
    e = jnp.exp(s - m)
    l = jnp.sum(e, axis=-1, keepdims=True)
    p = (e * (1.0 / l)).astype(BF16)
    ov = jnp.dot(p, v, preferred_element_type=F32)
    lse = m + jnp.log(l)
    o = jnp.zeros((BAND, GROUP_W), F32)
    lse_b = jnp.zeros((BAND, GROUP_W), F32)
    for h, hm in enumerate(masks):
        rows = slice(h * BAND, (h + 1) * BAND)
        o = o + jnp.where(hm, ov[rows], 0.0)
        lse_b = lse_b + jnp.where(hm, lse[rows], 0.0)
    return o, lse_b


def _attn_body(a0c_ref, a0p_ref, a1c_ref, a1p_ref, a2c_ref, a2p_ref, bias_ref, o_ref, og_ref, lg_ref):
    first_tile = pl.program_id(1) == 0
    first_off = jnp.where(first_tile, N_GROUPS * HEADS, 0)
    qs, ks, vs = (slice(0, GROUP_W), slice(GROUP_W, 2 * GROUP_W), slice(2 * GROUP_W, 3 * GROUP_W))

    def run_blocks(blocks):
        loaded = [load() for _, _, _, load in blocks]
        results = [_attn_unit(q, k, v, bias_ref, base) for (_, _, base, _), (q, k, v) in zip(blocks, loaded)]
        for (g, rows, _, _), (o, lse_b) in zip(blocks, results):
            for c in range(GROUP_W // LANES):
                og_ref[g, c, rows, :] = o[:, c * LANES:(c + 1) * LANES]
                lg_ref[g, c, rows, :] = lse_b[:, c * LANES:(c + 1) * LANES]

    def g0_block(n0, base):
        def load():
            q = a0c_ref[pl.ds(n0, BAND), qs]
            if isinstance(n0, int) and n0 == 0:
                k = jnp.concatenate([a0p_ref[:, ks], a0c_ref[0:BAND, ks]], axis=0)
                v = jnp.concatenate([a0p_ref[:, vs], a0c_ref[0:BAND, vs]], axis=0)
            else:
                k = a0c_ref[pl.ds(n0 - BAND, 2 * BAND), ks]
                v = a0c_ref[pl.ds(n0 - BAND, 2 * BAND), vs]
            return q, k, v
        return (0, pl.ds(n0, BAND), base, load)

    run_blocks([g0_block(0, first_off), g0_block(BAND, 0)])

    def g0_loop(t, carry):
        n0 = pl.multiple_of(t * (ATTN_PAIR * BAND), BAND)
        run_blocks([g0_block(n0 + u * BAND, 0) for u in range(ATTN_PAIR)])
        return carry

    lax.fori_loop(1, ATT_TILE // (ATTN_PAIR * BAND), g0_loop, 0)

    def g1_block(jj, r, prev_ref, prev_jj, base):
        def load():
            q = a1c_ref[jj, r, :, qs]
            k = jnp.concatenate([prev_ref[prev_jj, r, :, ks], a1c_ref[jj, r, :, ks]], axis=0)
            v = jnp.concatenate([prev_ref[prev_jj, r, :, vs], a1c_ref[jj, r, :, vs]], axis=0)
            return q, k, v
        return (1, pl.ds(jj * (4 * BAND) + r, BAND, stride=4), base, load)

    def g1_first(t, carry):
        run_blocks([g1_block(0, t * ATTN_PAIR + u, a1p_ref, 0, first_off + HEADS) for u in range(ATTN_PAIR)])
        return carry

    lax.fori_loop(0, 4 // ATTN_PAIR, g1_first, 0)

    def g1_rest(t, carry):
        jj = 1 + (t * ATTN_PAIR) // 4
        r0 = (t * ATTN_PAIR) % 4
        run_blocks([g1_block(jj, r0 + u, a1c_ref, jj - 1, HEADS) for u in range(ATTN_PAIR)])
        return carry

    lax.fori_loop(0, 12 // ATTN_PAIR, g1_rest, 0)

    n_sub = a2c_ref.shape[0]

    def g2_rows(ref, r, cols):
        return jnp.concatenate([ref[t, r, :, cols] for t in range(n_sub)], axis=0)

    def g2_block(r):
        def load():
            q = g2_rows(a2c_ref, r, qs)
            k = jnp.concatenate([g2_rows(a2p_ref, r, ks), g2_rows(a2c_ref, r, ks)], axis=0)
            v = jnp.concatenate([g2_rows(a2p_ref, r, vs), g2_rows(a2c_ref, r, vs)], axis=0)
            return q, k, v
        return (2, pl.ds(r, BAND, stride=16), first_off + 2 * HEADS, load)

    def g2_loop(t, carry):
        run_blocks([g2_block(t * ATTN_PAIR + u) for u in range(ATTN_PAIR)])
        return carry

    lax.fori_loop(0, 16 // ATTN_PAIR, g2_loop, 0)

    def mix(c, carry):
        rows = pl.ds(pl.multiple_of(c * BAND, BAND), BAND)
        for half in range(GROUP_W // LANES):
            l0, l1, l2 = lg_ref[0, half, rows, :], lg_ref[1, half, rows, :], lg_ref[2, half, rows, :]
            mx = jnp.maximum(jnp.maximum(l0, l1), l2)
            w0, w1, w2 = jnp.exp(l0 - mx), jnp.exp(l1 - mx), jnp.exp(l2 - mx)
            num = w0 * og_ref[0, half, rows, :] + w1 * og_ref[1, half, rows, :] + w2 * og_ref[2, half, rows, :]
            o_ref[rows, half * LANES:(half + 1) * LANES] = (num / (w0 + w1 + w2)).astype(o_ref.dtype)
        return carry

    lax.fori_loop(0, ATT_TILE // BAND, mix, 0)


def _attention(a0, a1, a2, bias, batch, seq):
    steps = seq // ATT_TILE
    sub = ATT_TILE // TM
    a0 = a0.reshape(batch, seq, ATTN_W)
    a1 = a1.reshape(batch, seq // TM, 4, TM // 4, ATTN_W)
    a2 = a2.reshape(batch, seq // TM, 16, TM // 16, ATTN_W)
    in_specs = [
        pl.BlockSpec((None, ATT_TILE, ATTN_W), lambda b, j: (b, j, 0)),
        pl.BlockSpec((None, BAND, ATTN_W), lambda b, j: (b, jnp.maximum(j * (ATT_TILE // BAND) - 1, 0), 0)),
        pl.BlockSpec((None, sub, 4, TM // 4, ATTN_W), lambda b, j: (b, j, 0, 0, 0)),
        pl.BlockSpec((None, 1, 4, TM // 4, ATTN_W), lambda b, j: (b, jnp.maximum(j * sub - 1, 0), 0, 0, 0)),
        pl.BlockSpec((None, sub, 16, TM // 16, ATTN_W), lambda b, j: (b, j, 0, 0, 0)),
        pl.BlockSpec((None, sub, 16, TM // 16, ATTN_W), lambda b, j: (b, jnp.maximum(j - 1, 0), 0, 0, 0)),
        _resident(bias.shape),
    ]
    return pl.pallas_call(
        _attn_body,
        grid=(batch, steps),
        in_specs=in_specs,
        out_specs=pl.BlockSpec((None, ATT_TILE, GROUP_W), lambda b, j: (b, j, 0)),
        out_shape=jax.ShapeDtypeStruct((batch, seq, GROUP_W), BF16),
        scratch_shapes=[pltpu.VMEM((N_GROUPS, GROUP_W // LANES, ATT_TILE, LANES), F32)] * 2,
        compiler_params=pltpu.CompilerParams(dimension_semantics=("arbitrary", "arbitrary"),
                                             vmem_limit_bytes=VMEM_LIMIT),
        name="prompt_attention",
    )(a0, a0, a1, a1, a2, a2, bias)


def _route(logits):
    lane = lax.broadcasted_iota(jnp.int32, logits.shape, 1)
    big = jnp.int32(1 << 20)
    gmask = lane < N_EXPERT_GROUPS
    lg = jnp.where(gmask, logits, MASKED)
    gmax = jnp.max(lg, axis=-1, keepdims=True)
    gidx = jnp.min(jnp.where(gmask & (lg == gmax), lane, big), axis=-1, keepdims=True)
    p_top = 1.0 / jnp.sum(jnp.where(gmask, jnp.exp(lg - gmax), 0.0), axis=-1, keepdims=True)
    lo = EXPERT_LANE0 + EXPERTS_PER_GROUP * gidx
    emask = (lane >= lo) & (lane < lo + EXPERTS_PER_GROUP)
    le = jnp.where(emask, logits, MASKED)
    v1 = jnp.max(le, axis=-1, keepdims=True)
    i1 = jnp.min(jnp.where(emask & (le == v1), lane, big), axis=-1, keepdims=True)
    emask2 = emask & (lane != i1)
    le2 = jnp.where(emask2, logits, MASKED)
    v2 = jnp.max(le2, axis=-1, keepdims=True)
    i2 = jnp.min(jnp.where(emask2 & (le2 == v2), lane, big), axis=-1, keepdims=True)
    e2 = jnp.exp(v2 - v1)
    den = 1.0 + e2
    w1 = (1.0 / den) * p_top
    w2 = (e2 / den) * p_top
    id1 = (i1 - EXPERT_LANE0).astype(F32)
    id2 = (i2 - EXPERT_LANE0).astype(F32)
    return jnp.where(lane == RT_W1, w1, jnp.where(lane == RT_W2, w2,
                     jnp.where(lane == RT_E1, id1, jnp.where(lane == RT_E2, id2, 0.0))))


def _back_body(oa_ref, oc_ref, sga_ref, sgb_ref, x_ref, mod_ref, wpa_ref, wpb_ref, wo_ref, g_ref,
               wrh_ref, wrl_ref, br_ref, x1_ref, h2_ref, comb_ref):
    pa = jnp.dot(oa_ref[...], wpa_ref[...], preferred_element_type=F32)
    pb = jnp.dot(oc_ref[...], wpb_ref[...], preferred_element_type=F32)
    merged = sga_ref[...].astype(F32) * pa + sgb_ref[...].astype(F32) * pb
    gate1 = mod_ref[:, 2 * D_MODEL:3 * D_MODEL]
    shift2 = mod_ref[:, 3 * D_MODEL:4 * D_MODEL]
    scale2 = mod_ref[:, 4 * D_MODEL:5 * D_MODEL]
    x1 = x_ref[...] + gate1 * jnp.dot(merged.astype(BF16), wo_ref[...], preferred_element_type=F32)
    x1_ref[...] = x1
    h2 = _rmsnorm(x1, g_ref[...]) * (1.0 + scale2) + shift2
    hi = h2.astype(BF16)
    h2_ref[...] = hi
    lo = (h2 - hi.astype(F32)).astype(BF16)
    logits = (jnp.dot(hi, wrh_ref[...], preferred_element_type=F32)
              + jnp.dot(lo, wrh_ref[...], preferred_element_type=F32)
              + jnp.dot(hi, wrl_ref[...], preferred_element_type=F32)) + br_ref[...]
    comb_ref[...] = _route(logits)


def _back(o_attn, oconv, sga, sgb, x2d, mod_p, wpa, wpb, wo, g_ffn, wr_hi, wr_lo, b_r, seq):
    n_tok = x2d.shape[0]
    n_all = n_tok + TM
    tps = seq // TM
    row = lambda w: pl.BlockSpec((TM, w), lambda i: (i, 0))
    return pl.pallas_call(
        _back_body,
        grid=(n_tok // TM,),
        in_specs=[row(GROUP_W), row(CONV_W), row(D_MODEL), row(D_MODEL), row(D_MODEL),
                  pl.BlockSpec((None, 1, 6 * D_MODEL), lambda i: (i // tps, 0, 0)),
                  _resident(wpa.shape), _resident(wpb.shape), _resident(wo.shape), _resident((1, D_MODEL)),
                  _resident(wr_hi.shape), _resident(wr_lo.shape), _resident(b_r.shape)],
        out_specs=(row(D_MODEL), row(D_MODEL), row(ROUTER_LANES)),
        out_shape=(jax.ShapeDtypeStruct((n_all, D_MODEL), F32),
                   jax.ShapeDtypeStruct((n_all, D_MODEL), BF16),
                   jax.ShapeDtypeStruct((n_all, ROUTER_LANES), F32)),
        compiler_params=pltpu.CompilerParams(dimension_semantics=("arbitrary",), vmem_limit_bytes=VMEM_LIMIT),
        name="prompt_back",
    )(o_attn, oconv, sga, sgb, x2d, mod_p, wpa, wpb, wo, g_ffn.reshape(1, -1), wr_hi, wr_lo, b_r)


def _plan_body(rt_ref, rank_ref, cnt_ref, base_ref):
    @pl.when(pl.program_id(0) == 0)
    def _():
        base_ref[...] = jnp.zeros_like(base_ref)

    rt = rt_ref[...]
    lane = lax.broadcasted_iota(jnp.int32, (TM, LANES), 1)
    lane_f = lane.astype(F32)
    oh1 = jnp.where(lane_f == rt[:, RT_E1:RT_E1 + 1], 1.0, 0.0)
    oh2 = jnp.where(lane_f == rt[:, RT_E2:RT_E2 + 1], 1.0, 0.0)
    before = jnp.where(lax.broadcasted_iota(jnp.int32, (TM, TM), 1) < lax.broadcasted_iota(jnp.int32, (TM, TM), 0),
                       1.0, 0.0).astype(BF16)
    p1 = jnp.dot(before, oh1.astype(BF16), preferred_element_type=F32)
    p2 = jnp.dot(before, oh2.astype(BF16), preferred_element_type=F32)
    c1 = jnp.sum(oh1, axis=0, keepdims=True)
    c2 = jnp.sum(oh2, axis=0, keepdims=True)
    base = base_ref[...]
    rank1 = jnp.sum(oh1 * (base + p1), axis=-1, keepdims=True)
    rank2 = jnp.sum(oh2 * (base + c1 + p2), axis=-1, keepdims=True)
    rank_ref[...] = jnp.where(lane == 0, rank1, jnp.where(lane == 1, rank2, 0.0))
    total = base + c1 + c2
    base_ref[...] = total
    cnt_ref[...] = total


def _plan(rt):
    n_all = rt.shape[0]
    return pl.pallas_call(
        _plan_body,
        grid=(n_all // TM,),
        in_specs=[pl.BlockSpec((TM, ROUTER_LANES), lambda i: (i, 0))],
        out_specs=(pl.BlockSpec((TM, LANES), lambda i: (i, 0)), pl.BlockSpec((1, LANES), lambda i: (0, 0))),
        out_shape=(jax.ShapeDtypeStruct((n_all, LANES), F32), jax.ShapeDtypeStruct((1, LANES), F32)),
        scratch_shapes=[pltpu.VMEM((1, LANES), F32)],
        compiler_params=pltpu.CompilerParams(dimension_semantics=("arbitrary",), vmem_limit_bytes=VMEM_LIMIT),
        name="moe_plan",
    )(rt)


def _valid_chunks(tile, n_prompt_tiles, n_sample):
    return jnp.where(tile < n_prompt_tiles, TM // DMA_CHUNK, n_sample // DMA_CHUNK)


def _dispatch_body(n_prompt_tiles, n_sample, dest_ref, tail_ref, h_ref, xs_ref, stage_ref, zero_ref, sem):
    i = pl.program_id(0)

    def tail_copy(e):
        start = pl.multiple_of(tail_ref[e] * ROW_SUB, ROW_SUB)
        return pltpu.make_async_copy(zero_ref, xs_ref.at[pl.ds(start, EXPERT_TILE * ROW_SUB)], sem)

    @pl.when(i == 0)
    def _():
        zero_ref[...] = jnp.zeros_like(zero_ref)
        for e in range(N_EXPERTS):
            @pl.when(tail_ref[e] >= 0)
            def _():
                tail_copy(e).start()
        for e in range(N_EXPERTS):
            @pl.when(tail_ref[e] >= 0)
            def _():
                tail_copy(e).wait()

    for j in range(ROW_SUB):
        stage_ref[pl.ds(j, TM, stride=ROW_SUB), :] = h_ref[:, j * LANES:(j + 1) * LANES].astype(F32)
    n_chunks = _valid_chunks(i, n_prompt_tiles, n_sample)

    def issue(c, carry):
        for j in range(DMA_CHUNK):
            r = c * DMA_CHUNK + j
            src = stage_ref.at[pl.ds(pl.multiple_of(r * ROW_SUB, ROW_SUB), ROW_SUB)]
            for k in range(2):
                d = pl.multiple_of(dest_ref[(i * TM + r) * 2 + k], ROW_SUB)
                pltpu.make_async_copy(src, xs_ref.at[pl.ds(d, ROW_SUB)], sem).start(priority=k)
        return carry

    lax.fori_loop(0, n_chunks, issue, 0)

    def drain(c, carry):
        n = 2 * DMA_CHUNK * ROW_SUB
        pltpu.make_async_copy(stage_ref.at[pl.ds(0, n)], xs_ref.at[pl.ds(0, n)], sem).wait()
        return carry

    lax.fori_loop(0, n_chunks, drain, 0)


def _dispatch(dest, tail_start, h2, n_rows, n_prompt_tiles, n_sample):
    n_all = h2.shape[0]
    return pl.pallas_call(
        functools.partial(_dispatch_body, n_prompt_tiles, n_sample),
        grid_spec=pltpu.PrefetchScalarGridSpec(
            num_scalar_prefetch=2,
            grid=(n_all // TM,),
            in_specs=[pl.BlockSpec((TM, D_MODEL), lambda i, *_: (i, 0))],
            out_specs=pl.BlockSpec(memory_space=pl.ANY),
            scratch_shapes=[pltpu.VMEM((TM * ROW_SUB, LANES), F32), pltpu.VMEM((EXPERT_TILE * ROW_SUB, LANES), F32),
                            pltpu.SemaphoreType.DMA],
        ),
        out_shape=jax.ShapeDtypeStruct((n_rows * ROW_SUB, LANES), F32),
        compiler_params=pltpu.CompilerParams(dimension_semantics=("arbitrary",), vmem_limit_bytes=VMEM_LIMIT,
                                             disable_bounds_checks=True),
        name="moe_dispatch",
    )(dest, tail_start, h2)


def _experts_body(te_ref, nt_ref, xs_ref, wg_ref, wu_ref, wd_ref, o_ref):
    @pl.when(pl.program_id(0) < nt_ref[0])
    def _():
        x = jnp.concatenate([xs_ref[pl.ds(j, EXPERT_TILE, stride=ROW_SUB), :].astype(BF16) for j in range(ROW_SUB)],
                            axis=1)
        a = jnp.dot(x, wg_ref[0].astype(BF16), preferred_element_type=F32)
        z = (a * _sigmoid(a)) * jnp.dot(x, wu_ref[0].astype(BF16), preferred_element_type=F32)
        o = jnp.dot(z.astype(BF16), wd_ref[0].astype(BF16), preferred_element_type=F32)
        for j in range(ROW_SUB):
            o_ref[pl.ds(j, EXPERT_TILE, stride=ROW_SUB), :] = o[:, j * LANES:(j + 1) * LANES]


def _experts(tile_expert, n_tiles, xs, w_gate_e, w_up_e, w_down_e):
    max_tiles = tile_expert.shape[0]
    rows = lambda s, te, nt: (jnp.minimum(s, nt[0] - 1), 0)
    row_tile = pl.BlockSpec((EXPERT_TILE * ROW_SUB, LANES), rows)
    weight = lambda shape: pl.BlockSpec((1,) + shape, lambda s, te, nt: (te[jnp.minimum(s, nt[0] - 1)], 0, 0))
    return pl.pallas_call(
        _experts_body,
        grid_spec=pltpu.PrefetchScalarGridSpec(
            num_scalar_prefetch=2,
            grid=(max_tiles,),
            in_specs=[row_tile,
                      weight((D_MODEL, EXPERT_FF)), weight((D_MODEL, EXPERT_FF)), weight((EXPERT_FF, D_MODEL))],
            out_specs=row_tile,
        ),
        out_shape=jax.ShapeDtypeStruct(xs.shape, F32),
        compiler_params=pltpu.CompilerParams(dimension_semantics=("arbitrary",), vmem_limit_bytes=VMEM_LIMIT),
        name="moe_experts",
    )(tile_expert, n_tiles, xs, w_gate_e, w_up_e, w_down_e)


def _combine_body(n_prompt_tiles, n_sample, dest_ref, x1_ref, rt_ref, gp_ref, gs_ref, gf_ref, eo_ref,
                  yp_ref, ys_ref, rows_ref, sem):
    i = pl.program_id(0)
    n_chunks = _valid_chunks(i, n_prompt_tiles, n_sample)

    def issue(c, carry):
        for j in range(DMA_CHUNK):
            r = c * DMA_CHUNK + j
            row = pl.ds(pl.multiple_of(r * ROW_SUB, ROW_SUB), ROW_SUB)
            for k in range(2):
                d = pl.multiple_of(dest_ref[(i * TM + r) * 2 + k], ROW_SUB)
                pltpu.make_async_copy(eo_ref.at[pl.ds(d, ROW_SUB)], rows_ref.at[k, row], sem).start(priority=k)
        return carry

    lax.fori_loop(0, n_chunks, issue, 0)

    def drain(c, carry):
        n = 2 * DMA_CHUNK * ROW_SUB
        pltpu.make_async_copy(eo_ref.at[pl.ds(0, n)], rows_ref.at[0, pl.ds(0, n)], sem).wait()
        return carry

    lax.fori_loop(0, n_chunks, drain, 0)

    rt = rt_ref[...]
    w1, w2 = rt[:, RT_W1:RT_W1 + 1], rt[:, RT_W2:RT_W2 + 1]
    lane_tile = lambda k, j: rows_ref[k, pl.ds(j, TM, stride=ROW_SUB), :]
    moe = jnp.concatenate([w1 * lane_tile(0, j) + w2 * lane_tile(1, j) for j in range(ROW_SUB)], axis=1)
    is_prompt = i < n_prompt_tiles
    gate2 = jnp.where(is_prompt, gp_ref[...], gs_ref[...])
    y = _rmsnorm(x1_ref[...] + gate2 * moe, gf_ref[...])

    @pl.when(is_prompt)
    def _():
        yp_ref[...] = y

    @pl.when(jnp.logical_not(is_prompt))
    def _():
        ys_ref[...] = y[:n_sample, :]


def _combine(dest, x1, rt, gate2_p, gate2_s, g_final, eo, n_prompt_tiles, n_sample, tiles_per_seq):
    n_all = x1.shape[0]
    last_p = n_prompt_tiles - 1
    return pl.pallas_call(
        functools.partial(_combine_body, n_prompt_tiles, n_sample),
        grid_spec=pltpu.PrefetchScalarGridSpec(
            num_scalar_prefetch=1,
            grid=(n_all // TM,),
            in_specs=[pl.BlockSpec((TM, D_MODEL), lambda i, *_: (i, 0)),
                      pl.BlockSpec((TM, ROUTER_LANES), lambda i, *_: (i, 0)),
                      pl.BlockSpec((None, 1, D_MODEL), lambda i, *_: (jnp.minimum(i, last_p) // tiles_per_seq, 0, 0)),
                      pl.BlockSpec((TM, D_MODEL), lambda i, *_: (0, 0)),
                      pl.BlockSpec((1, D_MODEL), lambda i, *_: (0, 0)),
                      pl.BlockSpec(memory_space=pl.ANY)],
            out_specs=(pl.BlockSpec((TM, D_MODEL), lambda i, *_: (jnp.minimum(i, last_p), 0)),
                       pl.BlockSpec((n_sample, D_MODEL), lambda i, *_: (0, 0))),
            scratch_shapes=[pltpu.VMEM((2, TM * ROW_SUB, LANES), F32), pltpu.SemaphoreType.DMA],
        ),
        out_shape=(jax.ShapeDtypeStruct((n_prompt_tiles * TM, D_MODEL), F32),
                   jax.ShapeDtypeStruct((n_sample, D_MODEL), F32)),
        compiler_params=pltpu.CompilerParams(dimension_semantics=("arbitrary",), vmem_limit_bytes=VMEM_LIMIT,
                                             disable_bounds_checks=True),
        name="moe_combine",
    )(dest, x1, rt, gate2_p, gate2_s, g_final.reshape(1, -1), eo)


def _routing_tables(rt, rank, counts, max_tiles):
    cnt = counts[0, :N_EXPERTS].astype(jnp.int32)
    padded = (cnt + EXPERT_TILE - 1) // EXPERT_TILE * EXPERT_TILE
    ends = jnp.cumsum(padded)
    starts = ends - padded
    n_tiles = (ends[-1] // EXPERT_TILE).reshape(1)
    tile_row0 = jnp.arange(max_tiles, dtype=jnp.int32) * EXPERT_TILE
    tile_expert = jnp.minimum(jnp.sum(ends[None, :] <= tile_row0[:, None], axis=1), N_EXPERTS - 1).astype(jnp.int32)
    tail_start = jnp.where(cnt > 0, ends - EXPERT_TILE, -1).astype(jnp.int32)
    eid = rt[:, RT_E1:RT_E2 + 1].astype(jnp.int32)
    onehot = eid[..., None] == jnp.arange(N_EXPERTS, dtype=jnp.int32)
    dest = rank[:, 0:2].astype(jnp.int32) + jnp.sum(jnp.where(onehot, starts, 0), axis=-1)
    dest = (jnp.where(eid >= 0, dest, 0) * ROW_SUB).reshape(-1)
    return dest, tail_start, tile_expert, n_tiles


def _sample_front_body(x_ref, mod_ref, g_ref, w_ref, o_ref):
    shift1 = mod_ref[:, 0:D_MODEL]
    scale1 = mod_ref[:, D_MODEL:2 * D_MODEL]
    h = _rmsnorm(x_ref[...], g_ref[...]) * (1.0 + scale1) + shift1
    o_ref[...] = jnp.dot(h, w_ref[...], precision=HIGHEST, preferred_element_type=F32)


def _sample_front(x_s, mod_s, g_mix, w_in):
    nb = x_s.shape[0]
    tn = 256
    return pl.pallas_call(
        _sample_front_body,
        grid=(IN_COLS // tn,),
        in_specs=[pl.BlockSpec((nb, D_MODEL), lambda j: (0, 0)),
                  pl.BlockSpec((nb, 2 * D_MODEL), lambda j: (0, 0)),
                  pl.BlockSpec((1, D_MODEL), lambda j: (0, 0)),
                  pl.BlockSpec((D_MODEL, tn), lambda j: (0, j))],
        out_specs=pl.BlockSpec((nb, tn), lambda j: (0, j)),
        out_shape=jax.ShapeDtypeStruct((nb, IN_COLS), F32),
        compiler_params=pltpu.CompilerParams(dimension_semantics=("arbitrary",), vmem_limit_bytes=VMEM_LIMIT),
        name="sample_front",
    )(x_s, mod_s, g_mix.reshape(1, -1), w_in)


def _sample_attn_body(qkv_ref, c0_ref, c1_ref, c2_ref, oa_ref):
    head = lax.broadcasted_iota(jnp.int32, (HEADS, 1, 1), 0)
    for b in range(qkv_ref.shape[0]):
        outs, lses = [], []
        for g, c_ref in enumerate((c0_ref, c1_ref, c2_ref)):
            window = c_ref.shape[-1]
            def new_rows(part):
                c0 = (part * N_GROUPS + g) * HEADS
                return jnp.stack([qkv_ref[b, :, c0 + h:c0 + h + 1] for h in range(HEADS)], axis=0)

            q = new_rows(0) * (HEAD_DIM ** -0.5)
            k_new = new_rows(1)
            v_new = new_rows(2)
            s = jnp.sum(c_ref[b, 0] * q, axis=1, keepdims=True)
            pos = lax.broadcasted_iota(jnp.int32, (1, 1, window), 2)
            slope = jnp.full((HEADS, 1, 1), _alibi_slope(g, HEADS - 1), F32)
            for h in range(HEADS - 1):
                slope = jnp.where(head == h, _alibi_slope(g, h), slope)
            on_band = (pos & (DILATIONS[g] - 1)) == 0
            s = jnp.where(on_band, s - slope * (window - pos).astype(F32), MASKED)
            s_self = jnp.sum(q * k_new, axis=1, keepdims=True)
            m = jnp.maximum(jnp.max(s, axis=2, keepdims=True), s_self)
            e = jnp.exp(s - m)
            e_self = jnp.exp(s_self - m)
            l = jnp.sum(e, axis=2, keepdims=True) + e_self
            o = jnp.sum(c_ref[b, 1] * e, axis=2, keepdims=True) + e_self * v_new
            outs.append(o / l)
            lses.append(m + jnp.log(l))
        mx = jnp.maximum(jnp.maximum(lses[0], lses[1]), lses[2])
        w = [jnp.exp(x - mx) for x in lses]
        mixed = (w[0] * outs[0] + w[1] * outs[1] + w[2] * outs[2]) / (w[0] + w[1] + w[2])
        for h in range(HEADS):
            oa_ref[b, :, h:h + 1] = mixed[h]


def _sample_attention(qkv_s, cache0, cache1, cache2):
    nb = qkv_s.shape[0]
    bb = 2
    hd = (HEADS, HEAD_DIM)
    caches = [jnp.transpose(c, (0, 1, 3, 4, 5, 2)) for c in (cache0, cache1, cache2)]
    cache_spec = lambda c: pl.BlockSpec((None, bb, 2, *hd, c.shape[-1]), lambda i: (0, i, 0, 0, 0, 0))
    return pl.pallas_call(
        _sample_attn_body,
        grid=(nb // bb,),
        in_specs=[pl.BlockSpec((bb,) + qkv_s.shape[1:], lambda i: (i, 0, 0))] + [cache_spec(c) for c in caches],
        out_specs=pl.BlockSpec((bb, HEAD_DIM, HEADS), lambda i: (i, 0, 0)),
        out_shape=jax.ShapeDtypeStruct((nb, HEAD_DIM, HEADS), F32),
        compiler_params=pltpu.CompilerParams(dimension_semantics=("arbitrary",), vmem_limit_bytes=VMEM_LIMIT),
        name="sample_attention",
    )(qkv_s, *caches)


def _sample_back_body(p_ref, oa_ref, cc_ref, x_ref, mod_ref, cw_ref, wpa_ref, wpb_ref, wo_ref, g_ref,
                      wr_ref, br_ref, x1_any, h2_any, rt_any, x1_ref, h2_ref, rt_ref, sconv_ref):
    del x1_any, h2_any, rt_any
    nb = x_ref.shape[0]
    hdot = functools.partial(jnp.dot, precision=HIGHEST, preferred_element_type=F32)
    bg = p_ref[:, BG_OFF:BG_OFF + CONV_W]
    u = p_ref[:, CG_OFF:CG_OFF + CONV_W] * p_ref[:, UI_OFF:UI_OFF + CONV_W]
    c_old, c_new = cc_ref[:, 0, :], cc_ref[:, 1, :]
    yconv = cw_ref[0:1, :] * c_old + cw_ref[1:2, :] * c_new + cw_ref[2:3, :] * u
    sconv_ref[:, 0, :] = c_new
    sconv_ref[:, 1, :] = u
    sga = _sigmoid(p_ref[:, GA_OFF:GA_OFF + D_MODEL])
    sgb = _sigmoid(p_ref[:, GB_OFF:GB_OFF + D_MODEL])
    merged = sga * hdot(oa_ref[...], wpa_ref[...]) + sgb * hdot(bg * yconv, wpb_ref[...])
    gate1 = mod_ref[:, 2 * D_MODEL:3 * D_MODEL]
    shift2 = mod_ref[:, 3 * D_MODEL:4 * D_MODEL]
    scale2 = mod_ref[:, 4 * D_MODEL:5 * D_MODEL]
    x1 = x_ref[...] + gate1 * hdot(merged, wo_ref[...])
    h2 = _rmsnorm(x1, g_ref[...]) * (1.0 + scale2) + shift2
    rt = _route(hdot(h2, wr_ref[...]) + br_ref[...])
    lane = lax.broadcasted_iota(jnp.int32, (TM, ROUTER_LANES), 1)
    x1_ref[...] = jnp.zeros_like(x1_ref)
    h2_ref[...] = jnp.zeros_like(h2_ref)
    rt_ref[...] = jnp.where((lane == RT_E1) | (lane == RT_E2), -1.0, 0.0)
    x1_ref[0:nb, :] = x1
    h2_ref[0:nb, :] = h2.astype(BF16)
    rt_ref[0:nb, :] = rt


def _sample_back(proj_s, oa_s, cache_conv, x_s, mod_s, conv_w, w_pa, w_pb, w_o, g_ffn, w_r, b_r, x1, h2, rt):
    nb = x_s.shape[0]
    last = x1.shape[0] // TM - 1
    args = (proj_s, oa_s, cache_conv, x_s, mod_s, conv_w, w_pa, w_pb, w_o, g_ffn.reshape(1, -1), w_r, b_r)
    full = lambda shape: pl.BlockSpec(shape, lambda i: (0,) * len(shape))
    last_tile = lambda a: pl.BlockSpec((TM, a.shape[1]), lambda i: (last, 0))
    shape_of = lambda a: jax.ShapeDtypeStruct(a.shape, a.dtype)
    return pl.pallas_call(
        _sample_back_body,
        grid=(1,),
        in_specs=[full(a.shape) for a in args] + [pl.BlockSpec(memory_space=pl.ANY)] * 3,
        out_specs=(last_tile(x1), last_tile(h2), last_tile(rt), full((nb, 2, CONV_W))),
        out_shape=(shape_of(x1), shape_of(h2), shape_of(rt), jax.ShapeDtypeStruct((nb, 2, CONV_W), F32)),
        input_output_aliases={len(args): 0, len(args) + 1: 1, len(args) + 2: 2},
        compiler_params=pltpu.CompilerParams(dimension_semantics=("arbitrary",), vmem_limit_bytes=VMEM_LIMIT),
        name="sample_back",
    )(*args, x1, h2, rt)


def kernel(x_prompt, x_sample, cache_kv_w128, cache_kv_w512, cache_kv_w2048, cache_conv, c_prompt, c_sample,
           g_mix, w_ada, b_ada, w_in, conv_w, w_pa, w_pb, w_o, g_ffn, w_router_group, b_router_group,
           w_router_expert, b_router_expert, w_gate_e, w_up_e, w_down_e, g_final):
    batch, seq, _ = x_prompt.shape
    nb = x_sample.shape[0]
    assert x_sample.shape[1] == 1 and g_mix.shape[0] == 1, "one layer, one new sample token per sequence"
    assert seq % ATT_TILE == 0 and cache_kv_w128.shape[2] == 128 and cache_kv_w512.shape[2] == 512 \
        and cache_kv_w2048.shape[2] == 2048
    (g_mix, w_ada, b_ada, w_in, conv_w, w_pa, w_pb, w_o, g_ffn, w_rg, b_rg, w_re, b_re, w_gate_e, w_up_e,
     w_down_e) = (a[0] for a in (g_mix, w_ada, b_ada, w_in, conv_w, w_pa, w_pb, w_o, g_ffn, w_router_group,
                                 b_router_group, w_router_expert, b_router_expert, w_gate_e, w_up_e, w_down_e))

    c_all = jnp.concatenate([c_prompt, jnp.zeros((8 - batch, D_MODEL), F32), c_sample], axis=0)
    mod = _modulation(c_all, w_ada, b_ada)
    mod_p = mod[:batch].reshape(batch, 1, 6 * D_MODEL)
    mod_s = mod[8:]

    w_r = jnp.zeros((D_MODEL, ROUTER_LANES), F32)
    w_r = w_r.at[:, :N_EXPERT_GROUPS].set(w_rg).at[:, EXPERT_LANE0:EXPERT_LANE0 + N_EXPERTS].set(w_re)
    b_r = jnp.full((1, ROUTER_LANES), MASKED, F32)
    b_r = b_r.at[0, :N_EXPERT_GROUPS].set(b_rg).at[0, EXPERT_LANE0:EXPERT_LANE0 + N_EXPERTS].set(b_re)
    wr_hi = w_r.astype(BF16)
    wr_lo = (w_r - wr_hi.astype(F32)).astype(BF16)

    x2d = x_prompt.reshape(batch * seq, D_MODEL)
    (a0, a1, a2, oconv, sga, sgb, kv0, kv1, kv2, pconv) = _front(
        x2d, mod_p, g_mix, w_in.astype(BF16), conv_w, batch, seq)
    o_attn = _attention(a0, a1, a2, _band_bias_table(), batch, seq).reshape(batch * seq, GROUP_W)
    x1, h2, rt = _back(o_attn, oconv, sga, sgb, x2d, mod_p, w_pa.astype(BF16), w_pb.astype(BF16),
                       w_o.astype(BF16), g_ffn, wr_hi, wr_lo, b_r, seq)

    x_s = x_sample.reshape(nb, D_MODEL)
    proj_s = _sample_front(x_s, mod_s[:, :2 * D_MODEL], g_mix, w_in)
    qkv_s = proj_s[:, :3 * ATTN_W].reshape(nb, 3, N_GROUPS, HEADS, HEAD_DIM)
    qkv_t = jnp.transpose(qkv_s.reshape(nb, 3 * N_GROUPS * HEADS, HEAD_DIM), (0, 2, 1))
    oa_s = _sample_attention(qkv_t, cache_kv_w128, cache_kv_w512, cache_kv_w2048)
    oa_s = jnp.transpose(oa_s, (0, 2, 1)).reshape(nb, GROUP_W)
    skv0, skv1, skv2 = (jnp.stack([qkv_s[:, 1, g], qkv_s[:, 2, g]], axis=1) for g in range(N_GROUPS))
    x1, h2, rt, sconv = _sample_back(proj_s, oa_s, cache_conv[0], x_s, mod_s, conv_w,
                                     w_pa, w_pb, w_o, g_ffn, w_r, b_r, x1, h2, rt)

    n_prompt_tiles = batch * seq // TM
    max_tiles = -(-(2 * (batch * seq + nb) + N_EXPERTS * (EXPERT_TILE - 1)) // EXPERT_TILE)
    rank, counts = _plan(rt)
    dest, tail_start, tile_expert, n_tiles = _routing_tables(rt, rank, counts, max_tiles)
    xs = _dispatch(dest, tail_start, h2, max_tiles * EXPERT_TILE, n_prompt_tiles, nb)
    eo = _experts(tile_expert, n_tiles, xs, w_gate_e, w_up_e, w_down_e)
    gate2_s = jnp.zeros((TM, D_MODEL), F32).at[:nb].set(mod_s[:, 5 * D_MODEL:])
    y_prompt, y_sample = _combine(dest, x1, rt, mod_p[:, :, 5 * D_MODEL:], gate2_s, g_final, eo,
                                  n_prompt_tiles, nb, seq // TM)

    def prompt_state(a):
        a = a.reshape(batch, 2, HEADS, HEAD_DIM, a.shape[-1])
        return jnp.transpose(a, (0, 4, 1, 2, 3))[None]

    sample_state = lambda a: a.reshape(1, nb, 1, 2, HEADS, HEAD_DIM)
    return (y_prompt.reshape(batch, seq, D_MODEL), y_sample.reshape(nb, 1, D_MODEL),
            prompt_state(kv0), prompt_state(kv1), prompt_state(kv2),
            pconv[:, 6:8, :].reshape(1, batch, 2, CONV_W),
            sample_state(skv0), sample_state(skv1), sample_state(skv2),
            sconv.reshape(1, nb, 2, CONV_W))
```

```python
import functools
import math

import jax
import jax.numpy as jnp
from jax import lax
from jax.experimental import pallas as pl
from jax.experimental.pallas import tpu as pltpu

F32 = jnp.float32
BF16 = jnp.bfloat16
HIGHEST = lax.Precision.HIGHEST

D_MODEL = 1024
HEAD_DIM = 64
HEADS = 4
N_GROUPS = 3
GROUP_W = HEADS * HEAD_DIM
ATTN_W = N_GROUPS * GROUP_W
CONV_W = 512
DILATIONS = (1, 4, 16)
BAND = 128
N_EXPERTS = 32
EXPERTS_PER_GROUP = 8
N_EXPERT_GROUPS = 4
EXPERT_FF = 256
RMS_EPS = 1e-6
MASKED = -1e30

Q_OFF, K_OFF, V_OFF = 0, ATTN_W, 2 * ATTN_W
BG_OFF = 3 * ATTN_W
CG_OFF = BG_OFF + CONV_W
UI_OFF = CG_OFF + CONV_W
GA_OFF = UI_OFF + CONV_W
GB_OFF = GA_OFF + D_MODEL
IN_COLS = GB_OFF + D_MODEL

LANES = 128
ROUTER_LANES = 128
EXPERT_LANE0 = N_EXPERT_GROUPS

TM = 512
ATT_TILE = 2048
ATTN_PAIR = 2
EXPERT_TILE = 512
RT_W1, RT_W2, RT_E1, RT_E2, RT_R1, RT_R2 = 0, 1, 2, 3, 4, 5
ROW_SUB = D_MODEL // LANES
DMA_CHUNK = 8
VMEM_LIMIT = 56 * 1024 * 1024


def _sigmoid(x):
    return 1.0 / (1.0 + jnp.exp(-x))


def _rmsnorm(x, g):
    return x * lax.rsqrt(jnp.mean(x * x, axis=-1, keepdims=True) + RMS_EPS) * g


def _alibi_slope(g, h):
    return 2.0 ** (-8.0 * (g * HEADS + h + 1) / (N_GROUPS * HEADS))


def _resident(shape):
    nd = len(shape)
    return pl.BlockSpec(shape, lambda *_: (0,) * nd, pipeline_mode=pl.Buffered(1))


def _mod_body(c_ref, w_ref, b_ref, o_ref):
    c = c_ref[...]
    s = c * _sigmoid(c)
    o_ref[...] = jnp.dot(s, w_ref[...], precision=HIGHEST, preferred_element_type=F32) + b_ref[...]


def _modulation(c_all, w_ada, b_ada):
    rows = c_all.shape[0]
    tn = 1024
    return pl.pallas_call(
        _mod_body,
        grid=(6 * D_MODEL // tn,),
        in_specs=[pl.BlockSpec((rows, D_MODEL), lambda j: (0, 0)),
                  pl.BlockSpec((D_MODEL, tn), lambda j: (0, j)),
                  pl.BlockSpec((1, tn), lambda j: (0, j))],
        out_specs=pl.BlockSpec((rows, tn), lambda j: (0, j)),
        out_shape=jax.ShapeDtypeStruct((rows, 6 * D_MODEL), F32),
        compiler_params=pltpu.CompilerParams(dimension_semantics=("arbitrary",), vmem_limit_bytes=VMEM_LIMIT),
        name="modulation",
    )(c_all, w_ada, b_ada.reshape(1, -1))


def _front_body(tiles_per_seq, x_ref, mod_ref, g_ref, w_ref, cw_ref,
                a0_ref, a1_ref, a2_ref, oconv_ref, sga_ref, sgb_ref,
                kv0_ref, kv1_ref, kv2_ref, pconv_ref, res_ref, uprev_ref):
    t_in_seq = pl.program_id(0) % tiles_per_seq
    x = x_ref[...]
    shift1 = mod_ref[:, 0:D_MODEL]
    scale1 = mod_ref[:, D_MODEL:2 * D_MODEL]
    h = (_rmsnorm(x, g_ref[...]) * (1.0 + scale1) + shift1).astype(BF16)

    def proj(c0, n):
        return jnp.dot(h, w_ref[:, c0:c0 + n], preferred_element_type=F32)

    a_refs = (a0_ref, a1_ref, a2_ref)
    kv_refs = (kv0_ref, kv1_ref, kv2_ref)
    for g in range(N_GROUPS):
        d = DILATIONS[g]
        n = TM // d
        for part, base in enumerate((Q_OFF, K_OFF, V_OFF)):
            r = proj(base + g * GROUP_W, GROUP_W)
            cols = slice(part * GROUP_W, (part + 1) * GROUP_W)
            if part == 0:
                r = r * (HEAD_DIM ** -0.5)
            else:
                kvc = slice((part - 1) * GROUP_W, part * GROUP_W)
                kv_refs[g][kvc, :] = (r[TM - BAND:, :] if g == 0 else r).T
            if g == 0:
                a_refs[g][:, cols] = r.astype(BF16)
            else:
                for c in range(GROUP_W // LANES):
                    res_ref[c] = r[:, c * LANES:(c + 1) * LANES]
                for rr in range(d):
                    for c in range(GROUP_W // LANES):
                        c0 = part * GROUP_W + c * LANES
                        a_refs[g][0, rr, :, c0:c0 + LANES] = res_ref[c, pl.ds(rr, n, stride=d), :].astype(BF16)

    bg = proj(BG_OFF, CONV_W)
    u = proj(CG_OFF, CONV_W) * proj(UI_OFF, CONV_W)
    tail = u[TM - 8:, :]
    pconv_ref[...] = tail
    prev = jnp.where(t_in_seq == 0, 0.0, uprev_ref[...])
    row = lax.broadcasted_iota(jnp.int32, (TM, 1), 0)
    u1 = jnp.where(row == 0, prev[7:8, :], pltpu.roll(u, 1, axis=0))
    u2 = jnp.where(row == 0, prev[6:7, :], jnp.where(row == 1, prev[7:8, :], pltpu.roll(u, 2, axis=0)))
    yconv = cw_ref[0:1, :] * u2 + cw_ref[1:2, :] * u1 + cw_ref[2:3, :] * u
    oconv_ref[...] = (bg * yconv).astype(BF16)
    uprev_ref[...] = tail

    sga_ref[...] = _sigmoid(proj(GA_OFF, D_MODEL)).astype(BF16)
    sgb_ref[...] = _sigmoid(proj(GB_OFF, D_MODEL)).astype(BF16)


def _front(x2d, mod_p, g_mix, w_in_bf16, conv_w, batch, seq):
    n_tok = x2d.shape[0]
    n_tiles = n_tok // TM
    tps = seq // TM
    kv2_blocks = ATT_TILE // TM
    out_shape = (
        jax.ShapeDtypeStruct((n_tok, ATTN_W), BF16),
        jax.ShapeDtypeStruct((n_tiles, 4, TM // 4, ATTN_W), BF16),
        jax.ShapeDtypeStruct((n_tiles, 16, TM // 16, ATTN_W), BF16),
        jax.ShapeDtypeStruct((n_tok, CONV_W), BF16),
        jax.ShapeDtypeStruct((n_tok, D_MODEL), BF16),
        jax.ShapeDtypeStruct((n_tok, D_MODEL), BF16),
        jax.ShapeDtypeStruct((batch, 2 * GROUP_W, 128), F32),
        jax.ShapeDtypeStruct((batch, 2 * GROUP_W, 512), F32),
        jax.ShapeDtypeStruct((batch, 2 * GROUP_W, 2048), F32),
        jax.ShapeDtypeStruct((batch, 8, CONV_W), F32),
    )
    out_specs = (
        pl.BlockSpec((TM, ATTN_W), lambda i: (i, 0)),
        pl.BlockSpec((1, 4, TM // 4, ATTN_W), lambda i: (i, 0, 0, 0)),
        pl.BlockSpec((1, 16, TM // 16, ATTN_W), lambda i: (i, 0, 0, 0)),
        pl.BlockSpec((TM, CONV_W), lambda i: (i, 0)),
        pl.BlockSpec((TM, D_MODEL), lambda i: (i, 0)),
        pl.BlockSpec((TM, D_MODEL), lambda i: (i, 0)),
        pl.BlockSpec((None, 2 * GROUP_W, 128), lambda i: (i // tps, 0, 0)),
        pl.BlockSpec((None, 2 * GROUP_W, TM), lambda i: (i // tps, 0, 0)),
        pl.BlockSpec((None, 2 * GROUP_W, TM),
                     lambda i: (i // tps, 0, jnp.maximum(i % tps - (tps - kv2_blocks), 0))),
        pl.BlockSpec((None, 8, CONV_W), lambda i: (i // tps, 0, 0)),
    )
    in_specs = [
        pl.BlockSpec((TM, D_MODEL), lambda i: (i, 0)),
        pl.BlockSpec((None, 1, 6 * D_MODEL), lambda i: (i // tps, 0, 0)),
        _resident((1, D_MODEL)),
        _resident((D_MODEL, IN_COLS)),
        _resident((3, CONV_W)),
    ]
    return pl.pallas_call(
        functools.partial(_front_body, tps),
        grid=(n_tiles,),
        in_specs=in_specs,
        out_specs=out_specs,
        out_shape=out_shape,
        scratch_shapes=[pltpu.VMEM((GROUP_W // LANES, TM, LANES), F32), pltpu.VMEM((8, CONV_W), F32)],
        compiler_params=pltpu.CompilerParams(dimension_semantics=("arbitrary",), vmem_limit_bytes=VMEM_LIMIT),
        name="prompt_front",
    )(x2d, mod_p, g_mix.reshape(1, -1), w_in_bf16, conv_w)


def _band_bias_table():
    qi = jnp.arange(BAND)[:, None]
    kc = jnp.arange(2 * BAND)[None, :]
    delta = qi - (kc - BAND)
    valid = (delta >= 0) & (delta <= BAND)
    tabs = []
    for first in (False, True):
        ok = valid & (kc >= BAND) if first else valid
        for g in range(N_GROUPS):
            for h in range(HEADS):
                b = -_alibi_slope(g, h) * (delta * DILATIONS[g]).astype(F32)
                tabs.append(jnp.where(ok, b, MASKED))
    return jnp.stack(tabs).astype(F32)


def _head_lane_mask(h, dtype=None):
    lane = lax.broadcasted_iota(jnp.int32, (1, GROUP_W), 1)
    return (lane >= h * HEAD_DIM) & (lane < (h + 1) * HEAD_DIM)


def _attn_unit(q, k, v, bias_ref, bias_base):
    masks = [_head_lane_mask(h) for h in range(HEADS)]
    qs = jnp.concatenate([jnp.where(hm, q, jnp.zeros_like(q)) for hm in masks], axis=0)
    s = lax.dot_general(qs, k, (((1,), (1,)), ((), ())), preferred_element_type=F32)
    s = s + bias_ref[pl.ds(bias_base, HEADS)].reshape(HEADS * BAND, 2 * BAND)
    m = jnp.max(s, axis=-1, keepdims=True)
    e = jnp.exp(s - m)
    l = jnp.sum(e, axis=-1, keepdims=True)
    p = (e * (1.0 / l)).astype(BF16)
    ov = jnp.dot(p, v, preferred_element_type=F32)
    lse = m + jnp.log(l)
    o = jnp.zeros((BAND, GROUP_W), F32)
    lse_b = jnp.zeros((BAND, GROUP_W), F32)
    for h, hm in enumerate(masks):
        rows = slice(h * BAND, (h + 1) * BAND)
        o = o + jnp.where(hm, ov[rows], 0.0)
        lse_b = lse_b + jnp.where(hm, lse[rows], 0.0)
    return o, lse_b


def _attn_body(a0c_ref, a0p_ref, a1c_ref, a1p_ref, a2c_ref, a2p_ref, bias_ref, o_ref, og_ref, lg_ref):
    first_tile = pl.program_id(1) == 0
    first_off = jnp.where(first_tile, N_GROUPS * HEADS, 0)
    qs, ks, vs = (slice(0, GROUP_W), slice(GROUP_W, 2 * GROUP_W), slice(2 * GROUP_W, 3 * GROUP_W))

    def run_blocks(blocks):
        loaded = [load() for _, _, _, load in blocks]
        results = [_attn_unit(q, k, v, bias_ref, base) for (_, _, base, _), (q, k, v) in zip(blocks, loaded)]
        for (g, rows, _, _), (o, lse_b) in zip(blocks, results):
            for c in range(GROUP_W // LANES):
                og_ref[g, c, rows, :] = o[:, c * LANES:(c + 1) * LANES]
                lg_ref[g, c, rows, :] = lse_b[:, c * LANES:(c + 1) * LANES]

    def g0_block(n0, base):
        def load():
            q = a0c_ref[pl.ds(n0, BAND), qs]
            if isinstance(n0, int) and n0 == 0:
                k = jnp.concatenate([a0p_ref[:, ks], a0c_ref[0:BAND, ks]], axis=0)
                v = jnp.concatenate([a0p_ref[:, vs], a0c_ref[0:BAND, vs]], axis=0)
            else:
                k = a0c_ref[pl.ds(n0 - BAND, 2 * BAND), ks]
                v = a0c_ref[pl.ds(n0 - BAND, 2 * BAND), vs]
            return q, k, v
        return (0, pl.ds(n0, BAND), base, load)

    run_blocks([g0_block(0, first_off), g0_block(BAND, 0)])

    def g0_loop(t, carry):
        n0 = pl.multiple_of(t * (ATTN_PAIR * BAND), BAND)
        run_blocks([g0_block(n0 + u * BAND, 0) for u in range(ATTN_PAIR)])
        return carry

    lax.fori_loop(1, ATT_TILE // (ATTN_PAIR * BAND), g0_loop, 0)

    def g1_block(jj, r, prev_ref, prev_jj, base):
        def load():
            q = a1c_ref[jj, r, :, qs]
            k = jnp.concatenate([prev_ref[prev_jj, r, :, ks], a1c_ref[jj, r, :, ks]], axis=0)
            v = jnp.concatenate([prev_ref[prev_jj, r, :, vs], a1c_ref[jj, r, :, vs]], axis=0)
            return q, k, v
        return (1, pl.ds(jj * (4 * BAND) + r, BAND, stride=4), base, load)

    def g1_first(t, carry):
        run_blocks([g1_block(0, t * ATTN_PAIR + u, a1p_ref, 0, first_off + HEADS) for u in range(ATTN_PAIR)])
        return carry

    lax.fori_loop(0, 4 // ATTN_PAIR, g1_first, 0)

    def g1_rest(t, carry):
        jj = 1 + (t * ATTN_PAIR) // 4
        r0 = (t * ATTN_PAIR) % 4
        run_blocks([g1_block(jj, r0 + u, a1c_ref, jj - 1, HEADS) for u in range(ATTN_PAIR)])
        return carry

    lax.fori_loop(0, 12 // ATTN_PAIR, g1_rest, 0)

    n_sub = a2c_ref.shape[0]

    def g2_rows(ref, r, cols):
        return jnp.concatenate([ref[t, r, :, cols] for t in range(n_sub)], axis=0)

    def g2_block(r):
        def load():
            q = g2_rows(a2c_ref, r, qs)
            k = jnp.concatenate([g2_rows(a2p_ref, r, ks), g2_rows(a2c_ref, r, ks)], axis=0)
            v = jnp.concatenate([g2_rows(a2p_ref, r, vs), g2_rows(a2c_ref, r, vs)], axis=0)
            return q, k, v
        return (2, pl.ds(r, BAND, stride=16), first_off + 2 * HEADS, load)

    def g2_loop(t, carry):
        run_blocks([g2_block(t * ATTN_PAIR + u) for u in range(ATTN_PAIR)])
        return carry

    lax.fori_loop(0, 16 // ATTN_PAIR, g2_loop, 0)

    def mix(c, carry):
        rows = pl.ds(pl.multiple_of(c * BAND, BAND), BAND)
        for half in range(GROUP_W // LANES):
            l0, l1, l2 = lg_ref[0, half, rows, :], lg_ref[1, half, rows, :], lg_ref[2, half, rows, :]
            mx = jnp.maximum(jnp.maximum(l0, l1), l2)
            w0, w1, w2 = jnp.exp(l0 - mx), jnp.exp(l1 - mx), jnp.exp(l2 - mx)
            num = w0 * og_ref[0, half, rows, :] + w1 * og_ref[1, half, rows, :] + w2 * og_ref[2, half, rows, :]
            o_ref[rows, half * LANES:(half + 1) * LANES] = (num / (w0 + w1 + w2)).astype(o_ref.dtype)
        return carry

    lax.fori_loop(0, ATT_TILE // BAND, mix, 0)


def _attention(a0, a1, a2, bias, batch, seq):
    steps = seq // ATT_TILE
    sub = ATT_TILE // TM
    a0 = a0.reshape(batch, seq, ATTN_W)
    a1 = a1.reshape(batch, seq // TM, 4, TM // 4, ATTN_W)
    a2 = a2.reshape(batch, seq // TM, 16, TM // 16, ATTN_W)
    in_specs = [
        pl.BlockSpec((None, ATT_TILE, ATTN_W), lambda b, j: (b, j, 0)),
        pl.BlockSpec((None, BAND, ATTN_W), lambda b, j: (b, jnp.maximum(j * (ATT_TILE // BAND) - 1, 0), 0)),
        pl.BlockSpec((None, sub, 4, TM // 4, ATTN_W), lambda b, j: (b, j, 0, 0, 0)),
        pl.BlockSpec((None, 1, 4, TM // 4, ATTN_W), lambda b, j: (b, jnp.maximum(j * sub - 1, 0), 0, 0, 0)),
        pl.BlockSpec((None, sub, 16, TM // 16, ATTN_W), lambda b, j: (b, j, 0, 0, 0)),
        pl.BlockSpec((None, sub, 16, TM // 16, ATTN_W), lambda b, j: (b, jnp.maximum(j - 1, 0), 0, 0, 0)),
        _resident(bias.shape),
    ]
    return pl.pallas_call(
        _attn_body,
        grid=(batch, steps),
        in_specs=in_specs,
        out_specs=pl.BlockSpec((None, ATT_TILE, GROUP_W), lambda b, j: (b, j, 0)),
        out_shape=jax.ShapeDtypeStruct((batch, seq, GROUP_W), BF16),
        scratch_shapes=[pltpu.VMEM((N_GROUPS, GROUP_W // LANES, ATT_TILE, LANES), F32)] * 2,
        compiler_params=pltpu.CompilerParams(dimension_semantics=("arbitrary", "arbitrary"),
                                             vmem_limit_bytes=VMEM_LIMIT),
        name="prompt_attention",
    )(a0, a0, a1, a1, a2, a2, bias)


def _route(logits):
    lane = lax.broadcasted_iota(jnp.int32, logits.shape, 1)
    big = jnp.int32(1 << 20)
    gmask = lane < N_EXPERT_GROUPS
    lg = jnp.where(gmask, logits, MASKED)
    gmax = jnp.max(lg, axis=-1, keepdims=True)
    gidx = jnp.min(jnp.where(gmask & (lg == gmax), lane, big), axis=-1, keepdims=True)
    p_top = 1.0 / jnp.sum(jnp.where(gmask, jnp.exp(lg - gmax), 0.0), axis=-1, keepdims=True)
    lo = EXPERT_LANE0 + EXPERTS_PER_GROUP * gidx
    emask = (lane >= lo) & (lane < lo + EXPERTS_PER_GROUP)
    le = jnp.where(emask, logits, MASKED)
    v1 = jnp.max(le, axis=-1, keepdims=True)
    i1 = jnp.min(jnp.where(emask & (le == v1), lane, big), axis=-1, keepdims=True)
    emask2 = emask & (lane != i1)
    le2 = jnp.where(emask2, logits, MASKED)
    v2 = jnp.max(le2, axis=-1, keepdims=True)
    i2 = jnp.min(jnp.where(emask2 & (le2 == v2), lane, big), axis=-1, keepdims=True)
    e2 = jnp.exp(v2 - v1)
    den = 1.0 + e2
    w1 = (1.0 / den) * p_top
    w2 = (e2 / den) * p_top
    id1 = (i1 - EXPERT_LANE0).astype(F32)
    id2 = (i2 - EXPERT_LANE0).astype(F32)
    return jnp.where(lane == RT_W1, w1, jnp.where(lane == RT_W2, w2,
                     jnp.where(lane == RT_E1, id1, jnp.where(lane == RT_E2, id2, 0.0))))


def _with_ranks(rt, base):
    n = rt.shape[0]
    lane = lax.broadcasted_iota(jnp.int32, rt.shape, 1)
    lane_f = lane.astype(F32)
    oh1 = jnp.where(lane_f == rt[:, RT_E1:RT_E1 + 1], 1.0, 0.0)
    oh2 = jnp.where(lane_f == rt[:, RT_E2:RT_E2 + 1], 1.0, 0.0)
    before = jnp.where(lax.broadcasted_iota(jnp.int32, (n, n), 1) < lax.broadcasted_iota(jnp.int32, (n, n), 0),
                       1.0, 0.0).astype(BF16)
    p1 = jnp.dot(before, oh1.astype(BF16), preferred_element_type=F32)
    p2 = jnp.dot(before, oh2.astype(BF16), preferred_element_type=F32)
    c1 = jnp.sum(oh1, axis=0, keepdims=True)
    c2 = jnp.sum(oh2, axis=0, keepdims=True)
    rank1 = jnp.sum(oh1 * (base + p1), axis=-1, keepdims=True)
    rank2 = jnp.sum(oh2 * (base + c1 + p2), axis=-1, keepdims=True)
    return jnp.where(lane == RT_R1, rank1, jnp.where(lane == RT_R2, rank2, rt)), base + c1 + c2


def _back_body(oa_ref, oc_ref, sga_ref, sgb_ref, x_ref, mod_ref, wpa_ref, wpb_ref, wo_ref, g_ref,
               wrh_ref, wrl_ref, br_ref, x1_ref, h2_ref, rt_ref, cnt_ref, base_ref):
    @pl.when(pl.program_id(0) == 0)
    def _():
        base_ref[...] = jnp.zeros_like(base_ref)

    pa = jnp.dot(oa_ref[...], wpa_ref[...], preferred_element_type=F32)
    pb = jnp.dot(oc_ref[...], wpb_ref[...], preferred_element_type=F32)
    merged = sga_ref[...].astype(F32) * pa + sgb_ref[...].astype(F32) * pb
    gate1 = mod_ref[:, 2 * D_MODEL:3 * D_MODEL]
    shift2 = mod_ref[:, 3 * D_MODEL:4 * D_MODEL]
    scale2 = mod_ref[:, 4 * D_MODEL:5 * D_MODEL]
    x1 = x_ref[...] + gate1 * jnp.dot(merged.astype(BF16), wo_ref[...], preferred_element_type=F32)
    x1_ref[...] = x1
    h2 = _rmsnorm(x1, g_ref[...]) * (1.0 + scale2) + shift2
    hi = h2.astype(BF16)
    h2_ref[...] = hi
    lo = (h2 - hi.astype(F32)).astype(BF16)
    logits = (jnp.dot(hi, wrh_ref[...], preferred_element_type=F32)
              + jnp.dot(lo, wrh_ref[...], preferred_element_type=F32)
              + jnp.dot(hi, wrl_ref[...], preferred_element_type=F32)) + br_ref[...]
    rt, total = _with_ranks(_route(logits), base_ref[...])
    rt_ref[...] = rt
    base_ref[...] = total
    cnt_ref[...] = total


def _back(o_attn, oconv, sga, sgb, x2d, mod_p, wpa, wpb, wo, g_ffn, wr_hi, wr_lo, b_r, seq):
    n_tok = x2d.shape[0]
    n_all = n_tok + TM
    tps = seq // TM
    row = lambda w: pl.BlockSpec((TM, w), lambda i: (i, 0))
    return pl.pallas_call(
        _back_body,
        grid=(n_tok // TM,),
        in_specs=[row(GROUP_W), row(CONV_W), row(D_MODEL), row(D_MODEL), row(D_MODEL),
                  pl.BlockSpec((None, 1, 6 * D_MODEL), lambda i: (i // tps, 0, 0)),
                  _resident(wpa.shape), _resident(wpb.shape), _resident(wo.shape), _resident((1, D_MODEL)),
                  _resident(wr_hi.shape), _resident(wr_lo.shape), _resident(b_r.shape)],
        out_specs=(row(D_MODEL), row(D_MODEL), row(ROUTER_LANES), pl.BlockSpec((1, LANES), lambda i: (0, 0))),
        out_shape=(jax.ShapeDtypeStruct((n_all, D_MODEL), F32),
                   jax.ShapeDtypeStruct((n_all, D_MODEL), BF16),
                   jax.ShapeDtypeStruct((n_all, ROUTER_LANES), F32),
                   jax.ShapeDtypeStruct((1, LANES), F32)),
        scratch_shapes=[pltpu.VMEM((1, LANES), F32)],
        compiler_params=pltpu.CompilerParams(dimension_semantics=("arbitrary",), vmem_limit_bytes=VMEM_LIMIT),
        name="prompt_back",
    )(o_attn, oconv, sga, sgb, x2d, mod_p, wpa, wpb, wo, g_ffn.reshape(1, -1), wr_hi, wr_lo, b_r)


def _valid_chunks(tile, n_prompt_tiles, n_sample):
    return jnp.where(tile < n_prompt_tiles, TM // DMA_CHUNK, n_sample // DMA_CHUNK)


def _dispatch_body(n_prompt_tiles, n_sample, dest_ref, tail_ref, h_ref, xs_ref, stage_ref, zero_ref, sem):
    i = pl.program_id(0)

    def tail_copy(e):
        start = pl.multiple_of(tail_ref[e] * ROW_SUB, ROW_SUB)
        return pltpu.make_async_copy(zero_ref, xs_ref.at[pl.ds(start, EXPERT_TILE * ROW_SUB)], sem)

    @pl.when(i == 0)
    def _():
        zero_ref[...] = jnp.zeros_like(zero_ref)
        for e in range(N_EXPERTS):
            @pl.when(tail_ref[e] >= 0)
            def _():
                tail_copy(e).start()
        for e in range(N_EXPERTS):
            @pl.when(tail_ref[e] >= 0)
            def _():
                tail_copy(e).wait()

    for j in range(ROW_SUB):
        stage_ref[pl.ds(j, TM, stride=ROW_SUB), :] = h_ref[:, j * LANES:(j + 1) * LANES].astype(F32)
    n_chunks = _valid_chunks(i, n_prompt_tiles, n_sample)

    def issue(c, carry):
        for j in range(DMA_CHUNK):
            r = c * DMA_CHUNK + j
            src = stage_ref.at[pl.ds(pl.multiple_of(r * ROW_SUB, ROW_SUB), ROW_SUB)]
            for k in range(2):
                d = pl.multiple_of(dest_ref[(i * TM + r) * 2 + k], ROW_SUB)
                pltpu.make_async_copy(src, xs_ref.at[pl.ds(d, ROW_SUB)], sem).start(priority=k)
        return carry

    lax.fori_loop(0, n_chunks, issue, 0)

    def drain(c, carry):
        n = 2 * DMA_CHUNK * ROW_SUB
        pltpu.make_async_copy(stage_ref.at[pl.ds(0, n)], xs_ref.at[pl.ds(0, n)], sem).wait()
        return carry

    lax.fori_loop(0, n_chunks, drain, 0)


def _dispatch(dest, tail_start, h2, n_rows, n_prompt_tiles, n_sample):
    n_all = h2.shape[0]
    return pl.pallas_call(
        functools.partial(_dispatch_body, n_prompt_tiles, n_sample),
        grid_spec=pltpu.PrefetchScalarGridSpec(
            num_scalar_prefetch=2,
            grid=(n_all // TM,),
            in_specs=[pl.BlockSpec((TM, D_MODEL), lambda i, *_: (i, 0))],
            out_specs=pl.BlockSpec(memory_space=pl.ANY),
            scratch_shapes=[pltpu.VMEM((TM * ROW_SUB, LANES), F32), pltpu.VMEM((EXPERT_TILE * ROW_SUB, LANES), F32),
                            pltpu.SemaphoreType.DMA],
        ),
        out_shape=jax.ShapeDtypeStruct((n_rows * ROW_SUB, LANES), F32),
        compiler_params=pltpu.CompilerParams(dimension_semantics=("arbitrary",), vmem_limit_bytes=VMEM_LIMIT,
                                             disable_bounds_checks=True),
        name="moe_dispatch",
    )(dest, tail_start, h2)


def _experts_body(te_ref, nt_ref, xs_ref, wg_ref, wu_ref, wd_ref, o_ref):
    @pl.when(pl.program_id(0) < nt_ref[0])
    def _():
        x = jnp.concatenate([xs_ref[pl.ds(j, EXPERT_TILE, stride=ROW_SUB), :].astype(BF16) for j in range(ROW_SUB)],
                            axis=1)
        a = jnp.dot(x, wg_ref[0].astype(BF16), preferred_element_type=F32)
        z = (a * _sigmoid(a)) * jnp.dot(x, wu_ref[0].astype(BF16), preferred_element_type=F32)
        o = jnp.dot(z.astype(BF16), wd_ref[0].astype(BF16), preferred_element_type=F32)
        for j in range(ROW_SUB):
            o_ref[pl.ds(j, EXPERT_TILE, stride=ROW_SUB), :] = o[:, j * LANES:(j + 1) * LANES]


def _experts(tile_expert, n_tiles, xs, w_gate_e, w_up_e, w_down_e):
    max_tiles = tile_expert.shape[0]
    rows = lambda s, te, nt: (jnp.minimum(s, nt[0] - 1), 0)
    row_tile = pl.BlockSpec((EXPERT_TILE * ROW_SUB, LANES), rows)
    weight = lambda shape: pl.BlockSpec((1,) + shape, lambda s, te, nt: (te[jnp.minimum(s, nt[0] - 1)], 0, 0))
    return pl.pallas_call(
        _experts_body,
        grid_spec=pltpu.PrefetchScalarGridSpec(
            num_scalar_prefetch=2,
            grid=(max_tiles,),
            in_specs=[row_tile,
                      weight((D_MODEL, EXPERT_FF)), weight((D_MODEL, EXPERT_FF)), weight((EXPERT_FF, D_MODEL))],
            out_specs=row_tile,
        ),
        out_shape=jax.ShapeDtypeStruct(xs.shape, F32),
        compiler_params=pltpu.CompilerParams(dimension_semantics=("arbitrary",), vmem_limit_bytes=VMEM_LIMIT),
        name="moe_experts",
    )(tile_expert, n_tiles, xs, w_gate_e, w_up_e, w_down_e)


def _combine_body(n_prompt_tiles, n_sample, dest_ref, x1_ref, rt_ref, gp_ref, gs_ref, gf_ref, eo_ref,
                  yp_ref, ys_ref, rows_ref, sem):
    i = pl.program_id(0)
    slot = i % 2

    def fetch(tile):
        buf = tile % 2

        def issue(c, carry):
            for j in range(DMA_CHUNK):
                r = c * DMA_CHUNK + j
                row = pl.ds(pl.multiple_of(r * ROW_SUB, ROW_SUB), ROW_SUB)
                for k in range(2):
                    d = pl.multiple_of(dest_ref[(tile * TM + r) * 2 + k], ROW_SUB)
                    pltpu.make_async_copy(eo_ref.at[pl.ds(d, ROW_SUB)], rows_ref.at[buf, k, row],
                                          sem.at[buf]).start(priority=k)
            return carry

        lax.fori_loop(0, _valid_chunks(tile, n_prompt_tiles, n_sample), issue, 0)

    @pl.when(i == 0)
    def _():
        fetch(i)

    @pl.when(i + 1 < pl.num_programs(0))
    def _():
        fetch(i + 1)

    def drain(c, carry):
        n = 2 * DMA_CHUNK * ROW_SUB
        pltpu.make_async_copy(eo_ref.at[pl.ds(0, n)], rows_ref.at[slot, 0, pl.ds(0, n)], sem.at[slot]).wait()
        return carry

    lax.fori_loop(0, _valid_chunks(i, n_prompt_tiles, n_sample), drain, 0)

    rt = rt_ref[...]
    w1, w2 = rt[:, RT_W1:RT_W1 + 1], rt[:, RT_W2:RT_W2 + 1]
    lane_tile = lambda k, j: rows_ref[slot, k, pl.ds(j, TM, stride=ROW_SUB), :]
    moe = jnp.concatenate([w1 * lane_tile(0, j) + w2 * lane_tile(1, j) for j in range(ROW_SUB)], axis=1)
    is_prompt = i < n_prompt_tiles
    gate2 = jnp.where(is_prompt, gp_ref[...], gs_ref[...])
    y = _rmsnorm(x1_ref[...] + gate2 * moe, gf_ref[...])

    @pl.when(is_prompt)
    def _():
        yp_ref[...] = y

    @pl.when(jnp.logical_not(is_prompt))
    def _():
        ys_ref[...] = y[:n_sample, :]


def _combine(dest, x1, rt, gate2_p, gate2_s, g_final, eo, n_prompt_tiles, n_sample, tiles_per_seq):
    n_all = x1.shape[0]
    last_p = n_prompt_tiles - 1
    return pl.pallas_call(
        functools.partial(_combine_body, n_prompt_tiles, n_sample),
        grid_spec=pltpu.PrefetchScalarGridSpec(
            num_scalar_prefetch=1,
            grid=(n_all // TM,),
            in_specs=[pl.BlockSpec((TM, D_MODEL), lambda i, *_: (i, 0)),
                      pl.BlockSpec((TM, ROUTER_LANES), lambda i, *_: (i, 0)),
                      pl.BlockSpec((None, 1, D_MODEL), lambda i, *_: (jnp.minimum(i, last_p) // tiles_per_seq, 0, 0)),
                      pl.BlockSpec((TM, D_MODEL), lambda i, *_: (0, 0)),
                      pl.BlockSpec((1, D_MODEL), lambda i, *_: (0, 0)),
                      pl.BlockSpec(memory_space=pl.ANY)],
            out_specs=(pl.BlockSpec((TM, D_MODEL), lambda i, *_: (jnp.minimum(i, last_p), 0)),
                       pl.BlockSpec((n_sample, D_MODEL), lambda i, *_: (0, 0))),
            scratch_shapes=[pltpu.VMEM((2, 2, TM * ROW_SUB, LANES), F32), pltpu.SemaphoreType.DMA((2,))],
        ),
        out_shape=(jax.ShapeDtypeStruct((n_prompt_tiles * TM, D_MODEL), F32),
                   jax.ShapeDtypeStruct((n_sample, D_MODEL), F32)),
        compiler_params=pltpu.CompilerParams(dimension_semantics=("arbitrary",), vmem_limit_bytes=VMEM_LIMIT,
                                             disable_bounds_checks=True),
        name="moe_combine",
    )(dest, x1, rt, gate2_p, gate2_s, g_final.reshape(1, -1), eo)


def _routing_tables(rt, counts, max_tiles):
    cnt = counts[0, :N_EXPERTS].astype(jnp.int32)
    padded = (cnt + EXPERT_TILE - 1) // EXPERT_TILE * EXPERT_TILE
    ends = jnp.cumsum(padded)
    starts = ends - padded
    n_tiles = (ends[-1] // EXPERT_TILE).reshape(1)
    tile_row0 = jnp.arange(max_tiles, dtype=jnp.int32) * EXPERT_TILE
    tile_expert = jnp.minimum(jnp.sum(ends[None, :] <= tile_row0[:, None], axis=1), N_EXPERTS - 1).astype(jnp.int32)
    tail_start = jnp.where(cnt > 0, ends - EXPERT_TILE, -1).astype(jnp.int32)
    eid = rt[:, RT_E1:RT_E2 + 1].astype(jnp.int32)
    onehot = eid[..., None] == jnp.arange(N_EXPERTS, dtype=jnp.int32)
    dest = rt[:, RT_R1:RT_R2 + 1].astype(jnp.int32) + jnp.sum(jnp.where(onehot, starts, 0), axis=-1)
    dest = (jnp.where(eid >= 0, dest, 0) * ROW_SUB).reshape(-1)
    return dest, tail_start, tile_expert, n_tiles


def _sample_front_body(x_ref, mod_ref, g_ref, w_ref, o_ref):
    shift1 = mod_ref[:, 0:D_MODEL]
    scale1 = mod_ref[:, D_MODEL:2 * D_MODEL]
    h = _rmsnorm(x_ref[...], g_ref[...]) * (1.0 + scale1) + shift1
    o_ref[...] = jnp.dot(h, w_ref[...], precision=HIGHEST, preferred_element_type=F32)


def _sample_front(x_s, mod_s, g_mix, w_in):
    nb = x_s.shape[0]
    tn = 256
    return pl.pallas_call(
        _sample_front_body,
        grid=(IN_COLS // tn,),
        in_specs=[pl.BlockSpec((nb, D_MODEL), lambda j: (0, 0)),
                  pl.BlockSpec((nb, 2 * D_MODEL), lambda j: (0, 0)),
                  pl.BlockSpec((1, D_MODEL), lambda j: (0, 0)),
                  pl.BlockSpec((D_MODEL, tn), lambda j: (0, j))],
        out_specs=pl.BlockSpec((nb, tn), lambda j: (0, j)),
        out_shape=jax.ShapeDtypeStruct((nb, IN_COLS), F32),
        compiler_params=pltpu.CompilerParams(dimension_semantics=("arbitrary",), vmem_limit_bytes=VMEM_LIMIT),
        name="sample_front",
    )(x_s, mod_s, g_mix.reshape(1, -1), w_in)


def _sample_attn_body(qkv_ref, c0_ref, c1_ref, c2_ref, oa_ref):
    head = lax.broadcasted_iota(jnp.int32, (HEADS, 1, 1), 0)
    for b in range(qkv_ref.shape[0]):
        outs, lses = [], []
        for g, c_ref in enumerate((c0_ref, c1_ref, c2_ref)):
            window = c_ref.shape[-1]
            def new_rows(part):
                c0 = (part * N_GROUPS + g) * HEADS
                return jnp.stack([qkv_ref[b, :, c0 + h:c0 + h + 1] for h in range(HEADS)], axis=0)

            q = new_rows(0) * (HEAD_DIM ** -0.5)
            k_new = new_rows(1)
            v_new = new_rows(2)
            s = jnp.sum(c_ref[b, 0] * q, axis=1, keepdims=True)
            pos = lax.broadcasted_iota(jnp.int32, (1, 1, window), 2)
            slope = jnp.full((HEADS, 1, 1), _alibi_slope(g, HEADS - 1), F32)
            for h in range(HEADS - 1):
                slope = jnp.where(head == h, _alibi_slope(g, h), slope)
            on_band = (pos & (DILATIONS[g] - 1)) == 0
            s = jnp.where(on_band, s - slope * (window - pos).astype(F32), MASKED)
            s_self = jnp.sum(q * k_new, axis=1, keepdims=True)
            m = jnp.maximum(jnp.max(s, axis=2, keepdims=True), s_self)
            e = jnp.exp(s - m)
            e_self = jnp.exp(s_self - m)
            l = jnp.sum(e, axis=2, keepdims=True) + e_self
            o = jnp.sum(c_ref[b, 1] * e, axis=2, keepdims=True) + e_self * v_new
            outs.append(o / l)
            lses.append(m + jnp.log(l))
        mx = jnp.maximum(jnp.maximum(lses[0], lses[1]), lses[2])
        w = [jnp.exp(x - mx) for x in lses]
        mixed = (w[0] * outs[0] + w[1] * outs[1] + w[2] * outs[2]) / (w[0] + w[1] + w[2])
        for h in range(HEADS):
            oa_ref[b, :, h:h + 1] = mixed[h]


def _sample_attention(qkv_s, cache0, cache1, cache2):
    nb = qkv_s.shape[0]
    bb = 2
    hd = (HEADS, HEAD_DIM)
    caches = [jnp.transpose(c, (0, 1, 3, 4, 5, 2)) for c in (cache0, cache1, cache2)]
    cache_spec = lambda c: pl.BlockSpec((None, bb, 2, *hd, c.shape[-1]), lambda i: (0, i, 0, 0, 0, 0))
    return pl.pallas_call(
        _sample_attn_body,
        grid=(nb // bb,),
        in_specs=[pl.BlockSpec((bb,) + qkv_s.shape[1:], lambda i: (i, 0, 0))] + [cache_spec(c) for c in caches],
        out_specs=pl.BlockSpec((bb, HEAD_DIM, HEADS), lambda i: (i, 0, 0)),
        out_shape=jax.ShapeDtypeStruct((nb, HEAD_DIM, HEADS), F32),
        compiler_params=pltpu.CompilerParams(dimension_semantics=("arbitrary",), vmem_limit_bytes=VMEM_LIMIT),
        name="sample_attention",
    )(qkv_s, *caches)


def _sample_back_body(p_ref, oa_ref, cc_ref, x_ref, mod_ref, cw_ref, wpa_ref, wpb_ref, wo_ref, g_ref,
                      wr_ref, br_ref, cntp_ref, x1_any, h2_any, rt_any, x1_ref, h2_ref, rt_ref, sconv_ref, cnt_ref):
    del x1_any, h2_any, rt_any
    nb = x_ref.shape[0]
    hdot = functools.partial(jnp.dot, precision=HIGHEST, preferred_element_type=F32)
    bg = p_ref[:, BG_OFF:BG_OFF + CONV_W]
    u = p_ref[:, CG_OFF:CG_OFF + CONV_W] * p_ref[:, UI_OFF:UI_OFF + CONV_W]
    c_old, c_new = cc_ref[:, 0, :], cc_ref[:, 1, :]
    yconv = cw_ref[0:1, :] * c_old + cw_ref[1:2, :] * c_new + cw_ref[2:3, :] * u
    sconv_ref[:, 0, :] = c_new
    sconv_ref[:, 1, :] = u
    sga = _sigmoid(p_ref[:, GA_OFF:GA_OFF + D_MODEL])
    sgb = _sigmoid(p_ref[:, GB_OFF:GB_OFF + D_MODEL])
    merged = sga * hdot(oa_ref[...], wpa_ref[...]) + sgb * hdot(bg * yconv, wpb_ref[...])
    gate1 = mod_ref[:, 2 * D_MODEL:3 * D_MODEL]
    shift2 = mod_ref[:, 3 * D_MODEL:4 * D_MODEL]
    scale2 = mod_ref[:, 4 * D_MODEL:5 * D_MODEL]
    x1 = x_ref[...] + gate1 * hdot(merged, wo_ref[...])
    h2 = _rmsnorm(x1, g_ref[...]) * (1.0 + scale2) + shift2
    rt = _route(hdot(h2, wr_ref[...]) + br_ref[...])
    lane = lax.broadcasted_iota(jnp.int32, (TM, ROUTER_LANES), 1)
    x1_ref[...] = jnp.zeros_like(x1_ref)
    h2_ref[...] = jnp.zeros_like(h2_ref)
    rt_ref[...] = jnp.where((lane == RT_E1) | (lane == RT_E2), -1.0, 0.0)
    x1_ref[0:nb, :] = x1
    h2_ref[0:nb, :] = h2.astype(BF16)
    rt_ref[0:nb, :] = rt
    ranked, total = _with_ranks(rt_ref[...], cntp_ref[...])
    rt_ref[...] = ranked
    cnt_ref[...] = total


def _sample_back(proj_s, oa_s, cache_conv, x_s, mod_s, conv_w, w_pa, w_pb, w_o, g_ffn, w_r, b_r, cnt_p, x1, h2, rt):
    nb = x_s.shape[0]
    last = x1.shape[0] // TM - 1
    args = (proj_s, oa_s, cache_conv, x_s, mod_s, conv_w, w_pa, w_pb, w_o, g_ffn.reshape(1, -1), w_r, b_r, cnt_p)
    full = lambda shape: pl.BlockSpec(shape, lambda i: (0,) * len(shape))
    last_tile = lambda a: pl.BlockSpec((TM, a.shape[1]), lambda i: (last, 0))
    shape_of = lambda a: jax.ShapeDtypeStruct(a.shape, a.dtype)
    return pl.pallas_call(
        _sample_back_body,
        grid=(1,),
        in_specs=[full(a.shape) for a in args] + [pl.BlockSpec(memory_space=pl.ANY)] * 3,
        out_specs=(last_tile(x1), last_tile(h2), last_tile(rt), full((nb, 2, CONV_W)), full((1, LANES))),
        out_shape=(shape_of(x1), shape_of(h2), shape_of(rt), jax.ShapeDtypeStruct((nb, 2, CONV_W), F32),
                   jax.ShapeDtypeStruct((1, LANES), F32)),
        input_output_aliases={len(args): 0, len(args) + 1: 1, len(args) + 2: 2},
        compiler_params=pltpu.CompilerParams(dimension_semantics=("arbitrary",), vmem_limit_bytes=VMEM_LIMIT),
        name="sample_back",
    )(*args, x1, h2, rt)


def kernel(x_prompt, x_sample, cache_kv_w128, cache_kv_w512, cache_kv_w2048, cache_conv, c_prompt, c_sample,
           g_mix, w_ada, b_ada, w_in, conv_w, w_pa, w_pb, w_o, g_ffn, w_router_group, b_router_group,
           w_router_expert, b_router_expert, w_gate_e, w_up_e, w_down_e, g_final):
    batch, seq, _ = x_prompt.shape
    nb = x_sample.shape[0]
    assert x_sample.shape[1] == 1 and g_mix.shape[0] == 1, "one layer, one new sample token per sequence"
    assert seq % ATT_TILE == 0 and cache_kv_w128.shape[2] == 128 and cache_kv_w512.shape[2] == 512 \
        and cache_kv_w2048.shape[2] == 2048
    (g_mix, w_ada, b_ada, w_in, conv_w, w_pa, w_pb, w_o, g_ffn, w_rg, b_rg, w_re, b_re, w_gate_e, w_up_e,
     w_down_e) = (a[0] for a in (g_mix, w_ada, b_ada, w_in, conv_w, w_pa, w_pb, w_o, g_ffn, w_router_group,
                                 b_router_group, w_router_expert, b_router_expert, w_gate_e, w_up_e, w_down_e))

    c_all = jnp.concatenate([c_prompt, jnp.zeros((8 - batch, D_MODEL), F32), c_sample], axis=0)
    mod = _modulation(c_all, w_ada, b_ada)
    mod_p = mod[:batch].reshape(batch, 1, 6 * D_MODEL)
    mod_s = mod[8:]

    w_r = jnp.zeros((D_MODEL, ROUTER_LANES), F32)
    w_r = w_r.at[:, :N_EXPERT_GROUPS].set(w_rg).at[:, EXPERT_LANE0:EXPERT_LANE0 + N_EXPERTS].set(w_re)
    b_r = jnp.full((1, ROUTER_LANES), MASKED, F32)
    b_r = b_r.at[0, :N_EXPERT_GROUPS].set(b_rg).at[0, EXPERT_LANE0:EXPERT_LANE0 + N_EXPERTS].set(b_re)
    wr_hi = w_r.astype(BF16)
    wr_lo = (w_r - wr_hi.astype(F32)).astype(BF16)

    x2d = x_prompt.reshape(batch * seq, D_MODEL)
    (a0, a1, a2, oconv, sga, sgb, kv0, kv1, kv2, pconv) = _front(
        x2d, mod_p, g_mix, w_in.astype(BF16), conv_w, batch, seq)
    o_attn = _attention(a0, a1, a2, _band_bias_table(), batch, seq).reshape(batch * seq, GROUP_W)
    x1, h2, rt, cnt_p = _back(o_attn, oconv, sga, sgb, x2d, mod_p, w_pa.astype(BF16), w_pb.astype(BF16),
                              w_o.astype(BF16), g_ffn, wr_hi, wr_lo, b_r, seq)

    x_s = x_sample.reshape(nb, D_MODEL)
    proj_s = _sample_front(x_s, mod_s[:, :2 * D_MODEL], g_mix, w_in)
    qkv_s = proj_s[:, :3 * ATTN_W].reshape(nb, 3, N_GROUPS, HEADS, HEAD_DIM)
    qkv_t = jnp.transpose(qkv_s.reshape(nb, 3 * N_GROUPS * HEADS, HEAD_DIM), (0, 2, 1))
    oa_s = _sample_attention(qkv_t, cache_kv_w128, cache_kv_w512, cache_kv_w2048)
    oa_s = jnp.transpose(oa_s, (0, 2, 1)).reshape(nb, GROUP_W)
    skv0, skv1, skv2 = (jnp.stack([qkv_s[:, 1, g], qkv_s[:, 2, g]], axis=1) for g in range(N_GROUPS))
    x1, h2, rt, sconv, counts = _sample_back(proj_s, oa_s, cache_conv[0], x_s, mod_s, conv_w,
                                             w_pa, w_pb, w_o, g_ffn, w_r, b_r, cnt_p, x1, h2, rt)

    n_prompt_tiles = batch * seq // TM
    max_tiles = -(-(2 * (batch * seq + nb) + N_EXPERTS * (EXPERT_TILE - 1)) // EXPERT_TILE)
    dest, tail_start, tile_expert, n_tiles = _routing_tables(rt, counts, max_tiles)
    xs = _dispatch(dest, tail_start, h2, max_tiles * EXPERT_TILE, n_prompt_tiles, nb)
    eo = _experts(tile_expert, n_tiles, xs, w_gate_e, w_up_e, w_down_e)
    gate2_s = jnp.zeros((TM, D_MODEL), F32).at[:nb].set(mod_s[:, 5 * D_MODEL:])
    y_prompt, y_sample = _combine(dest, x1, rt, mod_p[:, :, 5 * D_MODEL:], gate2_s, g_final, eo,
                                  n_prompt_tiles, nb, seq // TM)

    def prompt_state(a):
        a = a.reshape(batch, 2, HEADS, HEAD_DIM, a.shape[-1])
        return jnp.transpose(a, (0, 4, 1, 2, 3))[None]

    sample_state = lambda a: a.reshape(1, nb, 1, 2, HEADS, HEAD_DIM)
    return (y_prompt.reshape(batch, seq, D_MODEL), y_sample.reshape(nb, 1, D_MODEL),
            prompt_state(kv0), prompt_state(kv1), prompt_state(kv2),
            pconv[:, 6:8, :].reshape(1, batch, 2, CONV_W),
            sample_state(skv0), sample_state(skv1), sample_state(skv2),
            sconv.reshape(1, nb, 2, CONV_W))
```

```python
import functools
import math

import jax
import jax.numpy as jnp
from jax import lax
from jax.experimental import pallas as pl
from jax.experimental.pallas import tpu as pltpu

F32 = jnp.float32
BF16 = jnp.bfloat16
HIGHEST = lax.Precision.HIGHEST

D_MODEL = 1024
HEAD_DIM = 64
HEADS = 4
N_GROUPS = 3
GROUP_W = HEADS * HEAD_DIM
ATTN_W = N_GROUPS * GROUP_W
CONV_W = 512
DILATIONS = (1, 4, 16)
BAND = 128
N_EXPERTS = 32
EXPERTS_PER_GROUP = 8
N_EXPERT_GROUPS = 4
EXPERT_FF = 256
RMS_EPS = 1e-6
MASKED = -1e30

Q_OFF, K_OFF, V_OFF = 0, ATTN_W, 2 * ATTN_W
BG_OFF = 3 * ATTN_W
CG_OFF = BG_OFF + CONV_W
UI_OFF = CG_OFF + CONV_W
GA_OFF = UI_OFF + CONV_W
GB_OFF = GA_OFF + D_MODEL
IN_COLS = GB_OFF + D_MODEL

LANES = 128
ROUTER_LANES = 128
EXPERT_LANE0 = N_EXPERT_GROUPS

TM = 512
ATT_TILE = 2048
ATTN_PAIR = 4
EXPERT_TILE = 512
RT_W1, RT_W2, RT_E1, RT_E2, RT_R1, RT_R2 = 0, 1, 2, 3, 4, 5
ROW_SUB = D_MODEL // LANES
DMA_CHUNK = 8
VMEM_LIMIT = 56 * 1024 * 1024


def _sigmoid(x):
    return 1.0 / (1.0 + jnp.exp(-x))


def _rmsnorm(x, g):
    return x * lax.rsqrt(jnp.mean(x * x, axis=-1, keepdims=True) + RMS_EPS) * g


def _alibi_slope(g, h):
    return 2.0 ** (-8.0 * (g * HEADS + h + 1) / (N_GROUPS * HEADS))


def _resident(shape):
    nd = len(shape)
    return pl.BlockSpec(shape, lambda *_: (0,) * nd, pipeline_mode=pl.Buffered(1))


def _mod_body(c_ref, w_ref, b_ref, o_ref):
    c = c_ref[...]
    s = c * _sigmoid(c)
    o_ref[...] = jnp.dot(s, w_ref[...], precision=HIGHEST, preferred_element_type=F32) + b_ref[...]


def _modulation(c_all, w_ada, b_ada):
    rows = c_all.shape[0]
    tn = 1024
    return pl.pallas_call(
        _mod_body,
        grid=(6 * D_MODEL // tn,),
        in_specs=[pl.BlockSpec((rows, D_MODEL), lambda j: (0, 0)),
                  pl.BlockSpec((D_MODEL, tn), lambda j: (0, j)),
                  pl.BlockSpec((1, tn), lambda j: (0, j))],
        out_specs=pl.BlockSpec((rows, tn), lambda j: (0, j)),
        out_shape=jax.ShapeDtypeStruct((rows, 6 * D_MODEL), F32),
        compiler_params=pltpu.CompilerParams(dimension_semantics=("arbitrary",), vmem_limit_bytes=VMEM_LIMIT),
        name="modulation",
    )(c_all, w_ada, b_ada.reshape(1, -1))


def _front_body(tiles_per_seq, x_ref, mod_ref, g_ref, w_ref, cw_ref,
                a0_ref, a1_ref, a2_ref, oconv_ref, sga_ref, sgb_ref,
                kv0_ref, kv1_ref, kv2_ref, pconv_ref, res_ref, uprev_ref):
    t_in_seq = pl.program_id(0) % tiles_per_seq
    x = x_ref[...]
    shift1 = mod_ref[:, 0:D_MODEL]
    scale1 = mod_ref[:, D_MODEL:2 * D_MODEL]
    h = (_rmsnorm(x, g_ref[...]) * (1.0 + scale1) + shift1).astype(BF16)

    def proj(c0, n):
        return jnp.dot(h, w_ref[:, c0:c0 + n], preferred_element_type=F32)

    a_refs = (a0_ref, a1_ref, a2_ref)
    kv_refs = (kv0_ref, kv1_ref, kv2_ref)
    for g in range(N_GROUPS):
        d = DILATIONS[g]
        n = TM // d
        for part, base in enumerate((Q_OFF, K_OFF, V_OFF)):
            r = proj(base + g * GROUP_W, GROUP_W)
            cols = slice(part * GROUP_W, (part + 1) * GROUP_W)
            if part == 0:
                r = r * (HEAD_DIM ** -0.5)
            else:
                kvc = slice((part - 1) * GROUP_W, part * GROUP_W)
                kv_refs[g][kvc, :] = (r[TM - BAND:, :] if g == 0 else r).T
            if g == 0:
                a_refs[g][:, cols] = r.astype(BF16)
            else:
                for c in range(GROUP_W // LANES):
                    res_ref[c] = r[:, c * LANES:(c + 1) * LANES]
                for rr in range(d):
                    for c in range(GROUP_W // LANES):
                        c0 = part * GROUP_W + c * LANES
                        a_refs[g][0, rr, :, c0:c0 + LANES] = res_ref[c, pl.ds(rr, n, stride=d), :].astype(BF16)

    bg = proj(BG_OFF, CONV_W)
    u = proj(CG_OFF, CONV_W) * proj(UI_OFF, CONV_W)
    tail = u[TM - 8:, :]
    pconv_ref[...] = tail
    prev = jnp.where(t_in_seq == 0, 0.0, uprev_ref[...])
    row = lax.broadcasted_iota(jnp.int32, (TM, 1), 0)
    u1 = jnp.where(row == 0, prev[7:8, :], pltpu.roll(u, 1, axis=0))
    u2 = jnp.where(row == 0, prev[6:7, :], jnp.where(row == 1, prev[7:8, :], pltpu.roll(u, 2, axis=0)))
    yconv = cw_ref[0:1, :] * u2 + cw_ref[1:2, :] * u1 + cw_ref[2:3, :] * u
    oconv_ref[...] = (bg * yconv).astype(BF16)
    uprev_ref[...] = tail

    sga_ref[...] = _sigmoid(proj(GA_OFF, D_MODEL)).astype(BF16)
    sgb_ref[...] = _sigmoid(proj(GB_OFF, D_MODEL)).astype(BF16)


def _front(x2d, mod_p, g_mix, w_in_bf16, conv_w, batch, seq):
    n_tok = x2d.shape[0]
    n_tiles = n_tok // TM
    tps = seq // TM
    kv2_blocks = ATT_TILE // TM
    out_shape = (
        jax.ShapeDtypeStruct((n_tok, ATTN_W), BF16),
        jax.ShapeDtypeStruct((n_tiles, 4, TM // 4, ATTN_W), BF16),
        jax.ShapeDtypeStruct((n_tiles, 16, TM // 16, ATTN_W), BF16),
        jax.ShapeDtypeStruct((n_tok, CONV_W), BF16),
        jax.ShapeDtypeStruct((n_tok, D_MODEL), BF16),
        jax.ShapeDtypeStruct((n_tok, D_MODEL), BF16),
        jax.ShapeDtypeStruct((batch, 2 * GROUP_W, 128), F32),
        jax.ShapeDtypeStruct((batch, 2 * GROUP_W, 512), F32),
        jax.ShapeDtypeStruct((batch, 2 * GROUP_W, 2048), F32),
        jax.ShapeDtypeStruct((batch, 8, CONV_W), F32),
    )
    out_specs = (
        pl.BlockSpec((TM, ATTN_W), lambda i: (i, 0)),
        pl.BlockSpec((1, 4, TM // 4, ATTN_W), lambda i: (i, 0, 0, 0)),
        pl.BlockSpec((1, 16, TM // 16, ATTN_W), lambda i: (i, 0, 0, 0)),
        pl.BlockSpec((TM, CONV_W), lambda i: (i, 0)),
        pl.BlockSpec((TM, D_MODEL), lambda i: (i, 0)),
        pl.BlockSpec((TM, D_MODEL), lambda i: (i, 0)),
        pl.BlockSpec((None, 2 * GROUP_W, 128), lambda i: (i // tps, 0, 0)),
        pl.BlockSpec((None, 2 * GROUP_W, TM), lambda i: (i // tps, 0, 0)),
        pl.BlockSpec((None, 2 * GROUP_W, TM),
                     lambda i: (i // tps, 0, jnp.maximum(i % tps - (tps - kv2_blocks), 0))),
        pl.BlockSpec((None, 8, CONV_W), lambda i: (i // tps, 0, 0)),
    )
    in_specs = [
        pl.BlockSpec((TM, D_MODEL), lambda i: (i, 0)),
        pl.BlockSpec((None, 1, 6 * D_MODEL), lambda i: (i // tps, 0, 0)),
        _resident((1, D_MODEL)),
        _resident((D_MODEL, IN_COLS)),
        _resident((3, CONV_W)),
    ]
    return pl.pallas_call(
        functools.partial(_front_body, tps),
        grid=(n_tiles,),
        in_specs=in_specs,
        out_specs=out_specs,
        out_shape=out_shape,
        scratch_shapes=[pltpu.VMEM((GROUP_W // LANES, TM, LANES), F32), pltpu.VMEM((8, CONV_W), F32)],
        compiler_params=pltpu.CompilerParams(dimension_semantics=("arbitrary",), vmem_limit_bytes=VMEM_LIMIT),
        name="prompt_front",
    )(x2d, mod_p, g_mix.reshape(1, -1), w_in_bf16, conv_w)


def _band_bias_table():
    qi = jnp.arange(BAND)[:, None]
    kc = jnp.arange(2 * BAND)[None, :]
    delta = qi - (kc - BAND)
    valid = (delta >= 0) & (delta <= BAND)
    tabs = []
    for first in (False, True):
        ok = valid & (kc >= BAND) if first else valid
        for g in range(N_GROUPS):
            for h in range(HEADS):
                b = -_alibi_slope(g, h) * (delta * DILATIONS[g]).astype(F32)
                tabs.append(jnp.where(ok, b, MASKED))
    return jnp.stack(tabs).astype(F32)


def _head_lane_mask(h, dtype=None):
    lane = lax.broadcasted_iota(jnp.int32, (1, GROUP_W), 1)
    return (lane >= h * HEAD_DIM) & (lane < (h + 1) * HEAD_DIM)


def _attn_scores(q, k):
    qs = jnp.concatenate([jnp.where(_head_lane_mask(h), q, jnp.zeros_like(q)) for h in range(HEADS)], axis=0)
    return lax.dot_general(qs, k, (((1,), (1,)), ((), ())), preferred_element_type=F32)


def _attn_probs(s, bias_ref, bias_base):
    s = s + bias_ref[pl.ds(bias_base, HEADS)].reshape(HEADS * BAND, 2 * BAND)
    m = jnp.max(s, axis=-1, keepdims=True)
    e = jnp.exp(s - m)
    l = jnp.sum(e, axis=-1, keepdims=True)
    return (e * (1.0 / l)).astype(BF16), m + jnp.log(l)


def _attn_outputs(p, lse, v):
    ov = jnp.dot(p, v, preferred_element_type=F32)
    o = jnp.zeros((BAND, GROUP_W), F32)
    lse_b = jnp.zeros((BAND, GROUP_W), F32)
    for h in range(HEADS):
        hm = _head_lane_mask(h)
        rows = slice(h * BAND, (h + 1) * BAND)
        o = o + jnp.where(hm, ov[rows], 0.0)
        lse_b = lse_b + jnp.where(hm, lse[rows], 0.0)
    return o, lse_b


def _attn_body(a0c_ref, a0p_ref, a1c_ref, a1p_ref, a2c_ref, a2p_ref, bias_ref, o_ref, og_ref, lg_ref):
    first_tile = pl.program_id(1) == 0
    first_off = jnp.where(first_tile, N_GROUPS * HEADS, 0)
    qs, ks, vs = (slice(0, GROUP_W), slice(GROUP_W, 2 * GROUP_W), slice(2 * GROUP_W, 3 * GROUP_W))

    def run_blocks(blocks):
        loaded = [load() for _, _, _, load in blocks]
        scores = [_attn_scores(q, k) for q, k, _ in loaded]
        probs = [_attn_probs(s, bias_ref, base) for s, (_, _, base, _) in zip(scores, blocks)]
        results = [_attn_outputs(p, lse, v) for (p, lse), (_, _, v) in zip(probs, loaded)]
        for (g, rows, _, _), (o, lse_b) in zip(blocks, results):
            for c in range(GROUP_W // LANES):
                og_ref[g, c, rows, :] = o[:, c * LANES:(c + 1) * LANES]
                lg_ref[g, c, rows, :] = lse_b[:, c * LANES:(c + 1) * LANES]

    def g0_block(n0, base):
        def load():
            q = a0c_ref[pl.ds(n0, BAND), qs]
            if isinstance(n0, int) and n0 == 0:
                k = jnp.concatenate([a0p_ref[:, ks], a0c_ref[0:BAND, ks]], axis=0)
                v = jnp.concatenate([a0p_ref[:, vs], a0c_ref[0:BAND, vs]], axis=0)
            else:
                k = a0c_ref[pl.ds(n0 - BAND, 2 * BAND), ks]
                v = a0c_ref[pl.ds(n0 - BAND, 2 * BAND), vs]
            return q, k, v
        return (0, pl.ds(n0, BAND), base, load)

    run_blocks([g0_block(u * BAND, first_off if u == 0 else 0) for u in range(ATTN_PAIR)])

    def g0_loop(t, carry):
        n0 = pl.multiple_of(t * (ATTN_PAIR * BAND), BAND)
        run_blocks([g0_block(n0 + u * BAND, 0) for u in range(ATTN_PAIR)])
        return carry

    lax.fori_loop(1, ATT_TILE // (ATTN_PAIR * BAND), g0_loop, 0)

    def g1_block(jj, r, prev_ref, prev_jj, base):
        def load():
            q = a1c_ref[jj, r, :, qs]
            k = jnp.concatenate([prev_ref[prev_jj, r, :, ks], a1c_ref[jj, r, :, ks]], axis=0)
            v = jnp.concatenate([prev_ref[prev_jj, r, :, vs], a1c_ref[jj, r, :, vs]], axis=0)
            return q, k, v
        return (1, pl.ds(jj * (4 * BAND) + r, BAND, stride=4), base, load)

    def g1_first(t, carry):
        run_blocks([g1_block(0, t * ATTN_PAIR + u, a1p_ref, 0, first_off + HEADS) for u in range(ATTN_PAIR)])
        return carry

    lax.fori_loop(0, 4 // ATTN_PAIR, g1_first, 0)

    def g1_rest(t, carry):
        jj = 1 + (t * ATTN_PAIR) // 4
        r0 = (t * ATTN_PAIR) % 4
        run_blocks([g1_block(jj, r0 + u, a1c_ref, jj - 1, HEADS) for u in range(ATTN_PAIR)])
        return carry

    lax.fori_loop(0, 12 // ATTN_PAIR, g1_rest, 0)

    n_sub = a2c_ref.shape[0]

    def g2_rows(ref, r, cols):
        return jnp.concatenate([ref[t, r, :, cols] for t in range(n_sub)], axis=0)

    def g2_block(r):
        def load():
            q = g2_rows(a2c_ref, r, qs)
            k = jnp.concatenate([g2_rows(a2p_ref, r, ks), g2_rows(a2c_ref, r, ks)], axis=0)
            v = jnp.concatenate([g2_rows(a2p_ref, r, vs), g2_rows(a2c_ref, r, vs)], axis=0)
            return q, k, v
        return (2, pl.ds(r, BAND, stride=16), first_off + 2 * HEADS, load)

    def g2_loop(t, carry):
        run_blocks([g2_block(t * ATTN_PAIR + u) for u in range(ATTN_PAIR)])
        return carry

    lax.fori_loop(0, 16 // ATTN_PAIR, g2_loop, 0)

    def mix(c, carry):
        rows = pl.ds(pl.multiple_of(c * BAND, BAND), BAND)
        for half in range(GROUP_W // LANES):
            l0, l1, l2 = lg_ref[0, half, rows, :], lg_ref[1, half, rows, :], lg_ref[2, half, rows, :]
            mx = jnp.maximum(jnp.maximum(l0, l1), l2)
            w0, w1, w2 = jnp.exp(l0 - mx), jnp.exp(l1 - mx), jnp.exp(l2 - mx)
            num = w0 * og_ref[0, half, rows, :] + w1 * og_ref[1, half, rows, :] + w2 * og_ref[2, half, rows, :]
            o_ref[rows, half * LANES:(half + 1) * LANES] = (num / (w0 + w1 + w2)).astype(o_ref.dtype)
        return carry

    lax.fori_loop(0, ATT_TILE // BAND, mix, 0)


def _attention(a0, a1, a2, bias, batch, seq):
    steps = seq // ATT_TILE
    sub = ATT_TILE // TM
    a0 = a0.reshape(batch, seq, ATTN_W)
    a1 = a1.reshape(batch, seq // TM, 4, TM // 4, ATTN_W)
    a2 = a2.reshape(batch, seq // TM, 16, TM // 16, ATTN_W)
    in_specs = [
        pl.BlockSpec((None, ATT_TILE, ATTN_W), lambda b, j: (b, j, 0)),
        pl.BlockSpec((None, BAND, ATTN_W), lambda b, j: (b, jnp.maximum(j * (ATT_TILE // BAND) - 1, 0), 0)),
        pl.BlockSpec((None, sub, 4, TM // 4, ATTN_W), lambda b, j: (b, j, 0, 0, 0)),
        pl.BlockSpec((None, 1, 4, TM // 4, ATTN_W), lambda b, j: (b, jnp.maximum(j * sub - 1, 0), 0, 0, 0)),
        pl.BlockSpec((None, sub, 16, TM // 16, ATTN_W), lambda b, j: (b, j, 0, 0, 0)),
        pl.BlockSpec((None, sub, 16, TM // 16, ATTN_W), lambda b, j: (b, jnp.maximum(j - 1, 0), 0, 0, 0)),
        _resident(bias.shape),
    ]
    return pl.pallas_call(
        _attn_body,
        grid=(batch, steps),
        in_specs=in_specs,
        out_specs=pl.BlockSpec((None, ATT_TILE, GROUP_W), lambda b, j: (b, j, 0)),
        out_shape=jax.ShapeDtypeStruct((batch, seq, GROUP_W), BF16),
        scratch_shapes=[pltpu.VMEM((N_GROUPS, GROUP_W // LANES, ATT_TILE, LANES), F32)] * 2,
        compiler_params=pltpu.CompilerParams(dimension_semantics=("arbitrary", "arbitrary"),
                                             vmem_limit_bytes=VMEM_LIMIT),
        name="prompt_attention",
    )(a0, a0, a1, a1, a2, a2, bias)


def _route(logits):
    lane = lax.broadcasted_iota(jnp.int32, logits.shape, 1)
    big = jnp.int32(1 << 20)
    gmask = lane < N_EXPERT_GROUPS
    lg = jnp.where(gmask, logits, MASKED)
    gmax = jnp.max(lg, axis=-1, keepdims=True)
    gidx = jnp.min(jnp.where(gmask & (lg == gmax), lane, big), axis=-1, keepdims=True)
    p_top = 1.0 / jnp.sum(jnp.where(gmask, jnp.exp(lg - gmax), 0.0), axis=-1, keepdims=True)
    lo = EXPERT_LANE0 + EXPERTS_PER_GROUP * gidx
    emask = (lane >= lo) & (lane < lo + EXPERTS_PER_GROUP)
    le = jnp.where(emask, logits, MASKED)
    v1 = jnp.max(le, axis=-1, keepdims=True)
    i1 = jnp.min(jnp.where(emask & (le == v1), lane, big), axis=-1, keepdims=True)
    emask2 = emask & (lane != i1)
    le2 = jnp.where(emask2, logits, MASKED)
    v2 = jnp.max(le2, axis=-1, keepdims=True)
    i2 = jnp.min(jnp.where(emask2 & (le2 == v2), lane, big), axis=-1, keepdims=True)
    e2 = jnp.exp(v2 - v1)
    den = 1.0 + e2
    w1 = (1.0 / den) * p_top
    w2 = (e2 / den) * p_top
    id1 = (i1 - EXPERT_LANE0).astype(F32)
    id2 = (i2 - EXPERT_LANE0).astype(F32)
    return jnp.where(lane == RT_W1, w1, jnp.where(lane == RT_W2, w2,
                     jnp.where(lane == RT_E1, id1, jnp.where(lane == RT_E2, id2, 0.0))))


def _with_ranks(rt, base):
    n = rt.shape[0]
    lane = lax.broadcasted_iota(jnp.int32, rt.shape, 1)
    lane_f = lane.astype(F32)
    oh1 = jnp.where(lane_f == rt[:, RT_E1:RT_E1 + 1], 1.0, 0.0)
    oh2 = jnp.where(lane_f == rt[:, RT_E2:RT_E2 + 1], 1.0, 0.0)
    before = jnp.where(lax.broadcasted_iota(jnp.int32, (n, n), 1) < lax.broadcasted_iota(jnp.int32, (n, n), 0),
                       1.0, 0.0).astype(BF16)
    p1 = jnp.dot(before, oh1.astype(BF16), preferred_element_type=F32)
    p2 = jnp.dot(before, oh2.astype(BF16), preferred_element_type=F32)
    c1 = jnp.sum(oh1, axis=0, keepdims=True)
    c2 = jnp.sum(oh2, axis=0, keepdims=True)
    rank1 = jnp.sum(oh1 * (base + p1), axis=-1, keepdims=True)
    rank2 = jnp.sum(oh2 * (base + c1 + p2), axis=-1, keepdims=True)
    return jnp.where(lane == RT_R1, rank1, jnp.where(lane == RT_R2, rank2, rt)), base + c1 + c2


def _back_body(oa_ref, oc_ref, sga_ref, sgb_ref, x_ref, mod_ref, wpa_ref, wpb_ref, wo_ref, g_ref,
               wrh_ref, wrl_ref, br_ref, x1_ref, h2_ref, rt_ref, cnt_ref, base_ref):
    @pl.when(pl.program_id(0) == 0)
    def _():
        base_ref[...] = jnp.zeros_like(base_ref)

    pa = jnp.dot(oa_ref[...], wpa_ref[...], preferred_element_type=F32)
    pb = jnp.dot(oc_ref[...], wpb_ref[...], preferred_element_type=F32)
    merged = sga_ref[...].astype(F32) * pa + sgb_ref[...].astype(F32) * pb
    gate1 = mod_ref[:, 2 * D_MODEL:3 * D_MODEL]
    shift2 = mod_ref[:, 3 * D_MODEL:4 * D_MODEL]
    scale2 = mod_ref[:, 4 * D_MODEL:5 * D_MODEL]
    x1 = x_ref[...] + gate1 * jnp.dot(merged.astype(BF16), wo_ref[...], preferred_element_type=F32)
    x1_ref[...] = x1
    h2 = _rmsnorm(x1, g_ref[...]) * (1.0 + scale2) + shift2
    hi = h2.astype(BF16)
    h2_ref[...] = hi
    lo = (h2 - hi.astype(F32)).astype(BF16)
    logits = (jnp.dot(hi, wrh_ref[...], preferred_element_type=F32)
              + jnp.dot(lo, wrh_ref[...], preferred_element_type=F32)
              + jnp.dot(hi, wrl_ref[...], preferred_element_type=F32)) + br_ref[...]
    rt, total = _with_ranks(_route(logits), base_ref[...])
    rt_ref[...] = rt
    base_ref[...] = total
    cnt_ref[...] = total


def _back(o_attn, oconv, sga, sgb, x2d, mod_p, wpa, wpb, wo, g_ffn, wr_hi, wr_lo, b_r, seq):
    n_tok = x2d.shape[0]
    n_all = n_tok + TM
    tps = seq // TM
    row = lambda w: pl.BlockSpec((TM, w), lambda i: (i, 0))
    return pl.pallas_call(
        _back_body,
        grid=(n_tok // TM,),
        in_specs=[row(GROUP_W), row(CONV_W), row(D_MODEL), row(D_MODEL), row(D_MODEL),
                  pl.BlockSpec((None, 1, 6 * D_MODEL), lambda i: (i // tps, 0, 0)),
                  _resident(wpa.shape), _resident(wpb.shape), _resident(wo.shape), _resident((1, D_MODEL)),
                  _resident(wr_hi.shape), _resident(wr_lo.shape), _resident(b_r.shape)],
        out_specs=(row(D_MODEL), row(D_MODEL), row(ROUTER_LANES), pl.BlockSpec((1, LANES), lambda i: (0, 0))),
        out_shape=(jax.ShapeDtypeStruct((n_all, D_MODEL), F32),
                   jax.ShapeDtypeStruct((n_all, D_MODEL), BF16),
                   jax.ShapeDtypeStruct((n_all, ROUTER_LANES), F32),
                   jax.ShapeDtypeStruct((1, LANES), F32)),
        scratch_shapes=[pltpu.VMEM((1, LANES), F32)],
        compiler_params=pltpu.CompilerParams(dimension_semantics=("arbitrary",), vmem_limit_bytes=VMEM_LIMIT),
        name="prompt_back",
    )(o_attn, oconv, sga, sgb, x2d, mod_p, wpa, wpb, wo, g_ffn.reshape(1, -1), wr_hi, wr_lo, b_r)


def _valid_chunks(tile, n_prompt_tiles, n_sample):
    return jnp.where(tile < n_prompt_tiles, TM // DMA_CHUNK, n_sample // DMA_CHUNK)


def _dispatch_body(n_prompt_tiles, n_sample, dest_ref, tail_ref, h_ref, xs_ref, stage_ref, zero_ref, sem):
    i = pl.program_id(0)

    def tail_copy(e):
        start = pl.multiple_of(tail_ref[e] * ROW_SUB, ROW_SUB)
        return pltpu.make_async_copy(zero_ref, xs_ref.at[pl.ds(start, EXPERT_TILE * ROW_SUB)], sem)

    @pl.when(i == 0)
    def _():
        zero_ref[...] = jnp.zeros_like(zero_ref)
        for e in range(N_EXPERTS):
            @pl.when(tail_ref[e] >= 0)
            def _():
                tail_copy(e).start()
        for e in range(N_EXPERTS):
            @pl.when(tail_ref[e] >= 0)
            def _():
                tail_copy(e).wait()

    for j in range(ROW_SUB):
        stage_ref[pl.ds(j, TM, stride=ROW_SUB), :] = h_ref[:, j * LANES:(j + 1) * LANES].astype(F32)
    n_chunks = _valid_chunks(i, n_prompt_tiles, n_sample)

    def issue(c, carry):
        for j in range(DMA_CHUNK):
            r = c * DMA_CHUNK + j
            src = stage_ref.at[pl.ds(pl.multiple_of(r * ROW_SUB, ROW_SUB), ROW_SUB)]
            for k in range(2):
                d = pl.multiple_of(dest_ref[(i * TM + r) * 2 + k], ROW_SUB)
                pltpu.make_async_copy(src, xs_ref.at[pl.ds(d, ROW_SUB)], sem).start(priority=k)
        return carry

    lax.fori_loop(0, n_chunks, issue, 0)

    def drain(c, carry):
        n = 2 * DMA_CHUNK * ROW_SUB
        pltpu.make_async_copy(stage_ref.at[pl.ds(0, n)], xs_ref.at[pl.ds(0, n)], sem).wait()
        return carry

    lax.fori_loop(0, n_chunks, drain, 0)


def _dispatch(dest, tail_start, h2, n_rows, n_prompt_tiles, n_sample):
    n_all = h2.shape[0]
    return pl.pallas_call(
        functools.partial(_dispatch_body, n_prompt_tiles, n_sample),
        grid_spec=pltpu.PrefetchScalarGridSpec(
            num_scalar_prefetch=2,
            grid=(n_all // TM,),
            in_specs=[pl.BlockSpec((TM, D_MODEL), lambda i, *_: (i, 0))],
            out_specs=pl.BlockSpec(memory_space=pl.ANY),
            scratch_shapes=[pltpu.VMEM((TM * ROW_SUB, LANES), F32), pltpu.VMEM((EXPERT_TILE * ROW_SUB, LANES), F32),
                            pltpu.SemaphoreType.DMA],
        ),
        out_shape=jax.ShapeDtypeStruct((n_rows * ROW_SUB, LANES), F32),
        compiler_params=pltpu.CompilerParams(dimension_semantics=("arbitrary",), vmem_limit_bytes=VMEM_LIMIT,
                                             disable_bounds_checks=True),
        name="moe_dispatch",
    )(dest, tail_start, h2)


def _experts_body(te_ref, nt_ref, xs_ref, wg_ref, wu_ref, wd_ref, o_ref):
    @pl.when(pl.program_id(0) < nt_ref[0])
    def _():
        x = jnp.concatenate([xs_ref[pl.ds(j, EXPERT_TILE, stride=ROW_SUB), :].astype(BF16) for j in range(ROW_SUB)],
                            axis=1)
        a = jnp.dot(x, wg_ref[0].astype(BF16), preferred_element_type=F32)
        z = (a * _sigmoid(a)) * jnp.dot(x, wu_ref[0].astype(BF16), preferred_element_type=F32)
        o = jnp.dot(z.astype(BF16), wd_ref[0].astype(BF16), preferred_element_type=F32)
        for j in range(ROW_SUB):
            o_ref[pl.ds(j, EXPERT_TILE, stride=ROW_SUB), :] = o[:, j * LANES:(j + 1) * LANES]


def _experts(tile_expert, n_tiles, xs, w_gate_e, w_up_e, w_down_e):
    max_tiles = tile_expert.shape[0]
    rows = lambda s, te, nt: (jnp.minimum(s, nt[0] - 1), 0)
    row_tile = pl.BlockSpec((EXPERT_TILE * ROW_SUB, LANES), rows)
    weight = lambda shape: pl.BlockSpec((1,) + shape, lambda s, te, nt: (te[jnp.minimum(s, nt[0] - 1)], 0, 0))
    return pl.pallas_call(
        _experts_body,
        grid_spec=pltpu.PrefetchScalarGridSpec(
            num_scalar_prefetch=2,
            grid=(max_tiles,),
            in_specs=[row_tile,
                      weight((D_MODEL, EXPERT_FF)), weight((D_MODEL, EXPERT_FF)), weight((EXPERT_FF, D_MODEL))],
            out_specs=row_tile,
        ),
        out_shape=jax.ShapeDtypeStruct(xs.shape, F32),
        compiler_params=pltpu.CompilerParams(dimension_semantics=("arbitrary",), vmem_limit_bytes=VMEM_LIMIT),
        name="moe_experts",
    )(tile_expert, n_tiles, xs, w_gate_e, w_up_e, w_down_e)


def _combine_body(n_prompt_tiles, n_sample, dest_ref, x1_ref, rt_ref, gp_ref, gs_ref, gf_ref, eo_ref,
                  yp_ref, ys_ref, rows_ref, sem):
    i = pl.program_id(0)
    slot = i % 2

    def fetch(tile):
        buf = tile % 2

        def issue(c, carry):
            for j in range(DMA_CHUNK):
                r = c * DMA_CHUNK + j
                row = pl.ds(pl.multiple_of(r * ROW_SUB, ROW_SUB), ROW_SUB)
                for k in range(2):
                    d = pl.multiple_of(dest_ref[(tile * TM + r) * 2 + k], ROW_SUB)
                    pltpu.make_async_copy(eo_ref.at[pl.ds(d, ROW_SUB)], rows_ref.at[buf, k, row],
                                          sem.at[buf]).start(priority=k)
            return carry

        lax.fori_loop(0, _valid_chunks(tile, n_prompt_tiles, n_sample), issue, 0)

    @pl.when(i == 0)
    def _():
        fetch(i)

    @pl.when(i + 1 < pl.num_programs(0))
    def _():
        fetch(i + 1)

    def drain(c, carry):
        n = 2 * DMA_CHUNK * ROW_SUB
        pltpu.make_async_copy(eo_ref.at[pl.ds(0, n)], rows_ref.at[slot, 0, pl.ds(0, n)], sem.at[slot]).wait()
        return carry

    lax.fori_loop(0, _valid_chunks(i, n_prompt_tiles, n_sample), drain, 0)

    rt = rt_ref[...]
    w1, w2 = rt[:, RT_W1:RT_W1 + 1], rt[:, RT_W2:RT_W2 + 1]
    lane_tile = lambda k, j: rows_ref[slot, k, pl.ds(j, TM, stride=ROW_SUB), :]
    moe = jnp.concatenate([w1 * lane_tile(0, j) + w2 * lane_tile(1, j) for j in range(ROW_SUB)], axis=1)
    is_prompt = i < n_prompt_tiles
    gate2 = jnp.where(is_prompt, gp_ref[...], gs_ref[...])
    y = _rmsnorm(x1_ref[...] + gate2 * moe, gf_ref[...])

    @pl.when(is_prompt)
    def _():
        yp_ref[...] = y

    @pl.when(jnp.logical_not(is_prompt))
    def _():
        ys_ref[...] = y[:n_sample, :]


def _combine(dest, x1, rt, gate2_p, gate2_s, g_final, eo, n_prompt_tiles, n_sample, tiles_per_seq):
    n_all = x1.shape[0]
    last_p = n_prompt_tiles - 1
    return pl.pallas_call(
        functools.partial(_combine_body, n_prompt_tiles, n_sample),
        grid_spec=pltpu.PrefetchScalarGridSpec(
            num_scalar_prefetch=1,
            grid=(n_all // TM,),
            in_specs=[pl.BlockSpec((TM, D_MODEL), lambda i, *_: (i, 0)),
                      pl.BlockSpec((TM, ROUTER_LANES), lambda i, *_: (i, 0)),
                      pl.BlockSpec((None, 1, D_MODEL), lambda i, *_: (jnp.minimum(i, last_p) // tiles_per_seq, 0, 0)),
                      pl.BlockSpec((TM, D_MODEL), lambda i, *_: (0, 0)),
                      pl.BlockSpec((1, D_MODEL), lambda i, *_: (0, 0)),
                      pl.BlockSpec(memory_space=pl.ANY)],
            out_specs=(pl.BlockSpec((TM, D_MODEL), lambda i, *_: (jnp.minimum(i, last_p), 0)),
                       pl.BlockSpec((n_sample, D_MODEL), lambda i, *_: (0, 0))),
            scratch_shapes=[pltpu.VMEM((2, 2, TM * ROW_SUB, LANES), F32), pltpu.SemaphoreType.DMA((2,))],
        ),
        out_shape=(jax.ShapeDtypeStruct((n_prompt_tiles * TM, D_MODEL), F32),
                   jax.ShapeDtypeStruct((n_sample, D_MODEL), F32)),
        compiler_params=pltpu.CompilerParams(dimension_semantics=("arbitrary",), vmem_limit_bytes=VMEM_LIMIT,
                                             disable_bounds_checks=True),
        name="moe_combine",
    )(dest, x1, rt, gate2_p, gate2_s, g_final.reshape(1, -1), eo)


def _routing_tables(rt, counts, max_tiles):
    cnt = counts[0, :N_EXPERTS].astype(jnp.int32)
    padded = (cnt + EXPERT_TILE - 1) // EXPERT_TILE * EXPERT_TILE
    ends = jnp.cumsum(padded)
    starts = ends - padded
    n_tiles = (ends[-1] // EXPERT_TILE).reshape(1)
    tile_row0 = jnp.arange(max_tiles, dtype=jnp.int32) * EXPERT_TILE
    tile_expert = jnp.minimum(jnp.sum(ends[None, :] <= tile_row0[:, None], axis=1), N_EXPERTS - 1).astype(jnp.int32)
    tail_start = jnp.where(cnt > 0, ends - EXPERT_TILE, -1).astype(jnp.int32)
    eid = rt[:, RT_E1:RT_E2 + 1].astype(jnp.int32)
    onehot = eid[..., None] == jnp.arange(N_EXPERTS, dtype=jnp.int32)
    dest = rt[:, RT_R1:RT_R2 + 1].astype(jnp.int32) + jnp.sum(jnp.where(onehot, starts, 0), axis=-1)
    dest = (jnp.where(eid >= 0, dest, 0) * ROW_SUB).reshape(-1)
    return dest, tail_start, tile_expert, n_tiles


def _sample_front_body(x_ref, mod_ref, g_ref, w_ref, o_ref):
    shift1 = mod_ref[:, 0:D_MODEL]
    scale1 = mod_ref[:, D_MODEL:2 * D_MODEL]
    h = _rmsnorm(x_ref[...], g_ref[...]) * (1.0 + scale1) + shift1
    o_ref[...] = jnp.dot(h, w_ref[...], precision=HIGHEST, preferred_element_type=F32)


def _sample_front(x_s, mod_s, g_mix, w_in):
    nb = x_s.shape[0]
    tn = 256
    return pl.pallas_call(
        _sample_front_body,
        grid=(IN_COLS // tn,),
        in_specs=[pl.BlockSpec((nb, D_MODEL), lambda j: (0, 0)),
                  pl.BlockSpec((nb, 2 * D_MODEL), lambda j: (0, 0)),
                  pl.BlockSpec((1, D_MODEL), lambda j: (0, 0)),
                  pl.BlockSpec((D_MODEL, tn), lambda j: (0, j))],
        out_specs=pl.BlockSpec((nb, tn), lambda j: (0, j)),
        out_shape=jax.ShapeDtypeStruct((nb, IN_COLS), F32),
        compiler_params=pltpu.CompilerParams(dimension_semantics=("arbitrary",), vmem_limit_bytes=VMEM_LIMIT),
        name="sample_front",
    )(x_s, mod_s, g_mix.reshape(1, -1), w_in)


def _sample_attn_body(qkv_ref, c0_ref, c1_ref, c2_ref, oa_ref):
    head = lax.broadcasted_iota(jnp.int32, (HEADS, 1, 1), 0)
    for b in range(qkv_ref.shape[0]):
        outs, lses = [], []
        for g, c_ref in enumerate((c0_ref, c1_ref, c2_ref)):
            window = c_ref.shape[-1]
            def new_rows(part):
                c0 = (part * N_GROUPS + g) * HEADS
                return jnp.stack([qkv_ref[b, :, c0 + h:c0 + h + 1] for h in range(HEADS)], axis=0)

            q = new_rows(0) * (HEAD_DIM ** -0.5)
            k_new = new_rows(1)
            v_new = new_rows(2)
            s = jnp.sum(c_ref[b, 0] * q, axis=1, keepdims=True)
            pos = lax.broadcasted_iota(jnp.int32, (1, 1, window), 2)
            slope = jnp.full((HEADS, 1, 1), _alibi_slope(g, HEADS - 1), F32)
            for h in range(HEADS - 1):
                slope = jnp.where(head == h, _alibi_slope(g, h), slope)
            on_band = (pos & (DILATIONS[g] - 1)) == 0
            s = jnp.where(on_band, s - slope * (window - pos).astype(F32), MASKED)
            s_self = jnp.sum(q * k_new, axis=1, keepdims=True)
            m = jnp.maximum(jnp.max(s, axis=2, keepdims=True), s_self)
            e = jnp.exp(s - m)
            e_self = jnp.exp(s_self - m)
            l = jnp.sum(e, axis=2, keepdims=True) + e_self
            o = jnp.sum(c_ref[b, 1] * e, axis=2, keepdims=True) + e_self * v_new
            outs.append(o / l)
            lses.append(m + jnp.log(l))
        mx = jnp.maximum(jnp.maximum(lses[0], lses[1]), lses[2])
        w = [jnp.exp(x - mx) for x in lses]
        mixed = (w[0] * outs[0] + w[1] * outs[1] + w[2] * outs[2]) / (w[0] + w[1] + w[2])
        for h in range(HEADS):
            oa_ref[b, :, h:h + 1] = mixed[h]


def _sample_attention(qkv_s, cache0, cache1, cache2):
    nb = qkv_s.shape[0]
    bb = 2
    hd = (HEADS, HEAD_DIM)
    caches = [jnp.transpose(c, (0, 1, 3, 4, 5, 2)) for c in (cache0, cache1, cache2)]
    cache_spec = lambda c: pl.BlockSpec((None, bb, 2, *hd, c.shape[-1]), lambda i: (0, i, 0, 0, 0, 0))
    return pl.pallas_call(
        _sample_attn_body,
        grid=(nb // bb,),
        in_specs=[pl.BlockSpec((bb,) + qkv_s.shape[1:], lambda i: (i, 0, 0))] + [cache_spec(c) for c in caches],
        out_specs=pl.BlockSpec((bb, HEAD_DIM, HEADS), lambda i: (i, 0, 0)),
        out_shape=jax.ShapeDtypeStruct((nb, HEAD_DIM, HEADS), F32),
        compiler_params=pltpu.CompilerParams(dimension_semantics=("arbitrary",), vmem_limit_bytes=VMEM_LIMIT),
        name="sample_attention",
    )(qkv_s, *caches)


def _sample_back_body(p_ref, oa_ref, cc_ref, x_ref, mod_ref, cw_ref, wpa_ref, wpb_ref, wo_ref, g_ref,
                      wr_ref, br_ref, cntp_ref, x1_any, h2_any, rt_any, x1_ref, h2_ref, rt_ref, sconv_ref, cnt_ref):
    del x1_any, h2_any, rt_any
    nb = x_ref.shape[0]
    hdot = functools.partial(jnp.dot, precision=HIGHEST, preferred_element_type=F32)
    bg = p_ref[:, BG_OFF:BG_OFF + CONV_W]
    u = p_ref[:, CG_OFF:CG_OFF + CONV_W] * p_ref[:, UI_OFF:UI_OFF + CONV_W]
    c_old, c_new = cc_ref[:, 0, :], cc_ref[:, 1, :]
    yconv = cw_ref[0:1, :] * c_old + cw_ref[1:2, :] * c_new + cw_ref[2:3, :] * u
    sconv_ref[:, 0, :] = c_new
    sconv_ref[:, 1, :] = u
    sga = _sigmoid(p_ref[:, GA_OFF:GA_OFF + D_MODEL])
    sgb = _sigmoid(p_ref[:, GB_OFF:GB_OFF + D_MODEL])
    merged = sga * hdot(oa_ref[...], wpa_ref[...]) + sgb * hdot(bg * yconv, wpb_ref[...])
    gate1 = mod_ref[:, 2 * D_MODEL:3 * D_MODEL]
    shift2 = mod_ref[:, 3 * D_MODEL:4 * D_MODEL]
    scale2 = mod_ref[:, 4 * D_MODEL:5 * D_MODEL]
    x1 = x_ref[...] + gate1 * hdot(merged, wo_ref[...])
    h2 = _rmsnorm(x1, g_ref[...]) * (1.0 + scale2) + shift2
    rt = _route(hdot(h2, wr_ref[...]) + br_ref[...])
    lane = lax.broadcasted_iota(jnp.int32, (TM, ROUTER_LANES), 1)
    x1_ref[...] = jnp.zeros_like(x1_ref)
    h2_ref[...] = jnp.zeros_like(h2_ref)
    rt_ref[...] = jnp.where((lane == RT_E1) | (lane == RT_E2), -1.0, 0.0)
    x1_ref[0:nb, :] = x1
    h2_ref[0:nb, :] = h2.astype(BF16)
    rt_ref[0:nb, :] = rt
    ranked, total = _with_ranks(rt_ref[...], cntp_ref[...])
    rt_ref[...] = ranked
    cnt_ref[...] = total


def _sample_back(proj_s, oa_s, cache_conv, x_s, mod_s, conv_w, w_pa, w_pb, w_o, g_ffn, w_r, b_r, cnt_p, x1, h2, rt):
    nb = x_s.shape[0]
    last = x1.shape[0] // TM - 1
    args = (proj_s, oa_s, cache_conv, x_s, mod_s, conv_w, w_pa, w_pb, w_o, g_ffn.reshape(1, -1), w_r, b_r, cnt_p)
    full = lambda shape: pl.BlockSpec(shape, lambda i: (0,) * len(shape))
    last_tile = lambda a: pl.BlockSpec((TM, a.shape[1]), lambda i: (last, 0))
    shape_of = lambda a: jax.ShapeDtypeStruct(a.shape, a.dtype)
    return pl.pallas_call(
        _sample_back_body,
        grid=(1,),
        in_specs=[full(a.shape) for a in args] + [pl.BlockSpec(memory_space=pl.ANY)] * 3,
        out_specs=(last_tile(x1), last_tile(h2), last_tile(rt), full((nb, 2, CONV_W)), full((1, LANES))),
        out_shape=(shape_of(x1), shape_of(h2), shape_of(rt), jax.ShapeDtypeStruct((nb, 2, CONV_W), F32),
                   jax.ShapeDtypeStruct((1, LANES), F32)),
        input_output_aliases={len(args): 0, len(args) + 1: 1, len(args) + 2: 2},
        compiler_params=pltpu.CompilerParams(dimension_semantics=("arbitrary",), vmem_limit_bytes=VMEM_LIMIT),
        name="sample_back",
    )(*args, x1, h2, rt)


def kernel(x_prompt, x_sample, cache_kv_w128, cache_kv_w512, cache_kv_w2048, cache_conv, c_prompt, c_sample,
           g_mix, w_ada, b_ada, w_in, conv_w, w_pa, w_pb, w_o, g_ffn, w_router_group, b_router_group,
           w_router_expert, b_router_expert, w_gate_e, w_up_e, w_down_e, g_final):
    batch, seq, _ = x_prompt.shape
    nb = x_sample.shape[0]
    assert x_sample.shape[1] == 1 and g_mix.shape[0] == 1, "one layer, one new sample token per sequence"
    assert seq % ATT_TILE == 0 and cache_kv_w128.shape[2] == 128 and cache_kv_w512.shape[2] == 512 \
        and cache_kv_w2048.shape[2] == 2048
    (g_mix, w_ada, b_ada, w_in, conv_w, w_pa, w_pb, w_o, g_ffn, w_rg, b_rg, w_re, b_re, w_gate_e, w_up_e,
     w_down_e) = (a[0] for a in (g_mix, w_ada, b_ada, w_in, conv_w, w_pa, w_pb, w_o, g_ffn, w_router_group,
                                 b_router_group, w_router_expert, b_router_expert, w_gate_e, w_up_e, w_down_e))

    c_all = jnp.concatenate([c_prompt, jnp.zeros((8 - batch, D_MODEL), F32), c_sample], axis=0)
    mod = _modulation(c_all, w_ada, b_ada)
    mod_p = mod[:batch].reshape(batch, 1, 6 * D_MODEL)
    mod_s = mod[8:]

    w_r = jnp.zeros((D_MODEL, ROUTER_LANES), F32)
    w_r = w_r.at[:, :N_EXPERT_GROUPS].set(w_rg).at[:, EXPERT_LANE0:EXPERT_LANE0 + N_EXPERTS].set(w_re)
    b_r = jnp.full((1, ROUTER_LANES), MASKED, F32)
    b_r = b_r.at[0, :N_EXPERT_GROUPS].set(b_rg).at[0, EXPERT_LANE0:EXPERT_LANE0 + N_EXPERTS].set(b_re)
    wr_hi = w_r.astype(BF16)
    wr_lo = (w_r - wr_hi.astype(F32)).astype(BF16)

    x2d = x_prompt.reshape(batch * seq, D_MODEL)
    (a0, a1, a2, oconv, sga, sgb, kv0, kv1, kv2, pconv) = _front(
        x2d, mod_p, g_mix, w_in.astype(BF16), conv_w, batch, seq)
    o_attn = _attention(a0, a1, a2, _band_bias_table(), batch, seq).reshape(batch * seq, GROUP_W)
    x1, h2, rt, cnt_p = _back(o_attn, oconv, sga, sgb, x2d, mod_p, w_pa.astype(BF16), w_pb.astype(BF16),
                              w_o.astype(BF16), g_ffn, wr_hi, wr_lo, b_r, seq)

    x_s = x_sample.reshape(nb, D_MODEL)
    proj_s = _sample_front(x_s, mod_s[:, :2 * D_MODEL], g_mix, w_in)
    qkv_s = proj_s[:, :3 * ATTN_W].reshape(nb, 3, N_GROUPS, HEADS, HEAD_DIM)
    qkv_t = jnp.transpose(qkv_s.reshape(nb, 3 * N_GROUPS * HEADS, HEAD_DIM), (0, 2, 1))
    oa_s = _sample_attention(qkv_t, cache_kv_w128, cache_kv_w512, cache_kv_w2048)
    oa_s = jnp.transpose(oa_s, (0, 2, 1)).reshape(nb, GROUP_W)
    skv0, skv1, skv2 = (jnp.stack([qkv_s[:, 1, g], qkv_s[:, 2, g]], axis=1) for g in range(N_GROUPS))
    x1, h2, rt, sconv, counts = _sample_back(proj_s, oa_s, cache_conv[0], x_s, mod_s, conv_w,
                                             w_pa, w_pb, w_o, g_ffn, w_r, b_r, cnt_p, x1, h2, rt)

    n_prompt_tiles = batch * seq // TM
    max_tiles = -(-(2 * (batch * seq + nb) + N_EXPERTS * (EXPERT_TILE - 1)) // EXPERT_TILE)
    dest, tail_start, tile_expert, n_tiles = _routing_tables(rt, counts, max_tiles)
    xs = _dispatch(dest, tail_start, h2, max_tiles * EXPERT_TILE, n_prompt_tiles, nb)
    eo = _experts(tile_expert, n_tiles, xs, w_gate_e, w_up_e, w_down_e)
    gate2_s = jnp.zeros((TM, D_MODEL), F32).at[:nb].set(mod_s[:, 5 * D_MODEL:])
    y_prompt, y_sample = _combine(dest, x1, rt, mod_p[:, :, 5 * D_MODEL:], gate2_s, g_final, eo,
                                  n_prompt_tiles, nb, seq // TM)

    def prompt_state(a):
        a = a.reshape(batch, 2, HEADS, HEAD_DIM, a.shape[-1])
        return jnp.transpose(a, (0, 4, 1, 2, 3))[None]

    sample_state = lambda a: a.reshape(1, nb, 1, 2, HEADS, HEAD_DIM)
    return (y_prompt.reshape(batch, seq, D_MODEL), y_sample.reshape(nb, 1, D_MODEL),
            prompt_state(kv0), prompt_state(kv1), prompt_state(kv2),
            pconv[:, 6:8, :].reshape(1, batch, 2, CONV_W),
            sample_state(skv0), sample_state(skv1), sample_state(skv2),
            sconv.reshape(1, nb, 2, CONV_W))
```

```python
import functools
import math

import jax
import jax.numpy as jnp
from jax import lax
from jax.experimental import pallas as pl
from jax.experimental.pallas import tpu as pltpu

F32 = jnp.float32
BF16 = jnp.bfloat16
HIGHEST = lax.Precision.HIGHEST

D_MODEL = 1024
HEAD_DIM = 64
HEADS = 4
N_GROUPS = 3
GROUP_W = HEADS * HEAD_DIM
ATTN_W = N_GROUPS * GROUP_W
CONV_W = 512
DILATIONS = (1, 4, 16)
BAND = 128
N_EXPERTS = 32
EXPERTS_PER_GROUP = 8
N_EXPERT_GROUPS = 4
EXPERT_FF = 256
RMS_EPS = 1e-6
MASKED = -1e30

Q_OFF, K_OFF, V_OFF = 0, ATTN_W, 2 * ATTN_W
BG_OFF = 3 * ATTN_W
CG_OFF = BG_OFF + CONV_W
UI_OFF = CG_OFF + CONV_W
GA_OFF = UI_OFF + CONV_W
GB_OFF = GA_OFF + D_MODEL
IN_COLS = GB_OFF + D_MODEL

LANES = 128
ROUTER_LANES = 128
EXPERT_LANE0 = N_EXPERT_GROUPS

TM = 512
ATT_TILE = 2048
BACK_PARTS = 2
ATTN_PAIR = 4
EXPERT_TILE = 512
RT_W1, RT_W2, RT_E1, RT_E2, RT_R1, RT_R2 = 0, 1, 2, 3, 4, 5
RT_ROWS = 8
ROW_SUB = D_MODEL // LANES
DMA_CHUNK = 8
VMEM_LIMIT = 56 * 1024 * 1024


def _sigmoid(x):
    return 1.0 / (1.0 + jnp.exp(-x))


def _rmsnorm(x, g):
    return x * lax.rsqrt(jnp.mean(x * x, axis=-1, keepdims=True) + RMS_EPS) * g


def _alibi_slope(g, h):
    return 2.0 ** (-8.0 * (g * HEADS + h + 1) / (N_GROUPS * HEADS))


def _resident(shape):
    nd = len(shape)
    return pl.BlockSpec(shape, lambda *_: (0,) * nd, pipeline_mode=pl.Buffered(1))


def _mod_body(c_ref, w_ref, b_ref, o_ref):
    c = c_ref[...]
    s = c * _sigmoid(c)
    o_ref[...] = jnp.dot(s, w_ref[...], precision=HIGHEST, preferred_element_type=F32) + b_ref[...]


def _modulation(c_all, w_ada, b_ada):
    rows = c_all.shape[0]
    tn = 1024
    return pl.pallas_call(
        _mod_body,
        grid=(6 * D_MODEL // tn,),
        in_specs=[pl.BlockSpec((rows, D_MODEL), lambda j: (0, 0)),
                  pl.BlockSpec((D_MODEL, tn), lambda j: (0, j)),
                  pl.BlockSpec((1, tn), lambda j: (0, j))],
        out_specs=pl.BlockSpec((rows, tn), lambda j: (0, j)),
        out_shape=jax.ShapeDtypeStruct((rows, 6 * D_MODEL), F32),
        compiler_params=pltpu.CompilerParams(dimension_semantics=("arbitrary",), vmem_limit_bytes=VMEM_LIMIT),
        name="modulation",
    )(c_all, w_ada, b_ada.reshape(1, -1))


def _front_body(tiles_per_seq, x_ref, mod_ref, g_ref, w_ref, cw_ref,
                a0_ref, a1_ref, a2_ref, oconv_ref, sga_ref, sgb_ref,
                kv0_ref, kv1_ref, kv2_ref, pconv_ref, res_ref, uprev_ref):
    t_in_seq = pl.program_id(0) % tiles_per_seq
    x = x_ref[...]
    shift1 = mod_ref[:, 0:D_MODEL]
    scale1 = mod_ref[:, D_MODEL:2 * D_MODEL]
    h = (_rmsnorm(x, g_ref[...]) * (1.0 + scale1) + shift1).astype(BF16)

    def proj(c0, n):
        return jnp.dot(h, w_ref[:, c0:c0 + n], preferred_element_type=F32)

    a_refs = (a0_ref, a1_ref, a2_ref)
    kv_refs = (kv0_ref, kv1_ref, kv2_ref)
    for g in range(N_GROUPS):
        d = DILATIONS[g]
        n = TM // d
        for part, base in enumerate((Q_OFF, K_OFF, V_OFF)):
            r = proj(base + g * GROUP_W, GROUP_W)
            cols = slice(part * GROUP_W, (part + 1) * GROUP_W)
            if part == 0:
                r = r * (HEAD_DIM ** -0.5)
            else:
                kvc = slice((part - 1) * GROUP_W, part * GROUP_W)
                kv_refs[g][kvc, :] = (r[TM - BAND:, :] if g == 0 else r).T
            if g == 0:
                a_refs[g][:, cols] = r.astype(BF16)
            else:
                for c in range(GROUP_W // LANES):
                    res_ref[c] = r[:, c * LANES:(c + 1) * LANES]
                for rr in range(d):
                    for c in range(GROUP_W // LANES):
                        c0 = part * GROUP_W + c * LANES
                        a_refs[g][0, rr, :, c0:c0 + LANES] = res_ref[c, pl.ds(rr, n, stride=d), :].astype(BF16)

    bg = proj(BG_OFF, CONV_W)
    u = proj(CG_OFF, CONV_W) * proj(UI_OFF, CONV_W)
    tail = u[TM - 8:, :]
    pconv_ref[...] = tail
    prev = jnp.where(t_in_seq == 0, 0.0, uprev_ref[...])
    row = lax.broadcasted_iota(jnp.int32, (TM, 1), 0)
    u1 = jnp.where(row == 0, prev[7:8, :], pltpu.roll(u, 1, axis=0))
    u2 = jnp.where(row == 0, prev[6:7, :], jnp.where(row == 1, prev[7:8, :], pltpu.roll(u, 2, axis=0)))
    yconv = cw_ref[0:1, :] * u2 + cw_ref[1:2, :] * u1 + cw_ref[2:3, :] * u
    oconv_ref[...] = (bg * yconv).astype(BF16)
    uprev_ref[...] = tail

    sga_ref[...] = _sigmoid(proj(GA_OFF, D_MODEL)).astype(BF16)
    sgb_ref[...] = _sigmoid(proj(GB_OFF, D_MODEL)).astype(BF16)


def _front(x2d, mod_p, g_mix, w_in_bf16, conv_w, batch, seq):
    n_tok = x2d.shape[0]
    n_tiles = n_tok // TM
    tps = seq // TM
    kv2_blocks = ATT_TILE // TM
    out_shape = (
        jax.ShapeDtypeStruct((n_tok, ATTN_W), BF16),
        jax.ShapeDtypeStruct((n_tiles, 4, TM // 4, ATTN_W), BF16),
        jax.ShapeDtypeStruct((n_tiles, 16, TM // 16, ATTN_W), BF16),
        jax.ShapeDtypeStruct((n_tok, CONV_W), BF16),
        jax.ShapeDtypeStruct((n_tok, D_MODEL), BF16),
        jax.ShapeDtypeStruct((n_tok, D_MODEL), BF16),
        jax.ShapeDtypeStruct((batch, 2 * GROUP_W, 128), F32),
        jax.ShapeDtypeStruct((batch, 2 * GROUP_W, 512), F32),
        jax.ShapeDtypeStruct((batch, 2 * GROUP_W, 2048), F32),
        jax.ShapeDtypeStruct((batch, 8, CONV_W), F32),
    )
    out_specs = (
        pl.BlockSpec((TM, ATTN_W), lambda i: (i, 0)),
        pl.BlockSpec((1, 4, TM // 4, ATTN_W), lambda i: (i, 0, 0, 0)),
        pl.BlockSpec((1, 16, TM // 16, ATTN_W), lambda i: (i, 0, 0, 0)),
        pl.BlockSpec((TM, CONV_W), lambda i: (i, 0)),
        pl.BlockSpec((TM, D_MODEL), lambda i: (i, 0)),
        pl.BlockSpec((TM, D_MODEL), lambda i: (i, 0)),
        pl.BlockSpec((None, 2 * GROUP_W, 128), lambda i: (i // tps, 0, 0)),
        pl.BlockSpec((None, 2 * GROUP_W, TM), lambda i: (i // tps, 0, 0)),
        pl.BlockSpec((None, 2 * GROUP_W, TM),
                     lambda i: (i // tps, 0, jnp.maximum(i % tps - (tps - kv2_blocks), 0))),
        pl.BlockSpec((None, 8, CONV_W), lambda i: (i // tps, 0, 0)),
    )
    in_specs = [
        pl.BlockSpec((TM, D_MODEL), lambda i: (i, 0)),
        pl.BlockSpec((None, 1, 6 * D_MODEL), lambda i: (i // tps, 0, 0)),
        _resident((1, D_MODEL)),
        _resident((D_MODEL, IN_COLS)),
        _resident((3, CONV_W)),
    ]
    return pl.pallas_call(
        functools.partial(_front_body, tps),
        grid=(n_tiles,),
        in_specs=in_specs,
        out_specs=out_specs,
        out_shape=out_shape,
        scratch_shapes=[pltpu.VMEM((GROUP_W // LANES, TM, LANES), F32), pltpu.VMEM((8, CONV_W), F32)],
        compiler_params=pltpu.CompilerParams(dimension_semantics=("arbitrary",), vmem_limit_bytes=VMEM_LIMIT),
        name="prompt_front",
    )(x2d, mod_p, g_mix.reshape(1, -1), w_in_bf16, conv_w)


def _band_bias_table():
    qi = jnp.arange(BAND)[:, None]
    kc = jnp.arange(2 * BAND)[None, :]
    delta = qi - (kc - BAND)
    valid = (delta >= 0) & (delta <= BAND)
    tabs = []
    for first in (False, True):
        ok = valid & (kc >= BAND) if first else valid
        for g in range(N_GROUPS):
            for h in range(HEADS):
                b = -_alibi_slope(g, h) * (delta * DILATIONS[g]).astype(F32)
                tabs.append(jnp.where(ok, b, MASKED))
    return jnp.stack(tabs).astype(F32)


def _head_lane_mask(h, dtype=None):
    lane = lax.broadcasted_iota(jnp.int32, (1, GROUP_W), 1)
    return (lane >= h * HEAD_DIM) & (lane < (h + 1) * HEAD_DIM)


def _attn_scores(q, k):
    qs = jnp.concatenate([jnp.where(_head_lane_mask(h), q, jnp.zeros_like(q)) for h in range(HEADS)], axis=0)
    return lax.dot_general(qs, k, (((1,), (1,)), ((), ())), preferred_element_type=F32)


def _attn_probs(s, bias_ref, bias_base):
    s = s + bias_ref[pl.ds(bias_base, HEADS)].reshape(HEADS * BAND, 2 * BAND)
    m = jnp.max(s, axis=-1, keepdims=True)
    e = jnp.exp(s - m)
    l = jnp.sum(e, axis=-1, keepdims=True)
    return (e * (1.0 / l)).astype(BF16), m + jnp.log(l)


def _attn_outputs(p, lse, v):
    ov = jnp.dot(p, v, preferred_element_type=F32)
    o = jnp.zeros((BAND, GROUP_W), F32)
    lse_b = jnp.zeros((BAND, GROUP_W), F32)
    for h in range(HEADS):
        hm = _head_lane_mask(h)
        rows = slice(h * BAND, (h + 1) * BAND)
        o = o + jnp.where(hm, ov[rows], 0.0)
        lse_b = lse_b + jnp.where(hm, lse[rows], 0.0)
    return o, lse_b


def _attn_body(a0c_ref, a0p_ref, a1c_ref, a1p_ref, a2c_ref, a2p_ref, bias_ref, o_ref, og_ref, lg_ref):
    first_tile = pl.program_id(1) == 0
    first_off = jnp.where(first_tile, N_GROUPS * HEADS, 0)
    qs, ks, vs = (slice(0, GROUP_W), slice(GROUP_W, 2 * GROUP_W), slice(2 * GROUP_W, 3 * GROUP_W))

    def run_blocks(blocks):
        loaded = [load() for _, _, _, load in blocks]
        scores = [_attn_scores(q, k) for q, k, _ in loaded]
        probs = [_attn_probs(s, bias_ref, base) for s, (_, _, base, _) in zip(scores, blocks)]
        results = [_attn_outputs(p, lse, v) for (p, lse), (_, _, v) in zip(probs, loaded)]
        for (g, rows, _, _), (o, lse_b) in zip(blocks, results):
            for c in range(GROUP_W // LANES):
                og_ref[g, c, rows, :] = o[:, c * LANES:(c + 1) * LANES]
                lg_ref[g, c, rows, :] = lse_b[:, c * LANES:(c + 1) * LANES]

    def g0_block(n0, base):
        def load():
            q = a0c_ref[pl.ds(n0, BAND), qs]
            if isinstance(n0, int) and n0 == 0:
                k = jnp.concatenate([a0p_ref[:, ks], a0c_ref[0:BAND, ks]], axis=0)
                v = jnp.concatenate([a0p_ref[:, vs], a0c_ref[0:BAND, vs]], axis=0)
            else:
                k = a0c_ref[pl.ds(n0 - BAND, 2 * BAND), ks]
                v = a0c_ref[pl.ds(n0 - BAND, 2 * BAND), vs]
            return q, k, v
        return (0, pl.ds(n0, BAND), base, load)

    run_blocks([g0_block(u * BAND, first_off if u == 0 else 0) for u in range(ATTN_PAIR)])

    def g0_loop(t, carry):
        n0 = pl.multiple_of(t * (ATTN_PAIR * BAND), BAND)
        run_blocks([g0_block(n0 + u * BAND, 0) for u in range(ATTN_PAIR)])
        return carry

    lax.fori_loop(1, ATT_TILE // (ATTN_PAIR * BAND), g0_loop, 0)

    def g1_block(jj, r, prev_ref, prev_jj, base):
        def load():
            q = a1c_ref[jj, r, :, qs]
            k = jnp.concatenate([prev_ref[prev_jj, r, :, ks], a1c_ref[jj, r, :, ks]], axis=0)
            v = jnp.concatenate([prev_ref[prev_jj, r, :, vs], a1c_ref[jj, r, :, vs]], axis=0)
            return q, k, v
        return (1, pl.ds(jj * (4 * BAND) + r, BAND, stride=4), base, load)

    def g1_first(t, carry):
        run_blocks([g1_block(0, t * ATTN_PAIR + u, a1p_ref, 0, first_off + HEADS) for u in range(ATTN_PAIR)])
        return carry

    lax.fori_loop(0, 4 // ATTN_PAIR, g1_first, 0)

    def g1_rest(t, carry):
        jj = 1 + (t * ATTN_PAIR) // 4
        r0 = (t * ATTN_PAIR) % 4
        run_blocks([g1_block(jj, r0 + u, a1c_ref, jj - 1, HEADS) for u in range(ATTN_PAIR)])
        return carry

    lax.fori_loop(0, 12 // ATTN_PAIR, g1_rest, 0)

    n_sub = a2c_ref.shape[0]

    def g2_rows(ref, r, cols):
        return jnp.concatenate([ref[t, r, :, cols] for t in range(n_sub)], axis=0)

    def g2_block(r):
        def load():
            q = g2_rows(a2c_ref, r, qs)
            k = jnp.concatenate([g2_rows(a2p_ref, r, ks), g2_rows(a2c_ref, r, ks)], axis=0)
            v = jnp.concatenate([g2_rows(a2p_ref, r, vs), g2_rows(a2c_ref, r, vs)], axis=0)
            return q, k, v
        return (2, pl.ds(r, BAND, stride=16), first_off + 2 * HEADS, load)

    def g2_loop(t, carry):
        run_blocks([g2_block(t * ATTN_PAIR + u) for u in range(ATTN_PAIR)])
        return carry

    lax.fori_loop(0, 16 // ATTN_PAIR, g2_loop, 0)

    def mix(c, carry):
        rows = pl.ds(pl.multiple_of(c * BAND, BAND), BAND)
        for half in range(GROUP_W // LANES):
            l0, l1, l2 = lg_ref[0, half, rows, :], lg_ref[1, half, rows, :], lg_ref[2, half, rows, :]
            mx = jnp.maximum(jnp.maximum(l0, l1), l2)
            w0, w1, w2 = jnp.exp(l0 - mx), jnp.exp(l1 - mx), jnp.exp(l2 - mx)
            num = w0 * og_ref[0, half, rows, :] + w1 * og_ref[1, half, rows, :] + w2 * og_ref[2, half, rows, :]
            o_ref[rows, half * LANES:(half + 1) * LANES] = (num / (w0 + w1 + w2)).astype(o_ref.dtype)
        return carry

    lax.fori_loop(0, ATT_TILE // BAND, mix, 0)


def _attention(a0, a1, a2, bias, batch, seq):
    steps = seq // ATT_TILE
    sub = ATT_TILE // TM
    a0 = a0.reshape(batch, seq, ATTN_W)
    a1 = a1.reshape(batch, seq // TM, 4, TM // 4, ATTN_W)
    a2 = a2.reshape(batch, seq // TM, 16, TM // 16, ATTN_W)
    in_specs = [
        pl.BlockSpec((None, ATT_TILE, ATTN_W), lambda b, j: (b, j, 0)),
        pl.BlockSpec((None, BAND, ATTN_W), lambda b, j: (b, jnp.maximum(j * (ATT_TILE // BAND) - 1, 0), 0)),
        pl.BlockSpec((None, sub, 4, TM // 4, ATTN_W), lambda b, j: (b, j, 0, 0, 0)),
        pl.BlockSpec((None, 1, 4, TM // 4, ATTN_W), lambda b, j: (b, jnp.maximum(j * sub - 1, 0), 0, 0, 0)),
        pl.BlockSpec((None, sub, 16, TM // 16, ATTN_W), lambda b, j: (b, j, 0, 0, 0)),
        pl.BlockSpec((None, sub, 16, TM // 16, ATTN_W), lambda b, j: (b, jnp.maximum(j - 1, 0), 0, 0, 0)),
        _resident(bias.shape),
    ]
    return pl.pallas_call(
        _attn_body,
        grid=(batch, steps),
        in_specs=in_specs,
        out_specs=pl.BlockSpec((None, ATT_TILE, GROUP_W), lambda b, j: (b, j, 0)),
        out_shape=jax.ShapeDtypeStruct((batch, seq, GROUP_W), BF16),
        scratch_shapes=[pltpu.VMEM((N_GROUPS, GROUP_W // LANES, ATT_TILE, LANES), F32)] * 2,
        compiler_params=pltpu.CompilerParams(dimension_semantics=("arbitrary", "arbitrary"),
                                             vmem_limit_bytes=VMEM_LIMIT),
        name="prompt_attention",
    )(a0, a0, a1, a1, a2, a2, bias)


def _route(logits):
    lane = lax.broadcasted_iota(jnp.int32, logits.shape, 1)
    big = jnp.int32(1 << 20)
    gmask = lane < N_EXPERT_GROUPS
    lg = jnp.where(gmask, logits, MASKED)
    gmax = jnp.max(lg, axis=-1, keepdims=True)
    gidx = jnp.min(jnp.where(gmask & (lg == gmax), lane, big), axis=-1, keepdims=True)
    p_top = 1.0 / jnp.sum(jnp.where(gmask, jnp.exp(lg - gmax), 0.0), axis=-1, keepdims=True)
    lo = EXPERT_LANE0 + EXPERTS_PER_GROUP * gidx
    emask = (lane >= lo) & (lane < lo + EXPERTS_PER_GROUP)
    le = jnp.where(emask, logits, MASKED)
    v1 = jnp.max(le, axis=-1, keepdims=True)
    i1 = jnp.min(jnp.where(emask & (le == v1), lane, big), axis=-1, keepdims=True)
    emask2 = emask & (lane != i1)
    le2 = jnp.where(emask2, logits, MASKED)
    v2 = jnp.max(le2, axis=-1, keepdims=True)
    i2 = jnp.min(jnp.where(emask2 & (le2 == v2), lane, big), axis=-1, keepdims=True)
    e2 = jnp.exp(v2 - v1)
    den = 1.0 + e2
    w1 = (1.0 / den) * p_top
    w2 = (e2 / den) * p_top
    id1 = (i1 - EXPERT_LANE0).astype(F32)
    id2 = (i2 - EXPERT_LANE0).astype(F32)
    return jnp.where(lane == RT_W1, w1, jnp.where(lane == RT_W2, w2,
                     jnp.where(lane == RT_E1, id1, jnp.where(lane == RT_E2, id2, 0.0))))


def _with_ranks(rt, base):
    n = rt.shape[0]
    lane = lax.broadcasted_iota(jnp.int32, rt.shape, 1)
    lane_f = lane.astype(F32)
    oh1 = jnp.where(lane_f == rt[:, RT_E1:RT_E1 + 1], 1.0, 0.0)
    oh2 = jnp.where(lane_f == rt[:, RT_E2:RT_E2 + 1], 1.0, 0.0)
    before = jnp.where(lax.broadcasted_iota(jnp.int32, (n, n), 1) < lax.broadcasted_iota(jnp.int32, (n, n), 0),
                       1.0, 0.0).astype(BF16)
    p1 = jnp.dot(before, oh1.astype(BF16), preferred_element_type=F32)
    p2 = jnp.dot(before, oh2.astype(BF16), preferred_element_type=F32)
    c1 = jnp.sum(oh1, axis=0, keepdims=True)
    c2 = jnp.sum(oh2, axis=0, keepdims=True)
    rank1 = jnp.sum(oh1 * (base + p1), axis=-1, keepdims=True)
    rank2 = jnp.sum(oh2 * (base + c1 + p2), axis=-1, keepdims=True)
    return jnp.where(lane == RT_R1, rank1, jnp.where(lane == RT_R2, rank2, rt)), base + c1 + c2


def _back_body(oa_ref, oc_ref, sga_ref, sgb_ref, x_ref, mod_ref, wpa_ref, wpb_ref, wo_ref, g_ref,
               wrh_ref, wrl_ref, br_ref, x1_ref, h2_ref, rt_ref, rtt_ref, cnt_ref, base_ref):
    @pl.when(pl.program_id(0) == 0)
    def _():
        base_ref[...] = jnp.zeros_like(base_ref)

    gate1 = mod_ref[:, 2 * D_MODEL:3 * D_MODEL]
    shift2 = mod_ref[:, 3 * D_MODEL:4 * D_MODEL]
    scale2 = mod_ref[:, 4 * D_MODEL:5 * D_MODEL]
    n_part = TM // BACK_PARTS
    parts = [pl.ds(j * n_part, n_part) for j in range(BACK_PARTS)]
    dot = functools.partial(jnp.dot, preferred_element_type=F32)
    pab = [(dot(oa_ref[r, :], wpa_ref[...]), dot(oc_ref[r, :], wpb_ref[...])) for r in parts]
    merged = [(sga_ref[r, :].astype(F32) * pa + sgb_ref[r, :].astype(F32) * pb).astype(BF16)
              for r, (pa, pb) in zip(parts, pab)]
    delta = [dot(m, wo_ref[...]) for m in merged]
    x1 = [x_ref[r, :] + gate1 * d for r, d in zip(parts, delta)]
    h2 = [_rmsnorm(x, g_ref[...]) * (1.0 + scale2) + shift2 for x in x1]
    hi = [h.astype(BF16) for h in h2]
    lo = [(h - b.astype(F32)).astype(BF16) for h, b in zip(h2, hi)]
    logits = [dot(b, wrh_ref[...]) + dot(l, wrh_ref[...]) + dot(b, wrl_ref[...]) + br_ref[...]
              for b, l in zip(hi, lo)]
    routed = [_route(lg) for lg in logits]
    total = base_ref[...]
    for r, x, b, rt in zip(parts, x1, hi, routed):
        x1_ref[r, :] = x
        h2_ref[r, :] = b
        rt_ref[r, :], total = _with_ranks(rt, total)
    rtt_ref[...] = rt_ref[...].T[:RT_ROWS, :]
    base_ref[...] = total
    cnt_ref[...] = total


def _back(o_attn, oconv, sga, sgb, x2d, mod_p, wpa, wpb, wo, g_ffn, wr_hi, wr_lo, b_r, seq):
    n_tok = x2d.shape[0]
    n_all = n_tok + TM
    tps = seq // TM
    row = lambda w: pl.BlockSpec((TM, w), lambda i: (i, 0))
    return pl.pallas_call(
        _back_body,
        grid=(n_tok // TM,),
        in_specs=[row(GROUP_W), row(CONV_W), row(D_MODEL), row(D_MODEL), row(D_MODEL),
                  pl.BlockSpec((None, 1, 6 * D_MODEL), lambda i: (i // tps, 0, 0)),
                  _resident(wpa.shape), _resident(wpb.shape), _resident(wo.shape), _resident((1, D_MODEL)),
                  _resident(wr_hi.shape), _resident(wr_lo.shape), _resident(b_r.shape)],
        out_specs=(row(D_MODEL), row(D_MODEL), row(ROUTER_LANES), pl.BlockSpec((RT_ROWS, TM), lambda i: (0, i)),
                   pl.BlockSpec((1, LANES), lambda i: (0, 0))),
        out_shape=(jax.ShapeDtypeStruct((n_all, D_MODEL), F32),
                   jax.ShapeDtypeStruct((n_all, D_MODEL), BF16),
                   jax.ShapeDtypeStruct((n_all, ROUTER_LANES), F32),
                   jax.ShapeDtypeStruct((RT_ROWS, n_all), F32),
                   jax.ShapeDtypeStruct((1, LANES), F32)),
        scratch_shapes=[pltpu.VMEM((1, LANES), F32)],
        compiler_params=pltpu.CompilerParams(dimension_semantics=("arbitrary",), vmem_limit_bytes=VMEM_LIMIT),
        name="prompt_back",
    )(o_attn, oconv, sga, sgb, x2d, mod_p, wpa, wpb, wo, g_ffn.reshape(1, -1), wr_hi, wr_lo, b_r)


def _valid_chunks(tile, n_prompt_tiles, n_sample):
    return jnp.where(tile < n_prompt_tiles, TM // DMA_CHUNK, n_sample // DMA_CHUNK)


def _dispatch_body(n_prompt_tiles, n_sample, dest_ref, tail_ref, h_ref, xs_ref, stage_ref, zero_ref, sem):
    i = pl.program_id(0)
    n_all = (n_prompt_tiles + 1) * TM

    def tail_copy(e):
        start = pl.multiple_of(tail_ref[e] * ROW_SUB, ROW_SUB)
        return pltpu.make_async_copy(zero_ref, xs_ref.at[pl.ds(start, EXPERT_TILE * ROW_SUB)], sem)

    @pl.when(i == 0)
    def _():
        zero_ref[...] = jnp.zeros_like(zero_ref)
        for e in range(N_EXPERTS):
            @pl.when(tail_ref[e] >= 0)
            def _():
                tail_copy(e).start()
        for e in range(N_EXPERTS):
            @pl.when(tail_ref[e] >= 0)
            def _():
                tail_copy(e).wait()

    for j in range(ROW_SUB):
        stage_ref[pl.ds(j, TM, stride=ROW_SUB), :] = h_ref[:, j * LANES:(j + 1) * LANES].astype(F32)
    n_chunks = _valid_chunks(i, n_prompt_tiles, n_sample)

    def issue(c, carry):
        for j in range(DMA_CHUNK):
            r = c * DMA_CHUNK + j
            src = stage_ref.at[pl.ds(pl.multiple_of(r * ROW_SUB, ROW_SUB), ROW_SUB)]
            for k in range(2):
                d = pl.multiple_of(dest_ref[k * n_all + i * TM + r], ROW_SUB)
                pltpu.make_async_copy(src, xs_ref.at[pl.ds(d, ROW_SUB)], sem).start(priority=k)
        return carry

    lax.fori_loop(0, n_chunks, issue, 0)

    def drain(c, carry):
        n = 2 * DMA_CHUNK * ROW_SUB
        pltpu.make_async_copy(stage_ref.at[pl.ds(0, n)], xs_ref.at[pl.ds(0, n)], sem).wait()
        return carry

    lax.fori_loop(0, n_chunks, drain, 0)


def _dispatch(dest, tail_start, h2, n_rows, n_prompt_tiles, n_sample):
    n_all = h2.shape[0]
    return pl.pallas_call(
        functools.partial(_dispatch_body, n_prompt_tiles, n_sample),
        grid_spec=pltpu.PrefetchScalarGridSpec(
            num_scalar_prefetch=2,
            grid=(n_all // TM,),
            in_specs=[pl.BlockSpec((TM, D_MODEL), lambda i, *_: (i, 0))],
            out_specs=pl.BlockSpec(memory_space=pl.ANY),
            scratch_shapes=[pltpu.VMEM((TM * ROW_SUB, LANES), F32), pltpu.VMEM((EXPERT_TILE * ROW_SUB, LANES), F32),
                            pltpu.SemaphoreType.DMA],
        ),
        out_shape=jax.ShapeDtypeStruct((n_rows * ROW_SUB, LANES), F32),
        compiler_params=pltpu.CompilerParams(dimension_semantics=("arbitrary",), vmem_limit_bytes=VMEM_LIMIT,
                                             disable_bounds_checks=True),
        name="moe_dispatch",
    )(dest, tail_start, h2)


def _experts_body(te_ref, nt_ref, xs_ref, wg_ref, wu_ref, wd_ref, o_ref):
    @pl.when(pl.program_id(0) < nt_ref[0])
    def _():
        x = jnp.concatenate([xs_ref[pl.ds(j, EXPERT_TILE, stride=ROW_SUB), :].astype(BF16) for j in range(ROW_SUB)],
                            axis=1)
        a = jnp.dot(x, wg_ref[0].astype(BF16), preferred_element_type=F32)
        z = (a * _sigmoid(a)) * jnp.dot(x, wu_ref[0].astype(BF16), preferred_element_type=F32)
        o = jnp.dot(z.astype(BF16), wd_ref[0].astype(BF16), preferred_element_type=F32)
        for j in range(ROW_SUB):
            o_ref[pl.ds(j, EXPERT_TILE, stride=ROW_SUB), :] = o[:, j * LANES:(j + 1) * LANES]


def _experts(tile_expert, n_tiles, xs, w_gate_e, w_up_e, w_down_e):
    max_tiles = tile_expert.shape[0]
    rows = lambda s, te, nt: (jnp.minimum(s, nt[0] - 1), 0)
    row_tile = pl.BlockSpec((EXPERT_TILE * ROW_SUB, LANES), rows)
    weight = lambda shape: pl.BlockSpec((1,) + shape, lambda s, te, nt: (te[jnp.minimum(s, nt[0] - 1)], 0, 0))
    return pl.pallas_call(
        _experts_body,
        grid_spec=pltpu.PrefetchScalarGridSpec(
            num_scalar_prefetch=2,
            grid=(max_tiles,),
            in_specs=[row_tile,
                      weight((D_MODEL, EXPERT_FF)), weight((D_MODEL, EXPERT_FF)), weight((EXPERT_FF, D_MODEL))],
            out_specs=row_tile,
        ),
        out_shape=jax.ShapeDtypeStruct(xs.shape, F32),
        compiler_params=pltpu.CompilerParams(dimension_semantics=("arbitrary",), vmem_limit_bytes=VMEM_LIMIT),
        name="moe_experts",
    )(tile_expert, n_tiles, xs, w_gate_e, w_up_e, w_down_e)


def _combine_body(n_prompt_tiles, n_sample, dest_ref, x1_ref, rt_ref, gp_ref, gs_ref, gf_ref, eo_ref,
                  yp_ref, ys_ref, rows_ref, sem):
    i = pl.program_id(0)
    slot = i % 2
    n_all = (n_prompt_tiles + 1) * TM

    def fetch(tile):
        buf = tile % 2

        def issue(c, carry):
            for j in range(DMA_CHUNK):
                r = c * DMA_CHUNK + j
                row = pl.ds(pl.multiple_of(r * ROW_SUB, ROW_SUB), ROW_SUB)
                for k in range(2):
                    d = pl.multiple_of(dest_ref[k * n_all + tile * TM + r], ROW_SUB)
                    pltpu.make_async_copy(eo_ref.at[pl.ds(d, ROW_SUB)], rows_ref.at[buf, k, row],
                                          sem.at[buf]).start(priority=k)
            return carry

        lax.fori_loop(0, _valid_chunks(tile, n_prompt_tiles, n_sample), issue, 0)

    @pl.when(i == 0)
    def _():
        fetch(i)

    @pl.when(i + 1 < pl.num_programs(0))
    def _():
        fetch(i + 1)

    def drain(c, carry):
        n = 2 * DMA_CHUNK * ROW_SUB
        pltpu.make_async_copy(eo_ref.at[pl.ds(0, n)], rows_ref.at[slot, 0, pl.ds(0, n)], sem.at[slot]).wait()
        return carry

    lax.fori_loop(0, _valid_chunks(i, n_prompt_tiles, n_sample), drain, 0)

    rt = rt_ref[...]
    w1, w2 = rt[:, RT_W1:RT_W1 + 1], rt[:, RT_W2:RT_W2 + 1]
    lane_tile = lambda k, j: rows_ref[slot, k, pl.ds(j, TM, stride=ROW_SUB), :]
    moe = jnp.concatenate([w1 * lane_tile(0, j) + w2 * lane_tile(1, j) for j in range(ROW_SUB)], axis=1)
    is_prompt = i < n_prompt_tiles
    gate2 = jnp.where(is_prompt, gp_ref[...], gs_ref[...])
    y = _rmsnorm(x1_ref[...] + gate2 * moe, gf_ref[...])

    @pl.when(is_prompt)
    def _():
        yp_ref[...] = y

    @pl.when(jnp.logical_not(is_prompt))
    def _():
        ys_ref[...] = y[:n_sample, :]


def _combine(dest, x1, rt, gate2_p, gate2_s, g_final, eo, n_prompt_tiles, n_sample, tiles_per_seq):
    n_all = x1.shape[0]
    last_p = n_prompt_tiles - 1
    return pl.pallas_call(
        functools.partial(_combine_body, n_prompt_tiles, n_sample),
        grid_spec=pltpu.PrefetchScalarGridSpec(
            num_scalar_prefetch=1,
            grid=(n_all // TM,),
            in_specs=[pl.BlockSpec((TM, D_MODEL), lambda i, *_: (i, 0)),
                      pl.BlockSpec((TM, ROUTER_LANES), lambda i, *_: (i, 0)),
                      pl.BlockSpec((None, 1, D_MODEL), lambda i, *_: (jnp.minimum(i, last_p) // tiles_per_seq, 0, 0)),
                      pl.BlockSpec((TM, D_MODEL), lambda i, *_: (0, 0)),
                      pl.BlockSpec((1, D_MODEL), lambda i, *_: (0, 0)),
                      pl.BlockSpec(memory_space=pl.ANY)],
            out_specs=(pl.BlockSpec((TM, D_MODEL), lambda i, *_: (jnp.minimum(i, last_p), 0)),
                       pl.BlockSpec((n_sample, D_MODEL), lambda i, *_: (0, 0))),
            scratch_shapes=[pltpu.VMEM((2, 2, TM * ROW_SUB, LANES), F32), pltpu.SemaphoreType.DMA((2,))],
        ),
        out_shape=(jax.ShapeDtypeStruct((n_prompt_tiles * TM, D_MODEL), F32),
                   jax.ShapeDtypeStruct((n_sample, D_MODEL), F32)),
        compiler_params=pltpu.CompilerParams(dimension_semantics=("arbitrary",), vmem_limit_bytes=VMEM_LIMIT,
                                             disable_bounds_checks=True),
        name="moe_combine",
    )(dest, x1, rt, gate2_p, gate2_s, g_final.reshape(1, -1), eo)


def _routing_tables(rtt, counts, max_tiles):
    cnt = counts[0, :N_EXPERTS].astype(jnp.int32)
    padded = (cnt + EXPERT_TILE - 1) // EXPERT_TILE * EXPERT_TILE
    ends = jnp.cumsum(padded)
    starts = ends - padded
    n_tiles = (ends[-1] // EXPERT_TILE).reshape(1)
    tile_row0 = jnp.arange(max_tiles, dtype=jnp.int32) * EXPERT_TILE
    tile_expert = jnp.minimum(jnp.sum(ends[None, :] <= tile_row0[:, None], axis=1), N_EXPERTS - 1).astype(jnp.int32)
    tail_start = jnp.where(cnt > 0, ends - EXPERT_TILE, -1).astype(jnp.int32)
    eid = rtt[RT_E1:RT_E2 + 1, :].astype(jnp.int32)
    onehot = eid[None] == jnp.arange(N_EXPERTS, dtype=jnp.int32)[:, None, None]
    dest = rtt[RT_R1:RT_R2 + 1, :].astype(jnp.int32) + jnp.sum(jnp.where(onehot, starts[:, None, None], 0), axis=0)
    dest = (jnp.where(eid >= 0, dest, 0) * ROW_SUB).reshape(-1)
    return dest, tail_start, tile_expert, n_tiles


def _sample_front_body(x_ref, mod_ref, g_ref, w_ref, o_ref):
    shift1 = mod_ref[:, 0:D_MODEL]
    scale1 = mod_ref[:, D_MODEL:2 * D_MODEL]
    h = _rmsnorm(x_ref[...], g_ref[...]) * (1.0 + scale1) + shift1
    o_ref[...] = jnp.dot(h, w_ref[...], precision=HIGHEST, preferred_element_type=F32)


def _sample_front(x_s, mod_s, g_mix, w_in):
    nb = x_s.shape[0]
    tn = 256
    return pl.pallas_call(
        _sample_front_body,
        grid=(IN_COLS // tn,),
        in_specs=[pl.BlockSpec((nb, D_MODEL), lambda j: (0, 0)),
                  pl.BlockSpec((nb, 2 * D_MODEL), lambda j: (0, 0)),
                  pl.BlockSpec((1, D_MODEL), lambda j: (0, 0)),
                  pl.BlockSpec((D_MODEL, tn), lambda j: (0, j))],
        out_specs=pl.BlockSpec((nb, tn), lambda j: (0, j)),
        out_shape=jax.ShapeDtypeStruct((nb, IN_COLS), F32),
        compiler_params=pltpu.CompilerParams(dimension_semantics=("arbitrary",), vmem_limit_bytes=VMEM_LIMIT),
        name="sample_front",
    )(x_s, mod_s, g_mix.reshape(1, -1), w_in)


def _sample_attn_body(qkv_ref, c0_ref, c1_ref, c2_ref, oa_ref):
    head = lax.broadcasted_iota(jnp.int32, (HEADS, 1, 1), 0)
    for b in range(qkv_ref.shape[0]):
        outs, lses = [], []
        for g, c_ref in enumerate((c0_ref, c1_ref, c2_ref)):
            window = c_ref.shape[-1]
            def new_rows(part):
                c0 = (part * N_GROUPS + g) * HEADS
                return jnp.stack([qkv_ref[b, :, c0 + h:c0 + h + 1] for h in range(HEADS)], axis=0)

            q = new_rows(0) * (HEAD_DIM ** -0.5)
            k_new = new_rows(1)
            v_new = new_rows(2)
            s = jnp.sum(c_ref[b, 0] * q, axis=1, keepdims=True)
            pos = lax.broadcasted_iota(jnp.int32, (1, 1, window), 2)
            slope = jnp.full((HEADS, 1, 1), _alibi_slope(g, HEADS - 1), F32)
            for h in range(HEADS - 1):
                slope = jnp.where(head == h, _alibi_slope(g, h), slope)
            on_band = (pos & (DILATIONS[g] - 1)) == 0
            s = jnp.where(on_band, s - slope * (window - pos).astype(F32), MASKED)
            s_self = jnp.sum(q * k_new, axis=1, keepdims=True)
            m = jnp.maximum(jnp.max(s, axis=2, keepdims=True), s_self)
            e = jnp.exp(s - m)
            e_self = jnp.exp(s_self - m)
            l = jnp.sum(e, axis=2, keepdims=True) + e_self
            o = jnp.sum(c_ref[b, 1] * e, axis=2, keepdims=True) + e_self * v_new
            outs.append(o / l)
            lses.append(m + jnp.log(l))
        mx = jnp.maximum(jnp.maximum(lses[0], lses[1]), lses[2])
        w = [jnp.exp(x - mx) for x in lses]
        mixed = (w[0] * outs[0] + w[1] * outs[1] + w[2] * outs[2]) / (w[0] + w[1] + w[2])
        for h in range(HEADS):
            oa_ref[b, :, h:h + 1] = mixed[h]


def _sample_attention(qkv_s, cache0, cache1, cache2):
    nb = qkv_s.shape[0]
    bb = 2
    hd = (HEADS, HEAD_DIM)
    caches = [jnp.transpose(c, (0, 1, 3, 4, 5, 2)) for c in (cache0, cache1, cache2)]
    cache_spec = lambda c: pl.BlockSpec((None, bb, 2, *hd, c.shape[-1]), lambda i: (0, i, 0, 0, 0, 0))
    return pl.pallas_call(
        _sample_attn_body,
        grid=(nb // bb,),
        in_specs=[pl.BlockSpec((bb,) + qkv_s.shape[1:], lambda i: (i, 0, 0))] + [cache_spec(c) for c in caches],
        out_specs=pl.BlockSpec((bb, HEAD_DIM, HEADS), lambda i: (i, 0, 0)),
        out_shape=jax.ShapeDtypeStruct((nb, HEAD_DIM, HEADS), F32),
        compiler_params=pltpu.CompilerParams(dimension_semantics=("arbitrary",), vmem_limit_bytes=VMEM_LIMIT),
        name="sample_attention",
    )(qkv_s, *caches)


def _sample_back_body(p_ref, oa_ref, cc_ref, x_ref, mod_ref, cw_ref, wpa_ref, wpb_ref, wo_ref, g_ref,
                      wr_ref, br_ref, cntp_ref, x1_any, h2_any, rt_any, rtt_any,
                      x1_ref, h2_ref, rt_ref, rtt_ref, sconv_ref, cnt_ref):
    del x1_any, h2_any, rt_any, rtt_any
    nb = x_ref.shape[0]
    hdot = functools.partial(jnp.dot, precision=HIGHEST, preferred_element_type=F32)
    bg = p_ref[:, BG_OFF:BG_OFF + CONV_W]
    u = p_ref[:, CG_OFF:CG_OFF + CONV_W] * p_ref[:, UI_OFF:UI_OFF + CONV_W]
    c_old, c_new = cc_ref[:, 0, :], cc_ref[:, 1, :]
    yconv = cw_ref[0:1, :] * c_old + cw_ref[1:2, :] * c_new + cw_ref[2:3, :] * u
    sconv_ref[:, 0, :] = c_new
    sconv_ref[:, 1, :] = u
    sga = _sigmoid(p_ref[:, GA_OFF:GA_OFF + D_MODEL])
    sgb = _sigmoid(p_ref[:, GB_OFF:GB_OFF + D_MODEL])
    merged = sga * hdot(oa_ref[...], wpa_ref[...]) + sgb * hdot(bg * yconv, wpb_ref[...])
    gate1 = mod_ref[:, 2 * D_MODEL:3 * D_MODEL]
    shift2 = mod_ref[:, 3 * D_MODEL:4 * D_MODEL]
    scale2 = mod_ref[:, 4 * D_MODEL:5 * D_MODEL]
    x1 = x_ref[...] + gate1 * hdot(merged, wo_ref[...])
    h2 = _rmsnorm(x1, g_ref[...]) * (1.0 + scale2) + shift2
    rt = _route(hdot(h2, wr_ref[...]) + br_ref[...])
    lane = lax.broadcasted_iota(jnp.int32, (TM, ROUTER_LANES), 1)
    x1_ref[...] = jnp.zeros_like(x1_ref)
    h2_ref[...] = jnp.zeros_like(h2_ref)
    rt_ref[...] = jnp.where((lane == RT_E1) | (lane == RT_E2), -1.0, 0.0)
    x1_ref[0:nb, :] = x1
    h2_ref[0:nb, :] = h2.astype(BF16)
    rt_ref[0:nb, :] = rt
    ranked, total = _with_ranks(rt_ref[...], cntp_ref[...])
    rt_ref[...] = ranked
    rtt_ref[...] = ranked.T[:RT_ROWS, :]
    cnt_ref[...] = total


def _sample_back(proj_s, oa_s, cache_conv, x_s, mod_s, conv_w, w_pa, w_pb, w_o, g_ffn, w_r, b_r, cnt_p,
                 x1, h2, rt, rtt):
    nb = x_s.shape[0]
    last = x1.shape[0] // TM - 1
    args = (proj_s, oa_s, cache_conv, x_s, mod_s, conv_w, w_pa, w_pb, w_o, g_ffn.reshape(1, -1), w_r, b_r, cnt_p)
    full = lambda shape: pl.BlockSpec(shape, lambda i: (0,) * len(shape))
    last_tile = lambda a: pl.BlockSpec((TM, a.shape[1]), lambda i: (last, 0))
    shape_of = lambda a: jax.ShapeDtypeStruct(a.shape, a.dtype)
    return pl.pallas_call(
        _sample_back_body,
        grid=(1,),
        in_specs=[full(a.shape) for a in args] + [pl.BlockSpec(memory_space=pl.ANY)] * 4,
        out_specs=(last_tile(x1), last_tile(h2), last_tile(rt), pl.BlockSpec((RT_ROWS, TM), lambda i: (0, last)),
                   full((nb, 2, CONV_W)), full((1, LANES))),
        out_shape=(shape_of(x1), shape_of(h2), shape_of(rt), shape_of(rtt),
                   jax.ShapeDtypeStruct((nb, 2, CONV_W), F32), jax.ShapeDtypeStruct((1, LANES), F32)),
        input_output_aliases={len(args) + j: j for j in range(4)},
        compiler_params=pltpu.CompilerParams(dimension_semantics=("arbitrary",), vmem_limit_bytes=VMEM_LIMIT),
        name="sample_back",
    )(*args, x1, h2, rt, rtt)


def kernel(x_prompt, x_sample, cache_kv_w128, cache_kv_w512, cache_kv_w2048, cache_conv, c_prompt, c_sample,
           g_mix, w_ada, b_ada, w_in, conv_w, w_pa, w_pb, w_o, g_ffn, w_router_group, b_router_group,
           w_router_expert, b_router_expert, w_gate_e, w_up_e, w_down_e, g_final):
    batch, seq, _ = x_prompt.shape
    nb = x_sample.shape[0]
    assert x_sample.shape[1] == 1 and g_mix.shape[0] == 1, "one layer, one new sample token per sequence"
    assert seq % ATT_TILE == 0 and cache_kv_w128.shape[2] == 128 and cache_kv_w512.shape[2] == 512 \
        and cache_kv_w2048.shape[2] == 2048
    (g_mix, w_ada, b_ada, w_in, conv_w, w_pa, w_pb, w_o, g_ffn, w_rg, b_rg, w_re, b_re, w_gate_e, w_up_e,
     w_down_e) = (a[0] for a in (g_mix, w_ada, b_ada, w_in, conv_w, w_pa, w_pb, w_o, g_ffn, w_router_group,
                                 b_router_group, w_router_expert, b_router_expert, w_gate_e, w_up_e, w_down_e))

    c_all = jnp.concatenate([c_prompt, jnp.zeros((8 - batch, D_MODEL), F32), c_sample], axis=0)
    mod = _modulation(c_all, w_ada, b_ada)
    mod_p = mod[:batch].reshape(batch, 1, 6 * D_MODEL)
    mod_s = mod[8:]

    w_r = jnp.zeros((D_MODEL, ROUTER_LANES), F32)
    w_r = w_r.at[:, :N_EXPERT_GROUPS].set(w_rg).at[:, EXPERT_LANE0:EXPERT_LANE0 + N_EXPERTS].set(w_re)
    b_r = jnp.full((1, ROUTER_LANES), MASKED, F32)
    b_r = b_r.at[0, :N_EXPERT_GROUPS].set(b_rg).at[0, EXPERT_LANE0:EXPERT_LANE0 + N_EXPERTS].set(b_re)
    wr_hi = w_r.astype(BF16)
    wr_lo = (w_r - wr_hi.astype(F32)).astype(BF16)

    x2d = x_prompt.reshape(batch * seq, D_MODEL)
    (a0, a1, a2, oconv, sga, sgb, kv0, kv1, kv2, pconv) = _front(
        x2d, mod_p, g_mix, w_in.astype(BF16), conv_w, batch, seq)
    o_attn = _attention(a0, a1, a2, _band_bias_table(), batch, seq).reshape(batch * seq, GROUP_W)
    x1, h2, rt, rtt, cnt_p = _back(o_attn, oconv, sga, sgb, x2d, mod_p, w_pa.astype(BF16), w_pb.astype(BF16),
                              w_o.astype(BF16), g_ffn, wr_hi, wr_lo, b_r, seq)

    x_s = x_sample.reshape(nb, D_MODEL)
    proj_s = _sample_front(x_s, mod_s[:, :2 * D_MODEL], g_mix, w_in)
    qkv_s = proj_s[:, :3 * ATTN_W].reshape(nb, 3, N_GROUPS, HEADS, HEAD_DIM)
    qkv_t = jnp.transpose(qkv_s.reshape(nb, 3 * N_GROUPS * HEADS, HEAD_DIM), (0, 2, 1))
    oa_s = _sample_attention(qkv_t, cache_kv_w128, cache_kv_w512, cache_kv_w2048)
    oa_s = jnp.transpose(oa_s, (0, 2, 1)).reshape(nb, GROUP_W)
    skv0, skv1, skv2 = (jnp.stack([qkv_s[:, 1, g], qkv_s[:, 2, g]], axis=1) for g in range(N_GROUPS))
    x1, h2, rt, rtt, sconv, counts = _sample_back(proj_s, oa_s, cache_conv[0], x_s, mod_s, conv_w,
                                                  w_pa, w_pb, w_o, g_ffn, w_r, b_r, cnt_p, x1, h2, rt, rtt)

    n_prompt_tiles = batch * seq // TM
    max_tiles = -(-(2 * (batch * seq + nb) + N_EXPERTS * (EXPERT_TILE - 1)) // EXPERT_TILE)
    dest, tail_start, tile_expert, n_tiles = _routing_tables(rtt, counts, max_tiles)
    xs = _dispatch(dest, tail_start, h2, max_tiles * EXPERT_TILE, n_prompt_tiles, nb)
    eo = _experts(tile_expert, n_tiles, xs, w_gate_e, w_up_e, w_down_e)
    gate2_s = jnp.zeros((TM, D_MODEL), F32).at[:nb].set(mod_s[:, 5 * D_MODEL:])
    y_prompt, y_sample = _combine(dest, x1, rt, mod_p[:, :, 5 * D_MODEL:], gate2_s, g_final, eo,
                                  n_prompt_tiles, nb, seq // TM)

    def prompt_state(a):
        a = a.reshape(batch, 2, HEADS, HEAD_DIM, a.shape[-1])
        return jnp.transpose(a, (0, 4, 1, 2, 3))[None]

    sample_state = lambda a: a.reshape(1, nb, 1, 2, HEADS, HEAD_DIM)
    return (y_prompt.reshape(batch, seq, D_MODEL), y_sample.reshape(nb, 1, D_MODEL),
            prompt_state(kv0), prompt_state(kv1), prompt_state(kv2),
            pconv[:, 6:8, :].reshape(1, batch, 2, CONV_W),
            sample_state(skv0), sample_state(skv1), sample_state(skv2),
            sconv.reshape(1, nb, 2, CONV_W))
```

```python
import functools
import math

import jax
import jax.numpy as jnp
from jax import lax
from jax.experimental import pallas as pl
from jax.experimental.pallas import tpu as pltpu

F32 = jnp.float32
BF16 = jnp.bfloat16
HIGHEST = lax.Precision.HIGHEST

D_MODEL = 1024
HEAD_DIM = 64
HEADS = 4
N_GROUPS = 3
GROUP_W = HEADS * HEAD_DIM
ATTN_W = N_GROUPS * GROUP_W
CONV_W = 512
DILATIONS = (1, 4, 16)
BAND = 128
N_EXPERTS = 32
EXPERTS_PER_GROUP = 8
N_EXPERT_GROUPS = 4
EXPERT_FF = 256
RMS_EPS = 1e-6
MASKED = -1e30

Q_OFF, K_OFF, V_OFF = 0, ATTN_W, 2 * ATTN_W
BG_OFF = 3 * ATTN_W
CG_OFF = BG_OFF + CONV_W
UI_OFF = CG_OFF + CONV_W
GA_OFF = UI_OFF + CONV_W
GB_OFF = GA_OFF + D_MODEL
IN_COLS = GB_OFF + D_MODEL

LANES = 128
ROUTER_LANES = 128
EXPERT_LANE0 = N_EXPERT_GROUPS

TM = 512
ATT_TILE = 2048
BACK_PARTS = 2
ATTN_PAIR = 4
EXPERT_TILE = 512
RT_W1, RT_W2, RT_E1, RT_E2, RT_R1, RT_R2 = 0, 1, 2, 3, 4, 5
RT_ROWS = 8
ROW_SUB = D_MODEL // LANES
DMA_CHUNK = 8
VMEM_LIMIT = 56 * 1024 * 1024


def _sigmoid(x):
    return 1.0 / (1.0 + jnp.exp(-x))


def _rmsnorm(x, g):
    return x * lax.rsqrt(jnp.mean(x * x, axis=-1, keepdims=True) + RMS_EPS) * g


def _alibi_slope(g, h):
    return 2.0 ** (-8.0 * (g * HEADS + h + 1) / (N_GROUPS * HEADS))


def _resident(shape):
    nd = len(shape)
    return pl.BlockSpec(shape, lambda *_: (0,) * nd, pipeline_mode=pl.Buffered(1))


def _mod_body(c_ref, w_ref, b_ref, o_ref):
    c = c_ref[...]
    s = c * _sigmoid(c)
    o_ref[...] = jnp.dot(s, w_ref[...], precision=HIGHEST, preferred_element_type=F32) + b_ref[...]


def _modulation(c_all, w_ada, b_ada):
    rows = c_all.shape[0]
    tn = 1024
    return pl.pallas_call(
        _mod_body,
        grid=(6 * D_MODEL // tn,),
        in_specs=[pl.BlockSpec((rows, D_MODEL), lambda j: (0, 0)),
                  pl.BlockSpec((D_MODEL, tn), lambda j: (0, j)),
                  pl.BlockSpec((1, tn), lambda j: (0, j))],
        out_specs=pl.BlockSpec((rows, tn), lambda j: (0, j)),
        out_shape=jax.ShapeDtypeStruct((rows, 6 * D_MODEL), F32),
        compiler_params=pltpu.CompilerParams(dimension_semantics=("arbitrary",), vmem_limit_bytes=VMEM_LIMIT),
        name="modulation",
    )(c_all, w_ada, b_ada.reshape(1, -1))


def _front_body(tiles_per_seq, x_ref, mod_ref, g_ref, w_ref, cw_ref,
                a0_ref, a1_ref, a2_ref, oconv_ref, sga_ref, sgb_ref,
                kv0_ref, kv1_ref, kv2_ref, pconv_ref, res_ref, uprev_ref):
    t_in_seq = pl.program_id(0) % tiles_per_seq
    x = x_ref[...]
    shift1 = mod_ref[:, 0:D_MODEL]
    scale1 = mod_ref[:, D_MODEL:2 * D_MODEL]
    h = (_rmsnorm(x, g_ref[...]) * (1.0 + scale1) + shift1).astype(BF16)

    def proj(c0, n):
        return jnp.dot(h, w_ref[:, c0:c0 + n], preferred_element_type=F32)

    a_refs = (a0_ref, a1_ref, a2_ref)
    kv_refs = (kv0_ref, kv1_ref, kv2_ref)

    def qkv_dots(g):
        return [proj(base + g * GROUP_W, GROUP_W) for base in (Q_OFF, K_OFF, V_OFF)]

    ahead = qkv_dots(0)
    for g in range(N_GROUPS):
        d = DILATIONS[g]
        n = TM // d
        current = ahead
        if g + 1 < N_GROUPS:
            ahead = qkv_dots(g + 1)
        else:
            bg, cg, ui = proj(BG_OFF, CONV_W), proj(CG_OFF, CONV_W), proj(UI_OFF, CONV_W)
        for part, r in enumerate(current):
            cols = slice(part * GROUP_W, (part + 1) * GROUP_W)
            if part == 0:
                r = r * (HEAD_DIM ** -0.5)
            else:
                kvc = slice((part - 1) * GROUP_W, part * GROUP_W)
                kv_refs[g][kvc, :] = (r[TM - BAND:, :] if g == 0 else r).T
            if g == 0:
                a_refs[g][:, cols] = r.astype(BF16)
            else:
                for c in range(GROUP_W // LANES):
                    res_ref[c] = r[:, c * LANES:(c + 1) * LANES]
                for rr in range(d):
                    for c in range(GROUP_W // LANES):
                        c0 = part * GROUP_W + c * LANES
                        a_refs[g][0, rr, :, c0:c0 + LANES] = res_ref[c, pl.ds(rr, n, stride=d), :].astype(BF16)

    ga = proj(GA_OFF, D_MODEL)
    gb = proj(GB_OFF, D_MODEL)
    u = cg * ui
    tail = u[TM - 8:, :]
    pconv_ref[...] = tail
    prev = jnp.where(t_in_seq == 0, 0.0, uprev_ref[...])
    row = lax.broadcasted_iota(jnp.int32, (TM, 1), 0)
    u1 = jnp.where(row == 0, prev[7:8, :], pltpu.roll(u, 1, axis=0))
    u2 = jnp.where(row == 0, prev[6:7, :], jnp.where(row == 1, prev[7:8, :], pltpu.roll(u, 2, axis=0)))
    yconv = cw_ref[0:1, :] * u2 + cw_ref[1:2, :] * u1 + cw_ref[2:3, :] * u
    oconv_ref[...] = (bg * yconv).astype(BF16)
    uprev_ref[...] = tail

    sga_ref[...] = _sigmoid(ga).astype(BF16)
    sgb_ref[...] = _sigmoid(gb).astype(BF16)


def _front(x2d, mod_p, g_mix, w_in_bf16, conv_w, batch, seq):
    n_tok = x2d.shape[0]
    n_tiles = n_tok // TM
    tps = seq // TM
    kv2_blocks = ATT_TILE // TM
    out_shape = (
        jax.ShapeDtypeStruct((n_tok, ATTN_W), BF16),
        jax.ShapeDtypeStruct((n_tiles, 4, TM // 4, ATTN_W), BF16),
        jax.ShapeDtypeStruct((n_tiles, 16, TM // 16, ATTN_W), BF16),
        jax.ShapeDtypeStruct((n_tok, CONV_W), BF16),
        jax.ShapeDtypeStruct((n_tok, D_MODEL), BF16),
        jax.ShapeDtypeStruct((n_tok, D_MODEL), BF16),
        jax.ShapeDtypeStruct((batch, 2 * GROUP_W, 128), F32),
        jax.ShapeDtypeStruct((batch, 2 * GROUP_W, 512), F32),
        jax.ShapeDtypeStruct((batch, 2 * GROUP_W, 2048), F32),
        jax.ShapeDtypeStruct((batch, 8, CONV_W), F32),
    )
    out_specs = (
        pl.BlockSpec((TM, ATTN_W), lambda i: (i, 0)),
        pl.BlockSpec((1, 4, TM // 4, ATTN_W), lambda i: (i, 0, 0, 0)),
        pl.BlockSpec((1, 16, TM // 16, ATTN_W), lambda i: (i, 0, 0, 0)),
        pl.BlockSpec((TM, CONV_W), lambda i: (i, 0)),
        pl.BlockSpec((TM, D_MODEL), lambda i: (i, 0)),
        pl.BlockSpec((TM, D_MODEL), lambda i: (i, 0)),
        pl.BlockSpec((None, 2 * GROUP_W, 128), lambda i: (i // tps, 0, 0)),
        pl.BlockSpec((None, 2 * GROUP_W, TM), lambda i: (i // tps, 0, 0)),
        pl.BlockSpec((None, 2 * GROUP_W, TM),
                     lambda i: (i // tps, 0, jnp.maximum(i % tps - (tps - kv2_blocks), 0))),
        pl.BlockSpec((None, 8, CONV_W), lambda i: (i // tps, 0, 0)),
    )
    in_specs = [
        pl.BlockSpec((TM, D_MODEL), lambda i: (i, 0)),
        pl.BlockSpec((None, 1, 6 * D_MODEL), lambda i: (i // tps, 0, 0)),
        _resident((1, D_MODEL)),
        _resident((D_MODEL, IN_COLS)),
        _resident((3, CONV_W)),
    ]
    return pl.pallas_call(
        functools.partial(_front_body, tps),
        grid=(n_tiles,),
        in_specs=in_specs,
        out_specs=out_specs,
        out_shape=out_shape,
        scratch_shapes=[pltpu.VMEM((GROUP_W // LANES, TM, LANES), F32), pltpu.VMEM((8, CONV_W), F32)],
        compiler_params=pltpu.CompilerParams(dimension_semantics=("arbitrary",), vmem_limit_bytes=VMEM_LIMIT),
        name="prompt_front",
    )(x2d, mod_p, g_mix.reshape(1, -1), w_in_bf16, conv_w)


def _band_bias_table():
    qi = jnp.arange(BAND)[:, None]
    kc = jnp.arange(2 * BAND)[None, :]
    delta = qi - (kc - BAND)
    valid = (delta >= 0) & (delta <= BAND)
    tabs = []
    for first in (False, True):
        ok = valid & (kc >= BAND) if first else valid
        for g in range(N_GROUPS):
            for h in range(HEADS):
                b = -_alibi_slope(g, h) * (delta * DILATIONS[g]).astype(F32)
                tabs.append(jnp.where(ok, b, MASKED))
    return jnp.stack(tabs).astype(F32)


def _head_lane_mask(h, dtype=None):
    lane = lax.broadcasted_iota(jnp.int32, (1, GROUP_W), 1)
    return (lane >= h * HEAD_DIM) & (lane < (h + 1) * HEAD_DIM)


def _attn_scores(q, k):
    qs = jnp.concatenate([jnp.where(_head_lane_mask(h), q, jnp.zeros_like(q)) for h in range(HEADS)], axis=0)
    return lax.dot_general(qs, k, (((1,), (1,)), ((), ())), preferred_element_type=F32)


def _attn_probs(s, bias_ref, bias_base):
    s = s + bias_ref[pl.ds(bias_base, HEADS)].reshape(HEADS * BAND, 2 * BAND)
    m = jnp.max(s, axis=-1, keepdims=True)
    e = jnp.exp(s - m)
    l = jnp.sum(e, axis=-1, keepdims=True)
    return (e * (1.0 / l)).astype(BF16), m + jnp.log(l)


def _attn_outputs(p, lse, v):
    ov = jnp.dot(p, v, preferred_element_type=F32)
    o = jnp.zeros((BAND, GROUP_W), F32)
    lse_b = jnp.zeros((BAND, GROUP_W), F32)
    for h in range(HEADS):
        hm = _head_lane_mask(h)
        rows = slice(h * BAND, (h + 1) * BAND)
        o = o + jnp.where(hm, ov[rows], 0.0)
        lse_b = lse_b + jnp.where(hm, lse[rows], 0.0)
    return o, lse_b


def _attn_body(a0c_ref, a0p_ref, a1c_ref, a1p_ref, a2c_ref, a2p_ref, bias_ref, o_ref, og_ref, lg_ref):
    first_tile = pl.program_id(1) == 0
    first_off = jnp.where(first_tile, N_GROUPS * HEADS, 0)
    qs, ks, vs = (slice(0, GROUP_W), slice(GROUP_W, 2 * GROUP_W), slice(2 * GROUP_W, 3 * GROUP_W))

    def run_blocks(blocks):
        loaded = [load() for _, _, _, load in blocks]
        scores = [_attn_scores(q, k) for q, k, _ in loaded]
        probs = [_attn_probs(s, bias_ref, base) for s, (_, _, base, _) in zip(scores, blocks)]
        results = [_attn_outputs(p, lse, v) for (p, lse), (_, _, v) in zip(probs, loaded)]
        for (g, rows, _, _), (o, lse_b) in zip(blocks, results):
            for c in range(GROUP_W // LANES):
                og_ref[g, c, rows, :] = o[:, c * LANES:(c + 1) * LANES]
                lg_ref[g, c, rows, :] = lse_b[:, c * LANES:(c + 1) * LANES]

    def g0_block(n0, base):
        def load():
            q = a0c_ref[pl.ds(n0, BAND), qs]
            if isinstance(n0, int) and n0 == 0:
                k = jnp.concatenate([a0p_ref[:, ks], a0c_ref[0:BAND, ks]], axis=0)
                v = jnp.concatenate([a0p_ref[:, vs], a0c_ref[0:BAND, vs]], axis=0)
            else:
                k = a0c_ref[pl.ds(n0 - BAND, 2 * BAND), ks]
                v = a0c_ref[pl.ds(n0 - BAND, 2 * BAND), vs]
            return q, k, v
        return (0, pl.ds(n0, BAND), base, load)

    run_blocks([g0_block(u * BAND, first_off if u == 0 else 0) for u in range(ATTN_PAIR)])

    def g0_loop(t, carry):
        n0 = pl.multiple_of(t * (ATTN_PAIR * BAND), BAND)
        run_blocks([g0_block(n0 + u * BAND, 0) for u in range(ATTN_PAIR)])
        return carry

    lax.fori_loop(1, ATT_TILE // (ATTN_PAIR * BAND), g0_loop, 0)

    def g1_block(jj, r, prev_ref, prev_jj, base):
        def load():
            q = a1c_ref[jj, r, :, qs]
            k = jnp.concatenate([prev_ref[prev_jj, r, :, ks], a1c_ref[jj, r, :, ks]], axis=0)
            v = jnp.concatenate([prev_ref[prev_jj, r, :, vs], a1c_ref[jj, r, :, vs]], axis=0)
            return q, k, v
        return (1, pl.ds(jj * (4 * BAND) + r, BAND, stride=4), base, load)

    def g1_first(t, carry):
        run_blocks([g1_block(0, t * ATTN_PAIR + u, a1p_ref, 0, first_off + HEADS) for u in range(ATTN_PAIR)])
        return carry

    lax.fori_loop(0, 4 // ATTN_PAIR, g1_first, 0)

    def g1_rest(t, carry):
        jj = 1 + (t * ATTN_PAIR) // 4
        r0 = (t * ATTN_PAIR) % 4
        run_blocks([g1_block(jj, r0 + u, a1c_ref, jj - 1, HEADS) for u in range(ATTN_PAIR)])
        return carry

    lax.fori_loop(0, 12 // ATTN_PAIR, g1_rest, 0)

    n_sub = a2c_ref.shape[0]

    def g2_rows(ref, r, cols):
        return jnp.concatenate([ref[t, r, :, cols] for t in range(n_sub)], axis=0)

    def g2_block(r):
        def load():
            q = g2_rows(a2c_ref, r, qs)
            k = jnp.concatenate([g2_rows(a2p_ref, r, ks), g2_rows(a2c_ref, r, ks)], axis=0)
            v = jnp.concatenate([g2_rows(a2p_ref, r, vs), g2_rows(a2c_ref, r, vs)], axis=0)
            return q, k, v
        return (2, pl.ds(r, BAND, stride=16), first_off + 2 * HEADS, load)

    def g2_loop(t, carry):
        run_blocks([g2_block(t * ATTN_PAIR + u) for u in range(ATTN_PAIR)])
        return carry

    lax.fori_loop(0, 16 // ATTN_PAIR, g2_loop, 0)

    def mix(c, carry):
        rows = pl.ds(pl.multiple_of(c * BAND, BAND), BAND)
        for half in range(GROUP_W // LANES):
            l0, l1, l2 = lg_ref[0, half, rows, :], lg_ref[1, half, rows, :], lg_ref[2, half, rows, :]
            mx = jnp.maximum(jnp.maximum(l0, l1), l2)
            w0, w1, w2 = jnp.exp(l0 - mx), jnp.exp(l1 - mx), jnp.exp(l2 - mx)
            num = w0 * og_ref[0, half, rows, :] + w1 * og_ref[1, half, rows, :] + w2 * og_ref[2, half, rows, :]
            o_ref[rows, half * LANES:(half + 1) * LANES] = (num / (w0 + w1 + w2)).astype(o_ref.dtype)
        return carry

    lax.fori_loop(0, ATT_TILE // BAND, mix, 0)


def _attention(a0, a1, a2, bias, batch, seq):
    steps = seq // ATT_TILE
    sub = ATT_TILE // TM
    a0 = a0.reshape(batch, seq, ATTN_W)
    a1 = a1.reshape(batch, seq // TM, 4, TM // 4, ATTN_W)
    a2 = a2.reshape(batch, seq // TM, 16, TM // 16, ATTN_W)
    in_specs = [
        pl.BlockSpec((None, ATT_TILE, ATTN_W), lambda b, j: (b, j, 0)),
        pl.BlockSpec((None, BAND, ATTN_W), lambda b, j: (b, jnp.maximum(j * (ATT_TILE // BAND) - 1, 0), 0)),
        pl.BlockSpec((None, sub, 4, TM // 4, ATTN_W), lambda b, j: (b, j, 0, 0, 0)),
        pl.BlockSpec((None, 1, 4, TM // 4, ATTN_W), lambda b, j: (b, jnp.maximum(j * sub - 1, 0), 0, 0, 0)),
        pl.BlockSpec((None, sub, 16, TM // 16, ATTN_W), lambda b, j: (b, j, 0, 0, 0)),
        pl.BlockSpec((None, sub, 16, TM // 16, ATTN_W), lambda b, j: (b, jnp.maximum(j - 1, 0), 0, 0, 0)),
        _resident(bias.shape),
    ]
    return pl.pallas_call(
        _attn_body,
        grid=(batch, steps),
        in_specs=in_specs,
        out_specs=pl.BlockSpec((None, ATT_TILE, GROUP_W), lambda b, j: (b, j, 0)),
        out_shape=jax.ShapeDtypeStruct((batch, seq, GROUP_W), BF16),
        scratch_shapes=[pltpu.VMEM((N_GROUPS, GROUP_W // LANES, ATT_TILE, LANES), F32)] * 2,
        compiler_params=pltpu.CompilerParams(dimension_semantics=("arbitrary", "arbitrary"),
                                             vmem_limit_bytes=VMEM_LIMIT),
        name="prompt_attention",
    )(a0, a0, a1, a1, a2, a2, bias)


def _route(logits):
    lane = lax.broadcasted_iota(jnp.int32, logits.shape, 1)
    big = jnp.int32(1 << 20)
    gmask = lane < N_EXPERT_GROUPS
    lg = jnp.where(gmask, logits, MASKED)
    gmax = jnp.max(lg, axis=-1, keepdims=True)
    gidx = jnp.min(jnp.where(gmask & (lg == gmax), lane, big), axis=-1, keepdims=True)
    p_top = 1.0 / jnp.sum(jnp.where(gmask, jnp.exp(lg - gmax), 0.0), axis=-1, keepdims=True)
    lo = EXPERT_LANE0 + EXPERTS_PER_GROUP * gidx
    emask = (lane >= lo) & (lane < lo + EXPERTS_PER_GROUP)
    le = jnp.where(emask, logits, MASKED)
    v1 = jnp.max(le, axis=-1, keepdims=True)
    i1 = jnp.min(jnp.where(emask & (le == v1), lane, big), axis=-1, keepdims=True)
    emask2 = emask & (lane != i1)
    le2 = jnp.where(emask2, logits, MASKED)
    v2 = jnp.max(le2, axis=-1, keepdims=True)
    i2 = jnp.min(jnp.where(emask2 & (le2 == v2), lane, big), axis=-1, keepdims=True)
    e2 = jnp.exp(v2 - v1)
    den = 1.0 + e2
    w1 = (1.0 / den) * p_top
    w2 = (e2 / den) * p_top
    id1 = (i1 - EXPERT_LANE0).astype(F32)
    id2 = (i2 - EXPERT_LANE0).astype(F32)
    return jnp.where(lane == RT_W1, w1, jnp.where(lane == RT_W2, w2,
                     jnp.where(lane == RT_E1, id1, jnp.where(lane == RT_E2, id2, 0.0))))


def _with_ranks(rt, base, count=1.0):
    n = rt.shape[0]
    lane = lax.broadcasted_iota(jnp.int32, rt.shape, 1)
    lane_f = lane.astype(F32)
    oh1 = jnp.where(lane_f == rt[:, RT_E1:RT_E1 + 1], count, 0.0)
    oh2 = jnp.where(lane_f == rt[:, RT_E2:RT_E2 + 1], count, 0.0)
    before = jnp.where(lax.broadcasted_iota(jnp.int32, (n, n), 1) < lax.broadcasted_iota(jnp.int32, (n, n), 0),
                       1.0, 0.0).astype(BF16)
    p1 = jnp.dot(before, oh1.astype(BF16), preferred_element_type=F32)
    p2 = jnp.dot(before, oh2.astype(BF16), preferred_element_type=F32)
    c1 = jnp.sum(oh1, axis=0, keepdims=True)
    c2 = jnp.sum(oh2, axis=0, keepdims=True)
    rank1 = jnp.sum(oh1 * (base + p1), axis=-1, keepdims=True)
    rank2 = jnp.sum(oh2 * (base + c1 + p2), axis=-1, keepdims=True)
    return jnp.where(lane == RT_R1, rank1, jnp.where(lane == RT_R2, rank2, rt)), base + c1 + c2


def _back_body(oa_ref, oc_ref, sga_ref, sgb_ref, x_ref, mod_ref, wpa_ref, wpb_ref, wo_ref, g_ref,
               wrh_ref, wrl_ref, br_ref, x1_ref, h2_ref, rt_ref, rtt_ref, cnt_ref, base_ref, logit_ref):
    step = pl.program_id(0)

    @pl.when(step == 0)
    def _():
        base_ref[...] = jnp.zeros_like(base_ref)
        logit_ref[...] = jnp.zeros_like(logit_ref)

    routed_prev = _route(logit_ref[...])
    gate1 = mod_ref[:, 2 * D_MODEL:3 * D_MODEL]
    shift2 = mod_ref[:, 3 * D_MODEL:4 * D_MODEL]
    scale2 = mod_ref[:, 4 * D_MODEL:5 * D_MODEL]
    n_part = TM // BACK_PARTS
    parts = [pl.ds(j * n_part, n_part) for j in range(BACK_PARTS)]
    dot = functools.partial(jnp.dot, preferred_element_type=F32)
    pab = [(dot(oa_ref[r, :], wpa_ref[...]), dot(oc_ref[r, :], wpb_ref[...])) for r in parts]
    merged = [(sga_ref[r, :].astype(F32) * pa + sgb_ref[r, :].astype(F32) * pb).astype(BF16)
              for r, (pa, pb) in zip(parts, pab)]
    delta = [dot(m, wo_ref[...]) for m in merged]
    x1 = [x_ref[r, :] + gate1 * d for r, d in zip(parts, delta)]
    h2 = [_rmsnorm(x, g_ref[...]) * (1.0 + scale2) + shift2 for x in x1]
    hi = [h.astype(BF16) for h in h2]
    lo = [(h - b.astype(F32)).astype(BF16) for h, b in zip(h2, hi)]
    logits = [dot(b, wrh_ref[...]) + dot(l, wrh_ref[...]) + dot(b, wrl_ref[...]) + br_ref[...]
              for b, l in zip(hi, lo)]
    for r, x, b, lg in zip(parts, x1, hi, logits):
        x1_ref[r, :] = x
        h2_ref[r, :] = b
        logit_ref[r, :] = lg
    rt, total = _with_ranks(routed_prev, base_ref[...], jnp.where(step > 0, 1.0, 0.0))
    rt_ref[...] = rt
    rtt_ref[...] = rt.T[:RT_ROWS, :]
    base_ref[...] = total
    cnt_ref[...] = total


def _back(o_attn, oconv, sga, sgb, x2d, mod_p, wpa, wpb, wo, g_ffn, wr_hi, wr_lo, b_r, seq):
    n_tok = x2d.shape[0]
    n_all = n_tok + TM
    tps = seq // TM
    last = n_tok // TM - 1
    tile = lambda i: jnp.minimum(i, last)
    routed = lambda i: jnp.maximum(i - 1, 0)
    row = lambda w: pl.BlockSpec((TM, w), lambda i: (tile(i), 0))
    return pl.pallas_call(
        _back_body,
        grid=(last + 2,),
        in_specs=[row(GROUP_W), row(CONV_W), row(D_MODEL), row(D_MODEL), row(D_MODEL),
                  pl.BlockSpec((None, 1, 6 * D_MODEL), lambda i: (tile(i) // tps, 0, 0)),
                  _resident(wpa.shape), _resident(wpb.shape), _resident(wo.shape), _resident((1, D_MODEL)),
                  _resident(wr_hi.shape), _resident(wr_lo.shape), _resident(b_r.shape)],
        out_specs=(row(D_MODEL), row(D_MODEL),
                   pl.BlockSpec((TM, ROUTER_LANES), lambda i: (routed(i), 0)),
                   pl.BlockSpec((RT_ROWS, TM), lambda i: (0, routed(i))),
                   pl.BlockSpec((1, LANES), lambda i: (0, 0))),
        out_shape=(jax.ShapeDtypeStruct((n_all, D_MODEL), F32),
                   jax.ShapeDtypeStruct((n_all, D_MODEL), BF16),
                   jax.ShapeDtypeStruct((n_all, ROUTER_LANES), F32),
                   jax.ShapeDtypeStruct((RT_ROWS, n_all), F32),
                   jax.ShapeDtypeStruct((1, LANES), F32)),
        scratch_shapes=[pltpu.VMEM((1, LANES), F32), pltpu.VMEM((TM, ROUTER_LANES), F32)],
        compiler_params=pltpu.CompilerParams(dimension_semantics=("arbitrary",), vmem_limit_bytes=VMEM_LIMIT),
        name="prompt_back",
    )(o_attn, oconv, sga, sgb, x2d, mod_p, wpa, wpb, wo, g_ffn.reshape(1, -1), wr_hi, wr_lo, b_r)


def _valid_chunks(tile, n_prompt_tiles, n_sample):
    return jnp.where(tile < n_prompt_tiles, TM // DMA_CHUNK, n_sample // DMA_CHUNK)


def _dispatch_body(n_prompt_tiles, n_sample, dest_ref, tail_ref, h_ref, xs_ref, stage_ref, zero_ref, sem):
    i = pl.program_id(0)
    n_all = (n_prompt_tiles + 1) * TM

    def tail_copy(e):
        start = pl.multiple_of(tail_ref[e] * ROW_SUB, ROW_SUB)
        return pltpu.make_async_copy(zero_ref, xs_ref.at[pl.ds(start, EXPERT_TILE * ROW_SUB)], sem)

    @pl.when(i == 0)
    def _():
        zero_ref[...] = jnp.zeros_like(zero_ref)
        for e in range(N_EXPERTS):
            @pl.when(tail_ref[e] >= 0)
            def _():
                tail_copy(e).start()
        for e in range(N_EXPERTS):
            @pl.when(tail_ref[e] >= 0)
            def _():
                tail_copy(e).wait()

    for j in range(ROW_SUB):
        stage_ref[pl.ds(j, TM, stride=ROW_SUB), :] = h_ref[:, j * LANES:(j + 1) * LANES].astype(F32)
    n_chunks = _valid_chunks(i, n_prompt_tiles, n_sample)

    def issue(c, carry):
        for j in range(DMA_CHUNK):
            r = c * DMA_CHUNK + j
            src = stage_ref.at[pl.ds(pl.multiple_of(r * ROW_SUB, ROW_SUB), ROW_SUB)]
            for k in range(2):
                d = pl.multiple_of(dest_ref[k * n_all + i * TM + r], ROW_SUB)
                pltpu.make_async_copy(src, xs_ref.at[pl.ds(d, ROW_SUB)], sem).start(priority=k)
        return carry

    lax.fori_loop(0, n_chunks, issue, 0)

    def drain(c, carry):
        n = 2 * DMA_CHUNK * ROW_SUB
        pltpu.make_async_copy(stage_ref.at[pl.ds(0, n)], xs_ref.at[pl.ds(0, n)], sem).wait()
        return carry

    lax.fori_loop(0, n_chunks, drain, 0)


def _dispatch(dest, tail_start, h2, n_rows, n_prompt_tiles, n_sample):
    n_all = h2.shape[0]
    return pl.pallas_call(
        functools.partial(_dispatch_body, n_prompt_tiles, n_sample),
        grid_spec=pltpu.PrefetchScalarGridSpec(
            num_scalar_prefetch=2,
            grid=(n_all // TM,),
            in_specs=[pl.BlockSpec((TM, D_MODEL), lambda i, *_: (i, 0))],
            out_specs=pl.BlockSpec(memory_space=pl.ANY),
            scratch_shapes=[pltpu.VMEM((TM * ROW_SUB, LANES), F32), pltpu.VMEM((EXPERT_TILE * ROW_SUB, LANES), F32),
                            pltpu.SemaphoreType.DMA],
        ),
        out_shape=jax.ShapeDtypeStruct((n_rows * ROW_SUB, LANES), F32),
        compiler_params=pltpu.CompilerParams(dimension_semantics=("arbitrary",), vmem_limit_bytes=VMEM_LIMIT,
                                             disable_bounds_checks=True),
        name="moe_dispatch",
    )(dest, tail_start, h2)


def _experts_body(te_ref, nt_ref, xs_ref, wg_ref, wu_ref, wd_ref, o_ref):
    @pl.when(pl.program_id(0) < nt_ref[0])
    def _():
        x = jnp.concatenate([xs_ref[pl.ds(j, EXPERT_TILE, stride=ROW_SUB), :].astype(BF16) for j in range(ROW_SUB)],
                            axis=1)
        a = jnp.dot(x, wg_ref[0].astype(BF16), preferred_element_type=F32)
        z = (a * _sigmoid(a)) * jnp.dot(x, wu_ref[0].astype(BF16), preferred_element_type=F32)
        o = jnp.dot(z.astype(BF16), wd_ref[0].astype(BF16), preferred_element_type=F32)
        for j in range(ROW_SUB):
            o_ref[pl.ds(j, EXPERT_TILE, stride=ROW_SUB), :] = o[:, j * LANES:(j + 1) * LANES]


def _experts(tile_expert, n_tiles, xs, w_gate_e, w_up_e, w_down_e):
    max_tiles = tile_expert.shape[0]
    rows = lambda s, te, nt: (jnp.minimum(s, nt[0] - 1), 0)
    row_tile = pl.BlockSpec((EXPERT_TILE * ROW_SUB, LANES), rows)
    weight = lambda shape: pl.BlockSpec((1,) + shape, lambda s, te, nt: (te[jnp.minimum(s, nt[0] - 1)], 0, 0))
    return pl.pallas_call(
        _experts_body,
        grid_spec=pltpu.PrefetchScalarGridSpec(
            num_scalar_prefetch=2,
            grid=(max_tiles,),
            in_specs=[row_tile,
                      weight((D_MODEL, EXPERT_FF)), weight((D_MODEL, EXPERT_FF)), weight((EXPERT_FF, D_MODEL))],
            out_specs=row_tile,
        ),
        out_shape=jax.ShapeDtypeStruct(xs.shape, F32),
        compiler_params=pltpu.CompilerParams(dimension_semantics=("arbitrary",), vmem_limit_bytes=VMEM_LIMIT),
        name="moe_experts",
    )(tile_expert, n_tiles, xs, w_gate_e, w_up_e, w_down_e)


def _combine_body(n_prompt_tiles, n_sample, dest_ref, x1_ref, rt_ref, gp_ref, gs_ref, gf_ref, eo_ref,
                  yp_ref, ys_ref, rows_ref, sem):
    i = pl.program_id(0)
    slot = i % 2
    n_all = (n_prompt_tiles + 1) * TM

    def fetch(tile):
        buf = tile % 2

        def issue(c, carry):
            for j in range(DMA_CHUNK):
                r = c * DMA_CHUNK + j
                row = pl.ds(pl.multiple_of(r * ROW_SUB, ROW_SUB), ROW_SUB)
                for k in range(2):
                    d = pl.multiple_of(dest_ref[k * n_all + tile * TM + r], ROW_SUB)
                    pltpu.make_async_copy(eo_ref.at[pl.ds(d, ROW_SUB)], rows_ref.at[buf, k, row],
                                          sem.at[buf]).start(priority=k)
            return carry

        lax.fori_loop(0, _valid_chunks(tile, n_prompt_tiles, n_sample), issue, 0)

    @pl.when(i == 0)
    def _():
        fetch(i)

    @pl.when(i + 1 < pl.num_programs(0))
    def _():
        fetch(i + 1)

    def drain(c, carry):
        n = 2 * DMA_CHUNK * ROW_SUB
        pltpu.make_async_copy(eo_ref.at[pl.ds(0, n)], rows_ref.at[slot, 0, pl.ds(0, n)], sem.at[slot]).wait()
        return carry

    lax.fori_loop(0, _valid_chunks(i, n_prompt_tiles, n_sample), drain, 0)

    rt = rt_ref[...]
    w1, w2 = rt[:, RT_W1:RT_W1 + 1], rt[:, RT_W2:RT_W2 + 1]
    lane_tile = lambda k, j: rows_ref[slot, k, pl.ds(j, TM, stride=ROW_SUB), :]
    moe = jnp.concatenate([w1 * lane_tile(0, j) + w2 * lane_tile(1, j) for j in range(ROW_SUB)], axis=1)
    is_prompt = i < n_prompt_tiles
    gate2 = jnp.where(is_prompt, gp_ref[...], gs_ref[...])
    y = _rmsnorm(x1_ref[...] + gate2 * moe, gf_ref[...])

    @pl.when(is_prompt)
    def _():
        yp_ref[...] = y

    @pl.when(jnp.logical_not(is_prompt))
    def _():
        ys_ref[...] = y[:n_sample, :]


def _combine(dest, x1, rt, gate2_p, gate2_s, g_final, eo, n_prompt_tiles, n_sample, tiles_per_seq):
    n_all = x1.shape[0]
    last_p = n_prompt_tiles - 1
    return pl.pallas_call(
        functools.partial(_combine_body, n_prompt_tiles, n_sample),
        grid_spec=pltpu.PrefetchScalarGridSpec(
            num_scalar_prefetch=1,
            grid=(n_all // TM,),
            in_specs=[pl.BlockSpec((TM, D_MODEL), lambda i, *_: (i, 0)),
                      pl.BlockSpec((TM, ROUTER_LANES), lambda i, *_: (i, 0)),
                      pl.BlockSpec((None, 1, D_MODEL), lambda i, *_: (jnp.minimum(i, last_p) // tiles_per_seq, 0, 0)),
                      pl.BlockSpec((TM, D_MODEL), lambda i, *_: (0, 0)),
                      pl.BlockSpec((1, D_MODEL), lambda i, *_: (0, 0)),
                      pl.BlockSpec(memory_space=pl.ANY)],
            out_specs=(pl.BlockSpec((TM, D_MODEL), lambda i, *_: (jnp.minimum(i, last_p), 0)),
                       pl.BlockSpec((n_sample, D_MODEL), lambda i, *_: (0, 0))),
            scratch_shapes=[pltpu.VMEM((2, 2, TM * ROW_SUB, LANES), F32), pltpu.SemaphoreType.DMA((2,))],
        ),
        out_shape=(jax.ShapeDtypeStruct((n_prompt_tiles * TM, D_MODEL), F32),
                   jax.ShapeDtypeStruct((n_sample, D_MODEL), F32)),
        compiler_params=pltpu.CompilerParams(dimension_semantics=("arbitrary",), vmem_limit_bytes=VMEM_LIMIT,
                                             disable_bounds_checks=True),
        name="moe_combine",
    )(dest, x1, rt, gate2_p, gate2_s, g_final.reshape(1, -1), eo)


def _routing_tables(rtt, counts, max_tiles):
    cnt = counts[0, :N_EXPERTS].astype(jnp.int32)
    padded = (cnt + EXPERT_TILE - 1) // EXPERT_TILE * EXPERT_TILE
    ends = jnp.cumsum(padded)
    starts = ends - padded
    n_tiles = (ends[-1] // EXPERT_TILE).reshape(1)
    tile_row0 = jnp.arange(max_tiles, dtype=jnp.int32) * EXPERT_TILE
    tile_expert = jnp.minimum(jnp.sum(ends[None, :] <= tile_row0[:, None], axis=1), N_EXPERTS - 1).astype(jnp.int32)
    tail_start = jnp.where(cnt > 0, ends - EXPERT_TILE, -1).astype(jnp.int32)
    eid = rtt[RT_E1:RT_E2 + 1, :].astype(jnp.int32)
    onehot = eid[None] == jnp.arange(N_EXPERTS, dtype=jnp.int32)[:, None, None]
    dest = rtt[RT_R1:RT_R2 + 1, :].astype(jnp.int32) + jnp.sum(jnp.where(onehot, starts[:, None, None], 0), axis=0)
    dest = (jnp.where(eid >= 0, dest, 0) * ROW_SUB).reshape(-1)
    return dest, tail_start, tile_expert, n_tiles


def _sample_front_body(x_ref, mod_ref, g_ref, w_ref, o_ref):
    shift1 = mod_ref[:, 0:D_MODEL]
    scale1 = mod_ref[:, D_MODEL:2 * D_MODEL]
    h = _rmsnorm(x_ref[...], g_ref[...]) * (1.0 + scale1) + shift1
    o_ref[...] = jnp.dot(h, w_ref[...], precision=HIGHEST, preferred_element_type=F32)


def _sample_front(x_s, mod_s, g_mix, w_in):
    nb = x_s.shape[0]
    tn = 256
    return pl.pallas_call(
        _sample_front_body,
        grid=(IN_COLS // tn,),
        in_specs=[pl.BlockSpec((nb, D_MODEL), lambda j: (0, 0)),
                  pl.BlockSpec((nb, 2 * D_MODEL), lambda j: (0, 0)),
                  pl.BlockSpec((1, D_MODEL), lambda j: (0, 0)),
                  pl.BlockSpec((D_MODEL, tn), lambda j: (0, j))],
        out_specs=pl.BlockSpec((nb, tn), lambda j: (0, j)),
        out_shape=jax.ShapeDtypeStruct((nb, IN_COLS), F32),
        compiler_params=pltpu.CompilerParams(dimension_semantics=("arbitrary",), vmem_limit_bytes=VMEM_LIMIT),
        name="sample_front",
    )(x_s, mod_s, g_mix.reshape(1, -1), w_in)


def _sample_attn_body(qkv_ref, c0_ref, c1_ref, c2_ref, oa_ref):
    head = lax.broadcasted_iota(jnp.int32, (HEADS, 1, 1), 0)
    for b in range(qkv_ref.shape[0]):
        outs, lses = [], []
        for g, c_ref in enumerate((c0_ref, c1_ref, c2_ref)):
            window = c_ref.shape[-1]
            def new_rows(part):
                c0 = (part * N_GROUPS + g) * HEADS
                return jnp.stack([qkv_ref[b, :, c0 + h:c0 + h + 1] for h in range(HEADS)], axis=0)

            q = new_rows(0) * (HEAD_DIM ** -0.5)
            k_new = new_rows(1)
            v_new = new_rows(2)
            s = jnp.sum(c_ref[b, 0] * q, axis=1, keepdims=True)
            pos = lax.broadcasted_iota(jnp.int32, (1, 1, window), 2)
            slope = jnp.full((HEADS, 1, 1), _alibi_slope(g, HEADS - 1), F32)
            for h in range(HEADS - 1):
                slope = jnp.where(head == h, _alibi_slope(g, h), slope)
            on_band = (pos & (DILATIONS[g] - 1)) == 0
            s = jnp.where(on_band, s - slope * (window - pos).astype(F32), MASKED)
            s_self = jnp.sum(q * k_new, axis=1, keepdims=True)
            m = jnp.maximum(jnp.max(s, axis=2, keepdims=True), s_self)
            e = jnp.exp(s - m)
            e_self = jnp.exp(s_self - m)
            l = jnp.sum(e, axis=2, keepdims=True) + e_self
            o = jnp.sum(c_ref[b, 1] * e, axis=2, keepdims=True) + e_self * v_new
            outs.append(o / l)
            lses.append(m + jnp.log(l))
        mx = jnp.maximum(jnp.maximum(lses[0], lses[1]), lses[2])
        w = [jnp.exp(x - mx) for x in lses]
        mixed = (w[0] * outs[0] + w[1] * outs[1] + w[2] * outs[2]) / (w[0] + w[1] + w[2])
        for h in range(HEADS):
            oa_ref[b, :, h:h + 1] = mixed[h]


def _sample_attention(qkv_s, cache0, cache1, cache2):
    nb = qkv_s.shape[0]
    bb = 2
    hd = (HEADS, HEAD_DIM)
    caches = [jnp.transpose(c, (0, 1, 3, 4, 5, 2)) for c in (cache0, cache1, cache2)]
    cache_spec = lambda c: pl.BlockSpec((None, bb, 2, *hd, c.shape[-1]), lambda i: (0, i, 0, 0, 0, 0))
    return pl.pallas_call(
        _sample_attn_body,
        grid=(nb // bb,),
        in_specs=[pl.BlockSpec((bb,) + qkv_s.shape[1:], lambda i: (i, 0, 0))] + [cache_spec(c) for c in caches],
        out_specs=pl.BlockSpec((bb, HEAD_DIM, HEADS), lambda i: (i, 0, 0)),
        out_shape=jax.ShapeDtypeStruct((nb, HEAD_DIM, HEADS), F32),
        compiler_params=pltpu.CompilerParams(dimension_semantics=("arbitrary",), vmem_limit_bytes=VMEM_LIMIT),
        name="sample_attention",
    )(qkv_s, *caches)


def _sample_back_body(p_ref, oa_ref, cc_ref, x_ref, mod_ref, cw_ref, wpa_ref, wpb_ref, wo_ref, g_ref,
                      wr_ref, br_ref, cntp_ref, x1_any, h2_any, rt_any, rtt_any,
                      x1_ref, h2_ref, rt_ref, rtt_ref, sconv_ref, cnt_ref):
    del x1_any, h2_any, rt_any, rtt_any
    nb = x_ref.shape[0]
    hdot = functools.partial(jnp.dot, precision=HIGHEST, preferred_element_type=F32)
    bg = p_ref[:, BG_OFF:BG_OFF + CONV_W]
    u = p_ref[:, CG_OFF:CG_OFF + CONV_W] * p_ref[:, UI_OFF:UI_OFF + CONV_W]
    c_old, c_new = cc_ref[:, 0, :], cc_ref[:, 1, :]
    yconv = cw_ref[0:1, :] * c_old + cw_ref[1:2, :] * c_new + cw_ref[2:3, :] * u
    sconv_ref[:, 0, :] = c_new
    sconv_ref[:, 1, :] = u
    sga = _sigmoid(p_ref[:, GA_OFF:GA_OFF + D_MODEL])
    sgb = _sigmoid(p_ref[:, GB_OFF:GB_OFF + D_MODEL])
    merged = sga * hdot(oa_ref[...], wpa_ref[...]) + sgb * hdot(bg * yconv, wpb_ref[...])
    gate1 = mod_ref[:, 2 * D_MODEL:3 * D_MODEL]
    shift2 = mod_ref[:, 3 * D_MODEL:4 * D_MODEL]
    scale2 = mod_ref[:, 4 * D_MODEL:5 * D_MODEL]
    x1 = x_ref[...] + gate1 * hdot(merged, wo_ref[...])
    h2 = _rmsnorm(x1, g_ref[...]) * (1.0 + scale2) + shift2
    rt = _route(hdot(h2, wr_ref[...]) + br_ref[...])
    lane = lax.broadcasted_iota(jnp.int32, (TM, ROUTER_LANES), 1)
    x1_ref[...] = jnp.zeros_like(x1_ref)
    h2_ref[...] = jnp.zeros_like(h2_ref)
    rt_ref[...] = jnp.where((lane == RT_E1) | (lane == RT_E2), -1.0, 0.0)
    x1_ref[0:nb, :] = x1
    h2_ref[0:nb, :] = h2.astype(BF16)
    rt_ref[0:nb, :] = rt
    ranked, total = _with_ranks(rt_ref[...], cntp_ref[...])
    rt_ref[...] = ranked
    rtt_ref[...] = ranked.T[:RT_ROWS, :]
    cnt_ref[...] = total


def _sample_back(proj_s, oa_s, cache_conv, x_s, mod_s, conv_w, w_pa, w_pb, w_o, g_ffn, w_r, b_r, cnt_p,
                 x1, h2, rt, rtt):
    nb = x_s.shape[0]
    last = x1.shape[0] // TM - 1
    args = (proj_s, oa_s, cache_conv, x_s, mod_s, conv_w, w_pa, w_pb, w_o, g_ffn.reshape(1, -1), w_r, b_r, cnt_p)
    full = lambda shape: pl.BlockSpec(shape, lambda i: (0,) * len(shape))
    last_tile = lambda a: pl.BlockSpec((TM, a.shape[1]), lambda i: (last, 0))
    shape_of = lambda a: jax.ShapeDtypeStruct(a.shape, a.dtype)
    return pl.pallas_call(
        _sample_back_body,
        grid=(1,),
        in_specs=[full(a.shape) for a in args] + [pl.BlockSpec(memory_space=pl.ANY)] * 4,
        out_specs=(last_tile(x1), last_tile(h2), last_tile(rt), pl.BlockSpec((RT_ROWS, TM), lambda i: (0, last)),
                   full((nb, 2, CONV_W)), full((1, LANES))),
        out_shape=(shape_of(x1), shape_of(h2), shape_of(rt), shape_of(rtt),
                   jax.ShapeDtypeStruct((nb, 2, CONV_W), F32), jax.ShapeDtypeStruct((1, LANES), F32)),
        input_output_aliases={len(args) + j: j for j in range(4)},
        compiler_params=pltpu.CompilerParams(dimension_semantics=("arbitrary",), vmem_limit_bytes=VMEM_LIMIT),
        name="sample_back",
    )(*args, x1, h2, rt, rtt)


def kernel(x_prompt, x_sample, cache_kv_w128, cache_kv_w512, cache_kv_w2048, cache_conv, c_prompt, c_sample,
           g_mix, w_ada, b_ada, w_in, conv_w, w_pa, w_pb, w_o, g_ffn, w_router_group, b_router_group,
           w_router_expert, b_router_expert, w_gate_e, w_up_e, w_down_e, g_final):
    batch, seq, _ = x_prompt.shape
    nb = x_sample.shape[0]
    assert x_sample.shape[1] == 1 and g_mix.shape[0] == 1, "one layer, one new sample token per sequence"
    assert seq % ATT_TILE == 0 and cache_kv_w128.shape[2] == 128 and cache_kv_w512.shape[2] == 512 \
        and cache_kv_w2048.shape[2] == 2048
    (g_mix, w_ada, b_ada, w_in, conv_w, w_pa, w_pb, w_o, g_ffn, w_rg, b_rg, w_re, b_re, w_gate_e, w_up_e,
     w_down_e) = (a[0] for a in (g_mix, w_ada, b_ada, w_in, conv_w, w_pa, w_pb, w_o, g_ffn, w_router_group,
                                 b_router_group, w_router_expert, b_router_expert, w_gate_e, w_up_e, w_down_e))

    c_all = jnp.concatenate([c_prompt, jnp.zeros((8 - batch, D_MODEL), F32), c_sample], axis=0)
    mod = _modulation(c_all, w_ada, b_ada)
    mod_p = mod[:batch].reshape(batch, 1, 6 * D_MODEL)
    mod_s = mod[8:]

    w_r = jnp.zeros((D_MODEL, ROUTER_LANES), F32)
    w_r = w_r.at[:, :N_EXPERT_GROUPS].set(w_rg).at[:, EXPERT_LANE0:EXPERT_LANE0 + N_EXPERTS].set(w_re)
    b_r = jnp.full((1, ROUTER_LANES), MASKED, F32)
    b_r = b_r.at[0, :N_EXPERT_GROUPS].set(b_rg).at[0, EXPERT_LANE0:EXPERT_LANE0 + N_EXPERTS].set(b_re)
    wr_hi = w_r.astype(BF16)
    wr_lo = (w_r - wr_hi.astype(F32)).astype(BF16)

    x2d = x_prompt.reshape(batch * seq, D_MODEL)
    (a0, a1, a2, oconv, sga, sgb, kv0, kv1, kv2, pconv) = _front(
        x2d, mod_p, g_mix, w_in.astype(BF16), conv_w, batch, seq)
    o_attn = _attention(a0, a1, a2, _band_bias_table(), batch, seq).reshape(batch * seq, GROUP_W)
    x1, h2, rt, rtt, cnt_p = _back(o_attn, oconv, sga, sgb, x2d, mod_p, w_pa.astype(BF16), w_pb.astype(BF16),
                              w_o.astype(BF16), g_ffn, wr_hi, wr_lo, b_r, seq)

    x_s = x_sample.reshape(nb, D_MODEL)
    proj_s = _sample_front(x_s, mod_s[:, :2 * D_MODEL], g_mix, w_in)
    qkv_s = proj_s[:, :3 * ATTN_W].reshape(nb, 3, N_GROUPS, HEADS, HEAD_DIM)
    qkv_t = jnp.transpose(qkv_s.reshape(nb, 3 * N_GROUPS * HEADS, HEAD_DIM), (0, 2, 1))
    oa_s = _sample_attention(qkv_t, cache_kv_w128, cache_kv_w512, cache_kv_w2048)
    oa_s = jnp.transpose(oa_s, (0, 2, 1)).reshape(nb, GROUP_W)
    skv0, skv1, skv2 = (jnp.stack([qkv_s[:, 1, g], qkv_s[:, 2, g]], axis=1) for g in range(N_GROUPS))
    x1, h2, rt, rtt, sconv, counts = _sample_back(proj_s, oa_s, cache_conv[0], x_s, mod_s, conv_w,
                                                  w_pa, w_pb, w_o, g_ffn, w_r, b_r, cnt_p, x1, h2, rt, rtt)

    n_prompt_tiles = batch * seq // TM
    max_tiles = -(-(2 * (batch * seq + nb) + N_EXPERTS * (EXPERT_TILE - 1)) // EXPERT_TILE)
    dest, tail_start, tile_expert, n_tiles = _routing_tables(rtt, counts, max_tiles)
    xs = _dispatch(dest, tail_start, h2, max_tiles * EXPERT_TILE, n_prompt_tiles, nb)
    eo = _experts(tile_expert, n_tiles, xs, w_gate_e, w_up_e, w_down_e)
    gate2_s = jnp.zeros((TM, D_MODEL), F32).at[:nb].set(mod_s[:, 5 * D_MODEL:])
    y_prompt, y_sample = _combine(dest, x1, rt, mod_p[:, :, 5 * D_MODEL:], gate2_s, g_final, eo,
                                  n_prompt_tiles, nb, seq // TM)

    def prompt_state(a):
        a = a.reshape(batch, 2, HEADS, HEAD_DIM, a.shape[-1])
        return jnp.transpose(a, (0, 4, 1, 2, 3))[None]

    sample_state = lambda a: a.reshape(1, nb, 1, 2, HEADS, HEAD_DIM)
    return (y_prompt.reshape(batch, seq, D_MODEL), y_sample.reshape(nb, 1, D_MODEL),
            prompt_state(kv0), prompt_state(kv1), prompt_state(kv2),
            pconv[:, 6:8, :].reshape(1, batch, 2, CONV_W),
            sample_state(skv0), sample_state(skv1), sample_state(skv2),
            sconv.reshape(1, nb, 2, CONV_W))
```

```python
import functools
import math

import jax
import jax.numpy as jnp
from jax import lax
from jax.experimental import pallas as pl
from jax.experimental.pallas import tpu as pltpu

F32 = jnp.float32
BF16 = jnp.bfloat16
HIGHEST = lax.Precision.HIGHEST

D_MODEL = 1024
HEAD_DIM = 64
HEADS = 4
N_GROUPS = 3
GROUP_W = HEADS * HEAD_DIM
ATTN_W = N_GROUPS * GROUP_W
CONV_W = 512
DILATIONS = (1, 4, 16)
BAND = 128
N_EXPERTS = 32
EXPERTS_PER_GROUP = 8
N_EXPERT_GROUPS = 4
EXPERT_FF = 256
RMS_EPS = 1e-6
MASKED = -1e30

Q_OFF, K_OFF, V_OFF = 0, ATTN_W, 2 * ATTN_W
BG_OFF = 3 * ATTN_W
CG_OFF = BG_OFF + CONV_W
UI_OFF = CG_OFF + CONV_W
GA_OFF = UI_OFF + CONV_W
GB_OFF = GA_OFF + D_MODEL
IN_COLS = GB_OFF + D_MODEL

LANES = 128
ROUTER_LANES = 128
EXPERT_LANE0 = N_EXPERT_GROUPS

TM = 512
ATT_TILE = 2048
BACK_PARTS = 2
ATTN_PAIR = 4
EXPERT_TILE = 512
RT_W1, RT_W2, RT_E1, RT_E2, RT_R1, RT_R2 = 0, 1, 2, 3, 4, 5
RT_ROWS = 8
ROW_SUB = D_MODEL // LANES
DMA_CHUNK = 8
VMEM_LIMIT = 56 * 1024 * 1024


def _sigmoid(x):
    return 1.0 / (1.0 + jnp.exp(-x))


def _rmsnorm(x, g):
    return x * lax.rsqrt(jnp.mean(x * x, axis=-1, keepdims=True) + RMS_EPS) * g


def _alibi_slope(g, h):
    return 2.0 ** (-8.0 * (g * HEADS + h + 1) / (N_GROUPS * HEADS))


def _resident(shape):
    nd = len(shape)
    return pl.BlockSpec(shape, lambda *_: (0,) * nd, pipeline_mode=pl.Buffered(1))


def _mod_body(c_ref, w_ref, b_ref, o_ref):
    c = c_ref[...]
    s = c * _sigmoid(c)
    o_ref[...] = jnp.dot(s, w_ref[...], precision=HIGHEST, preferred_element_type=F32) + b_ref[...]


def _modulation(c_all, w_ada, b_ada):
    rows = c_all.shape[0]
    tn = 1024
    return pl.pallas_call(
        _mod_body,
        grid=(6 * D_MODEL // tn,),
        in_specs=[pl.BlockSpec((rows, D_MODEL), lambda j: (0, 0)),
                  pl.BlockSpec((D_MODEL, tn), lambda j: (0, j)),
                  pl.BlockSpec((1, tn), lambda j: (0, j))],
        out_specs=pl.BlockSpec((rows, tn), lambda j: (0, j)),
        out_shape=jax.ShapeDtypeStruct((rows, 6 * D_MODEL), F32),
        compiler_params=pltpu.CompilerParams(dimension_semantics=("arbitrary",), vmem_limit_bytes=VMEM_LIMIT),
        name="modulation",
    )(c_all, w_ada, b_ada.reshape(1, -1))


def _front_body(tiles_per_seq, x_ref, mod_ref, g_ref, w_ref, cw_ref,
                a0_ref, a1_ref, a2_ref, oconv_ref, sga_ref, sgb_ref,
                kv0_ref, kv1_ref, kv2_ref, pconv_ref, res_ref, uprev_ref):
    t_in_seq = pl.program_id(0) % tiles_per_seq
    x = x_ref[...]
    shift1 = mod_ref[:, 0:D_MODEL]
    scale1 = mod_ref[:, D_MODEL:2 * D_MODEL]
    h = (_rmsnorm(x, g_ref[...]) * (1.0 + scale1) + shift1).astype(BF16)

    def proj(c0, n):
        return jnp.dot(h, w_ref[:, c0:c0 + n], preferred_element_type=F32)

    a_refs = (a0_ref, a1_ref, a2_ref)
    kv_refs = (kv0_ref, kv1_ref, kv2_ref)

    for g in range(N_GROUPS):
        d = DILATIONS[g]
        n = TM // d
        for part, base in enumerate((Q_OFF, K_OFF, V_OFF)):
            r = proj(base + g * GROUP_W, GROUP_W)
            cols = slice(part * GROUP_W, (part + 1) * GROUP_W)
            if part == 0:
                r = r * (HEAD_DIM ** -0.5)
            else:
                kvc = slice((part - 1) * GROUP_W, part * GROUP_W)
                kv_refs[g][kvc, :] = (r[TM - BAND:, :] if g == 0 else r).T
            if g == 0:
                a_refs[g][:, cols] = r.astype(BF16)
            else:
                for c in range(GROUP_W // LANES):
                    res_ref[c] = r[:, c * LANES:(c + 1) * LANES]
                for rr in range(d):
                    for c in range(GROUP_W // LANES):
                        c0 = part * GROUP_W + c * LANES
                        a_refs[g][0, rr, :, c0:c0 + LANES] = res_ref[c, pl.ds(rr, n, stride=d), :].astype(BF16)

    bg = proj(BG_OFF, CONV_W)
    u = proj(CG_OFF, CONV_W) * proj(UI_OFF, CONV_W)
    tail = u[TM - 8:, :]
    pconv_ref[...] = tail
    prev = jnp.where(t_in_seq == 0, 0.0, uprev_ref[...])
    row = lax.broadcasted_iota(jnp.int32, (TM, 1), 0)
    u1 = jnp.where(row == 0, prev[7:8, :], pltpu.roll(u, 1, axis=0))
    u2 = jnp.where(row == 0, prev[6:7, :], jnp.where(row == 1, prev[7:8, :], pltpu.roll(u, 2, axis=0)))
    yconv = cw_ref[0:1, :] * u2 + cw_ref[1:2, :] * u1 + cw_ref[2:3, :] * u
    oconv_ref[...] = (bg * yconv).astype(BF16)
    uprev_ref[...] = tail

    sga_ref[...] = _sigmoid(proj(GA_OFF, D_MODEL)).astype(BF16)
    sgb_ref[...] = _sigmoid(proj(GB_OFF, D_MODEL)).astype(BF16)


def _front(x2d, mod_p, g_mix, w_in_bf16, conv_w, batch, seq):
    n_tok = x2d.shape[0]
    n_tiles = n_tok // TM
    tps = seq // TM
    kv2_blocks = ATT_TILE // TM
    out_shape = (
        jax.ShapeDtypeStruct((n_tok, ATTN_W), BF16),
        jax.ShapeDtypeStruct((n_tiles, 4, TM // 4, ATTN_W), BF16),
        jax.ShapeDtypeStruct((n_tiles, 16, TM // 16, ATTN_W), BF16),
        jax.ShapeDtypeStruct((n_tok, CONV_W), BF16),
        jax.ShapeDtypeStruct((n_tok, D_MODEL), BF16),
        jax.ShapeDtypeStruct((n_tok, D_MODEL), BF16),
        jax.ShapeDtypeStruct((batch, 2 * GROUP_W, 128), F32),
        jax.ShapeDtypeStruct((batch, 2 * GROUP_W, 512), F32),
        jax.ShapeDtypeStruct((batch, 2 * GROUP_W, 2048), F32),
        jax.ShapeDtypeStruct((batch, 8, CONV_W), F32),
    )
    out_specs = (
        pl.BlockSpec((TM, ATTN_W), lambda i: (i, 0)),
        pl.BlockSpec((1, 4, TM // 4, ATTN_W), lambda i: (i, 0, 0, 0)),
        pl.BlockSpec((1, 16, TM // 16, ATTN_W), lambda i: (i, 0, 0, 0)),
        pl.BlockSpec((TM, CONV_W), lambda i: (i, 0)),
        pl.BlockSpec((TM, D_MODEL), lambda i: (i, 0)),
        pl.BlockSpec((TM, D_MODEL), lambda i: (i, 0)),
        pl.BlockSpec((None, 2 * GROUP_W, 128), lambda i: (i // tps, 0, 0)),
        pl.BlockSpec((None, 2 * GROUP_W, TM), lambda i: (i // tps, 0, 0)),
        pl.BlockSpec((None, 2 * GROUP_W, TM),
                     lambda i: (i // tps, 0, jnp.maximum(i % tps - (tps - kv2_blocks), 0))),
        pl.BlockSpec((None, 8, CONV_W), lambda i: (i // tps, 0, 0)),
    )
    in_specs = [
        pl.BlockSpec((TM, D_MODEL), lambda i: (i, 0)),
        pl.BlockSpec((None, 1, 6 * D_MODEL), lambda i: (i // tps, 0, 0)),
        _resident((1, D_MODEL)),
        _resident((D_MODEL, IN_COLS)),
        _resident((3, CONV_W)),
    ]
    return pl.pallas_call(
        functools.partial(_front_body, tps),
        grid=(n_tiles,),
        in_specs=in_specs,
        out_specs=out_specs,
        out_shape=out_shape,
        scratch_shapes=[pltpu.VMEM((GROUP_W // LANES, TM, LANES), F32), pltpu.VMEM((8, CONV_W), F32)],
        compiler_params=pltpu.CompilerParams(dimension_semantics=("arbitrary",), vmem_limit_bytes=VMEM_LIMIT),
        name="prompt_front",
    )(x2d, mod_p, g_mix.reshape(1, -1), w_in_bf16, conv_w)


def _band_bias_table():
    qi = jnp.arange(BAND)[:, None]
    kc = jnp.arange(2 * BAND)[None, :]
    delta = qi - (kc - BAND)
    valid = (delta >= 0) & (delta <= BAND)
    tabs = []
    for first in (False, True):
        ok = valid & (kc >= BAND) if first else valid
        for g in range(N_GROUPS):
            for h in range(HEADS):
                b = -_alibi_slope(g, h) * (delta * DILATIONS[g]).astype(F32)
                tabs.append(jnp.where(ok, b, MASKED))
    return jnp.stack(tabs).astype(F32)


def _head_lane_mask(h):
    lane = lax.broadcasted_iota(jnp.int32, (1, GROUP_W), 1)
    return (lane >= h * HEAD_DIM) & (lane < (h + 1) * HEAD_DIM)


def _attn_scores(q, k):
    qs = jnp.concatenate([jnp.where(_head_lane_mask(h), q, jnp.zeros_like(q)) for h in range(HEADS)], axis=0)
    return lax.dot_general(qs, k, (((1,), (1,)), ((), ())), preferred_element_type=F32)


def _attn_probs(s, bias_ref, bias_base):
    s = s + bias_ref[pl.ds(bias_base, HEADS)].reshape(HEADS * BAND, 2 * BAND)
    m = jnp.max(s, axis=-1, keepdims=True)
    e = jnp.exp(s - m)
    l = jnp.sum(e, axis=-1, keepdims=True)
    return (e * (1.0 / l)).astype(BF16), m + jnp.log(l)


def _attn_outputs(p, lse, v):
    ov = jnp.dot(p, v, preferred_element_type=F32)
    o = jnp.zeros((BAND, GROUP_W), F32)
    lse_b = jnp.zeros((BAND, GROUP_W), F32)
    for h in range(HEADS):
        hm = _head_lane_mask(h)
        rows = slice(h * BAND, (h + 1) * BAND)
        o = o + jnp.where(hm, ov[rows], 0.0)
        lse_b = lse_b + jnp.where(hm, lse[rows], 0.0)
    return o, lse_b


def _attn_body(a0c_ref, a0p_ref, a1c_ref, a1p_ref, a2c_ref, a2p_ref, bias_ref, o_ref, og_ref, lg_ref):
    first_tile = pl.program_id(1) == 0
    first_off = jnp.where(first_tile, N_GROUPS * HEADS, 0)
    qs, ks, vs = (slice(0, GROUP_W), slice(GROUP_W, 2 * GROUP_W), slice(2 * GROUP_W, 3 * GROUP_W))

    def run_blocks(blocks):
        loaded = [load() for _, _, _, load in blocks]
        scores = [_attn_scores(q, k) for q, k, _ in loaded]
        probs = [_attn_probs(s, bias_ref, base) for s, (_, _, base, _) in zip(scores, blocks)]
        results = [_attn_outputs(p, lse, v) for (p, lse), (_, _, v) in zip(probs, loaded)]
        for (g, rows, _, _), (o, lse_b) in zip(blocks, results):
            for c in range(GROUP_W // LANES):
                og_ref[g, c, rows, :] = o[:, c * LANES:(c + 1) * LANES]
                lg_ref[g, c, rows, :] = lse_b[:, c * LANES:(c + 1) * LANES]

    def g0_block(n0, base):
        def load():
            q = a0c_ref[pl.ds(n0, BAND), qs]
            if isinstance(n0, int) and n0 == 0:
                k = jnp.concatenate([a0p_ref[:, ks], a0c_ref[0:BAND, ks]], axis=0)
                v = jnp.concatenate([a0p_ref[:, vs], a0c_ref[0:BAND, vs]], axis=0)
            else:
                k = a0c_ref[pl.ds(n0 - BAND, 2 * BAND), ks]
                v = a0c_ref[pl.ds(n0 - BAND, 2 * BAND), vs]
            return q, k, v
        return (0, pl.ds(n0, BAND), base, load)

    run_blocks([g0_block(u * BAND, first_off if u == 0 else 0) for u in range(ATTN_PAIR)])

    def g0_loop(t, carry):
        n0 = pl.multiple_of(t * (ATTN_PAIR * BAND), BAND)
        run_blocks([g0_block(n0 + u * BAND, 0) for u in range(ATTN_PAIR)])
        return carry

    lax.fori_loop(1, ATT_TILE // (ATTN_PAIR * BAND), g0_loop, 0)

    def g1_block(jj, r, prev_ref, prev_jj, base):
        def load():
            q = a1c_ref[jj, r, :, qs]
            k = jnp.concatenate([prev_ref[prev_jj, r, :, ks], a1c_ref[jj, r, :, ks]], axis=0)
            v = jnp.concatenate([prev_ref[prev_jj, r, :, vs], a1c_ref[jj, r, :, vs]], axis=0)
            return q, k, v
        return (1, pl.ds(jj * (4 * BAND) + r, BAND, stride=4), base, load)

    def g1_first(t, carry):
        run_blocks([g1_block(0, t * ATTN_PAIR + u, a1p_ref, 0, first_off + HEADS) for u in range(ATTN_PAIR)])
        return carry

    lax.fori_loop(0, 4 // ATTN_PAIR, g1_first, 0)

    def g1_rest(t, carry):
        jj = 1 + (t * ATTN_PAIR) // 4
        r0 = (t * ATTN_PAIR) % 4
        run_blocks([g1_block(jj, r0 + u, a1c_ref, jj - 1, HEADS) for u in range(ATTN_PAIR)])
        return carry

    lax.fori_loop(0, 12 // ATTN_PAIR, g1_rest, 0)

    n_sub = a2c_ref.shape[0]

    def g2_rows(ref, r, cols):
        return jnp.concatenate([ref[t, r, :, cols] for t in range(n_sub)], axis=0)

    def g2_block(r):
        def load():
            q = g2_rows(a2c_ref, r, qs)
            k = jnp.concatenate([g2_rows(a2p_ref, r, ks), g2_rows(a2c_ref, r, ks)], axis=0)
            v = jnp.concatenate([g2_rows(a2p_ref, r, vs), g2_rows(a2c_ref, r, vs)], axis=0)
            return q, k, v
        return (2, pl.ds(r, BAND, stride=16), first_off + 2 * HEADS, load)

    def g2_loop(t, carry):
        run_blocks([g2_block(t * ATTN_PAIR + u) for u in range(ATTN_PAIR)])
        return carry

    lax.fori_loop(0, 16 // ATTN_PAIR, g2_loop, 0)

    def mix(c, carry):
        rows = pl.ds(pl.multiple_of(c * BAND, BAND), BAND)
        for half in range(GROUP_W // LANES):
            l0, l1, l2 = lg_ref[0, half, rows, :], lg_ref[1, half, rows, :], lg_ref[2, half, rows, :]
            mx = jnp.maximum(jnp.maximum(l0, l1), l2)
            w0, w1, w2 = jnp.exp(l0 - mx), jnp.exp(l1 - mx), jnp.exp(l2 - mx)
            num = w0 * og_ref[0, half, rows, :] + w1 * og_ref[1, half, rows, :] + w2 * og_ref[2, half, rows, :]
            o_ref[rows, half * LANES:(half + 1) * LANES] = (num / (w0 + w1 + w2)).astype(o_ref.dtype)
        return carry

    lax.fori_loop(0, ATT_TILE // BAND, mix, 0)


def _attention(a0, a1, a2, bias, batch, seq):
    steps = seq // ATT_TILE
    sub = ATT_TILE // TM
    a0 = a0.reshape(batch, seq, ATTN_W)
    a1 = a1.reshape(batch, seq // TM, 4, TM // 4, ATTN_W)
    a2 = a2.reshape(batch, seq // TM, 16, TM // 16, ATTN_W)
    in_specs = [
        pl.BlockSpec((None, ATT_TILE, ATTN_W), lambda b, j: (b, j, 0)),
        pl.BlockSpec((None, BAND, ATTN_W), lambda b, j: (b, jnp.maximum(j * (ATT_TILE // BAND) - 1, 0), 0)),
        pl.BlockSpec((None, sub, 4, TM // 4, ATTN_W), lambda b, j: (b, j, 0, 0, 0)),
        pl.BlockSpec((None, 1, 4, TM // 4, ATTN_W), lambda b, j: (b, jnp.maximum(j * sub - 1, 0), 0, 0, 0)),
        pl.BlockSpec((None, sub, 16, TM // 16, ATTN_W), lambda b, j: (b, j, 0, 0, 0)),
        pl.BlockSpec((None, sub, 16, TM // 16, ATTN_W), lambda b, j: (b, jnp.maximum(j - 1, 0), 0, 0, 0)),
        _resident(bias.shape),
    ]
    return pl.pallas_call(
        _attn_body,
        grid=(batch, steps),
        in_specs=in_specs,
        out_specs=pl.BlockSpec((None, ATT_TILE, GROUP_W), lambda b, j: (b, j, 0)),
        out_shape=jax.ShapeDtypeStruct((batch, seq, GROUP_W), BF16),
        scratch_shapes=[pltpu.VMEM((N_GROUPS, GROUP_W // LANES, ATT_TILE, LANES), F32)] * 2,
        compiler_params=pltpu.CompilerParams(dimension_semantics=("arbitrary", "arbitrary"),
                                             vmem_limit_bytes=VMEM_LIMIT),
        name="prompt_attention",
    )(a0, a0, a1, a1, a2, a2, bias)


def _route(logits):
    lane = lax.broadcasted_iota(jnp.int32, logits.shape, 1)
    big = jnp.int32(1 << 20)
    gmask = lane < N_EXPERT_GROUPS
    lg = jnp.where(gmask, logits, MASKED)
    gmax = jnp.max(lg, axis=-1, keepdims=True)
    gidx = jnp.min(jnp.where(gmask & (lg == gmax), lane, big), axis=-1, keepdims=True)
    p_top = 1.0 / jnp.sum(jnp.where(gmask, jnp.exp(lg - gmax), 0.0), axis=-1, keepdims=True)
    lo = EXPERT_LANE0 + EXPERTS_PER_GROUP * gidx
    emask = (lane >= lo) & (lane < lo + EXPERTS_PER_GROUP)
    le = jnp.where(emask, logits, MASKED)
    v1 = jnp.max(le, axis=-1, keepdims=True)
    i1 = jnp.min(jnp.where(emask & (le == v1), lane, big), axis=-1, keepdims=True)
    emask2 = emask & (lane != i1)
    le2 = jnp.where(emask2, logits, MASKED)
    v2 = jnp.max(le2, axis=-1, keepdims=True)
    i2 = jnp.min(jnp.where(emask2 & (le2 == v2), lane, big), axis=-1, keepdims=True)
    e2 = jnp.exp(v2 - v1)
    den = 1.0 + e2
    w1 = (1.0 / den) * p_top
    w2 = (e2 / den) * p_top
    id1 = (i1 - EXPERT_LANE0).astype(F32)
    id2 = (i2 - EXPERT_LANE0).astype(F32)
    return jnp.where(lane == RT_W1, w1, jnp.where(lane == RT_W2, w2,
                     jnp.where(lane == RT_E1, id1, jnp.where(lane == RT_E2, id2, 0.0))))


def _with_ranks(rt, base, count=1.0):
    n = rt.shape[0]
    lane = lax.broadcasted_iota(jnp.int32, rt.shape, 1)
    lane_f = lane.astype(F32)
    oh1 = jnp.where(lane_f == rt[:, RT_E1:RT_E1 + 1], count, 0.0)
    oh2 = jnp.where(lane_f == rt[:, RT_E2:RT_E2 + 1], count, 0.0)
    before = jnp.where(lax.broadcasted_iota(jnp.int32, (n, n), 1) < lax.broadcasted_iota(jnp.int32, (n, n), 0),
                       1.0, 0.0).astype(BF16)
    p1 = jnp.dot(before, oh1.astype(BF16), preferred_element_type=F32)
    p2 = jnp.dot(before, oh2.astype(BF16), preferred_element_type=F32)
    c1 = jnp.sum(oh1, axis=0, keepdims=True)
    c2 = jnp.sum(oh2, axis=0, keepdims=True)
    rank1 = jnp.sum(oh1 * (base + p1), axis=-1, keepdims=True)
    rank2 = jnp.sum(oh2 * (base + c1 + p2), axis=-1, keepdims=True)
    return jnp.where(lane == RT_R1, rank1, jnp.where(lane == RT_R2, rank2, rt)), base + c1 + c2


def _back_body(oa_ref, oc_ref, sga_ref, sgb_ref, x_ref, mod_ref, wpa_ref, wpb_ref, wo_ref, g_ref,
               wrh_ref, wrl_ref, br_ref, x1_ref, h2_ref, rt_ref, rtt_ref, cnt_ref, base_ref, logit_ref):
    step = pl.program_id(0)

    @pl.when(step == 0)
    def _():
        base_ref[...] = jnp.zeros_like(base_ref)
        logit_ref[...] = jnp.zeros_like(logit_ref)

    routed_prev = _route(logit_ref[...])
    gate1 = mod_ref[:, 2 * D_MODEL:3 * D_MODEL]
    shift2 = mod_ref[:, 3 * D_MODEL:4 * D_MODEL]
    scale2 = mod_ref[:, 4 * D_MODEL:5 * D_MODEL]
    n_part = TM // BACK_PARTS
    parts = [pl.ds(j * n_part, n_part) for j in range(BACK_PARTS)]
    dot = functools.partial(jnp.dot, preferred_element_type=F32)
    pab = [(dot(oa_ref[r, :], wpa_ref[...]), dot(oc_ref[r, :], wpb_ref[...])) for r in parts]
    merged = [(sga_ref[r, :].astype(F32) * pa + sgb_ref[r, :].astype(F32) * pb).astype(BF16)
              for r, (pa, pb) in zip(parts, pab)]
    delta = [dot(m, wo_ref[...]) for m in merged]
    x1 = [x_ref[r, :] + gate1 * d for r, d in zip(parts, delta)]
    h2 = [_rmsnorm(x, g_ref[...]) * (1.0 + scale2) + shift2 for x in x1]
    hi = [h.astype(BF16) for h in h2]
    lo = [(h - b.astype(F32)).astype(BF16) for h, b in zip(h2, hi)]
    logits = [dot(b, wrh_ref[...]) + dot(l, wrh_ref[...]) + dot(b, wrl_ref[...]) + br_ref[...]
              for b, l in zip(hi, lo)]
    for r, x, b, lg in zip(parts, x1, hi, logits):
        x1_ref[r, :] = x
        h2_ref[r, :] = b
        logit_ref[r, :] = lg
    rt, total = _with_ranks(routed_prev, base_ref[...], jnp.where(step > 0, 1.0, 0.0))
    rt_ref[...] = rt
    rtt_ref[...] = rt.T[:RT_ROWS, :]
    base_ref[...] = total
    cnt_ref[...] = total


def _back(o_attn, oconv, sga, sgb, x2d, mod_p, wpa, wpb, wo, g_ffn, wr_hi, wr_lo, b_r, seq):
    n_tok = x2d.shape[0]
    n_all = n_tok + TM
    tps = seq // TM
    last = n_tok // TM - 1
    tile = lambda i: jnp.minimum(i, last)
    routed = lambda i: jnp.maximum(i - 1, 0)
    row = lambda w: pl.BlockSpec((TM, w), lambda i: (tile(i), 0))
    return pl.pallas_call(
        _back_body,
        grid=(last + 2,),
        in_specs=[row(GROUP_W), row(CONV_W), row(D_MODEL), row(D_MODEL), row(D_MODEL),
                  pl.BlockSpec((None, 1, 6 * D_MODEL), lambda i: (tile(i) // tps, 0, 0)),
                  _resident(wpa.shape), _resident(wpb.shape), _resident(wo.shape), _resident((1, D_MODEL)),
                  _resident(wr_hi.shape), _resident(wr_lo.shape), _resident(b_r.shape)],
        out_specs=(row(D_MODEL), row(D_MODEL),
                   pl.BlockSpec((TM, ROUTER_LANES), lambda i: (routed(i), 0)),
                   pl.BlockSpec((RT_ROWS, TM), lambda i: (0, routed(i))),
                   pl.BlockSpec((1, LANES), lambda i: (0, 0))),
        out_shape=(jax.ShapeDtypeStruct((n_all, D_MODEL), F32),
                   jax.ShapeDtypeStruct((n_all, D_MODEL), BF16),
                   jax.ShapeDtypeStruct((n_all, ROUTER_LANES), F32),
                   jax.ShapeDtypeStruct((RT_ROWS, n_all), F32),
                   jax.ShapeDtypeStruct((1, LANES), F32)),
        scratch_shapes=[pltpu.VMEM((1, LANES), F32), pltpu.VMEM((TM, ROUTER_LANES), F32)],
        compiler_params=pltpu.CompilerParams(dimension_semantics=("arbitrary",), vmem_limit_bytes=VMEM_LIMIT),
        name="prompt_back",
    )(o_attn, oconv, sga, sgb, x2d, mod_p, wpa, wpb, wo, g_ffn.reshape(1, -1), wr_hi, wr_lo, b_r)


def _valid_chunks(tile, n_prompt_tiles, n_sample):
    return jnp.where(tile < n_prompt_tiles, TM // DMA_CHUNK, n_sample // DMA_CHUNK)


def _dispatch_body(n_prompt_tiles, n_sample, dest_ref, tail_ref, h_ref, xs_ref, stage_ref, zero_ref, sem):
    i = pl.program_id(0)
    n_all = (n_prompt_tiles + 1) * TM

    def tail_copy(e):
        start = pl.multiple_of(tail_ref[e] * ROW_SUB, ROW_SUB)
        return pltpu.make_async_copy(zero_ref, xs_ref.at[pl.ds(start, EXPERT_TILE * ROW_SUB)], sem)

    @pl.when(i == 0)
    def _():
        zero_ref[...] = jnp.zeros_like(zero_ref)
        for e in range(N_EXPERTS):
            @pl.when(tail_ref[e] >= 0)
            def _():
                tail_copy(e).start()
        for e in range(N_EXPERTS):
            @pl.when(tail_ref[e] >= 0)
            def _():
                tail_copy(e).wait()

    for j in range(ROW_SUB):
        stage_ref[pl.ds(j, TM, stride=ROW_SUB), :] = h_ref[:, j * LANES:(j + 1) * LANES].astype(F32)
    n_chunks = _valid_chunks(i, n_prompt_tiles, n_sample)

    def issue(c, carry):
        for j in range(DMA_CHUNK):
            r = c * DMA_CHUNK + j
            src = stage_ref.at[pl.ds(pl.multiple_of(r * ROW_SUB, ROW_SUB), ROW_SUB)]
            for k in range(2):
                d = pl.multiple_of(dest_ref[k * n_all + i * TM + r], ROW_SUB)
                pltpu.make_async_copy(src, xs_ref.at[pl.ds(d, ROW_SUB)], sem).start(priority=k)
        return carry

    lax.fori_loop(0, n_chunks, issue, 0)

    def drain(c, carry):
        n = 2 * DMA_CHUNK * ROW_SUB
        pltpu.make_async_copy(stage_ref.at[pl.ds(0, n)], xs_ref.at[pl.ds(0, n)], sem).wait()
        return carry

    lax.fori_loop(0, n_chunks, drain, 0)


def _dispatch(dest, tail_start, h2, n_rows, n_prompt_tiles, n_sample):
    n_all = h2.shape[0]
    return pl.pallas_call(
        functools.partial(_dispatch_body, n_prompt_tiles, n_sample),
        grid_spec=pltpu.PrefetchScalarGridSpec(
            num_scalar_prefetch=2,
            grid=(n_all // TM,),
            in_specs=[pl.BlockSpec((TM, D_MODEL), lambda i, *_: (i, 0))],
            out_specs=pl.BlockSpec(memory_space=pl.ANY),
            scratch_shapes=[pltpu.VMEM((TM * ROW_SUB, LANES), F32), pltpu.VMEM((EXPERT_TILE * ROW_SUB, LANES), F32),
                            pltpu.SemaphoreType.DMA],
        ),
        out_shape=jax.ShapeDtypeStruct((n_rows * ROW_SUB, LANES), F32),
        compiler_params=pltpu.CompilerParams(dimension_semantics=("arbitrary",), vmem_limit_bytes=VMEM_LIMIT,
                                             disable_bounds_checks=True),
        name="moe_dispatch",
    )(dest, tail_start, h2)


def _experts_body(te_ref, nt_ref, xs_ref, wg_ref, wu_ref, wd_ref, o_ref):
    @pl.when(pl.program_id(0) < nt_ref[0])
    def _():
        x = jnp.concatenate([xs_ref[pl.ds(j, EXPERT_TILE, stride=ROW_SUB), :].astype(BF16) for j in range(ROW_SUB)],
                            axis=1)
        a = jnp.dot(x, wg_ref[0].astype(BF16), preferred_element_type=F32)
        z = (a * _sigmoid(a)) * jnp.dot(x, wu_ref[0].astype(BF16), preferred_element_type=F32)
        o = jnp.dot(z.astype(BF16), wd_ref[0].astype(BF16), preferred_element_type=F32)
        for j in range(ROW_SUB):
            o_ref[pl.ds(j, EXPERT_TILE, stride=ROW_SUB), :] = o[:, j * LANES:(j + 1) * LANES]


def _experts(tile_expert, n_tiles, xs, w_gate_e, w_up_e, w_down_e):
    max_tiles = tile_expert.shape[0]
    rows = lambda s, te, nt: (jnp.minimum(s, nt[0] - 1), 0)
    row_tile = pl.BlockSpec((EXPERT_TILE * ROW_SUB, LANES), rows)
    weight = lambda shape: pl.BlockSpec((1,) + shape, lambda s, te, nt: (te[jnp.minimum(s, nt[0] - 1)], 0, 0))
    return pl.pallas_call(
        _experts_body,
        grid_spec=pltpu.PrefetchScalarGridSpec(
            num_scalar_prefetch=2,
            grid=(max_tiles,),
            in_specs=[row_tile,
                      weight((D_MODEL, EXPERT_FF)), weight((D_MODEL, EXPERT_FF)), weight((EXPERT_FF, D_MODEL))],
            out_specs=row_tile,
        ),
        out_shape=jax.ShapeDtypeStruct(xs.shape, F32),
        compiler_params=pltpu.CompilerParams(dimension_semantics=("arbitrary",), vmem_limit_bytes=VMEM_LIMIT),
        name="moe_experts",
    )(tile_expert, n_tiles, xs, w_gate_e, w_up_e, w_down_e)


def _combine_body(n_prompt_tiles, n_sample, dest_ref, x1_ref, rt_ref, gp_ref, gs_ref, gf_ref, eo_ref,
                  yp_ref, ys_ref, rows_ref, sem):
    i = pl.program_id(0)
    slot = i % 2
    n_all = (n_prompt_tiles + 1) * TM

    def fetch(tile):
        buf = tile % 2

        def issue(c, carry):
            for j in range(DMA_CHUNK):
                r = c * DMA_CHUNK + j
                row = pl.ds(pl.multiple_of(r * ROW_SUB, ROW_SUB), ROW_SUB)
                for k in range(2):
                    d = pl.multiple_of(dest_ref[k * n_all + tile * TM + r], ROW_SUB)
                    pltpu.make_async_copy(eo_ref.at[pl.ds(d, ROW_SUB)], rows_ref.at[buf, k, row],
                                          sem.at[buf]).start(priority=k)
            return carry

        lax.fori_loop(0, _valid_chunks(tile, n_prompt_tiles, n_sample), issue, 0)

    @pl.when(i == 0)
    def _():
        fetch(i)

    @pl.when(i + 1 < pl.num_programs(0))
    def _():
        fetch(i + 1)

    def drain(c, carry):
        n = 2 * DMA_CHUNK * ROW_SUB
        pltpu.make_async_copy(eo_ref.at[pl.ds(0, n)], rows_ref.at[slot, 0, pl.ds(0, n)], sem.at[slot]).wait()
        return carry

    lax.fori_loop(0, _valid_chunks(i, n_prompt_tiles, n_sample), drain, 0)

    rt = rt_ref[...]
    w1, w2 = rt[:, RT_W1:RT_W1 + 1], rt[:, RT_W2:RT_W2 + 1]
    lane_tile = lambda k, j: rows_ref[slot, k, pl.ds(j, TM, stride=ROW_SUB), :]
    moe = jnp.concatenate([w1 * lane_tile(0, j) + w2 * lane_tile(1, j) for j in range(ROW_SUB)], axis=1)
    is_prompt = i < n_prompt_tiles
    gate2 = jnp.where(is_prompt, gp_ref[...], gs_ref[...])
    y = _rmsnorm(x1_ref[...] + gate2 * moe, gf_ref[...])

    @pl.when(is_prompt)
    def _():
        yp_ref[...] = y

    @pl.when(jnp.logical_not(is_prompt))
    def _():
        ys_ref[...] = y[:n_sample, :]


def _combine(dest, x1, rt, gate2_p, gate2_s, g_final, eo, n_prompt_tiles, n_sample, tiles_per_seq):
    n_all = x1.shape[0]
    last_p = n_prompt_tiles - 1
    return pl.pallas_call(
        functools.partial(_combine_body, n_prompt_tiles, n_sample),
        grid_spec=pltpu.PrefetchScalarGridSpec(
            num_scalar_prefetch=1,
            grid=(n_all // TM,),
            in_specs=[pl.BlockSpec((TM, D_MODEL), lambda i, *_: (i, 0)),
                      pl.BlockSpec((TM, ROUTER_LANES), lambda i, *_: (i, 0)),
                      pl.BlockSpec((None, 1, D_MODEL), lambda i, *_: (jnp.minimum(i, last_p) // tiles_per_seq, 0, 0)),
                      pl.BlockSpec((TM, D_MODEL), lambda i, *_: (0, 0)),
                      pl.BlockSpec((1, D_MODEL), lambda i, *_: (0, 0)),
                      pl.BlockSpec(memory_space=pl.ANY)],
            out_specs=(pl.BlockSpec((TM, D_MODEL), lambda i, *_: (jnp.minimum(i, last_p), 0)),
                       pl.BlockSpec((n_sample, D_MODEL), lambda i, *_: (0, 0))),
            scratch_shapes=[pltpu.VMEM((2, 2, TM * ROW_SUB, LANES), F32), pltpu.SemaphoreType.DMA((2,))],
        ),
        out_shape=(jax.ShapeDtypeStruct((n_prompt_tiles * TM, D_MODEL), F32),
                   jax.ShapeDtypeStruct((n_sample, D_MODEL), F32)),
        compiler_params=pltpu.CompilerParams(dimension_semantics=("arbitrary",), vmem_limit_bytes=VMEM_LIMIT,
                                             disable_bounds_checks=True),
        name="moe_combine",
    )(dest, x1, rt, gate2_p, gate2_s, g_final.reshape(1, -1), eo)


def _routing_tables(rtt, counts, max_tiles):
    cnt = counts[0, :N_EXPERTS].astype(jnp.int32)
    padded = (cnt + EXPERT_TILE - 1) // EXPERT_TILE * EXPERT_TILE
    ends = jnp.cumsum(padded)
    starts = ends - padded
    n_tiles = (ends[-1] // EXPERT_TILE).reshape(1)
    tile_row0 = jnp.arange(max_tiles, dtype=jnp.int32) * EXPERT_TILE
    tile_expert = jnp.minimum(jnp.sum(ends[None, :] <= tile_row0[:, None], axis=1), N_EXPERTS - 1).astype(jnp.int32)
    tail_start = jnp.where(cnt > 0, ends - EXPERT_TILE, -1).astype(jnp.int32)
    eid = rtt[RT_E1:RT_E2 + 1, :].astype(jnp.int32)
    onehot = eid[None] == jnp.arange(N_EXPERTS, dtype=jnp.int32)[:, None, None]
    dest = rtt[RT_R1:RT_R2 + 1, :].astype(jnp.int32) + jnp.sum(jnp.where(onehot, starts[:, None, None], 0), axis=0)
    dest = (jnp.where(eid >= 0, dest, 0) * ROW_SUB).reshape(-1)
    return dest, tail_start, tile_expert, n_tiles


def _sample_front_body(x_ref, mod_ref, g_ref, w_ref, o_ref):
    shift1 = mod_ref[:, 0:D_MODEL]
    scale1 = mod_ref[:, D_MODEL:2 * D_MODEL]
    h = _rmsnorm(x_ref[...], g_ref[...]) * (1.0 + scale1) + shift1
    o_ref[...] = jnp.dot(h, w_ref[...], precision=HIGHEST, preferred_element_type=F32)


def _sample_front(x_s, mod_s, g_mix, w_in):
    nb = x_s.shape[0]
    tn = 256
    return pl.pallas_call(
        _sample_front_body,
        grid=(IN_COLS // tn,),
        in_specs=[pl.BlockSpec((nb, D_MODEL), lambda j: (0, 0)),
                  pl.BlockSpec((nb, 2 * D_MODEL), lambda j: (0, 0)),
                  pl.BlockSpec((1, D_MODEL), lambda j: (0, 0)),
                  pl.BlockSpec((D_MODEL, tn), lambda j: (0, j))],
        out_specs=pl.BlockSpec((nb, tn), lambda j: (0, j)),
        out_shape=jax.ShapeDtypeStruct((nb, IN_COLS), F32),
        compiler_params=pltpu.CompilerParams(dimension_semantics=("arbitrary",), vmem_limit_bytes=VMEM_LIMIT),
        name="sample_front",
    )(x_s, mod_s, g_mix.reshape(1, -1), w_in)


def _sample_attn_body(qkv_ref, c0_ref, c1_ref, c2_ref, oa_ref):
    head = lax.broadcasted_iota(jnp.int32, (HEADS, 1, 1), 0)
    for b in range(qkv_ref.shape[0]):
        outs, lses = [], []
        for g, c_ref in enumerate((c0_ref, c1_ref, c2_ref)):
            window = c_ref.shape[-1]
            def new_rows(part):
                c0 = (part * N_GROUPS + g) * HEADS
                return jnp.stack([qkv_ref[b, :, c0 + h:c0 + h + 1] for h in range(HEADS)], axis=0)

            q = new_rows(0) * (HEAD_DIM ** -0.5)
            k_new = new_rows(1)
            v_new = new_rows(2)
            s = jnp.sum(c_ref[b, 0] * q, axis=1, keepdims=True)
            pos = lax.broadcasted_iota(jnp.int32, (1, 1, window), 2)
            slope = jnp.full((HEADS, 1, 1), _alibi_slope(g, HEADS - 1), F32)
            for h in range(HEADS - 1):
                slope = jnp.where(head == h, _alibi_slope(g, h), slope)
            on_band = (pos & (DILATIONS[g] - 1)) == 0
            s = jnp.where(on_band, s - slope * (window - pos).astype(F32), MASKED)
            s_self = jnp.sum(q * k_new, axis=1, keepdims=True)
            m = jnp.maximum(jnp.max(s, axis=2, keepdims=True), s_self)
            e = jnp.exp(s - m)
            e_self = jnp.exp(s_self - m)
            l = jnp.sum(e, axis=2, keepdims=True) + e_self
            o = jnp.sum(c_ref[b, 1] * e, axis=2, keepdims=True) + e_self * v_new
            outs.append(o / l)
            lses.append(m + jnp.log(l))
        mx = jnp.maximum(jnp.maximum(lses[0], lses[1]), lses[2])
        w = [jnp.exp(x - mx) for x in lses]
        mixed = (w[0] * outs[0] + w[1] * outs[1] + w[2] * outs[2]) / (w[0] + w[1] + w[2])
        for h in range(HEADS):
            oa_ref[b, :, h:h + 1] = mixed[h]


def _sample_attention(qkv_s, cache0, cache1, cache2):
    nb = qkv_s.shape[0]
    bb = 2
    hd = (HEADS, HEAD_DIM)
    caches = [jnp.transpose(c, (0, 1, 3, 4, 5, 2)) for c in (cache0, cache1, cache2)]
    cache_spec = lambda c: pl.BlockSpec((None, bb, 2, *hd, c.shape[-1]), lambda i: (0, i, 0, 0, 0, 0))
    return pl.pallas_call(
        _sample_attn_body,
        grid=(nb // bb,),
        in_specs=[pl.BlockSpec((bb,) + qkv_s.shape[1:], lambda i: (i, 0, 0))] + [cache_spec(c) for c in caches],
        out_specs=pl.BlockSpec((bb, HEAD_DIM, HEADS), lambda i: (i, 0, 0)),
        out_shape=jax.ShapeDtypeStruct((nb, HEAD_DIM, HEADS), F32),
        compiler_params=pltpu.CompilerParams(dimension_semantics=("arbitrary",), vmem_limit_bytes=VMEM_LIMIT),
        name="sample_attention",
    )(qkv_s, *caches)


def _sample_back_body(p_ref, oa_ref, cc_ref, x_ref, mod_ref, cw_ref, wpa_ref, wpb_ref, wo_ref, g_ref,
                      wr_ref, br_ref, cntp_ref, x1_any, h2_any, rt_any, rtt_any,
                      x1_ref, h2_ref, rt_ref, rtt_ref, sconv_ref, cnt_ref):
    del x1_any, h2_any, rt_any, rtt_any
    nb = x_ref.shape[0]
    hdot = functools.partial(jnp.dot, precision=HIGHEST, preferred_element_type=F32)
    bg = p_ref[:, BG_OFF:BG_OFF + CONV_W]
    u = p_ref[:, CG_OFF:CG_OFF + CONV_W] * p_ref[:, UI_OFF:UI_OFF + CONV_W]
    c_old, c_new = cc_ref[:, 0, :], cc_ref[:, 1, :]
    yconv = cw_ref[0:1, :] * c_old + cw_ref[1:2, :] * c_new + cw_ref[2:3, :] * u
    sconv_ref[:, 0, :] = c_new
    sconv_ref[:, 1, :] = u
    sga = _sigmoid(p_ref[:, GA_OFF:GA_OFF + D_MODEL])
    sgb = _sigmoid(p_ref[:, GB_OFF:GB_OFF + D_MODEL])
    merged = sga * hdot(oa_ref[...], wpa_ref[...]) + sgb * hdot(bg * yconv, wpb_ref[...])
    gate1 = mod_ref[:, 2 * D_MODEL:3 * D_MODEL]
    shift2 = mod_ref[:, 3 * D_MODEL:4 * D_MODEL]
    scale2 = mod_ref[:, 4 * D_MODEL:5 * D_MODEL]
    x1 = x_ref[...] + gate1 * hdot(merged, wo_ref[...])
    h2 = _rmsnorm(x1, g_ref[...]) * (1.0 + scale2) + shift2
    rt = _route(hdot(h2, wr_ref[...]) + br_ref[...])
    lane = lax.broadcasted_iota(jnp.int32, (TM, ROUTER_LANES), 1)
    x1_ref[...] = jnp.zeros_like(x1_ref)
    h2_ref[...] = jnp.zeros_like(h2_ref)
    rt_ref[...] = jnp.where((lane == RT_E1) | (lane == RT_E2), -1.0, 0.0)
    x1_ref[0:nb, :] = x1
    h2_ref[0:nb, :] = h2.astype(BF16)
    rt_ref[0:nb, :] = rt
    ranked, total = _with_ranks(rt_ref[...], cntp_ref[...])
    rt_ref[...] = ranked
    rtt_ref[...] = ranked.T[:RT_ROWS, :]
    cnt_ref[...] = total


def _sample_back(proj_s, oa_s, cache_conv, x_s, mod_s, conv_w, w_pa, w_pb, w_o, g_ffn, w_r, b_r, cnt_p,
                 x1, h2, rt, rtt):
    nb = x_s.shape[0]
    last = x1.shape[0] // TM - 1
    args = (proj_s, oa_s, cache_conv, x_s, mod_s, conv_w, w_pa, w_pb, w_o, g_ffn.reshape(1, -1), w_r, b_r, cnt_p)
    full = lambda shape: pl.BlockSpec(shape, lambda i: (0,) * len(shape))
    last_tile = lambda a: pl.BlockSpec((TM, a.shape[1]), lambda i: (last, 0))
    shape_of = lambda a: jax.ShapeDtypeStruct(a.shape, a.dtype)
    return pl.pallas_call(
        _sample_back_body,
        grid=(1,),
        in_specs=[full(a.shape) for a in args] + [pl.BlockSpec(memory_space=pl.ANY)] * 4,
        out_specs=(last_tile(x1), last_tile(h2), last_tile(rt), pl.BlockSpec((RT_ROWS, TM), lambda i: (0, last)),
                   full((nb, 2, CONV_W)), full((1, LANES))),
        out_shape=(shape_of(x1), shape_of(h2), shape_of(rt), shape_of(rtt),
                   jax.ShapeDtypeStruct((nb, 2, CONV_W), F32), jax.ShapeDtypeStruct((1, LANES), F32)),
        input_output_aliases={len(args) + j: j for j in range(4)},
        compiler_params=pltpu.CompilerParams(dimension_semantics=("arbitrary",), vmem_limit_bytes=VMEM_LIMIT),
        name="sample_back",
    )(*args, x1, h2, rt, rtt)


def kernel(x_prompt, x_sample, cache_kv_w128, cache_kv_w512, cache_kv_w2048, cache_conv, c_prompt, c_sample,
           g_mix, w_ada, b_ada, w_in, conv_w, w_pa, w_pb, w_o, g_ffn, w_router_group, b_router_group,
           w_router_expert, b_router_expert, w_gate_e, w_up_e, w_down_e, g_final):
    batch, seq, _ = x_prompt.shape
    nb = x_sample.shape[0]
    assert x_sample.shape[1] == 1 and g_mix.shape[0] == 1, "one layer, one new sample token per sequence"
    assert seq % ATT_TILE == 0 and cache_kv_w128.shape[2] == 128 and cache_kv_w512.shape[2] == 512 \
        and cache_kv_w2048.shape[2] == 2048
    (g_mix, w_ada, b_ada, w_in, conv_w, w_pa, w_pb, w_o, g_ffn, w_rg, b_rg, w_re, b_re, w_gate_e, w_up_e,
     w_down_e) = (a[0] for a in (g_mix, w_ada, b_ada, w_in, conv_w, w_pa, w_pb, w_o, g_ffn, w_router_group,
                                 b_router_group, w_router_expert, b_router_expert, w_gate_e, w_up_e, w_down_e))

    c_all = jnp.concatenate([c_prompt, jnp.zeros((8 - batch, D_MODEL), F32), c_sample], axis=0)
    mod = _modulation(c_all, w_ada, b_ada)
    mod_p = mod[:batch].reshape(batch, 1, 6 * D_MODEL)
    mod_s = mod[8:]

    w_r = jnp.zeros((D_MODEL, ROUTER_LANES), F32)
    w_r = w_r.at[:, :N_EXPERT_GROUPS].set(w_rg).at[:, EXPERT_LANE0:EXPERT_LANE0 + N_EXPERTS].set(w_re)
    b_r = jnp.full((1, ROUTER_LANES), MASKED, F32)
    b_r = b_r.at[0, :N_EXPERT_GROUPS].set(b_rg).at[0, EXPERT_LANE0:EXPERT_LANE0 + N_EXPERTS].set(b_re)
    wr_hi = w_r.astype(BF16)
    wr_lo = (w_r - wr_hi.astype(F32)).astype(BF16)

    x2d = x_prompt.reshape(batch * seq, D_MODEL)
    (a0, a1, a2, oconv, sga, sgb, kv0, kv1, kv2, pconv) = _front(
        x2d, mod_p, g_mix, w_in.astype(BF16), conv_w, batch, seq)
    o_attn = _attention(a0, a1, a2, _band_bias_table(), batch, seq).reshape(batch * seq, GROUP_W)
    x1, h2, rt, rtt, cnt_p = _back(o_attn, oconv, sga, sgb, x2d, mod_p, w_pa.astype(BF16), w_pb.astype(BF16),
                              w_o.astype(BF16), g_ffn, wr_hi, wr_lo, b_r, seq)

    x_s = x_sample.reshape(nb, D_MODEL)
    proj_s = _sample_front(x_s, mod_s[:, :2 * D_MODEL], g_mix, w_in)
    qkv_s = proj_s[:, :3 * ATTN_W].reshape(nb, 3, N_GROUPS, HEADS, HEAD_DIM)
    qkv_t = jnp.transpose(qkv_s.reshape(nb, 3 * N_GROUPS * HEADS, HEAD_DIM), (0, 2, 1))
    oa_s = _sample_attention(qkv_t, cache_kv_w128, cache_kv_w512, cache_kv_w2048)
    oa_s = jnp.transpose(oa_s, (0, 2, 1)).reshape(nb, GROUP_W)
    skv0, skv1, skv2 = (jnp.stack([qkv_s[:, 1, g], qkv_s[:, 2, g]], axis=1) for g in range(N_GROUPS))
    x1, h2, rt, rtt, sconv, counts = _sample_back(proj_s, oa_s, cache_conv[0], x_s, mod_s, conv_w,
                                                  w_pa, w_pb, w_o, g_ffn, w_r, b_r, cnt_p, x1, h2, rt, rtt)

    n_prompt_tiles = batch * seq // TM
    max_tiles = -(-(2 * (batch * seq + nb) + N_EXPERTS * (EXPERT_TILE - 1)) // EXPERT_TILE)
    dest, tail_start, tile_expert, n_tiles = _routing_tables(rtt, counts, max_tiles)
    xs = _dispatch(dest, tail_start, h2, max_tiles * EXPERT_TILE, n_prompt_tiles, nb)
    eo = _experts(tile_expert, n_tiles, xs, w_gate_e, w_up_e, w_down_e)
    gate2_s = jnp.zeros((TM, D_MODEL), F32).at[:nb].set(mod_s[:, 5 * D_MODEL:])
    y_prompt, y_sample = _combine(dest, x1, rt, mod_p[:, :, 5 * D_MODEL:], gate2_s, g_final, eo,
                                  n_prompt_tiles, nb, seq // TM)

    def prompt_state(a):
        a = a.reshape(batch, 2, HEADS, HEAD_DIM, a.shape[-1])
        return jnp.transpose(a, (0, 4, 1, 2, 3))[None]

    sample_state = lambda a: a.reshape(1, nb, 1, 2, HEADS, HEAD_DIM)
    return (y_prompt.reshape(batch, seq, D_MODEL), y_sample.reshape(nb, 1, D_MODEL),
            prompt_state(kv0), prompt_state(kv1), prompt_state(kv2),
            pconv[:, 6:8, :].reshape(1, batch, 2, CONV_W),
            sample_state(skv0), sample_state(skv1), sample_state(skv2),
            sconv.reshape(1, nb, 2, CONV_W))
```

```python
import functools
import math

import jax
import jax.numpy as jnp
from jax import lax
from jax.experimental import pallas as pl
from jax.experimental.pallas import tpu as pltpu

F32 = jnp.float32
BF16 = jnp.bfloat16
HIGHEST = lax.Precision.HIGHEST

D_MODEL = 1024
HEAD_DIM = 64
HEADS = 4
N_GROUPS = 3
GROUP_W = HEADS * HEAD_DIM
ATTN_W = N_GROUPS * GROUP_W
CONV_W = 512
DILATIONS = (1, 4, 16)
BAND = 128
N_EXPERTS = 32
EXPERTS_PER_GROUP = 8
N_EXPERT_GROUPS = 4
EXPERT_FF = 256
RMS_EPS = 1e-6
MASKED = -1e30

Q_OFF, K_OFF, V_OFF = 0, ATTN_W, 2 * ATTN_W
BG_OFF = 3 * ATTN_W
CG_OFF = BG_OFF + CONV_W
UI_OFF = CG_OFF + CONV_W
GA_OFF = UI_OFF + CONV_W
GB_OFF = GA_OFF + D_MODEL
IN_COLS = GB_OFF + D_MODEL

LANES = 128
ROUTER_LANES = 128
EXPERT_LANE0 = N_EXPERT_GROUPS

TM = 512
ATT_TILE = 2048
BACK_PARTS = 2
ATTN_PAIR = 4
EXPERT_TILE = 512
RT_W1, RT_W2, RT_E1, RT_E2, RT_R1, RT_R2 = 0, 1, 2, 3, 4, 5
RT_ROWS = 8
ROW_SUB = D_MODEL // LANES
DMA_CHUNK = 8
VMEM_LIMIT = 56 * 1024 * 1024


def _sigmoid(x):
    return 1.0 / (1.0 + jnp.exp(-x))


def _rmsnorm(x, g):
    return x * lax.rsqrt(jnp.mean(x * x, axis=-1, keepdims=True) + RMS_EPS) * g


def _alibi_slope(g, h):
    return 2.0 ** (-8.0 * (g * HEADS + h + 1) / (N_GROUPS * HEADS))


def _resident(shape):
    nd = len(shape)
    return pl.BlockSpec(shape, lambda *_: (0,) * nd, pipeline_mode=pl.Buffered(1))


def _mod_body(c_ref, w_ref, b_ref, o_ref):
    c = c_ref[...]
    s = c * _sigmoid(c)
    o_ref[...] = jnp.dot(s, w_ref[...], precision=HIGHEST, preferred_element_type=F32) + b_ref[...]


def _modulation(c_all, w_ada, b_ada):
    rows = c_all.shape[0]
    tn = 1024
    return pl.pallas_call(
        _mod_body,
        grid=(6 * D_MODEL // tn,),
        in_specs=[pl.BlockSpec((rows, D_MODEL), lambda j: (0, 0)),
                  pl.BlockSpec((D_MODEL, tn), lambda j: (0, j)),
                  pl.BlockSpec((1, tn), lambda j: (0, j))],
        out_specs=pl.BlockSpec((rows, tn), lambda j: (0, j)),
        out_shape=jax.ShapeDtypeStruct((rows, 6 * D_MODEL), F32),
        compiler_params=pltpu.CompilerParams(dimension_semantics=("arbitrary",), vmem_limit_bytes=VMEM_LIMIT),
        name="modulation",
    )(c_all, w_ada, b_ada.reshape(1, -1))


def _front_body(tiles_per_seq, x_ref, mod_ref, g_ref, w_ref, cw_ref,
                a0_ref, a1_ref, a2_ref, oconv_ref, sga_ref, sgb_ref,
                kv0_ref, kv1_ref, kv2_ref, pconv_ref, res_ref, uprev_ref):
    t_in_seq = pl.program_id(0) % tiles_per_seq
    x = x_ref[...]
    shift1 = mod_ref[:, 0:D_MODEL]
    scale1 = mod_ref[:, D_MODEL:2 * D_MODEL]
    h = (_rmsnorm(x, g_ref[...]) * (1.0 + scale1) + shift1).astype(BF16)

    def proj(c0, n):
        return jnp.dot(h, w_ref[:, c0:c0 + n], preferred_element_type=F32)

    a_refs = (a0_ref, a1_ref, a2_ref)
    kv_refs = (kv0_ref, kv1_ref, kv2_ref)

    for g in range(N_GROUPS):
        d = DILATIONS[g]
        n = TM // d
        for part, base in enumerate((Q_OFF, K_OFF, V_OFF)):
            r = proj(base + g * GROUP_W, GROUP_W)
            cols = slice(part * GROUP_W, (part + 1) * GROUP_W)
            if part == 0:
                r = r * (HEAD_DIM ** -0.5)
            else:
                kvc = slice((part - 1) * GROUP_W, part * GROUP_W)
                kv_refs[g][kvc, :] = (r[TM - BAND:, :] if g == 0 else r).T
            if g == 0:
                a_refs[g][:, cols] = r.astype(BF16)
            else:
                for c in range(GROUP_W // LANES):
                    res_ref[c] = r[:, c * LANES:(c + 1) * LANES]
                for rr in range(d):
                    for c in range(GROUP_W // LANES):
                        c0 = part * GROUP_W + c * LANES
                        a_refs[g][0, rr, :, c0:c0 + LANES] = res_ref[c, pl.ds(rr, n, stride=d), :].astype(BF16)

    bg = proj(BG_OFF, CONV_W)
    u = proj(CG_OFF, CONV_W) * proj(UI_OFF, CONV_W)
    tail = u[TM - 8:, :]
    pconv_ref[...] = tail
    prev = jnp.where(t_in_seq == 0, 0.0, uprev_ref[...])
    row = lax.broadcasted_iota(jnp.int32, (TM, 1), 0)
    u1 = jnp.where(row == 0, prev[7:8, :], pltpu.roll(u, 1, axis=0))
    u2 = jnp.where(row == 0, prev[6:7, :], jnp.where(row == 1, prev[7:8, :], pltpu.roll(u, 2, axis=0)))
    yconv = cw_ref[0:1, :] * u2 + cw_ref[1:2, :] * u1 + cw_ref[2:3, :] * u
    oconv_ref[...] = (bg * yconv).astype(BF16)
    uprev_ref[...] = tail

    sga_ref[...] = _sigmoid(proj(GA_OFF, D_MODEL)).astype(BF16)
    sgb_ref[...] = _sigmoid(proj(GB_OFF, D_MODEL)).astype(BF16)


def _front(x2d, mod_p, g_mix, w_in_bf16, conv_w, batch, seq):
    n_tok = x2d.shape[0]
    n_tiles = n_tok // TM
    tps = seq // TM
    kv2_blocks = ATT_TILE // TM
    out_shape = (
        jax.ShapeDtypeStruct((n_tok, ATTN_W), BF16),
        jax.ShapeDtypeStruct((n_tiles, 4, TM // 4, ATTN_W), BF16),
        jax.ShapeDtypeStruct((n_tiles, 16, TM // 16, ATTN_W), BF16),
        jax.ShapeDtypeStruct((n_tok, CONV_W), BF16),
        jax.ShapeDtypeStruct((n_tok, D_MODEL), BF16),
        jax.ShapeDtypeStruct((n_tok, D_MODEL), BF16),
        jax.ShapeDtypeStruct((batch, 2 * GROUP_W, 128), F32),
        jax.ShapeDtypeStruct((batch, 2 * GROUP_W, 512), F32),
        jax.ShapeDtypeStruct((batch, 2 * GROUP_W, 2048), F32),
        jax.ShapeDtypeStruct((batch, 8, CONV_W), F32),
    )
    out_specs = (
        pl.BlockSpec((TM, ATTN_W), lambda i: (i, 0)),
        pl.BlockSpec((1, 4, TM // 4, ATTN_W), lambda i: (i, 0, 0, 0)),
        pl.BlockSpec((1, 16, TM // 16, ATTN_W), lambda i: (i, 0, 0, 0)),
        pl.BlockSpec((TM, CONV_W), lambda i: (i, 0)),
        pl.BlockSpec((TM, D_MODEL), lambda i: (i, 0)),
        pl.BlockSpec((TM, D_MODEL), lambda i: (i, 0)),
        pl.BlockSpec((None, 2 * GROUP_W, 128), lambda i: (i // tps, 0, 0)),
        pl.BlockSpec((None, 2 * GROUP_W, TM), lambda i: (i // tps, 0, 0)),
        pl.BlockSpec((None, 2 * GROUP_W, TM),
                     lambda i: (i // tps, 0, jnp.maximum(i % tps - (tps - kv2_blocks), 0))),
        pl.BlockSpec((None, 8, CONV_W), lambda i: (i // tps, 0, 0)),
    )
    in_specs = [
        pl.BlockSpec((TM, D_MODEL), lambda i: (i, 0)),
        pl.BlockSpec((None, 1, 6 * D_MODEL), lambda i: (i // tps, 0, 0)),
        _resident((1, D_MODEL)),
        _resident((D_MODEL, IN_COLS)),
        _resident((3, CONV_W)),
    ]
    return pl.pallas_call(
        functools.partial(_front_body, tps),
        grid=(n_tiles,),
        in_specs=in_specs,
        out_specs=out_specs,
        out_shape=out_shape,
        scratch_shapes=[pltpu.VMEM((GROUP_W // LANES, TM, LANES), F32), pltpu.VMEM((8, CONV_W), F32)],
        compiler_params=pltpu.CompilerParams(dimension_semantics=("arbitrary",), vmem_limit_bytes=VMEM_LIMIT),
        name="prompt_front",
    )(x2d, mod_p, g_mix.reshape(1, -1), w_in_bf16, conv_w)


def _band_bias_table():
    qi = jnp.arange(BAND)[:, None]
    kc = jnp.arange(2 * BAND)[None, :]
    delta = qi - (kc - BAND)
    valid = (delta >= 0) & (delta <= BAND)
    tabs = []
    for first in (False, True):
        ok = valid & (kc >= BAND) if first else valid
        for g in range(N_GROUPS):
            for h in range(HEADS):
                b = -_alibi_slope(g, h) * (delta * DILATIONS[g]).astype(F32)
                tabs.append(jnp.where(ok, b, MASKED))
    return jnp.stack(tabs).astype(F32)


def _head_lane_mask(h):
    lane = lax.broadcasted_iota(jnp.int32, (1, GROUP_W), 1)
    return (lane >= h * HEAD_DIM) & (lane < (h + 1) * HEAD_DIM)


def _attn_scores(q, k):
    qs = jnp.concatenate([jnp.where(_head_lane_mask(h), q, jnp.zeros_like(q)) for h in range(HEADS)], axis=0)
    return lax.dot_general(qs, k, (((1,), (1,)), ((), ())), preferred_element_type=F32)


def _attn_probs(s, bias_ref, bias_base):
    s = s + bias_ref[pl.ds(bias_base, HEADS)].reshape(HEADS * BAND, 2 * BAND)
    m = jnp.max(s, axis=-1, keepdims=True)
    e = jnp.exp(s - m)
    l = jnp.sum(e, axis=-1, keepdims=True)
    return (e * (1.0 / l)).astype(BF16), m + jnp.log(l)


def _attn_outputs(p, lse, v):
    ov = jnp.dot(p, v, preferred_element_type=F32)
    o = jnp.zeros((BAND, GROUP_W), F32)
    lse_b = jnp.zeros((BAND, GROUP_W), F32)
    for h in range(HEADS):
        hm = _head_lane_mask(h)
        rows = slice(h * BAND, (h + 1) * BAND)
        o = o + jnp.where(hm, ov[rows], 0.0)
        lse_b = lse_b + jnp.where(hm, lse[rows], 0.0)
    return o, lse_b


def _attn_body(a0c_ref, a0p_ref, a1c_ref, a1p_ref, a2c_ref, a2p_ref, bias_ref, o_ref, og_ref, lg_ref):
    first_tile = pl.program_id(1) == 0
    first_off = jnp.where(first_tile, N_GROUPS * HEADS, 0)
    qs, ks, vs = (slice(0, GROUP_W), slice(GROUP_W, 2 * GROUP_W), slice(2 * GROUP_W, 3 * GROUP_W))

    def run_blocks(blocks):
        loaded = [load() for _, _, _, load in blocks]
        scores = [_attn_scores(q, k) for q, k, _ in loaded]
        probs = [_attn_probs(s, bias_ref, base) for s, (_, _, base, _) in zip(scores, blocks)]
        results = [_attn_outputs(p, lse, v) for (p, lse), (_, _, v) in zip(probs, loaded)]
        for (g, rows, _, _), (o, lse_b) in zip(blocks, results):
            for c in range(GROUP_W // LANES):
                og_ref[g, c, rows, :] = o[:, c * LANES:(c + 1) * LANES]
                lg_ref[g, c, rows, :] = lse_b[:, c * LANES:(c + 1) * LANES]

    def g0_block(n0, base):
        def load():
            q = a0c_ref[pl.ds(n0, BAND), qs]
            if isinstance(n0, int) and n0 == 0:
                k = jnp.concatenate([a0p_ref[:, ks], a0c_ref[0:BAND, ks]], axis=0)
                v = jnp.concatenate([a0p_ref[:, vs], a0c_ref[0:BAND, vs]], axis=0)
            else:
                k = a0c_ref[pl.ds(n0 - BAND, 2 * BAND), ks]
                v = a0c_ref[pl.ds(n0 - BAND, 2 * BAND), vs]
            return q, k, v
        return (0, pl.ds(n0, BAND), base, load)

    run_blocks([g0_block(u * BAND, first_off if u == 0 else 0) for u in range(ATTN_PAIR)])

    def g0_loop(t, carry):
        n0 = pl.multiple_of(t * (ATTN_PAIR * BAND), BAND)
        run_blocks([g0_block(n0 + u * BAND, 0) for u in range(ATTN_PAIR)])
        return carry

    lax.fori_loop(1, ATT_TILE // (ATTN_PAIR * BAND), g0_loop, 0)

    def g1_block(jj, r, prev_ref, prev_jj, base):
        def load():
            q = a1c_ref[jj, r, :, qs]
            k = jnp.concatenate([prev_ref[prev_jj, r, :, ks], a1c_ref[jj, r, :, ks]], axis=0)
            v = jnp.concatenate([prev_ref[prev_jj, r, :, vs], a1c_ref[jj, r, :, vs]], axis=0)
            return q, k, v
        return (1, pl.ds(jj * (4 * BAND) + r, BAND, stride=4), base, load)

    def g1_first(t, carry):
        run_blocks([g1_block(0, t * ATTN_PAIR + u, a1p_ref, 0, first_off + HEADS) for u in range(ATTN_PAIR)])
        return carry

    lax.fori_loop(0, 4 // ATTN_PAIR, g1_first, 0)

    def g1_rest(t, carry):
        jj = 1 + (t * ATTN_PAIR) // 4
        r0 = (t * ATTN_PAIR) % 4
        run_blocks([g1_block(jj, r0 + u, a1c_ref, jj - 1, HEADS) for u in range(ATTN_PAIR)])
        return carry

    lax.fori_loop(0, 12 // ATTN_PAIR, g1_rest, 0)

    n_sub = a2c_ref.shape[0]

    def g2_rows(ref, r, cols):
        return jnp.concatenate([ref[t, r, :, cols] for t in range(n_sub)], axis=0)

    def g2_block(r):
        def load():
            q = g2_rows(a2c_ref, r, qs)
            k = jnp.concatenate([g2_rows(a2p_ref, r, ks), g2_rows(a2c_ref, r, ks)], axis=0)
            v = jnp.concatenate([g2_rows(a2p_ref, r, vs), g2_rows(a2c_ref, r, vs)], axis=0)
            return q, k, v
        return (2, pl.ds(r, BAND, stride=16), first_off + 2 * HEADS, load)

    def g2_loop(t, carry):
        run_blocks([g2_block(t * ATTN_PAIR + u) for u in range(ATTN_PAIR)])
        return carry

    lax.fori_loop(0, 16 // ATTN_PAIR, g2_loop, 0)

    def mix(c, carry):
        rows = pl.ds(pl.multiple_of(c * BAND, BAND), BAND)
        for half in range(GROUP_W // LANES):
            l0, l1, l2 = lg_ref[0, half, rows, :], lg_ref[1, half, rows, :], lg_ref[2, half, rows, :]
            mx = jnp.maximum(jnp.maximum(l0, l1), l2)
            w0, w1, w2 = jnp.exp(l0 - mx), jnp.exp(l1 - mx), jnp.exp(l2 - mx)
            num = w0 * og_ref[0, half, rows, :] + w1 * og_ref[1, half, rows, :] + w2 * og_ref[2, half, rows, :]
            o_ref[rows, half * LANES:(half + 1) * LANES] = (num / (w0 + w1 + w2)).astype(o_ref.dtype)
        return carry

    lax.fori_loop(0, ATT_TILE // BAND, mix, 0)


def _attention(a0, a1, a2, bias, batch, seq):
    steps = seq // ATT_TILE
    sub = ATT_TILE // TM
    a0 = a0.reshape(batch, seq, ATTN_W)
    a1 = a1.reshape(batch, seq // TM, 4, TM // 4, ATTN_W)
    a2 = a2.reshape(batch, seq // TM, 16, TM // 16, ATTN_W)
    in_specs = [
        pl.BlockSpec((None, ATT_TILE, ATTN_W), lambda b, j: (b, j, 0)),
        pl.BlockSpec((None, BAND, ATTN_W), lambda b, j: (b, jnp.maximum(j * (ATT_TILE // BAND) - 1, 0), 0)),
        pl.BlockSpec((None, sub, 4, TM // 4, ATTN_W), lambda b, j: (b, j, 0, 0, 0)),
        pl.BlockSpec((None, 1, 4, TM // 4, ATTN_W), lambda b, j: (b, jnp.maximum(j * sub - 1, 0), 0, 0, 0)),
        pl.BlockSpec((None, sub, 16, TM // 16, ATTN_W), lambda b, j: (b, j, 0, 0, 0)),
        pl.BlockSpec((None, sub, 16, TM // 16, ATTN_W), lambda b, j: (b, jnp.maximum(j - 1, 0), 0, 0, 0)),
        _resident(bias.shape),
    ]
    return pl.pallas_call(
        _attn_body,
        grid=(batch, steps),
        in_specs=in_specs,
        out_specs=pl.BlockSpec((None, ATT_TILE, GROUP_W), lambda b, j: (b, j, 0)),
        out_shape=jax.ShapeDtypeStruct((batch, seq, GROUP_W), BF16),
        scratch_shapes=[pltpu.VMEM((N_GROUPS, GROUP_W // LANES, ATT_TILE, LANES), F32)] * 2,
        compiler_params=pltpu.CompilerParams(dimension_semantics=("arbitrary", "arbitrary"),
                                             vmem_limit_bytes=VMEM_LIMIT),
        name="prompt_attention",
    )(a0, a0, a1, a1, a2, a2, bias)


def _route(logits):
    lane = lax.broadcasted_iota(jnp.int32, logits.shape, 1)
    big = jnp.int32(1 << 20)
    gmask = lane < N_EXPERT_GROUPS
    lg = jnp.where(gmask, logits, MASKED)
    gmax = jnp.max(lg, axis=-1, keepdims=True)
    gidx = jnp.min(jnp.where(gmask & (lg == gmax), lane, big), axis=-1, keepdims=True)
    p_top = 1.0 / jnp.sum(jnp.where(gmask, jnp.exp(lg - gmax), 0.0), axis=-1, keepdims=True)
    lo = EXPERT_LANE0 + EXPERTS_PER_GROUP * gidx
    emask = (lane >= lo) & (lane < lo + EXPERTS_PER_GROUP)
    le = jnp.where(emask, logits, MASKED)
    v1 = jnp.max(le, axis=-1, keepdims=True)
    i1 = jnp.min(jnp.where(emask & (le == v1), lane, big), axis=-1, keepdims=True)
    emask2 = emask & (lane != i1)
    le2 = jnp.where(emask2, logits, MASKED)
    v2 = jnp.max(le2, axis=-1, keepdims=True)
    i2 = jnp.min(jnp.where(emask2 & (le2 == v2), lane, big), axis=-1, keepdims=True)
    e2 = jnp.exp(v2 - v1)
    den = 1.0 + e2
    w1 = (1.0 / den) * p_top
    w2 = (e2 / den) * p_top
    id1 = (i1 - EXPERT_LANE0).astype(F32)
    id2 = (i2 - EXPERT_LANE0).astype(F32)
    return jnp.where(lane == RT_W1, w1, jnp.where(lane == RT_W2, w2,
                     jnp.where(lane == RT_E1, id1, jnp.where(lane == RT_E2, id2, 0.0))))


def _with_ranks(rt, base, count=1.0):
    n = rt.shape[0]
    lane = lax.broadcasted_iota(jnp.int32, rt.shape, 1)
    lane_f = lane.astype(F32)
    oh1 = jnp.where(lane_f == rt[:, RT_E1:RT_E1 + 1], count, 0.0)
    oh2 = jnp.where(lane_f == rt[:, RT_E2:RT_E2 + 1], count, 0.0)
    before = jnp.where(lax.broadcasted_iota(jnp.int32, (n, n), 1) < lax.broadcasted_iota(jnp.int32, (n, n), 0),
                       1.0, 0.0).astype(BF16)
    p1 = jnp.dot(before, oh1.astype(BF16), preferred_element_type=F32)
    p2 = jnp.dot(before, oh2.astype(BF16), preferred_element_type=F32)
    c1 = jnp.sum(oh1, axis=0, keepdims=True)
    c2 = jnp.sum(oh2, axis=0, keepdims=True)
    rank1 = jnp.sum(oh1 * (base + p1), axis=-1, keepdims=True)
    rank2 = jnp.sum(oh2 * (base + c1 + p2), axis=-1, keepdims=True)
    return jnp.where(lane == RT_R1, rank1, jnp.where(lane == RT_R2, rank2, rt)), base + c1 + c2


def _back_body(oa_ref, oc_ref, sga_ref, sgb_ref, x_ref, mod_ref, wpa_ref, wpb_ref, wo_ref, g_ref,
               wrh_ref, wrl_ref, br_ref, x1_ref, h2_ref, rt_ref, rtt_ref, cnt_ref, base_ref, logit_ref):
    step = pl.program_id(0)

    @pl.when(step == 0)
    def _():
        base_ref[...] = jnp.zeros_like(base_ref)
        logit_ref[...] = jnp.zeros_like(logit_ref)

    routed_prev = _route(logit_ref[...])
    gate1 = mod_ref[:, 2 * D_MODEL:3 * D_MODEL]
    shift2 = mod_ref[:, 3 * D_MODEL:4 * D_MODEL]
    scale2 = mod_ref[:, 4 * D_MODEL:5 * D_MODEL]
    n_part = TM // BACK_PARTS
    parts = [pl.ds(j * n_part, n_part) for j in range(BACK_PARTS)]
    dot = functools.partial(jnp.dot, preferred_element_type=F32)
    pab = [(dot(oa_ref[r, :], wpa_ref[...]), dot(oc_ref[r, :], wpb_ref[...])) for r in parts]
    merged = [(sga_ref[r, :].astype(F32) * pa + sgb_ref[r, :].astype(F32) * pb).astype(BF16)
              for r, (pa, pb) in zip(parts, pab)]
    delta = [dot(m, wo_ref[...]) for m in merged]
    x1 = [x_ref[r, :] + gate1 * d for r, d in zip(parts, delta)]
    h2 = [_rmsnorm(x, g_ref[...]) * (1.0 + scale2) + shift2 for x in x1]
    hi = [h.astype(BF16) for h in h2]
    lo = [(h - b.astype(F32)).astype(BF16) for h, b in zip(h2, hi)]
    logits = [dot(b, wrh_ref[...]) + dot(l, wrh_ref[...]) + dot(b, wrl_ref[...]) + br_ref[...]
              for b, l in zip(hi, lo)]
    for r, x, b, lg in zip(parts, x1, hi, logits):
        x1_ref[r, :] = x
        h2_ref[r, :] = b
        logit_ref[r, :] = lg
    rt, total = _with_ranks(routed_prev, base_ref[...], jnp.where(step > 0, 1.0, 0.0))
    rt_ref[...] = rt
    rtt_ref[...] = rt.T[:RT_ROWS, :]
    base_ref[...] = total
    cnt_ref[...] = total


def _back(o_attn, oconv, sga, sgb, x2d, mod_p, wpa, wpb, wo, g_ffn, wr_hi, wr_lo, b_r, seq):
    n_tok = x2d.shape[0]
    n_all = n_tok + TM
    tps = seq // TM
    last = n_tok // TM - 1
    tile = lambda i: jnp.minimum(i, last)
    routed = lambda i: jnp.maximum(i - 1, 0)
    row = lambda w: pl.BlockSpec((TM, w), lambda i: (tile(i), 0))
    return pl.pallas_call(
        _back_body,
        grid=(last + 2,),
        in_specs=[row(GROUP_W), row(CONV_W), row(D_MODEL), row(D_MODEL), row(D_MODEL),
                  pl.BlockSpec((None, 1, 6 * D_MODEL), lambda i: (tile(i) // tps, 0, 0)),
                  _resident(wpa.shape), _resident(wpb.shape), _resident(wo.shape), _resident((1, D_MODEL)),
                  _resident(wr_hi.shape), _resident(wr_lo.shape), _resident(b_r.shape)],
        out_specs=(row(D_MODEL), row(D_MODEL),
                   pl.BlockSpec((TM, ROUTER_LANES), lambda i: (routed(i), 0)),
                   pl.BlockSpec((RT_ROWS, TM), lambda i: (0, routed(i))),
                   pl.BlockSpec((1, LANES), lambda i: (0, 0))),
        out_shape=(jax.ShapeDtypeStruct((n_all, D_MODEL), F32),
                   jax.ShapeDtypeStruct((n_all, D_MODEL), BF16),
                   jax.ShapeDtypeStruct((n_all, ROUTER_LANES), F32),
                   jax.ShapeDtypeStruct((RT_ROWS, n_all), F32),
                   jax.ShapeDtypeStruct((1, LANES), F32)),
        scratch_shapes=[pltpu.VMEM((1, LANES), F32), pltpu.VMEM((TM, ROUTER_LANES), F32)],
        compiler_params=pltpu.CompilerParams(dimension_semantics=("arbitrary",), vmem_limit_bytes=VMEM_LIMIT),
        name="prompt_back",
    )(o_attn, oconv, sga, sgb, x2d, mod_p, wpa, wpb, wo, g_ffn.reshape(1, -1), wr_hi, wr_lo, b_r)


def _valid_chunks(tile, n_prompt_tiles, n_sample):
    return jnp.where(tile < n_prompt_tiles, TM // DMA_CHUNK, n_sample // DMA_CHUNK)


def _dispatch_body(n_prompt_tiles, n_sample, dest_ref, tail_ref, h_ref, xs_ref, stage_ref, zero_ref, sem):
    i = pl.program_id(0)
    n_all = (n_prompt_tiles + 1) * TM

    def tail_copy(e):
        start = pl.multiple_of(tail_ref[e] * ROW_SUB, ROW_SUB)
        return pltpu.make_async_copy(zero_ref, xs_ref.at[pl.ds(start, EXPERT_TILE * ROW_SUB)], sem)

    @pl.when(i == 0)
    def _():
        zero_ref[...] = jnp.zeros_like(zero_ref)
        for e in range(N_EXPERTS):
            @pl.when(tail_ref[e] >= 0)
            def _():
                tail_copy(e).start()
        for e in range(N_EXPERTS):
            @pl.when(tail_ref[e] >= 0)
            def _():
                tail_copy(e).wait()

    for j in range(ROW_SUB):
        stage_ref[pl.ds(j, TM, stride=ROW_SUB), :] = h_ref[:, j * LANES:(j + 1) * LANES].astype(F32)
    n_chunks = _valid_chunks(i, n_prompt_tiles, n_sample)

    def issue(c, carry):
        for j in range(DMA_CHUNK):
            r = c * DMA_CHUNK + j
            src = stage_ref.at[pl.ds(pl.multiple_of(r * ROW_SUB, ROW_SUB), ROW_SUB)]
            for k in range(2):
                d = pl.multiple_of(dest_ref[k * n_all + i * TM + r], ROW_SUB)
                pltpu.make_async_copy(src, xs_ref.at[pl.ds(d, ROW_SUB)], sem).start(priority=k)
        return carry

    lax.fori_loop(0, n_chunks, issue, 0)

    def drain(c, carry):
        n = 2 * DMA_CHUNK * ROW_SUB
        pltpu.make_async_copy(stage_ref.at[pl.ds(0, n)], xs_ref.at[pl.ds(0, n)], sem).wait()
        return carry

    lax.fori_loop(0, n_chunks, drain, 0)


def _dispatch(dest, tail_start, h2, n_rows, n_prompt_tiles, n_sample):
    n_all = h2.shape[0]
    return pl.pallas_call(
        functools.partial(_dispatch_body, n_prompt_tiles, n_sample),
        grid_spec=pltpu.PrefetchScalarGridSpec(
            num_scalar_prefetch=2,
            grid=(n_all // TM,),
            in_specs=[pl.BlockSpec((TM, D_MODEL), lambda i, *_: (i, 0))],
            out_specs=pl.BlockSpec(memory_space=pl.ANY),
            scratch_shapes=[pltpu.VMEM((TM * ROW_SUB, LANES), F32), pltpu.VMEM((EXPERT_TILE * ROW_SUB, LANES), F32),
                            pltpu.SemaphoreType.DMA],
        ),
        out_shape=jax.ShapeDtypeStruct((n_rows * ROW_SUB, LANES), F32),
        compiler_params=pltpu.CompilerParams(dimension_semantics=("arbitrary",), vmem_limit_bytes=VMEM_LIMIT,
                                             disable_bounds_checks=True),
        name="moe_dispatch",
    )(dest, tail_start, h2)


def _experts_body(te_ref, nt_ref, xs_ref, wg_ref, wu_ref, wd_ref, o_ref):
    @pl.when(pl.program_id(0) < nt_ref[0])
    def _():
        x = jnp.concatenate([xs_ref[pl.ds(j, EXPERT_TILE, stride=ROW_SUB), :].astype(BF16) for j in range(ROW_SUB)],
                            axis=1)
        a = jnp.dot(x, wg_ref[0].astype(BF16), preferred_element_type=F32)
        z = (a * _sigmoid(a)) * jnp.dot(x, wu_ref[0].astype(BF16), preferred_element_type=F32)
        o = jnp.dot(z.astype(BF16), wd_ref[0].astype(BF16), preferred_element_type=F32)
        for j in range(ROW_SUB):
            o_ref[pl.ds(j, EXPERT_TILE, stride=ROW_SUB), :] = o[:, j * LANES:(j + 1) * LANES]


def _experts(tile_expert, n_tiles, xs, w_gate_e, w_up_e, w_down_e):
    max_tiles = tile_expert.shape[0]
    rows = lambda s, te, nt: (jnp.minimum(s, nt[0] - 1), 0)
    row_tile = pl.BlockSpec((EXPERT_TILE * ROW_SUB, LANES), rows)
    weight = lambda shape: pl.BlockSpec((1,) + shape, lambda s, te, nt: (te[jnp.minimum(s, nt[0] - 1)], 0, 0))
    return pl.pallas_call(
        _experts_body,
        grid_spec=pltpu.PrefetchScalarGridSpec(
            num_scalar_prefetch=2,
            grid=(max_tiles,),
            in_specs=[row_tile,
                      weight((D_MODEL, EXPERT_FF)), weight((D_MODEL, EXPERT_FF)), weight((EXPERT_FF, D_MODEL))],
            out_specs=row_tile,
        ),
        out_shape=jax.ShapeDtypeStruct(xs.shape, F32),
        compiler_params=pltpu.CompilerParams(dimension_semantics=("arbitrary",), vmem_limit_bytes=VMEM_LIMIT),
        name="moe_experts",
    )(tile_expert, n_tiles, xs, w_gate_e, w_up_e, w_down_e)


def _combine_body(n_prompt_tiles, n_sample, dest_ref, x1_ref, rt_ref, gp_ref, gs_ref, gf_ref, eo_ref,
                  yp_ref, ys_ref, rows_ref, sem):
    i = pl.program_id(0)
    slot = i % 2
    n_all = (n_prompt_tiles + 1) * TM

    def fetch(tile, part=None):
        buf = tile % 2
        n_chunks = _valid_chunks(tile, n_prompt_tiles, n_sample)
        lo, hi = {None: (0, n_chunks), 0: (0, n_chunks // 2), 1: (n_chunks // 2, n_chunks)}[part]

        def issue(c, carry):
            for j in range(DMA_CHUNK):
                r = c * DMA_CHUNK + j
                row = pl.ds(pl.multiple_of(r * ROW_SUB, ROW_SUB), ROW_SUB)
                for k in range(2):
                    d = pl.multiple_of(dest_ref[k * n_all + tile * TM + r], ROW_SUB)
                    pltpu.make_async_copy(eo_ref.at[pl.ds(d, ROW_SUB)], rows_ref.at[buf, k, row],
                                          sem.at[buf]).start(priority=k)
            return carry

        lax.fori_loop(lo, hi, issue, 0)

    @pl.when(i == 0)
    def _():
        fetch(i)

    def drain(c, carry):
        n = 2 * DMA_CHUNK * ROW_SUB
        pltpu.make_async_copy(eo_ref.at[pl.ds(0, n)], rows_ref.at[slot, 0, pl.ds(0, n)], sem.at[slot]).wait()
        return carry

    lax.fori_loop(0, _valid_chunks(i, n_prompt_tiles, n_sample), drain, 0)

    has_next = i + 1 < pl.num_programs(0)

    @pl.when(has_next)
    def _():
        fetch(i + 1, 0)

    rt = rt_ref[...]
    w1, w2 = rt[:, RT_W1:RT_W1 + 1], rt[:, RT_W2:RT_W2 + 1]
    lane_tile = lambda k, j: rows_ref[slot, k, pl.ds(j, TM, stride=ROW_SUB), :]
    moe = jnp.concatenate([w1 * lane_tile(0, j) + w2 * lane_tile(1, j) for j in range(ROW_SUB)], axis=1)
    is_prompt = i < n_prompt_tiles
    gate2 = jnp.where(is_prompt, gp_ref[...], gs_ref[...])
    y = _rmsnorm(x1_ref[...] + gate2 * moe, gf_ref[...])

    @pl.when(is_prompt)
    def _():
        yp_ref[...] = y

    @pl.when(has_next)
    def _():
        fetch(i + 1, 1)

    @pl.when(jnp.logical_not(is_prompt))
    def _():
        ys_ref[...] = y[:n_sample, :]


def _combine(dest, x1, rt, gate2_p, gate2_s, g_final, eo, n_prompt_tiles, n_sample, tiles_per_seq):
    n_all = x1.shape[0]
    last_p = n_prompt_tiles - 1
    return pl.pallas_call(
        functools.partial(_combine_body, n_prompt_tiles, n_sample),
        grid_spec=pltpu.PrefetchScalarGridSpec(
            num_scalar_prefetch=1,
            grid=(n_all // TM,),
            in_specs=[pl.BlockSpec((TM, D_MODEL), lambda i, *_: (i, 0)),
                      pl.BlockSpec((TM, ROUTER_LANES), lambda i, *_: (i, 0)),
                      pl.BlockSpec((None, 1, D_MODEL), lambda i, *_: (jnp.minimum(i, last_p) // tiles_per_seq, 0, 0)),
                      pl.BlockSpec((TM, D_MODEL), lambda i, *_: (0, 0)),
                      pl.BlockSpec((1, D_MODEL), lambda i, *_: (0, 0)),
                      pl.BlockSpec(memory_space=pl.ANY)],
            out_specs=(pl.BlockSpec((TM, D_MODEL), lambda i, *_: (jnp.minimum(i, last_p), 0)),
                       pl.BlockSpec((n_sample, D_MODEL), lambda i, *_: (0, 0))),
            scratch_shapes=[pltpu.VMEM((2, 2, TM * ROW_SUB, LANES), F32), pltpu.SemaphoreType.DMA((2,))],
        ),
        out_shape=(jax.ShapeDtypeStruct((n_prompt_tiles * TM, D_MODEL), F32),
                   jax.ShapeDtypeStruct((n_sample, D_MODEL), F32)),
        compiler_params=pltpu.CompilerParams(dimension_semantics=("arbitrary",), vmem_limit_bytes=VMEM_LIMIT,
                                             disable_bounds_checks=True),
        name="moe_combine",
    )(dest, x1, rt, gate2_p, gate2_s, g_final.reshape(1, -1), eo)


def _routing_tables(rtt, counts, max_tiles):
    cnt = counts[0, :N_EXPERTS].astype(jnp.int32)
    padded = (cnt + EXPERT_TILE - 1) // EXPERT_TILE * EXPERT_TILE
    ends = jnp.cumsum(padded)
    starts = ends - padded
    n_tiles = (ends[-1] // EXPERT_TILE).reshape(1)
    tile_row0 = jnp.arange(max_tiles, dtype=jnp.int32) * EXPERT_TILE
    tile_expert = jnp.minimum(jnp.sum(ends[None, :] <= tile_row0[:, None], axis=1), N_EXPERTS - 1).astype(jnp.int32)
    tail_start = jnp.where(cnt > 0, ends - EXPERT_TILE, -1).astype(jnp.int32)
    eid = rtt[RT_E1:RT_E2 + 1, :].astype(jnp.int32)
    onehot = eid[None] == jnp.arange(N_EXPERTS, dtype=jnp.int32)[:, None, None]
    dest = rtt[RT_R1:RT_R2 + 1, :].astype(jnp.int32) + jnp.sum(jnp.where(onehot, starts[:, None, None], 0), axis=0)
    dest = (jnp.where(eid >= 0, dest, 0) * ROW_SUB).reshape(-1)
    return dest, tail_start, tile_expert, n_tiles


def _sample_front_body(x_ref, mod_ref, g_ref, w_ref, o_ref):
    shift1 = mod_ref[:, 0:D_MODEL]
    scale1 = mod_ref[:, D_MODEL:2 * D_MODEL]
    h = _rmsnorm(x_ref[...], g_ref[...]) * (1.0 + scale1) + shift1
    o_ref[...] = jnp.dot(h, w_ref[...], precision=HIGHEST, preferred_element_type=F32)


def _sample_front(x_s, mod_s, g_mix, w_in):
    nb = x_s.shape[0]
    tn = 256
    return pl.pallas_call(
        _sample_front_body,
        grid=(IN_COLS // tn,),
        in_specs=[pl.BlockSpec((nb, D_MODEL), lambda j: (0, 0)),
                  pl.BlockSpec((nb, 2 * D_MODEL), lambda j: (0, 0)),
                  pl.BlockSpec((1, D_MODEL), lambda j: (0, 0)),
                  pl.BlockSpec((D_MODEL, tn), lambda j: (0, j))],
        out_specs=pl.BlockSpec((nb, tn), lambda j: (0, j)),
        out_shape=jax.ShapeDtypeStruct((nb, IN_COLS), F32),
        compiler_params=pltpu.CompilerParams(dimension_semantics=("arbitrary",), vmem_limit_bytes=VMEM_LIMIT),
        name="sample_front",
    )(x_s, mod_s, g_mix.reshape(1, -1), w_in)


def _sample_attn_body(qkv_ref, c0_ref, c1_ref, c2_ref, oa_ref):
    head = lax.broadcasted_iota(jnp.int32, (HEADS, 1, 1), 0)
    for b in range(qkv_ref.shape[0]):
        outs, lses = [], []
        for g, c_ref in enumerate((c0_ref, c1_ref, c2_ref)):
            window = c_ref.shape[-1]
            def new_rows(part):
                c0 = (part * N_GROUPS + g) * HEADS
                return jnp.stack([qkv_ref[b, :, c0 + h:c0 + h + 1] for h in range(HEADS)], axis=0)

            q = new_rows(0) * (HEAD_DIM ** -0.5)
            k_new = new_rows(1)
            v_new = new_rows(2)
            s = jnp.sum(c_ref[b, 0] * q, axis=1, keepdims=True)
            pos = lax.broadcasted_iota(jnp.int32, (1, 1, window), 2)
            slope = jnp.full((HEADS, 1, 1), _alibi_slope(g, HEADS - 1), F32)
            for h in range(HEADS - 1):
                slope = jnp.where(head == h, _alibi_slope(g, h), slope)
            on_band = (pos & (DILATIONS[g] - 1)) == 0
            s = jnp.where(on_band, s - slope * (window - pos).astype(F32), MASKED)
            s_self = jnp.sum(q * k_new, axis=1, keepdims=True)
            m = jnp.maximum(jnp.max(s, axis=2, keepdims=True), s_self)
            e = jnp.exp(s - m)
            e_self = jnp.exp(s_self - m)
            l = jnp.sum(e, axis=2, keepdims=True) + e_self
            o = jnp.sum(c_ref[b, 1] * e, axis=2, keepdims=True) + e_self * v_new
            outs.append(o / l)
            lses.append(m + jnp.log(l))
        mx = jnp.maximum(jnp.maximum(lses[0], lses[1]), lses[2])
        w = [jnp.exp(x - mx) for x in lses]
        mixed = (w[0] * outs[0] + w[1] * outs[1] + w[2] * outs[2]) / (w[0] + w[1] + w[2])
        for h in range(HEADS):
            oa_ref[b, :, h:h + 1] = mixed[h]


def _sample_attention(qkv_s, cache0, cache1, cache2):
    nb = qkv_s.shape[0]
    bb = 2
    hd = (HEADS, HEAD_DIM)
    caches = [jnp.transpose(c, (0, 1, 3, 4, 5, 2)) for c in (cache0, cache1, cache2)]
    cache_spec = lambda c: pl.BlockSpec((None, bb, 2, *hd, c.shape[-1]), lambda i: (0, i, 0, 0, 0, 0))
    return pl.pallas_call(
        _sample_attn_body,
        grid=(nb // bb,),
        in_specs=[pl.BlockSpec((bb,) + qkv_s.shape[1:], lambda i: (i, 0, 0))] + [cache_spec(c) for c in caches],
        out_specs=pl.BlockSpec((bb, HEAD_DIM, HEADS), lambda i: (i, 0, 0)),
        out_shape=jax.ShapeDtypeStruct((nb, HEAD_DIM, HEADS), F32),
        compiler_params=pltpu.CompilerParams(dimension_semantics=("arbitrary",), vmem_limit_bytes=VMEM_LIMIT),
        name="sample_attention",
    )(qkv_s, *caches)


def _sample_back_body(p_ref, oa_ref, cc_ref, x_ref, mod_ref, cw_ref, wpa_ref, wpb_ref, wo_ref, g_ref,
                      wr_ref, br_ref, cntp_ref, x1_any, h2_any, rt_any, rtt_any,
                      x1_ref, h2_ref, rt_ref, rtt_ref, sconv_ref, cnt_ref):
    del x1_any, h2_any, rt_any, rtt_any
    nb = x_ref.shape[0]
    hdot = functools.partial(jnp.dot, precision=HIGHEST, preferred_element_type=F32)
    bg = p_ref[:, BG_OFF:BG_OFF + CONV_W]
    u = p_ref[:, CG_OFF:CG_OFF + CONV_W] * p_ref[:, UI_OFF:UI_OFF + CONV_W]
    c_old, c_new = cc_ref[:, 0, :], cc_ref[:, 1, :]
    yconv = cw_ref[0:1, :] * c_old + cw_ref[1:2, :] * c_new + cw_ref[2:3, :] * u
    sconv_ref[:, 0, :] = c_new
    sconv_ref[:, 1, :] = u
    sga = _sigmoid(p_ref[:, GA_OFF:GA_OFF + D_MODEL])
    sgb = _sigmoid(p_ref[:, GB_OFF:GB_OFF + D_MODEL])
    merged = sga * hdot(oa_ref[...], wpa_ref[...]) + sgb * hdot(bg * yconv, wpb_ref[...])
    gate1 = mod_ref[:, 2 * D_MODEL:3 * D_MODEL]
    shift2 = mod_ref[:, 3 * D_MODEL:4 * D_MODEL]
    scale2 = mod_ref[:, 4 * D_MODEL:5 * D_MODEL]
    x1 = x_ref[...] + gate1 * hdot(merged, wo_ref[...])
    h2 = _rmsnorm(x1, g_ref[...]) * (1.0 + scale2) + shift2
    rt = _route(hdot(h2, wr_ref[...]) + br_ref[...])
    lane = lax.broadcasted_iota(jnp.int32, (TM, ROUTER_LANES), 1)
    x1_ref[...] = jnp.zeros_like(x1_ref)
    h2_ref[...] = jnp.zeros_like(h2_ref)
    rt_ref[...] = jnp.where((lane == RT_E1) | (lane == RT_E2), -1.0, 0.0)
    x1_ref[0:nb, :] = x1
    h2_ref[0:nb, :] = h2.astype(BF16)
    rt_ref[0:nb, :] = rt
    ranked, total = _with_ranks(rt_ref[...], cntp_ref[...])
    rt_ref[...] = ranked
    rtt_ref[...] = ranked.T[:RT_ROWS, :]
    cnt_ref[...] = total


def _sample_back(proj_s, oa_s, cache_conv, x_s, mod_s, conv_w, w_pa, w_pb, w_o, g_ffn, w_r, b_r, cnt_p,
                 x1, h2, rt, rtt):
    nb = x_s.shape[0]
    last = x1.shape[0] // TM - 1
    args = (proj_s, oa_s, cache_conv, x_s, mod_s, conv_w, w_pa, w_pb, w_o, g_ffn.reshape(1, -1), w_r, b_r, cnt_p)
    full = lambda shape: pl.BlockSpec(shape, lambda i: (0,) * len(shape))
    last_tile = lambda a: pl.BlockSpec((TM, a.shape[1]), lambda i: (last, 0))
    shape_of = lambda a: jax.ShapeDtypeStruct(a.shape, a.dtype)
    return pl.pallas_call(
        _sample_back_body,
        grid=(1,),
        in_specs=[full(a.shape) for a in args] + [pl.BlockSpec(memory_space=pl.ANY)] * 4,
        out_specs=(last_tile(x1), last_tile(h2), last_tile(rt), pl.BlockSpec((RT_ROWS, TM), lambda i: (0, last)),
                   full((nb, 2, CONV_W)), full((1, LANES))),
        out_shape=(shape_of(x1), shape_of(h2), shape_of(rt), shape_of(rtt),
                   jax.ShapeDtypeStruct((nb, 2, CONV_W), F32), jax.ShapeDtypeStruct((1, LANES), F32)),
        input_output_aliases={len(args) + j: j for j in range(4)},
        compiler_params=pltpu.CompilerParams(dimension_semantics=("arbitrary",), vmem_limit_bytes=VMEM_LIMIT),
        name="sample_back",
    )(*args, x1, h2, rt, rtt)


def kernel(x_prompt, x_sample, cache_kv_w128, cache_kv_w512, cache_kv_w2048, cache_conv, c_prompt, c_sample,
           g_mix, w_ada, b_ada, w_in, conv_w, w_pa, w_pb, w_o, g_ffn, w_router_group, b_router_group,
           w_router_expert, b_router_expert, w_gate_e, w_up_e, w_down_e, g_final):
    batch, seq, _ = x_prompt.shape
    nb = x_sample.shape[0]
    assert x_sample.shape[1] == 1 and g_mix.shape[0] == 1, "one layer, one new sample token per sequence"
    assert seq % ATT_TILE == 0 and cache_kv_w128.shape[2] == 128 and cache_kv_w512.shape[2] == 512 \
        and cache_kv_w2048.shape[2] == 2048
    (g_mix, w_ada, b_ada, w_in, conv_w, w_pa, w_pb, w_o, g_ffn, w_rg, b_rg, w_re, b_re, w_gate_e, w_up_e,
     w_down_e) = (a[0] for a in (g_mix, w_ada, b_ada, w_in, conv_w, w_pa, w_pb, w_o, g_ffn, w_router_group,
                                 b_router_group, w_router_expert, b_router_expert, w_gate_e, w_up_e, w_down_e))

    c_all = jnp.concatenate([c_prompt, jnp.zeros((8 - batch, D_MODEL), F32), c_sample], axis=0)
    mod = _modulation(c_all, w_ada, b_ada)
    mod_p = mod[:batch].reshape(batch, 1, 6 * D_MODEL)
    mod_s = mod[8:]

    w_r = jnp.zeros((D_MODEL, ROUTER_LANES), F32)
    w_r = w_r.at[:, :N_EXPERT_GROUPS].set(w_rg).at[:, EXPERT_LANE0:EXPERT_LANE0 + N_EXPERTS].set(w_re)
    b_r = jnp.full((1, ROUTER_LANES), MASKED, F32)
    b_r = b_r.at[0, :N_EXPERT_GROUPS].set(b_rg).at[0, EXPERT_LANE0:EXPERT_LANE0 + N_EXPERTS].set(b_re)
    wr_hi = w_r.astype(BF16)
    wr_lo = (w_r - wr_hi.astype(F32)).astype(BF16)

    x2d = x_prompt.reshape(batch * seq, D_MODEL)
    (a0, a1, a2, oconv, sga, sgb, kv0, kv1, kv2, pconv) = _front(
        x2d, mod_p, g_mix, w_in.astype(BF16), conv_w, batch, seq)
    o_attn = _attention(a0, a1, a2, _band_bias_table(), batch, seq).reshape(batch * seq, GROUP_W)
    x1, h2, rt, rtt, cnt_p = _back(o_attn, oconv, sga, sgb, x2d, mod_p, w_pa.astype(BF16), w_pb.astype(BF16),
                              w_o.astype(BF16), g_ffn, wr_hi, wr_lo, b_r, seq)

    x_s = x_sample.reshape(nb, D_MODEL)
    proj_s = _sample_front(x_s, mod_s[:, :2 * D_MODEL], g_mix, w_in)
    qkv_s = proj_s[:, :3 * ATTN_W].reshape(nb, 3, N_GROUPS, HEADS, HEAD_DIM)
    qkv_t = jnp.transpose(qkv_s.reshape(nb, 3 * N_GROUPS * HEADS, HEAD_DIM), (0, 2, 1))
    oa_s = _sample_attention(qkv_t, cache_kv_w128, cache_kv_w512, cache_kv_w2048)
    oa_s = jnp.transpose(oa_s, (0, 2, 1)).reshape(nb, GROUP_W)
    skv0, skv1, skv2 = (jnp.stack([qkv_s[:, 1, g], qkv_s[:, 2, g]], axis=1) for g in range(N_GROUPS))
    x1, h2, rt, rtt, sconv, counts = _sample_back(proj_s, oa_s, cache_conv[0], x_s, mod_s, conv_w,
                                                  w_pa, w_pb, w_o, g_ffn, w_r, b_r, cnt_p, x1, h2, rt, rtt)

    n_prompt_tiles = batch * seq // TM
    max_tiles = -(-(2 * (batch * seq + nb) + N_EXPERTS * (EXPERT_TILE - 1)) // EXPERT_TILE)
    dest, tail_start, tile_expert, n_tiles = _routing_tables(rtt, counts, max_tiles)
    xs = _dispatch(dest, tail_start, h2, max_tiles * EXPERT_TILE, n_prompt_tiles, nb)
    eo = _experts(tile_expert, n_tiles, xs, w_gate_e, w_up_e, w_down_e)
    gate2_s = jnp.zeros((TM, D_MODEL), F32).at[:nb].set(mod_s[:, 5 * D_MODEL:])
    y_prompt, y_sample = _combine(dest, x1, rt, mod_p[:, :, 5 * D_MODEL:], gate2_s, g_final, eo,
                                  n_prompt_tiles, nb, seq // TM)

    def prompt_state(a):
        a = a.reshape(batch, 2, HEADS, HEAD_DIM, a.shape[-1])
        return jnp.transpose(a, (0, 4, 1, 2, 3))[None]

    sample_state = lambda a: a.reshape(1, nb, 1, 2, HEADS, HEAD_DIM)
    return (y_prompt.reshape(batch, seq, D_MODEL), y_sample.reshape(nb, 1, D_MODEL),
            prompt_state(kv0), prompt_state(kv1), prompt_state(kv2),
            pconv[:, 6:8, :].reshape(1, batch, 2, CONV_W),
            sample_state(skv0), sample_state(skv1), sample_state(skv2),
            sconv.reshape(1, nb, 2, CONV_W))
```

```python
import functools
import math

import jax
import jax.numpy as jnp
from jax import lax
from jax.experimental import pallas as pl
from jax.experimental.pallas import tpu as pltpu

F32 = jnp.float32
BF16 = jnp.bfloat16
HIGHEST = lax.Precision.HIGHEST

D_MODEL = 1024
HEAD_DIM = 64
HEADS = 4
N_GROUPS = 3
GROUP_W = HEADS * HEAD_DIM
ATTN_W = N_GROUPS * GROUP_W
CONV_W = 512
DILATIONS = (1, 4, 16)
BAND = 128
N_EXPERTS = 32
EXPERTS_PER_GROUP = 8
N_EXPERT_GROUPS = 4
EXPERT_FF = 256
RMS_EPS = 1e-6
MASKED = -1e30

Q_OFF, K_OFF, V_OFF = 0, ATTN_W, 2 * ATTN_W
BG_OFF = 3 * ATTN_W
CG_OFF = BG_OFF + CONV_W
UI_OFF = CG_OFF + CONV_W
GA_OFF = UI_OFF + CONV_W
GB_OFF = GA_OFF + D_MODEL
IN_COLS = GB_OFF + D_MODEL

LANES = 128
ROUTER_LANES = 128
EXPERT_LANE0 = N_EXPERT_GROUPS

TM = 512
ATT_TILE = 2048
BACK_PARTS = 2
ATTN_PAIR = 4
EXPERT_BUFS = 3
EXPERT_TILE = 512
RT_W1, RT_W2, RT_E1, RT_E2, RT_R1, RT_R2 = 0, 1, 2, 3, 4, 5
RT_ROWS = 8
ROW_SUB = D_MODEL // LANES
DMA_CHUNK = 8
VMEM_LIMIT = 56 * 1024 * 1024


def _sigmoid(x):
    return 1.0 / (1.0 + jnp.exp(-x))


def _rmsnorm(x, g):
    return x * lax.rsqrt(jnp.mean(x * x, axis=-1, keepdims=True) + RMS_EPS) * g


def _alibi_slope(g, h):
    return 2.0 ** (-8.0 * (g * HEADS + h + 1) / (N_GROUPS * HEADS))


def _resident(shape):
    nd = len(shape)
    return pl.BlockSpec(shape, lambda *_: (0,) * nd, pipeline_mode=pl.Buffered(1))


def _mod_body(c_ref, w_ref, b_ref, o_ref):
    c = c_ref[...]
    s = c * _sigmoid(c)
    o_ref[...] = jnp.dot(s, w_ref[...], precision=HIGHEST, preferred_element_type=F32) + b_ref[...]


def _modulation(c_all, w_ada, b_ada):
    rows = c_all.shape[0]
    tn = 1024
    return pl.pallas_call(
        _mod_body,
        grid=(6 * D_MODEL // tn,),
        in_specs=[pl.BlockSpec((rows, D_MODEL), lambda j: (0, 0)),
                  pl.BlockSpec((D_MODEL, tn), lambda j: (0, j)),
                  pl.BlockSpec((1, tn), lambda j: (0, j))],
        out_specs=pl.BlockSpec((rows, tn), lambda j: (0, j)),
        out_shape=jax.ShapeDtypeStruct((rows, 6 * D_MODEL), F32),
        compiler_params=pltpu.CompilerParams(dimension_semantics=("arbitrary",), vmem_limit_bytes=VMEM_LIMIT),
        name="modulation",
    )(c_all, w_ada, b_ada.reshape(1, -1))


def _front_body(tiles_per_seq, x_ref, mod_ref, g_ref, w_ref, cw_ref,
                a0_ref, a1_ref, a2_ref, oconv_ref, sga_ref, sgb_ref,
                kv0_ref, kv1_ref, kv2_ref, pconv_ref, res_ref, uprev_ref):
    t_in_seq = pl.program_id(0) % tiles_per_seq
    x = x_ref[...]
    shift1 = mod_ref[:, 0:D_MODEL]
    scale1 = mod_ref[:, D_MODEL:2 * D_MODEL]
    h = (_rmsnorm(x, g_ref[...]) * (1.0 + scale1) + shift1).astype(BF16)

    def proj(c0, n):
        return jnp.dot(h, w_ref[:, c0:c0 + n], preferred_element_type=F32)

    a_refs = (a0_ref, a1_ref, a2_ref)
    kv_refs = (kv0_ref, kv1_ref, kv2_ref)

    for g in range(N_GROUPS):
        d = DILATIONS[g]
        n = TM // d
        for part, base in enumerate((Q_OFF, K_OFF, V_OFF)):
            r = proj(base + g * GROUP_W, GROUP_W)
            cols = slice(part * GROUP_W, (part + 1) * GROUP_W)
            if part == 0:
                r = r * (HEAD_DIM ** -0.5)
            else:
                kvc = slice((part - 1) * GROUP_W, part * GROUP_W)
                kv_refs[g][kvc, :] = (r[TM - BAND:, :] if g == 0 else r).T
            if g == 0:
                a_refs[g][:, cols] = r.astype(BF16)
            else:
                for c in range(GROUP_W // LANES):
                    res_ref[c] = r[:, c * LANES:(c + 1) * LANES]
                for rr in range(d):
                    for c in range(GROUP_W // LANES):
                        c0 = part * GROUP_W + c * LANES
                        a_refs[g][0, rr, :, c0:c0 + LANES] = res_ref[c, pl.ds(rr, n, stride=d), :].astype(BF16)

    bg = proj(BG_OFF, CONV_W)
    u = proj(CG_OFF, CONV_W) * proj(UI_OFF, CONV_W)
    tail = u[TM - 8:, :]
    pconv_ref[...] = tail
    prev = jnp.where(t_in_seq == 0, 0.0, uprev_ref[...])
    row = lax.broadcasted_iota(jnp.int32, (TM, 1), 0)
    u1 = jnp.where(row == 0, prev[7:8, :], pltpu.roll(u, 1, axis=0))
    u2 = jnp.where(row == 0, prev[6:7, :], jnp.where(row == 1, prev[7:8, :], pltpu.roll(u, 2, axis=0)))
    yconv = cw_ref[0:1, :] * u2 + cw_ref[1:2, :] * u1 + cw_ref[2:3, :] * u
    oconv_ref[...] = (bg * yconv).astype(BF16)
    uprev_ref[...] = tail

    sga_ref[...] = _sigmoid(proj(GA_OFF, D_MODEL)).astype(BF16)
    sgb_ref[...] = _sigmoid(proj(GB_OFF, D_MODEL)).astype(BF16)


def _front(x2d, mod_p, g_mix, w_in_bf16, conv_w, batch, seq):
    n_tok = x2d.shape[0]
    n_tiles = n_tok // TM
    tps = seq // TM
    kv2_blocks = ATT_TILE // TM
    out_shape = (
        jax.ShapeDtypeStruct((n_tok, ATTN_W), BF16),
        jax.ShapeDtypeStruct((n_tiles, 4, TM // 4, ATTN_W), BF16),
        jax.ShapeDtypeStruct((n_tiles, 16, TM // 16, ATTN_W), BF16),
        jax.ShapeDtypeStruct((n_tok, CONV_W), BF16),
        jax.ShapeDtypeStruct((n_tok, D_MODEL), BF16),
        jax.ShapeDtypeStruct((n_tok, D_MODEL), BF16),
        jax.ShapeDtypeStruct((batch, 2 * GROUP_W, 128), F32),
        jax.ShapeDtypeStruct((batch, 2 * GROUP_W, 512), F32),
        jax.ShapeDtypeStruct((batch, 2 * GROUP_W, 2048), F32),
        jax.ShapeDtypeStruct((batch, 8, CONV_W), F32),
    )
    out_specs = (
        pl.BlockSpec((TM, ATTN_W), lambda i: (i, 0)),
        pl.BlockSpec((1, 4, TM // 4, ATTN_W), lambda i: (i, 0, 0, 0)),
        pl.BlockSpec((1, 16, TM // 16, ATTN_W), lambda i: (i, 0, 0, 0)),
        pl.BlockSpec((TM, CONV_W), lambda i: (i, 0)),
        pl.BlockSpec((TM, D_MODEL), lambda i: (i, 0)),
        pl.BlockSpec((TM, D_MODEL), lambda i: (i, 0)),
        pl.BlockSpec((None, 2 * GROUP_W, 128), lambda i: (i // tps, 0, 0)),
        pl.BlockSpec((None, 2 * GROUP_W, TM), lambda i: (i // tps, 0, 0)),
        pl.BlockSpec((None, 2 * GROUP_W, TM),
                     lambda i: (i // tps, 0, jnp.maximum(i % tps - (tps - kv2_blocks), 0))),
        pl.BlockSpec((None, 8, CONV_W), lambda i: (i // tps, 0, 0)),
    )
    in_specs = [
        pl.BlockSpec((TM, D_MODEL), lambda i: (i, 0)),
        pl.BlockSpec((None, 1, 6 * D_MODEL), lambda i: (i // tps, 0, 0)),
        _resident((1, D_MODEL)),
        _resident((D_MODEL, IN_COLS)),
        _resident((3, CONV_W)),
    ]
    return pl.pallas_call(
        functools.partial(_front_body, tps),
        grid=(n_tiles,),
        in_specs=in_specs,
        out_specs=out_specs,
        out_shape=out_shape,
        scratch_shapes=[pltpu.VMEM((GROUP_W // LANES, TM, LANES), F32), pltpu.VMEM((8, CONV_W), F32)],
        compiler_params=pltpu.CompilerParams(dimension_semantics=("arbitrary",), vmem_limit_bytes=VMEM_LIMIT),
        name="prompt_front",
    )(x2d, mod_p, g_mix.reshape(1, -1), w_in_bf16, conv_w)


def _band_bias_table():
    qi = jnp.arange(BAND)[:, None]
    kc = jnp.arange(2 * BAND)[None, :]
    delta = qi - (kc - BAND)
    valid = (delta >= 0) & (delta <= BAND)
    tabs = []
    for first in (False, True):
        ok = valid & (kc >= BAND) if first else valid
        for g in range(N_GROUPS):
            for h in range(HEADS):
                b = -_alibi_slope(g, h) * (delta * DILATIONS[g]).astype(F32)
                tabs.append(jnp.where(ok, b, MASKED))
    return jnp.stack(tabs).astype(F32)


def _head_lane_mask(h):
    lane = lax.broadcasted_iota(jnp.int32, (1, GROUP_W), 1)
    return (lane >= h * HEAD_DIM) & (lane < (h + 1) * HEAD_DIM)


def _attn_scores(q, k):
    qs = jnp.concatenate([jnp.where(_head_lane_mask(h), q, jnp.zeros_like(q)) for h in range(HEADS)], axis=0)
    return lax.dot_general(qs, k, (((1,), (1,)), ((), ())), preferred_element_type=F32)


def _attn_probs(s, bias_ref, bias_base):
    s = s + bias_ref[pl.ds(bias_base, HEADS)].reshape(HEADS * BAND, 2 * BAND)
    m = jnp.max(s, axis=-1, keepdims=True)
    e = jnp.exp(s - m)
    l = jnp.sum(e, axis=-1, keepdims=True)
    return (e * (1.0 / l)).astype(BF16), m + jnp.log(l)


def _attn_outputs(p, lse, v):
    ov = jnp.dot(p, v, preferred_element_type=F32)
    o = jnp.zeros((BAND, GROUP_W), F32)
    lse_b = jnp.zeros((BAND, GROUP_W), F32)
    for h in range(HEADS):
        hm = _head_lane_mask(h)
        rows = slice(h * BAND, (h + 1) * BAND)
        o = o + jnp.where(hm, ov[rows], 0.0)
        lse_b = lse_b + jnp.where(hm, lse[rows], 0.0)
    return o, lse_b


def _attn_body(a0c_ref, a0p_ref, a1c_ref, a1p_ref, a2c_ref, a2p_ref, bias_ref, o_ref, og_ref, lg_ref):
    first_tile = pl.program_id(1) == 0
    first_off = jnp.where(first_tile, N_GROUPS * HEADS, 0)
    qs, ks, vs = (slice(0, GROUP_W), slice(GROUP_W, 2 * GROUP_W), slice(2 * GROUP_W, 3 * GROUP_W))

    def run_blocks(blocks):
        loaded = [load() for _, _, _, load in blocks]
        scores = [_attn_scores(q, k) for q, k, _ in loaded]
        probs = [_attn_probs(s, bias_ref, base) for s, (_, _, base, _) in zip(scores, blocks)]
        results = [_attn_outputs(p, lse, v) for (p, lse), (_, _, v) in zip(probs, loaded)]
        for (g, rows, _, _), (o, lse_b) in zip(blocks, results):
            for c in range(GROUP_W // LANES):
                og_ref[g, c, rows, :] = o[:, c * LANES:(c + 1) * LANES]
                lg_ref[g, c, rows, :] = lse_b[:, c * LANES:(c + 1) * LANES]

    def g0_block(n0, base):
        def load():
            q = a0c_ref[pl.ds(n0, BAND), qs]
            if isinstance(n0, int) and n0 == 0:
                k = jnp.concatenate([a0p_ref[:, ks], a0c_ref[0:BAND, ks]], axis=0)
                v = jnp.concatenate([a0p_ref[:, vs], a0c_ref[0:BAND, vs]], axis=0)
            else:
                k = a0c_ref[pl.ds(n0 - BAND, 2 * BAND), ks]
                v = a0c_ref[pl.ds(n0 - BAND, 2 * BAND), vs]
            return q, k, v
        return (0, pl.ds(n0, BAND), base, load)

    run_blocks([g0_block(u * BAND, first_off if u == 0 else 0) for u in range(ATTN_PAIR)])

    def g0_loop(t, carry):
        n0 = pl.multiple_of(t * (ATTN_PAIR * BAND), BAND)
        run_blocks([g0_block(n0 + u * BAND, 0) for u in range(ATTN_PAIR)])
        return carry

    lax.fori_loop(1, ATT_TILE // (ATTN_PAIR * BAND), g0_loop, 0)

    def g1_block(jj, r, prev_ref, prev_jj, base):
        def load():
            q = a1c_ref[jj, r, :, qs]
            k = jnp.concatenate([prev_ref[prev_jj, r, :, ks], a1c_ref[jj, r, :, ks]], axis=0)
            v = jnp.concatenate([prev_ref[prev_jj, r, :, vs], a1c_ref[jj, r, :, vs]], axis=0)
            return q, k, v
        return (1, pl.ds(jj * (4 * BAND) + r, BAND, stride=4), base, load)

    def g1_first(t, carry):
        run_blocks([g1_block(0, t * ATTN_PAIR + u, a1p_ref, 0, first_off + HEADS) for u in range(ATTN_PAIR)])
        return carry

    lax.fori_loop(0, 4 // ATTN_PAIR, g1_first, 0)

    def g1_rest(t, carry):
        jj = 1 + (t * ATTN_PAIR) // 4
        r0 = (t * ATTN_PAIR) % 4
        run_blocks([g1_block(jj, r0 + u, a1c_ref, jj - 1, HEADS) for u in range(ATTN_PAIR)])
        return carry

    lax.fori_loop(0, 12 // ATTN_PAIR, g1_rest, 0)

    n_sub = a2c_ref.shape[0]

    def g2_rows(ref, r, cols):
        return jnp.concatenate([ref[t, r, :, cols] for t in range(n_sub)], axis=0)

    def g2_block(r):
        def load():
            q = g2_rows(a2c_ref, r, qs)
            k = jnp.concatenate([g2_rows(a2p_ref, r, ks), g2_rows(a2c_ref, r, ks)], axis=0)
            v = jnp.concatenate([g2_rows(a2p_ref, r, vs), g2_rows(a2c_ref, r, vs)], axis=0)
            return q, k, v
        return (2, pl.ds(r, BAND, stride=16), first_off + 2 * HEADS, load)

    def g2_loop(t, carry):
        run_blocks([g2_block(t * ATTN_PAIR + u) for u in range(ATTN_PAIR)])
        return carry

    lax.fori_loop(0, 16 // ATTN_PAIR, g2_loop, 0)

    def mix(c, carry):
        rows = pl.ds(pl.multiple_of(c * BAND, BAND), BAND)
        for half in range(GROUP_W // LANES):
            l0, l1, l2 = lg_ref[0, half, rows, :], lg_ref[1, half, rows, :], lg_ref[2, half, rows, :]
            mx = jnp.maximum(jnp.maximum(l0, l1), l2)
            w0, w1, w2 = jnp.exp(l0 - mx), jnp.exp(l1 - mx), jnp.exp(l2 - mx)
            num = w0 * og_ref[0, half, rows, :] + w1 * og_ref[1, half, rows, :] + w2 * og_ref[2, half, rows, :]
            o_ref[rows, half * LANES:(half + 1) * LANES] = (num / (w0 + w1 + w2)).astype(o_ref.dtype)
        return carry

    lax.fori_loop(0, ATT_TILE // BAND, mix, 0)


def _attention(a0, a1, a2, bias, batch, seq):
    steps = seq // ATT_TILE
    sub = ATT_TILE // TM
    a0 = a0.reshape(batch, seq, ATTN_W)
    a1 = a1.reshape(batch, seq // TM, 4, TM // 4, ATTN_W)
    a2 = a2.reshape(batch, seq // TM, 16, TM // 16, ATTN_W)
    in_specs = [
        pl.BlockSpec((None, ATT_TILE, ATTN_W), lambda b, j: (b, j, 0)),
        pl.BlockSpec((None, BAND, ATTN_W), lambda b, j: (b, jnp.maximum(j * (ATT_TILE // BAND) - 1, 0), 0)),
        pl.BlockSpec((None, sub, 4, TM // 4, ATTN_W), lambda b, j: (b, j, 0, 0, 0)),
        pl.BlockSpec((None, 1, 4, TM // 4, ATTN_W), lambda b, j: (b, jnp.maximum(j * sub - 1, 0), 0, 0, 0)),
        pl.BlockSpec((None, sub, 16, TM // 16, ATTN_W), lambda b, j: (b, j, 0, 0, 0)),
        pl.BlockSpec((None, sub, 16, TM // 16, ATTN_W), lambda b, j: (b, jnp.maximum(j - 1, 0), 0, 0, 0)),
        _resident(bias.shape),
    ]
    return pl.pallas_call(
        _attn_body,
        grid=(batch, steps),
        in_specs=in_specs,
        out_specs=pl.BlockSpec((None, ATT_TILE, GROUP_W), lambda b, j: (b, j, 0)),
        out_shape=jax.ShapeDtypeStruct((batch, seq, GROUP_W), BF16),
        scratch_shapes=[pltpu.VMEM((N_GROUPS, GROUP_W // LANES, ATT_TILE, LANES), F32)] * 2,
        compiler_params=pltpu.CompilerParams(dimension_semantics=("arbitrary", "arbitrary"),
                                             vmem_limit_bytes=VMEM_LIMIT),
        name="prompt_attention",
    )(a0, a0, a1, a1, a2, a2, bias)


def _route(logits):
    lane = lax.broadcasted_iota(jnp.int32, logits.shape, 1)
    big = jnp.int32(1 << 20)
    gmask = lane < N_EXPERT_GROUPS
    lg = jnp.where(gmask, logits, MASKED)
    gmax = jnp.max(lg, axis=-1, keepdims=True)
    gidx = jnp.min(jnp.where(gmask & (lg == gmax), lane, big), axis=-1, keepdims=True)
    p_top = 1.0 / jnp.sum(jnp.where(gmask, jnp.exp(lg - gmax), 0.0), axis=-1, keepdims=True)
    lo = EXPERT_LANE0 + EXPERTS_PER_GROUP * gidx
    emask = (lane >= lo) & (lane < lo + EXPERTS_PER_GROUP)
    le = jnp.where(emask, logits, MASKED)
    v1 = jnp.max(le, axis=-1, keepdims=True)
    i1 = jnp.min(jnp.where(emask & (le == v1), lane, big), axis=-1, keepdims=True)
    emask2 = emask & (lane != i1)
    le2 = jnp.where(emask2, logits, MASKED)
    v2 = jnp.max(le2, axis=-1, keepdims=True)
    i2 = jnp.min(jnp.where(emask2 & (le2 == v2), lane, big), axis=-1, keepdims=True)
    e2 = jnp.exp(v2 - v1)
    den = 1.0 + e2
    w1 = (1.0 / den) * p_top
    w2 = (e2 / den) * p_top
    id1 = (i1 - EXPERT_LANE0).astype(F32)
    id2 = (i2 - EXPERT_LANE0).astype(F32)
    return jnp.where(lane == RT_W1, w1, jnp.where(lane == RT_W2, w2,
                     jnp.where(lane == RT_E1, id1, jnp.where(lane == RT_E2, id2, 0.0))))


def _with_ranks(rt, base, count=1.0):
    n = rt.shape[0]
    lane = lax.broadcasted_iota(jnp.int32, rt.shape, 1)
    lane_f = lane.astype(F32)
    oh1 = jnp.where(lane_f == rt[:, RT_E1:RT_E1 + 1], count, 0.0)
    oh2 = jnp.where(lane_f == rt[:, RT_E2:RT_E2 + 1], count, 0.0)
    before = jnp.where(lax.broadcasted_iota(jnp.int32, (n, n), 1) < lax.broadcasted_iota(jnp.int32, (n, n), 0),
                       1.0, 0.0).astype(BF16)
    p1 = jnp.dot(before, oh1.astype(BF16), preferred_element_type=F32)
    p2 = jnp.dot(before, oh2.astype(BF16), preferred_element_type=F32)
    c1 = jnp.sum(oh1, axis=0, keepdims=True)
    c2 = jnp.sum(oh2, axis=0, keepdims=True)
    rank1 = jnp.sum(oh1 * (base + p1), axis=-1, keepdims=True)
    rank2 = jnp.sum(oh2 * (base + c1 + p2), axis=-1, keepdims=True)
    return jnp.where(lane == RT_R1, rank1, jnp.where(lane == RT_R2, rank2, rt)), base + c1 + c2


def _back_body(oa_ref, oc_ref, sga_ref, sgb_ref, x_ref, mod_ref, wpa_ref, wpb_ref, wo_ref, g_ref,
               wrh_ref, wrl_ref, br_ref, x1_ref, h2_ref, rt_ref, rtt_ref, cnt_ref, base_ref, logit_ref):
    step = pl.program_id(0)

    @pl.when(step == 0)
    def _():
        base_ref[...] = jnp.zeros_like(base_ref)
        logit_ref[...] = jnp.zeros_like(logit_ref)

    routed_prev = _route(logit_ref[...])
    gate1 = mod_ref[:, 2 * D_MODEL:3 * D_MODEL]
    shift2 = mod_ref[:, 3 * D_MODEL:4 * D_MODEL]
    scale2 = mod_ref[:, 4 * D_MODEL:5 * D_MODEL]
    n_part = TM // BACK_PARTS
    parts = [pl.ds(j * n_part, n_part) for j in range(BACK_PARTS)]
    dot = functools.partial(jnp.dot, preferred_element_type=F32)
    pab = [(dot(oa_ref[r, :], wpa_ref[...]), dot(oc_ref[r, :], wpb_ref[...])) for r in parts]
    merged = [(sga_ref[r, :].astype(F32) * pa + sgb_ref[r, :].astype(F32) * pb).astype(BF16)
              for r, (pa, pb) in zip(parts, pab)]
    delta = [dot(m, wo_ref[...]) for m in merged]
    x1 = [x_ref[r, :] + gate1 * d for r, d in zip(parts, delta)]
    h2 = [_rmsnorm(x, g_ref[...]) * (1.0 + scale2) + shift2 for x in x1]
    hi = [h.astype(BF16) for h in h2]
    lo = [(h - b.astype(F32)).astype(BF16) for h, b in zip(h2, hi)]
    logits = [dot(b, wrh_ref[...]) + dot(l, wrh_ref[...]) + dot(b, wrl_ref[...]) + br_ref[...]
              for b, l in zip(hi, lo)]
    for r, x, b, lg in zip(parts, x1, hi, logits):
        x1_ref[r, :] = x
        h2_ref[r, :] = b
        logit_ref[r, :] = lg
    rt, total = _with_ranks(routed_prev, base_ref[...], jnp.where(step > 0, 1.0, 0.0))
    rt_ref[...] = rt
    rtt_ref[...] = rt.T[:RT_ROWS, :]
    base_ref[...] = total
    cnt_ref[...] = total


def _back(o_attn, oconv, sga, sgb, x2d, mod_p, wpa, wpb, wo, g_ffn, wr_hi, wr_lo, b_r, seq):
    n_tok = x2d.shape[0]
    n_all = n_tok + TM
    tps = seq // TM
    last = n_tok // TM - 1
    tile = lambda i: jnp.minimum(i, last)
    routed = lambda i: jnp.maximum(i - 1, 0)
    row = lambda w: pl.BlockSpec((TM, w), lambda i: (tile(i), 0))
    return pl.pallas_call(
        _back_body,
        grid=(last + 2,),
        in_specs=[row(GROUP_W), row(CONV_W), row(D_MODEL), row(D_MODEL), row(D_MODEL),
                  pl.BlockSpec((None, 1, 6 * D_MODEL), lambda i: (tile(i) // tps, 0, 0)),
                  _resident(wpa.shape), _resident(wpb.shape), _resident(wo.shape), _resident((1, D_MODEL)),
                  _resident(wr_hi.shape), _resident(wr_lo.shape), _resident(b_r.shape)],
        out_specs=(row(D_MODEL), row(D_MODEL),
                   pl.BlockSpec((TM, ROUTER_LANES), lambda i: (routed(i), 0)),
                   pl.BlockSpec((RT_ROWS, TM), lambda i: (0, routed(i))),
                   pl.BlockSpec((1, LANES), lambda i: (0, 0))),
        out_shape=(jax.ShapeDtypeStruct((n_all, D_MODEL), F32),
                   jax.ShapeDtypeStruct((n_all, D_MODEL), BF16),
                   jax.ShapeDtypeStruct((n_all, ROUTER_LANES), F32),
                   jax.ShapeDtypeStruct((RT_ROWS, n_all), F32),
                   jax.ShapeDtypeStruct((1, LANES), F32)),
        scratch_shapes=[pltpu.VMEM((1, LANES), F32), pltpu.VMEM((TM, ROUTER_LANES), F32)],
        compiler_params=pltpu.CompilerParams(dimension_semantics=("arbitrary",), vmem_limit_bytes=VMEM_LIMIT),
        name="prompt_back",
    )(o_attn, oconv, sga, sgb, x2d, mod_p, wpa, wpb, wo, g_ffn.reshape(1, -1), wr_hi, wr_lo, b_r)


def _valid_chunks(tile, n_prompt_tiles, n_sample):
    return jnp.where(tile < n_prompt_tiles, TM // DMA_CHUNK, n_sample // DMA_CHUNK)


def _dispatch_body(n_prompt_tiles, n_sample, dest_ref, tail_ref, h_ref, xs_ref, stage_ref, zero_ref, sem):
    i = pl.program_id(0)
    n_all = (n_prompt_tiles + 1) * TM

    def tail_copy(e):
        start = pl.multiple_of(tail_ref[e] * ROW_SUB, ROW_SUB)
        return pltpu.make_async_copy(zero_ref, xs_ref.at[pl.ds(start, EXPERT_TILE * ROW_SUB)], sem)

    @pl.when(i == 0)
    def _():
        zero_ref[...] = jnp.zeros_like(zero_ref)
        for e in range(N_EXPERTS):
            @pl.when(tail_ref[e] >= 0)
            def _():
                tail_copy(e).start()
        for e in range(N_EXPERTS):
            @pl.when(tail_ref[e] >= 0)
            def _():
                tail_copy(e).wait()

    for j in range(ROW_SUB):
        stage_ref[pl.ds(j, TM, stride=ROW_SUB), :] = h_ref[:, j * LANES:(j + 1) * LANES].astype(F32)
    n_chunks = _valid_chunks(i, n_prompt_tiles, n_sample)

    def issue(c, carry):
        for j in range(DMA_CHUNK):
            r = c * DMA_CHUNK + j
            src = stage_ref.at[pl.ds(pl.multiple_of(r * ROW_SUB, ROW_SUB), ROW_SUB)]
            for k in range(2):
                d = pl.multiple_of(dest_ref[k * n_all + i * TM + r], ROW_SUB)
                pltpu.make_async_copy(src, xs_ref.at[pl.ds(d, ROW_SUB)], sem).start(priority=k)
        return carry

    lax.fori_loop(0, n_chunks, issue, 0)

    def drain(c, carry):
        n = 2 * DMA_CHUNK * ROW_SUB
        pltpu.make_async_copy(stage_ref.at[pl.ds(0, n)], xs_ref.at[pl.ds(0, n)], sem).wait()
        return carry

    lax.fori_loop(0, n_chunks, drain, 0)


def _dispatch(dest, tail_start, h2, n_rows, n_prompt_tiles, n_sample):
    n_all = h2.shape[0]
    return pl.pallas_call(
        functools.partial(_dispatch_body, n_prompt_tiles, n_sample),
        grid_spec=pltpu.PrefetchScalarGridSpec(
            num_scalar_prefetch=2,
            grid=(n_all // TM,),
            in_specs=[pl.BlockSpec((TM, D_MODEL), lambda i, *_: (i, 0))],
            out_specs=pl.BlockSpec(memory_space=pl.ANY),
            scratch_shapes=[pltpu.VMEM((TM * ROW_SUB, LANES), F32), pltpu.VMEM((EXPERT_TILE * ROW_SUB, LANES), F32),
                            pltpu.SemaphoreType.DMA],
        ),
        out_shape=jax.ShapeDtypeStruct((n_rows * ROW_SUB, LANES), F32),
        compiler_params=pltpu.CompilerParams(dimension_semantics=("arbitrary",), vmem_limit_bytes=VMEM_LIMIT,
                                             disable_bounds_checks=True),
        name="moe_dispatch",
    )(dest, tail_start, h2)


def _experts_body(te_ref, nt_ref, xs_ref, wg_ref, wu_ref, wd_ref, o_ref, buf_ref, sem):
    step = pl.program_id(0)
    n_tiles = nt_ref[0]
    tile_rows = EXPERT_TILE * ROW_SUB

    def tile_copy(t):
        src = xs_ref.at[pl.ds(pl.multiple_of(t * tile_rows, tile_rows), tile_rows)]
        return pltpu.make_async_copy(src, buf_ref.at[t % EXPERT_BUFS], sem.at[t % EXPERT_BUFS])

    @pl.when(step == 0)
    def _():
        for t in range(EXPERT_BUFS - 1):
            @pl.when(t < n_tiles)
            def _():
                tile_copy(t).start()

    @pl.when(step + (EXPERT_BUFS - 1) < n_tiles)
    def _():
        tile_copy(step + (EXPERT_BUFS - 1)).start()

    @pl.when(step < n_tiles)
    def _():
        tile_copy(step).wait()
        slot = step % EXPERT_BUFS
        x = jnp.concatenate([buf_ref[slot, pl.ds(j, EXPERT_TILE, stride=ROW_SUB), :].astype(BF16)
                             for j in range(ROW_SUB)], axis=1)
        a = jnp.dot(x, wg_ref[0].astype(BF16), preferred_element_type=F32)
        z = (a * _sigmoid(a)) * jnp.dot(x, wu_ref[0].astype(BF16), preferred_element_type=F32)
        o = jnp.dot(z.astype(BF16), wd_ref[0].astype(BF16), preferred_element_type=F32)
        for j in range(ROW_SUB):
            o_ref[pl.ds(j, EXPERT_TILE, stride=ROW_SUB), :] = o[:, j * LANES:(j + 1) * LANES]


def _experts(tile_expert, n_tiles, xs, w_gate_e, w_up_e, w_down_e):
    max_tiles = tile_expert.shape[0]
    rows = lambda s, te, nt: (jnp.minimum(s, nt[0] - 1), 0)
    row_tile = pl.BlockSpec((EXPERT_TILE * ROW_SUB, LANES), rows)
    weight = lambda shape: pl.BlockSpec((1,) + shape, lambda s, te, nt: (te[jnp.minimum(s, nt[0] - 1)], 0, 0))
    return pl.pallas_call(
        _experts_body,
        grid_spec=pltpu.PrefetchScalarGridSpec(
            num_scalar_prefetch=2,
            grid=(max_tiles,),
            in_specs=[pl.BlockSpec(memory_space=pl.ANY),
                      weight((D_MODEL, EXPERT_FF)), weight((D_MODEL, EXPERT_FF)), weight((EXPERT_FF, D_MODEL))],
            out_specs=row_tile,
            scratch_shapes=[pltpu.VMEM((EXPERT_BUFS, EXPERT_TILE * ROW_SUB, LANES), F32),
                            pltpu.SemaphoreType.DMA((EXPERT_BUFS,))],
        ),
        out_shape=jax.ShapeDtypeStruct(xs.shape, F32),
        compiler_params=pltpu.CompilerParams(dimension_semantics=("arbitrary",), vmem_limit_bytes=VMEM_LIMIT),
        name="moe_experts",
    )(tile_expert, n_tiles, xs, w_gate_e, w_up_e, w_down_e)


def _combine_body(n_prompt_tiles, n_sample, dest_ref, x1_ref, rt_ref, gp_ref, gs_ref, gf_ref, eo_ref,
                  yp_ref, ys_ref, rows_ref, sem):
    i = pl.program_id(0)
    slot = i % 2
    n_all = (n_prompt_tiles + 1) * TM

    def fetch(tile):
        buf = tile % 2

        def issue(c, carry):
            for j in range(DMA_CHUNK):
                r = c * DMA_CHUNK + j
                row = pl.ds(pl.multiple_of(r * ROW_SUB, ROW_SUB), ROW_SUB)
                for k in range(2):
                    d = pl.multiple_of(dest_ref[k * n_all + tile * TM + r], ROW_SUB)
                    pltpu.make_async_copy(eo_ref.at[pl.ds(d, ROW_SUB)], rows_ref.at[buf, k, row],
                                          sem.at[buf]).start(priority=k)
            return carry

        lax.fori_loop(0, _valid_chunks(tile, n_prompt_tiles, n_sample), issue, 0)

    @pl.when(i == 0)
    def _():
        fetch(i)

    @pl.when(i + 1 < pl.num_programs(0))
    def _():
        fetch(i + 1)

    def drain(c, carry):
        n = 2 * DMA_CHUNK * ROW_SUB
        pltpu.make_async_copy(eo_ref.at[pl.ds(0, n)], rows_ref.at[slot, 0, pl.ds(0, n)], sem.at[slot]).wait()
        return carry

    lax.fori_loop(0, _valid_chunks(i, n_prompt_tiles, n_sample), drain, 0)

    rt = rt_ref[...]
    w1, w2 = rt[:, RT_W1:RT_W1 + 1], rt[:, RT_W2:RT_W2 + 1]
    lane_tile = lambda k, j: rows_ref[slot, k, pl.ds(j, TM, stride=ROW_SUB), :]
    moe = jnp.concatenate([w1 * lane_tile(0, j) + w2 * lane_tile(1, j) for j in range(ROW_SUB)], axis=1)
    is_prompt = i < n_prompt_tiles
    gate2 = jnp.where(is_prompt, gp_ref[...], gs_ref[...])
    y = _rmsnorm(x1_ref[...] + gate2 * moe, gf_ref[...])

    @pl.when(is_prompt)
    def _():
        yp_ref[...] = y

    @pl.when(jnp.logical_not(is_prompt))
    def _():
        ys_ref[...] = y[:n_sample, :]


def _combine(dest, x1, rt, gate2_p, gate2_s, g_final, eo, n_prompt_tiles, n_sample, tiles_per_seq):
    n_all = x1.shape[0]
    last_p = n_prompt_tiles - 1
    return pl.pallas_call(
        functools.partial(_combine_body, n_prompt_tiles, n_sample),
        grid_spec=pltpu.PrefetchScalarGridSpec(
            num_scalar_prefetch=1,
            grid=(n_all // TM,),
            in_specs=[pl.BlockSpec((TM, D_MODEL), lambda i, *_: (i, 0)),
                      pl.BlockSpec((TM, ROUTER_LANES), lambda i, *_: (i, 0)),
                      pl.BlockSpec((None, 1, D_MODEL), lambda i, *_: (jnp.minimum(i, last_p) // tiles_per_seq, 0, 0)),
                      pl.BlockSpec((TM, D_MODEL), lambda i, *_: (0, 0)),
                      pl.BlockSpec((1, D_MODEL), lambda i, *_: (0, 0)),
                      pl.BlockSpec(memory_space=pl.ANY)],
            out_specs=(pl.BlockSpec((TM, D_MODEL), lambda i, *_: (jnp.minimum(i, last_p), 0)),
                       pl.BlockSpec((n_sample, D_MODEL), lambda i, *_: (0, 0))),
            scratch_shapes=[pltpu.VMEM((2, 2, TM * ROW_SUB, LANES), F32), pltpu.SemaphoreType.DMA((2,))],
        ),
        out_shape=(jax.ShapeDtypeStruct((n_prompt_tiles * TM, D_MODEL), F32),
                   jax.ShapeDtypeStruct((n_sample, D_MODEL), F32)),
        compiler_params=pltpu.CompilerParams(dimension_semantics=("arbitrary",), vmem_limit_bytes=VMEM_LIMIT,
                                             disable_bounds_checks=True),
        name="moe_combine",
    )(dest, x1, rt, gate2_p, gate2_s, g_final.reshape(1, -1), eo)


def _routing_tables(rtt, counts, max_tiles):
    cnt = counts[0, :N_EXPERTS].astype(jnp.int32)
    padded = (cnt + EXPERT_TILE - 1) // EXPERT_TILE * EXPERT_TILE
    ends = jnp.cumsum(padded)
    starts = ends - padded
    n_tiles = (ends[-1] // EXPERT_TILE).reshape(1)
    tile_row0 = jnp.arange(max_tiles, dtype=jnp.int32) * EXPERT_TILE
    tile_expert = jnp.minimum(jnp.sum(ends[None, :] <= tile_row0[:, None], axis=1), N_EXPERTS - 1).astype(jnp.int32)
    tail_start = jnp.where(cnt > 0, ends - EXPERT_TILE, -1).astype(jnp.int32)
    eid = rtt[RT_E1:RT_E2 + 1, :].astype(jnp.int32)
    onehot = eid[None] == jnp.arange(N_EXPERTS, dtype=jnp.int32)[:, None, None]
    dest = rtt[RT_R1:RT_R2 + 1, :].astype(jnp.int32) + jnp.sum(jnp.where(onehot, starts[:, None, None], 0), axis=0)
    dest = (jnp.where(eid >= 0, dest, 0) * ROW_SUB).reshape(-1)
    return dest, tail_start, tile_expert, n_tiles


def _sample_front_body(x_ref, mod_ref, g_ref, w_ref, o_ref):
    shift1 = mod_ref[:, 0:D_MODEL]
    scale1 = mod_ref[:, D_MODEL:2 * D_MODEL]
    h = _rmsnorm(x_ref[...], g_ref[...]) * (1.0 + scale1) + shift1
    o_ref[...] = jnp.dot(h, w_ref[...], precision=HIGHEST, preferred_element_type=F32)


def _sample_front(x_s, mod_s, g_mix, w_in):
    nb = x_s.shape[0]
    tn = 256
    return pl.pallas_call(
        _sample_front_body,
        grid=(IN_COLS // tn,),
        in_specs=[pl.BlockSpec((nb, D_MODEL), lambda j: (0, 0)),
                  pl.BlockSpec((nb, 2 * D_MODEL), lambda j: (0, 0)),
                  pl.BlockSpec((1, D_MODEL), lambda j: (0, 0)),
                  pl.BlockSpec((D_MODEL, tn), lambda j: (0, j))],
        out_specs=pl.BlockSpec((nb, tn), lambda j: (0, j)),
        out_shape=jax.ShapeDtypeStruct((nb, IN_COLS), F32),
        compiler_params=pltpu.CompilerParams(dimension_semantics=("arbitrary",), vmem_limit_bytes=VMEM_LIMIT),
        name="sample_front",
    )(x_s, mod_s, g_mix.reshape(1, -1), w_in)


def _sample_attn_body(qkv_ref, c0_ref, c1_ref, c2_ref, oa_ref):
    head = lax.broadcasted_iota(jnp.int32, (HEADS, 1, 1), 0)
    for b in range(qkv_ref.shape[0]):
        outs, lses = [], []
        for g, c_ref in enumerate((c0_ref, c1_ref, c2_ref)):
            window = c_ref.shape[-1]
            def new_rows(part):
                c0 = (part * N_GROUPS + g) * HEADS
                return jnp.stack([qkv_ref[b, :, c0 + h:c0 + h + 1] for h in range(HEADS)], axis=0)

            q = new_rows(0) * (HEAD_DIM ** -0.5)
            k_new = new_rows(1)
            v_new = new_rows(2)
            s = jnp.sum(c_ref[b, 0] * q, axis=1, keepdims=True)
            pos = lax.broadcasted_iota(jnp.int32, (1, 1, window), 2)
            slope = jnp.full((HEADS, 1, 1), _alibi_slope(g, HEADS - 1), F32)
            for h in range(HEADS - 1):
                slope = jnp.where(head == h, _alibi_slope(g, h), slope)
            on_band = (pos & (DILATIONS[g] - 1)) == 0
            s = jnp.where(on_band, s - slope * (window - pos).astype(F32), MASKED)
            s_self = jnp.sum(q * k_new, axis=1, keepdims=True)
            m = jnp.maximum(jnp.max(s, axis=2, keepdims=True), s_self)
            e = jnp.exp(s - m)
            e_self = jnp.exp(s_self - m)
            l = jnp.sum(e, axis=2, keepdims=True) + e_self
            o = jnp.sum(c_ref[b, 1] * e, axis=2, keepdims=True) + e_self * v_new
            outs.append(o / l)
            lses.append(m + jnp.log(l))
        mx = jnp.maximum(jnp.maximum(lses[0], lses[1]), lses[2])
        w = [jnp.exp(x - mx) for x in lses]
        mixed = (w[0] * outs[0] + w[1] * outs[1] + w[2] * outs[2]) / (w[0] + w[1] + w[2])
        for h in range(HEADS):
            oa_ref[b, :, h:h + 1] = mixed[h]


def _sample_attention(qkv_s, cache0, cache1, cache2):
    nb = qkv_s.shape[0]
    bb = 2
    hd = (HEADS, HEAD_DIM)
    caches = [jnp.transpose(c, (0, 1, 3, 4, 5, 2)) for c in (cache0, cache1, cache2)]
    cache_spec = lambda c: pl.BlockSpec((None, bb, 2, *hd, c.shape[-1]), lambda i: (0, i, 0, 0, 0, 0))
    return pl.pallas_call(
        _sample_attn_body,
        grid=(nb // bb,),
        in_specs=[pl.BlockSpec((bb,) + qkv_s.shape[1:], lambda i: (i, 0, 0))] + [cache_spec(c) for c in caches],
        out_specs=pl.BlockSpec((bb, HEAD_DIM, HEADS), lambda i: (i, 0, 0)),
        out_shape=jax.ShapeDtypeStruct((nb, HEAD_DIM, HEADS), F32),
        compiler_params=pltpu.CompilerParams(dimension_semantics=("arbitrary",), vmem_limit_bytes=VMEM_LIMIT),
        name="sample_attention",
    )(qkv_s, *caches)


def _sample_back_body(p_ref, oa_ref, cc_ref, x_ref, mod_ref, cw_ref, wpa_ref, wpb_ref, wo_ref, g_ref,
                      wr_ref, br_ref, cntp_ref, x1_any, h2_any, rt_any, rtt_any,
                      x1_ref, h2_ref, rt_ref, rtt_ref, sconv_ref, cnt_ref):
    del x1_any, h2_any, rt_any, rtt_any
    nb = x_ref.shape[0]
    hdot = functools.partial(jnp.dot, precision=HIGHEST, preferred_element_type=F32)
    bg = p_ref[:, BG_OFF:BG_OFF + CONV_W]
    u = p_ref[:, CG_OFF:CG_OFF + CONV_W] * p_ref[:, UI_OFF:UI_OFF + CONV_W]
    c_old, c_new = cc_ref[:, 0, :], cc_ref[:, 1, :]
    yconv = cw_ref[0:1, :] * c_old + cw_ref[1:2, :] * c_new + cw_ref[2:3, :] * u
    sconv_ref[:, 0, :] = c_new
    sconv_ref[:, 1, :] = u
    sga = _sigmoid(p_ref[:, GA_OFF:GA_OFF + D_MODEL])
    sgb = _sigmoid(p_ref[:, GB_OFF:GB_OFF + D_MODEL])
    merged = sga * hdot(oa_ref[...], wpa_ref[...]) + sgb * hdot(bg * yconv, wpb_ref[...])
    gate1 = mod_ref[:, 2 * D_MODEL:3 * D_MODEL]
    shift2 = mod_ref[:, 3 * D_MODEL:4 * D_MODEL]
    scale2 = mod_ref[:, 4 * D_MODEL:5 * D_MODEL]
    x1 = x_ref[...] + gate1 * hdot(merged, wo_ref[...])
    h2 = _rmsnorm(x1, g_ref[...]) * (1.0 + scale2) + shift2
    rt = _route(hdot(h2, wr_ref[...]) + br_ref[...])
    lane = lax.broadcasted_iota(jnp.int32, (TM, ROUTER_LANES), 1)
    x1_ref[...] = jnp.zeros_like(x1_ref)
    h2_ref[...] = jnp.zeros_like(h2_ref)
    rt_ref[...] = jnp.where((lane == RT_E1) | (lane == RT_E2), -1.0, 0.0)
    x1_ref[0:nb, :] = x1
    h2_ref[0:nb, :] = h2.astype(BF16)
    rt_ref[0:nb, :] = rt
    ranked, total = _with_ranks(rt_ref[...], cntp_ref[...])
    rt_ref[...] = ranked
    rtt_ref[...] = ranked.T[:RT_ROWS, :]
    cnt_ref[...] = total


def _sample_back(proj_s, oa_s, cache_conv, x_s, mod_s, conv_w, w_pa, w_pb, w_o, g_ffn, w_r, b_r, cnt_p,
                 x1, h2, rt, rtt):
    nb = x_s.shape[0]
    last = x1.shape[0] // TM - 1
    args = (proj_s, oa_s, cache_conv, x_s, mod_s, conv_w, w_pa, w_pb, w_o, g_ffn.reshape(1, -1), w_r, b_r, cnt_p)
    full = lambda shape: pl.BlockSpec(shape, lambda i: (0,) * len(shape))
    last_tile = lambda a: pl.BlockSpec((TM, a.shape[1]), lambda i: (last, 0))
    shape_of = lambda a: jax.ShapeDtypeStruct(a.shape, a.dtype)
    return pl.pallas_call(
        _sample_back_body,
        grid=(1,),
        in_specs=[full(a.shape) for a in args] + [pl.BlockSpec(memory_space=pl.ANY)] * 4,
        out_specs=(last_tile(x1), last_tile(h2), last_tile(rt), pl.BlockSpec((RT_ROWS, TM), lambda i: (0, last)),
                   full((nb, 2, CONV_W)), full((1, LANES))),
        out_shape=(shape_of(x1), shape_of(h2), shape_of(rt), shape_of(rtt),
                   jax.ShapeDtypeStruct((nb, 2, CONV_W), F32), jax.ShapeDtypeStruct((1, LANES), F32)),
        input_output_aliases={len(args) + j: j for j in range(4)},
        compiler_params=pltpu.CompilerParams(dimension_semantics=("arbitrary",), vmem_limit_bytes=VMEM_LIMIT),
        name="sample_back",
    )(*args, x1, h2, rt, rtt)


def kernel(x_prompt, x_sample, cache_kv_w128, cache_kv_w512, cache_kv_w2048, cache_conv, c_prompt, c_sample,
           g_mix, w_ada, b_ada, w_in, conv_w, w_pa, w_pb, w_o, g_ffn, w_router_group, b_router_group,
           w_router_expert, b_router_expert, w_gate_e, w_up_e, w_down_e, g_final):
    batch, seq, _ = x_prompt.shape
    nb = x_sample.shape[0]
    assert x_sample.shape[1] == 1 and g_mix.shape[0] == 1, "one layer, one new sample token per sequence"
    assert seq % ATT_TILE == 0 and cache_kv_w128.shape[2] == 128 and cache_kv_w512.shape[2] == 512 \
        and cache_kv_w2048.shape[2] == 2048
    (g_mix, w_ada, b_ada, w_in, conv_w, w_pa, w_pb, w_o, g_ffn, w_rg, b_rg, w_re, b_re, w_gate_e, w_up_e,
     w_down_e) = (a[0] for a in (g_mix, w_ada, b_ada, w_in, conv_w, w_pa, w_pb, w_o, g_ffn, w_router_group,
                                 b_router_group, w_router_expert, b_router_expert, w_gate_e, w_up_e, w_down_e))

    c_all = jnp.concatenate([c_prompt, jnp.zeros((8 - batch, D_MODEL), F32), c_sample], axis=0)
    mod = _modulation(c_all, w_ada, b_ada)
    mod_p = mod[:batch].reshape(batch, 1, 6 * D_MODEL)
    mod_s = mod[8:]

    w_r = jnp.zeros((D_MODEL, ROUTER_LANES), F32)
    w_r = w_r.at[:, :N_EXPERT_GROUPS].set(w_rg).at[:, EXPERT_LANE0:EXPERT_LANE0 + N_EXPERTS].set(w_re)
    b_r = jnp.full((1, ROUTER_LANES), MASKED, F32)
    b_r = b_r.at[0, :N_EXPERT_GROUPS].set(b_rg).at[0, EXPERT_LANE0:EXPERT_LANE0 + N_EXPERTS].set(b_re)
    wr_hi = w_r.astype(BF16)
    wr_lo = (w_r - wr_hi.astype(F32)).astype(BF16)

    x2d = x_prompt.reshape(batch * seq, D_MODEL)
    (a0, a1, a2, oconv, sga, sgb, kv0, kv1, kv2, pconv) = _front(
        x2d, mod_p, g_mix, w_in.astype(BF16), conv_w, batch, seq)
    o_attn = _attention(a0, a1, a2, _band_bias_table(), batch, seq).reshape(batch * seq, GROUP_W)
    x1, h2, rt, rtt, cnt_p = _back(o_attn, oconv, sga, sgb, x2d, mod_p, w_pa.astype(BF16), w_pb.astype(BF16),
                              w_o.astype(BF16), g_ffn, wr_hi, wr_lo, b_r, seq)

    x_s = x_sample.reshape(nb, D_MODEL)
    proj_s = _sample_front(x_s, mod_s[:, :2 * D_MODEL], g_mix, w_in)
    qkv_s = proj_s[:, :3 * ATTN_W].reshape(nb, 3, N_GROUPS, HEADS, HEAD_DIM)
    qkv_t = jnp.transpose(qkv_s.reshape(nb, 3 * N_GROUPS * HEADS, HEAD_DIM), (0, 2, 1))
    oa_s = _sample_attention(qkv_t, cache_kv_w128, cache_kv_w512, cache_kv_w2048)
    oa_s = jnp.transpose(oa_s, (0, 2, 1)).reshape(nb, GROUP_W)
    skv0, skv1, skv2 = (jnp.stack([qkv_s[:, 1, g], qkv_s[:, 2, g]], axis=1) for g in range(N_GROUPS))
    x1, h2, rt, rtt, sconv, counts = _sample_back(proj_s, oa_s, cache_conv[0], x_s, mod_s, conv_w,
                                                  w_pa, w_pb, w_o, g_ffn, w_r, b_r, cnt_p, x1, h2, rt, rtt)

    n_prompt_tiles = batch * seq // TM
    max_tiles = -(-(2 * (batch * seq + nb) + N_EXPERTS * (EXPERT_TILE - 1)) // EXPERT_TILE)
    dest, tail_start, tile_expert, n_tiles = _routing_tables(rtt, counts, max_tiles)
    xs = _dispatch(dest, tail_start, h2, max_tiles * EXPERT_TILE, n_prompt_tiles, nb)
    eo = _experts(tile_expert, n_tiles, xs, w_gate_e, w_up_e, w_down_e)
    gate2_s = jnp.zeros((TM, D_MODEL), F32).at[:nb].set(mod_s[:, 5 * D_MODEL:])
    y_prompt, y_sample = _combine(dest, x1, rt, mod_p[:, :, 5 * D_MODEL:], gate2_s, g_final, eo,
                                  n_prompt_tiles, nb, seq // TM)

    def prompt_state(a):
        a = a.reshape(batch, 2, HEADS, HEAD_DIM, a.shape[-1])
        return jnp.transpose(a, (0, 4, 1, 2, 3))[None]

    sample_state = lambda a: a.reshape(1, nb, 1, 2, HEADS, HEAD_DIM)
    return (y_prompt.reshape(batch, seq, D_MODEL), y_sample.reshape(nb, 1, D_MODEL),
            prompt_state(kv0), prompt_state(kv1), prompt_state(kv2),
            pconv[:, 6:8, :].reshape(1, batch, 2, CONV_W),
            sample_state(skv0), sample_state(skv1), sample_state(skv2),
            sconv.reshape(1, nb, 2, CONV_W))
```

```python
import functools
import math

import jax
import jax.numpy as jnp
from jax import lax
from jax.experimental import pallas as pl
from jax.experimental.pallas import tpu as pltpu

F32 = jnp.float32
BF16 = jnp.bfloat16
HIGHEST = lax.Precision.HIGHEST

D_MODEL = 1024
HEAD_DIM = 64
HEADS = 4
N_GROUPS = 3
GROUP_W = HEADS * HEAD_DIM
ATTN_W = N_GROUPS * GROUP_W
CONV_W = 512
DILATIONS = (1, 4, 16)
BAND = 128
N_EXPERTS = 32
EXPERTS_PER_GROUP = 8
N_EXPERT_GROUPS = 4
EXPERT_FF = 256
RMS_EPS = 1e-6
MASKED = -1e30

Q_OFF, K_OFF, V_OFF = 0, ATTN_W, 2 * ATTN_W
BG_OFF = 3 * ATTN_W
CG_OFF = BG_OFF + CONV_W
UI_OFF = CG_OFF + CONV_W
GA_OFF = UI_OFF + CONV_W
GB_OFF = GA_OFF + D_MODEL
IN_COLS = GB_OFF + D_MODEL

LANES = 128
ROUTER_LANES = 128
EXPERT_LANE0 = N_EXPERT_GROUPS

TM = 512
ATT_TILE = 2048
BACK_PARTS = 2
ATTN_PAIR = 4
EXPERT_BUFS = 3
EXPERT_TILE = 512
RT_W1, RT_W2, RT_E1, RT_E2, RT_R1, RT_R2 = 0, 1, 2, 3, 4, 5
RT_ROWS = 8
ROW_SUB = D_MODEL // LANES
DMA_CHUNK = 8
VMEM_LIMIT = 56 * 1024 * 1024


def _sigmoid(x):
    return 1.0 / (1.0 + jnp.exp(-x))


def _rmsnorm(x, g):
    return x * lax.rsqrt(jnp.mean(x * x, axis=-1, keepdims=True) + RMS_EPS) * g


def _alibi_slope(g, h):
    return 2.0 ** (-8.0 * (g * HEADS + h + 1) / (N_GROUPS * HEADS))


def _resident(shape):
    nd = len(shape)
    return pl.BlockSpec(shape, lambda *_: (0,) * nd, pipeline_mode=pl.Buffered(1))


def _mod_body(c_ref, w_ref, b_ref, o_ref):
    c = c_ref[...]
    s = c * _sigmoid(c)
    o_ref[...] = jnp.dot(s, w_ref[...], precision=HIGHEST, preferred_element_type=F32) + b_ref[...]


def _modulation(c_all, w_ada, b_ada):
    rows = c_all.shape[0]
    tn = 1024
    return pl.pallas_call(
        _mod_body,
        grid=(6 * D_MODEL // tn,),
        in_specs=[pl.BlockSpec((rows, D_MODEL), lambda j: (0, 0)),
                  pl.BlockSpec((D_MODEL, tn), lambda j: (0, j)),
                  pl.BlockSpec((1, tn), lambda j: (0, j))],
        out_specs=pl.BlockSpec((rows, tn), lambda j: (0, j)),
        out_shape=jax.ShapeDtypeStruct((rows, 6 * D_MODEL), F32),
        compiler_params=pltpu.CompilerParams(dimension_semantics=("arbitrary",), vmem_limit_bytes=VMEM_LIMIT),
        name="modulation",
    )(c_all, w_ada, b_ada.reshape(1, -1))


def _front_body(tiles_per_seq, x_ref, mod_ref, g_ref, w_ref, cw_ref,
                a0_ref, a1_ref, a2_ref, oconv_ref, sga_ref, sgb_ref,
                kv0_ref, kv1_ref, kv2_ref, pconv_ref, res_ref, uprev_ref):
    t_in_seq = pl.program_id(0) % tiles_per_seq
    x = x_ref[...]
    shift1 = mod_ref[:, 0:D_MODEL]
    scale1 = mod_ref[:, D_MODEL:2 * D_MODEL]
    h = (_rmsnorm(x, g_ref[...]) * (1.0 + scale1) + shift1).astype(BF16)

    def proj(c0, n):
        return jnp.dot(h, w_ref[:, c0:c0 + n], preferred_element_type=F32)

    a_refs = (a0_ref, a1_ref, a2_ref)
    kv_refs = (kv0_ref, kv1_ref, kv2_ref)

    for g in range(N_GROUPS):
        d = DILATIONS[g]
        n = TM // d
        for part, base in enumerate((Q_OFF, K_OFF, V_OFF)):
            r = proj(base + g * GROUP_W, GROUP_W)
            cols = slice(part * GROUP_W, (part + 1) * GROUP_W)
            if part == 0:
                r = r * (HEAD_DIM ** -0.5)
            else:
                kvc = slice((part - 1) * GROUP_W, part * GROUP_W)
                kv_refs[g][kvc, :] = (r[TM - BAND:, :] if g == 0 else r).T
            if g == 0:
                a_refs[g][:, cols] = r.astype(BF16)
            else:
                for c in range(GROUP_W // LANES):
                    res_ref[c] = r[:, c * LANES:(c + 1) * LANES]
                for rr in range(d):
                    for c in range(GROUP_W // LANES):
                        c0 = part * GROUP_W + c * LANES
                        a_refs[g][0, rr, :, c0:c0 + LANES] = res_ref[c, pl.ds(rr, n, stride=d), :].astype(BF16)

    bg = proj(BG_OFF, CONV_W)
    u = proj(CG_OFF, CONV_W) * proj(UI_OFF, CONV_W)
    tail = u[TM - 8:, :]
    pconv_ref[...] = tail
    prev = jnp.where(t_in_seq == 0, 0.0, uprev_ref[...])
    row = lax.broadcasted_iota(jnp.int32, (TM, 1), 0)
    u1 = jnp.where(row == 0, prev[7:8, :], pltpu.roll(u, 1, axis=0))
    u2 = jnp.where(row == 0, prev[6:7, :], jnp.where(row == 1, prev[7:8, :], pltpu.roll(u, 2, axis=0)))
    yconv = cw_ref[0:1, :] * u2 + cw_ref[1:2, :] * u1 + cw_ref[2:3, :] * u
    oconv_ref[...] = (bg * yconv).astype(BF16)
    uprev_ref[...] = tail

    sga_ref[...] = _sigmoid(proj(GA_OFF, D_MODEL)).astype(BF16)
    sgb_ref[...] = _sigmoid(proj(GB_OFF, D_MODEL)).astype(BF16)


def _front(x2d, mod_p, g_mix, w_in_bf16, conv_w, batch, seq):
    n_tok = x2d.shape[0]
    n_tiles = n_tok // TM
    tps = seq // TM
    kv2_blocks = ATT_TILE // TM
    out_shape = (
        jax.ShapeDtypeStruct((n_tok, ATTN_W), BF16),
        jax.ShapeDtypeStruct((n_tiles, 4, TM // 4, ATTN_W), BF16),
        jax.ShapeDtypeStruct((n_tiles, 16, TM // 16, ATTN_W), BF16),
        jax.ShapeDtypeStruct((n_tok, CONV_W), BF16),
        jax.ShapeDtypeStruct((n_tok, D_MODEL), BF16),
        jax.ShapeDtypeStruct((n_tok, D_MODEL), BF16),
        jax.ShapeDtypeStruct((batch, 2 * GROUP_W, 128), F32),
        jax.ShapeDtypeStruct((batch, 2 * GROUP_W, 512), F32),
        jax.ShapeDtypeStruct((batch, 2 * GROUP_W, 2048), F32),
        jax.ShapeDtypeStruct((batch, 8, CONV_W), F32),
    )
    out_specs = (
        pl.BlockSpec((TM, ATTN_W), lambda i: (i, 0)),
        pl.BlockSpec((1, 4, TM // 4, ATTN_W), lambda i: (i, 0, 0, 0)),
        pl.BlockSpec((1, 16, TM // 16, ATTN_W), lambda i: (i, 0, 0, 0)),
        pl.BlockSpec((TM, CONV_W), lambda i: (i, 0)),
        pl.BlockSpec((TM, D_MODEL), lambda i: (i, 0)),
        pl.BlockSpec((TM, D_MODEL), lambda i: (i, 0)),
        pl.BlockSpec((None, 2 * GROUP_W, 128), lambda i: (i // tps, 0, 0)),
        pl.BlockSpec((None, 2 * GROUP_W, TM), lambda i: (i // tps, 0, 0)),
        pl.BlockSpec((None, 2 * GROUP_W, TM),
                     lambda i: (i // tps, 0, jnp.maximum(i % tps - (tps - kv2_blocks), 0))),
        pl.BlockSpec((None, 8, CONV_W), lambda i: (i // tps, 0, 0)),
    )
    in_specs = [
        pl.BlockSpec((TM, D_MODEL), lambda i: (i, 0)),
        pl.BlockSpec((None, 1, 6 * D_MODEL), lambda i: (i // tps, 0, 0)),
        _resident((1, D_MODEL)),
        _resident((D_MODEL, IN_COLS)),
        _resident((3, CONV_W)),
    ]
    return pl.pallas_call(
        functools.partial(_front_body, tps),
        grid=(n_tiles,),
        in_specs=in_specs,
        out_specs=out_specs,
        out_shape=out_shape,
        scratch_shapes=[pltpu.VMEM((GROUP_W // LANES, TM, LANES), F32), pltpu.VMEM((8, CONV_W), F32)],
        compiler_params=pltpu.CompilerParams(dimension_semantics=("arbitrary",), vmem_limit_bytes=VMEM_LIMIT),
        name="prompt_front",
    )(x2d, mod_p, g_mix.reshape(1, -1), w_in_bf16, conv_w)


def _band_bias_table():
    qi = jnp.arange(BAND)[:, None]
    kc = jnp.arange(2 * BAND)[None, :]
    delta = qi - (kc - BAND)
    valid = (delta >= 0) & (delta <= BAND)
    tabs = []
    for first in (False, True):
        ok = valid & (kc >= BAND) if first else valid
        for g in range(N_GROUPS):
            for h in range(HEADS):
                b = -_alibi_slope(g, h) * (delta * DILATIONS[g]).astype(F32)
                tabs.append(jnp.where(ok, b, MASKED))
    return jnp.stack(tabs).astype(F32)


def _head_lane_mask(h):
    lane = lax.broadcasted_iota(jnp.int32, (1, GROUP_W), 1)
    return (lane >= h * HEAD_DIM) & (lane < (h + 1) * HEAD_DIM)


def _attn_scores(q, k):
    qs = jnp.concatenate([jnp.where(_head_lane_mask(h), q, jnp.zeros_like(q)) for h in range(HEADS)], axis=0)
    return lax.dot_general(qs, k, (((1,), (1,)), ((), ())), preferred_element_type=F32)


def _attn_probs(s, bias_ref, bias_base):
    s = s + bias_ref[pl.ds(bias_base, HEADS)].reshape(HEADS * BAND, 2 * BAND)
    m = jnp.max(s, axis=-1, keepdims=True)
    e = jnp.exp(s - m)
    l = jnp.sum(e, axis=-1, keepdims=True)
    return (e * (1.0 / l)).astype(BF16), m + jnp.log(l)


def _attn_outputs(p, lse, v):
    ov = jnp.dot(p, v, preferred_element_type=F32)
    o = jnp.zeros((BAND, GROUP_W), F32)
    lse_b = jnp.zeros((BAND, GROUP_W), F32)
    for h in range(HEADS):
        hm = _head_lane_mask(h)
        rows = slice(h * BAND, (h + 1) * BAND)
        o = o + jnp.where(hm, ov[rows], 0.0)
        lse_b = lse_b + jnp.where(hm, lse[rows], 0.0)
    return o, lse_b


def _attn_body(a0c_ref, a0p_ref, a1c_ref, a1p_ref, a2c_ref, a2p_ref, bias_ref, o_ref, og_ref, lg_ref):
    first_tile = pl.program_id(1) == 0
    first_off = jnp.where(first_tile, N_GROUPS * HEADS, 0)
    qs, ks, vs = (slice(0, GROUP_W), slice(GROUP_W, 2 * GROUP_W), slice(2 * GROUP_W, 3 * GROUP_W))

    def run_blocks(blocks):
        loaded = [load() for _, _, _, load in blocks]
        scores = [_attn_scores(q, k) for q, k, _ in loaded]
        probs = [_attn_probs(s, bias_ref, base) for s, (_, _, base, _) in zip(scores, blocks)]
        results = [_attn_outputs(p, lse, v) for (p, lse), (_, _, v) in zip(probs, loaded)]
        for (g, rows, _, _), (o, lse_b) in zip(blocks, results):
            for c in range(GROUP_W // LANES):
                og_ref[g, c, rows, :] = o[:, c * LANES:(c + 1) * LANES]
                lg_ref[g, c, rows, :] = lse_b[:, c * LANES:(c + 1) * LANES]

    def g0_block(n0, base):
        def load():
            q = a0c_ref[pl.ds(n0, BAND), qs]
            if isinstance(n0, int) and n0 == 0:
                k = jnp.concatenate([a0p_ref[:, ks], a0c_ref[0:BAND, ks]], axis=0)
                v = jnp.concatenate([a0p_ref[:, vs], a0c_ref[0:BAND, vs]], axis=0)
            else:
                k = a0c_ref[pl.ds(n0 - BAND, 2 * BAND), ks]
                v = a0c_ref[pl.ds(n0 - BAND, 2 * BAND), vs]
            return q, k, v
        return (0, pl.ds(n0, BAND), base, load)

    run_blocks([g0_block(u * BAND, first_off if u == 0 else 0) for u in range(ATTN_PAIR)])

    def g0_loop(t, carry):
        n0 = pl.multiple_of(t * (ATTN_PAIR * BAND), BAND)
        run_blocks([g0_block(n0 + u * BAND, 0) for u in range(ATTN_PAIR)])
        return carry

    lax.fori_loop(1, ATT_TILE // (ATTN_PAIR * BAND), g0_loop, 0)

    def g1_block(jj, r, prev_ref, prev_jj, base):
        def load():
            q = a1c_ref[jj, r, :, qs]
            k = jnp.concatenate([prev_ref[prev_jj, r, :, ks], a1c_ref[jj, r, :, ks]], axis=0)
            v = jnp.concatenate([prev_ref[prev_jj, r, :, vs], a1c_ref[jj, r, :, vs]], axis=0)
            return q, k, v
        return (1, pl.ds(jj * (4 * BAND) + r, BAND, stride=4), base, load)

    def g1_first(t, carry):
        run_blocks([g1_block(0, t * ATTN_PAIR + u, a1p_ref, 0, first_off + HEADS) for u in range(ATTN_PAIR)])
        return carry

    lax.fori_loop(0, 4 // ATTN_PAIR, g1_first, 0)

    def g1_rest(t, carry):
        jj = 1 + (t * ATTN_PAIR) // 4
        r0 = (t * ATTN_PAIR) % 4
        run_blocks([g1_block(jj, r0 + u, a1c_ref, jj - 1, HEADS) for u in range(ATTN_PAIR)])
        return carry

    lax.fori_loop(0, 12 // ATTN_PAIR, g1_rest, 0)

    n_sub = a2c_ref.shape[0]

    def g2_rows(ref, r, cols):
        return jnp.concatenate([ref[t, r, :, cols] for t in range(n_sub)], axis=0)

    def g2_block(r):
        def load():
            q = g2_rows(a2c_ref, r, qs)
            k = jnp.concatenate([g2_rows(a2p_ref, r, ks), g2_rows(a2c_ref, r, ks)], axis=0)
            v = jnp.concatenate([g2_rows(a2p_ref, r, vs), g2_rows(a2c_ref, r, vs)], axis=0)
            return q, k, v
        return (2, pl.ds(r, BAND, stride=16), first_off + 2 * HEADS, load)

    def g2_loop(t, carry):
        run_blocks([g2_block(t * ATTN_PAIR + u) for u in range(ATTN_PAIR)])
        return carry

    lax.fori_loop(0, 16 // ATTN_PAIR, g2_loop, 0)

    def mix(c, carry):
        rows = pl.ds(pl.multiple_of(c * BAND, BAND), BAND)
        for half in range(GROUP_W // LANES):
            l0, l1, l2 = lg_ref[0, half, rows, :], lg_ref[1, half, rows, :], lg_ref[2, half, rows, :]
            mx = jnp.maximum(jnp.maximum(l0, l1), l2)
            w0, w1, w2 = jnp.exp(l0 - mx), jnp.exp(l1 - mx), jnp.exp(l2 - mx)
            num = w0 * og_ref[0, half, rows, :] + w1 * og_ref[1, half, rows, :] + w2 * og_ref[2, half, rows, :]
            o_ref[rows, half * LANES:(half + 1) * LANES] = (num / (w0 + w1 + w2)).astype(o_ref.dtype)
        return carry

    lax.fori_loop(0, ATT_TILE // BAND, mix, 0)


def _attention(a0, a1, a2, bias, batch, seq):
    steps = seq // ATT_TILE
    sub = ATT_TILE // TM
    a0 = a0.reshape(batch, seq, ATTN_W)
    a1 = a1.reshape(batch, seq // TM, 4, TM // 4, ATTN_W)
    a2 = a2.reshape(batch, seq // TM, 16, TM // 16, ATTN_W)
    in_specs = [
        pl.BlockSpec((None, ATT_TILE, ATTN_W), lambda b, j: (b, j, 0)),
        pl.BlockSpec((None, BAND, ATTN_W), lambda b, j: (b, jnp.maximum(j * (ATT_TILE // BAND) - 1, 0), 0)),
        pl.BlockSpec((None, sub, 4, TM // 4, ATTN_W), lambda b, j: (b, j, 0, 0, 0)),
        pl.BlockSpec((None, 1, 4, TM // 4, ATTN_W), lambda b, j: (b, jnp.maximum(j * sub - 1, 0), 0, 0, 0)),
        pl.BlockSpec((None, sub, 16, TM // 16, ATTN_W), lambda b, j: (b, j, 0, 0, 0)),
        pl.BlockSpec((None, sub, 16, TM // 16, ATTN_W), lambda b, j: (b, jnp.maximum(j - 1, 0), 0, 0, 0)),
        _resident(bias.shape),
    ]
    return pl.pallas_call(
        _attn_body,
        grid=(batch, steps),
        in_specs=in_specs,
        out_specs=pl.BlockSpec((None, ATT_TILE, GROUP_W), lambda b, j: (b, j, 0)),
        out_shape=jax.ShapeDtypeStruct((batch, seq, GROUP_W), BF16),
        scratch_shapes=[pltpu.VMEM((N_GROUPS, GROUP_W // LANES, ATT_TILE, LANES), F32)] * 2,
        compiler_params=pltpu.CompilerParams(dimension_semantics=("arbitrary", "arbitrary"),
                                             vmem_limit_bytes=VMEM_LIMIT),
        name="prompt_attention",
    )(a0, a0, a1, a1, a2, a2, bias)


def _route(logits):
    lane = lax.broadcasted_iota(jnp.int32, logits.shape, 1)
    big = jnp.int32(1 << 20)
    gmask = lane < N_EXPERT_GROUPS
    lg = jnp.where(gmask, logits, MASKED)
    gmax = jnp.max(lg, axis=-1, keepdims=True)
    gidx = jnp.min(jnp.where(gmask & (lg == gmax), lane, big), axis=-1, keepdims=True)
    p_top = 1.0 / jnp.sum(jnp.where(gmask, jnp.exp(lg - gmax), 0.0), axis=-1, keepdims=True)
    lo = EXPERT_LANE0 + EXPERTS_PER_GROUP * gidx
    emask = (lane >= lo) & (lane < lo + EXPERTS_PER_GROUP)
    le = jnp.where(emask, logits, MASKED)
    v1 = jnp.max(le, axis=-1, keepdims=True)
    i1 = jnp.min(jnp.where(emask & (le == v1), lane, big), axis=-1, keepdims=True)
    emask2 = emask & (lane != i1)
    le2 = jnp.where(emask2, logits, MASKED)
    v2 = jnp.max(le2, axis=-1, keepdims=True)
    i2 = jnp.min(jnp.where(emask2 & (le2 == v2), lane, big), axis=-1, keepdims=True)
    e2 = jnp.exp(v2 - v1)
    den = 1.0 + e2
    w1 = (1.0 / den) * p_top
    w2 = (e2 / den) * p_top
    id1 = (i1 - EXPERT_LANE0).astype(F32)
    id2 = (i2 - EXPERT_LANE0).astype(F32)
    return jnp.where(lane == RT_W1, w1, jnp.where(lane == RT_W2, w2,
                     jnp.where(lane == RT_E1, id1, jnp.where(lane == RT_E2, id2, 0.0))))


def _with_ranks(rt, base, count=1.0):
    n = rt.shape[0]
    lane = lax.broadcasted_iota(jnp.int32, rt.shape, 1)
    lane_f = lane.astype(F32)
    oh1 = jnp.where(lane_f == rt[:, RT_E1:RT_E1 + 1], count, 0.0)
    oh2 = jnp.where(lane_f == rt[:, RT_E2:RT_E2 + 1], count, 0.0)
    before = jnp.where(lax.broadcasted_iota(jnp.int32, (n, n), 1) < lax.broadcasted_iota(jnp.int32, (n, n), 0),
                       1.0, 0.0).astype(BF16)
    p1 = jnp.dot(before, oh1.astype(BF16), preferred_element_type=F32)
    p2 = jnp.dot(before, oh2.astype(BF16), preferred_element_type=F32)
    c1 = jnp.sum(oh1, axis=0, keepdims=True)
    c2 = jnp.sum(oh2, axis=0, keepdims=True)
    rank1 = jnp.sum(oh1 * (base + p1), axis=-1, keepdims=True)
    rank2 = jnp.sum(oh2 * (base + c1 + p2), axis=-1, keepdims=True)
    return jnp.where(lane == RT_R1, rank1, jnp.where(lane == RT_R2, rank2, rt)), base + c1 + c2


def _back_body(oa_ref, oc_ref, sga_ref, sgb_ref, x_ref, mod_ref, wpa_ref, wpb_ref, wo_ref, g_ref,
               wrh_ref, wrl_ref, br_ref, x1_ref, h2_ref, rt_ref, rtt_ref, cnt_ref, base_ref, logit_ref):
    step = pl.program_id(0)

    @pl.when(step == 0)
    def _():
        base_ref[...] = jnp.zeros_like(base_ref)
        logit_ref[...] = jnp.zeros_like(logit_ref)

    routed_prev = _route(logit_ref[...])
    gate1 = mod_ref[:, 2 * D_MODEL:3 * D_MODEL]
    shift2 = mod_ref[:, 3 * D_MODEL:4 * D_MODEL]
    scale2 = mod_ref[:, 4 * D_MODEL:5 * D_MODEL]
    n_part = TM // BACK_PARTS
    parts = [pl.ds(j * n_part, n_part) for j in range(BACK_PARTS)]
    dot = functools.partial(jnp.dot, preferred_element_type=F32)
    pab = [(dot(oa_ref[r, :], wpa_ref[...]), dot(oc_ref[r, :], wpb_ref[...])) for r in parts]
    merged = [(sga_ref[r, :].astype(F32) * pa + sgb_ref[r, :].astype(F32) * pb).astype(BF16)
              for r, (pa, pb) in zip(parts, pab)]
    delta = [dot(m, wo_ref[...]) for m in merged]
    x1 = [x_ref[r, :] + gate1 * d for r, d in zip(parts, delta)]
    h2 = [_rmsnorm(x, g_ref[...]) * (1.0 + scale2) + shift2 for x in x1]
    hi = [h.astype(BF16) for h in h2]
    lo = [(h - b.astype(F32)).astype(BF16) for h, b in zip(h2, hi)]
    logits = [dot(b, wrh_ref[...]) + dot(l, wrh_ref[...]) + dot(b, wrl_ref[...]) + br_ref[...]
              for b, l in zip(hi, lo)]
    for r, x, b, lg in zip(parts, x1, hi, logits):
        x1_ref[r, :] = x
        h2_ref[r, :] = b
        logit_ref[r, :] = lg
    rt, total = _with_ranks(routed_prev, base_ref[...], jnp.where(step > 0, 1.0, 0.0))
    rt_ref[...] = rt
    rtt_ref[...] = rt.T[:RT_ROWS, :]
    base_ref[...] = total
    cnt_ref[...] = total


def _back(o_attn, oconv, sga, sgb, x2d, mod_p, wpa, wpb, wo, g_ffn, wr_hi, wr_lo, b_r, seq):
    n_tok = x2d.shape[0]
    n_all = n_tok + TM
    tps = seq // TM
    last = n_tok // TM - 1
    tile = lambda i: jnp.minimum(i, last)
    routed = lambda i: jnp.maximum(i - 1, 0)
    row = lambda w: pl.BlockSpec((TM, w), lambda i: (tile(i), 0))
    return pl.pallas_call(
        _back_body,
        grid=(last + 2,),
        in_specs=[row(GROUP_W), row(CONV_W), row(D_MODEL), row(D_MODEL), row(D_MODEL),
                  pl.BlockSpec((None, 1, 6 * D_MODEL), lambda i: (tile(i) // tps, 0, 0)),
                  _resident(wpa.shape), _resident(wpb.shape), _resident(wo.shape), _resident((1, D_MODEL)),
                  _resident(wr_hi.shape), _resident(wr_lo.shape), _resident(b_r.shape)],
        out_specs=(row(D_MODEL), row(D_MODEL),
                   pl.BlockSpec((TM, ROUTER_LANES), lambda i: (routed(i), 0)),
                   pl.BlockSpec((RT_ROWS, TM), lambda i: (0, routed(i))),
                   pl.BlockSpec((1, LANES), lambda i: (0, 0))),
        out_shape=(jax.ShapeDtypeStruct((n_all, D_MODEL), F32),
                   jax.ShapeDtypeStruct((n_all, D_MODEL), BF16),
                   jax.ShapeDtypeStruct((n_all, ROUTER_LANES), F32),
                   jax.ShapeDtypeStruct((RT_ROWS, n_all), F32),
                   jax.ShapeDtypeStruct((1, LANES), F32)),
        scratch_shapes=[pltpu.VMEM((1, LANES), F32), pltpu.VMEM((TM, ROUTER_LANES), F32)],
        compiler_params=pltpu.CompilerParams(dimension_semantics=("arbitrary",), vmem_limit_bytes=VMEM_LIMIT),
        name="prompt_back",
    )(o_attn, oconv, sga, sgb, x2d, mod_p, wpa, wpb, wo, g_ffn.reshape(1, -1), wr_hi, wr_lo, b_r)


def _valid_chunks(tile, n_prompt_tiles, n_sample):
    return jnp.where(tile < n_prompt_tiles, TM // DMA_CHUNK, n_sample // DMA_CHUNK)


def _dispatch_body(n_prompt_tiles, n_sample, dest_ref, tail_ref, h_ref, xs_ref, stage_ref, zero_ref, sem):
    i = pl.program_id(0)
    n_all = (n_prompt_tiles + 1) * TM

    def tail_copy(e):
        start = pl.multiple_of(tail_ref[e] * ROW_SUB, ROW_SUB)
        return pltpu.make_async_copy(zero_ref, xs_ref.at[pl.ds(start, EXPERT_TILE * ROW_SUB)], sem)

    @pl.when(i == 0)
    def _():
        zero_ref[...] = jnp.zeros_like(zero_ref)
        for e in range(N_EXPERTS):
            @pl.when(tail_ref[e] >= 0)
            def _():
                tail_copy(e).start()
        for e in range(N_EXPERTS):
            @pl.when(tail_ref[e] >= 0)
            def _():
                tail_copy(e).wait()

    for j in range(ROW_SUB):
        stage_ref[pl.ds(j, TM, stride=ROW_SUB), :] = h_ref[:, j * LANES:(j + 1) * LANES].astype(F32)
    n_chunks = _valid_chunks(i, n_prompt_tiles, n_sample)

    def issue(c, carry):
        for j in range(DMA_CHUNK):
            r = c * DMA_CHUNK + j
            src = stage_ref.at[pl.ds(pl.multiple_of(r * ROW_SUB, ROW_SUB), ROW_SUB)]
            for k in range(2):
                d = pl.multiple_of(dest_ref[k * n_all + i * TM + r], ROW_SUB)
                pltpu.make_async_copy(src, xs_ref.at[pl.ds(d, ROW_SUB)], sem).start(priority=k)
        return carry

    lax.fori_loop(0, n_chunks, issue, 0)

    def drain(c, carry):
        n = 2 * DMA_CHUNK * ROW_SUB
        pltpu.make_async_copy(stage_ref.at[pl.ds(0, n)], xs_ref.at[pl.ds(0, n)], sem).wait()
        return carry

    lax.fori_loop(0, n_chunks, drain, 0)


def _dispatch(dest, tail_start, h2, n_rows, n_prompt_tiles, n_sample):
    n_all = h2.shape[0]
    return pl.pallas_call(
        functools.partial(_dispatch_body, n_prompt_tiles, n_sample),
        grid_spec=pltpu.PrefetchScalarGridSpec(
            num_scalar_prefetch=2,
            grid=(n_all // TM,),
            in_specs=[pl.BlockSpec((TM, D_MODEL), lambda i, *_: (i, 0))],
            out_specs=pl.BlockSpec(memory_space=pl.ANY),
            scratch_shapes=[pltpu.VMEM((TM * ROW_SUB, LANES), F32), pltpu.VMEM((EXPERT_TILE * ROW_SUB, LANES), F32),
                            pltpu.SemaphoreType.DMA],
        ),
        out_shape=jax.ShapeDtypeStruct((n_rows * ROW_SUB, LANES), F32),
        compiler_params=pltpu.CompilerParams(dimension_semantics=("arbitrary",), vmem_limit_bytes=VMEM_LIMIT,
                                             disable_bounds_checks=True),
        name="moe_dispatch",
    )(dest, tail_start, h2)


def _experts_body(te_ref, nt_ref, xs_ref, wg_ref, wu_ref, wd_ref, o_ref, buf_ref, sem, obuf_ref, osem):
    step = pl.program_id(0)
    n_tiles = nt_ref[0]
    tile_rows = EXPERT_TILE * ROW_SUB

    def tile_copy(t):
        src = xs_ref.at[pl.ds(pl.multiple_of(t * tile_rows, tile_rows), tile_rows)]
        return pltpu.make_async_copy(src, buf_ref.at[t % EXPERT_BUFS], sem.at[t % EXPERT_BUFS])

    @pl.when(step == 0)
    def _():
        for t in range(EXPERT_BUFS - 1):
            @pl.when(t < n_tiles)
            def _():
                tile_copy(t).start()

    @pl.when(step + (EXPERT_BUFS - 1) < n_tiles)
    def _():
        tile_copy(step + (EXPERT_BUFS - 1)).start()

    def out_copy(t):
        dst = o_ref.at[pl.ds(pl.multiple_of(t * tile_rows, tile_rows), tile_rows)]
        return pltpu.make_async_copy(obuf_ref.at[t % 2], dst, osem.at[t % 2])

    @pl.when((step >= 2) & (step - 2 < n_tiles))
    def _():
        out_copy(step - 2).wait()

    @pl.when(step < n_tiles)
    def _():
        tile_copy(step).wait()
        slot = step % EXPERT_BUFS
        x = jnp.concatenate([buf_ref[slot, pl.ds(j, EXPERT_TILE, stride=ROW_SUB), :].astype(BF16)
                             for j in range(ROW_SUB)], axis=1)
        a = jnp.dot(x, wg_ref[0].astype(BF16), preferred_element_type=F32)
        z = (a * _sigmoid(a)) * jnp.dot(x, wu_ref[0].astype(BF16), preferred_element_type=F32)
        o = jnp.dot(z.astype(BF16), wd_ref[0].astype(BF16), preferred_element_type=F32)
        for j in range(ROW_SUB):
            obuf_ref[step % 2, pl.ds(j, EXPERT_TILE, stride=ROW_SUB), :] = o[:, j * LANES:(j + 1) * LANES]
        out_copy(step).start()

    last = pl.num_programs(0) - 1

    @pl.when(step == last)
    def _():
        for t in (last - 1, last):
            @pl.when((t >= 0) & (t < n_tiles))
            def _():
                out_copy(t).wait()


def _experts(tile_expert, n_tiles, xs, w_gate_e, w_up_e, w_down_e):
    max_tiles = tile_expert.shape[0]
    rows = lambda s, te, nt: (jnp.minimum(s, nt[0] - 1), 0)
    row_tile = pl.BlockSpec((EXPERT_TILE * ROW_SUB, LANES), rows)
    weight = lambda shape: pl.BlockSpec((1,) + shape, lambda s, te, nt: (te[jnp.minimum(s, nt[0] - 1)], 0, 0))
    return pl.pallas_call(
        _experts_body,
        grid_spec=pltpu.PrefetchScalarGridSpec(
            num_scalar_prefetch=2,
            grid=(max_tiles,),
            in_specs=[pl.BlockSpec(memory_space=pl.ANY),
                      weight((D_MODEL, EXPERT_FF)), weight((D_MODEL, EXPERT_FF)), weight((EXPERT_FF, D_MODEL))],
            out_specs=pl.BlockSpec(memory_space=pl.ANY),
            scratch_shapes=[pltpu.VMEM((EXPERT_BUFS, EXPERT_TILE * ROW_SUB, LANES), F32),
                            pltpu.SemaphoreType.DMA((EXPERT_BUFS,)),
                            pltpu.VMEM((2, EXPERT_TILE * ROW_SUB, LANES), F32), pltpu.SemaphoreType.DMA((2,))],
        ),
        out_shape=jax.ShapeDtypeStruct(xs.shape, F32),
        compiler_params=pltpu.CompilerParams(dimension_semantics=("arbitrary",), vmem_limit_bytes=VMEM_LIMIT),
        name="moe_experts",
    )(tile_expert, n_tiles, xs, w_gate_e, w_up_e, w_down_e)


def _combine_body(n_prompt_tiles, n_sample, dest_ref, x1_ref, rt_ref, gp_ref, gs_ref, gf_ref, eo_ref,
                  yp_ref, ys_ref, rows_ref, sem):
    i = pl.program_id(0)
    slot = i % 2
    n_all = (n_prompt_tiles + 1) * TM

    def fetch(tile):
        buf = tile % 2

        def issue(c, carry):
            for j in range(DMA_CHUNK):
                r = c * DMA_CHUNK + j
                row = pl.ds(pl.multiple_of(r * ROW_SUB, ROW_SUB), ROW_SUB)
                for k in range(2):
                    d = pl.multiple_of(dest_ref[k * n_all + tile * TM + r], ROW_SUB)
                    pltpu.make_async_copy(eo_ref.at[pl.ds(d, ROW_SUB)], rows_ref.at[buf, k, row],
                                          sem.at[buf]).start(priority=k)
            return carry

        lax.fori_loop(0, _valid_chunks(tile, n_prompt_tiles, n_sample), issue, 0)

    @pl.when(i == 0)
    def _():
        fetch(i)

    @pl.when(i + 1 < pl.num_programs(0))
    def _():
        fetch(i + 1)

    def drain(c, carry):
        n = 2 * DMA_CHUNK * ROW_SUB
        pltpu.make_async_copy(eo_ref.at[pl.ds(0, n)], rows_ref.at[slot, 0, pl.ds(0, n)], sem.at[slot]).wait()
        return carry

    lax.fori_loop(0, _valid_chunks(i, n_prompt_tiles, n_sample), drain, 0)

    rt = rt_ref[...]
    w1, w2 = rt[:, RT_W1:RT_W1 + 1], rt[:, RT_W2:RT_W2 + 1]
    lane_tile = lambda k, j: rows_ref[slot, k, pl.ds(j, TM, stride=ROW_SUB), :]
    moe = jnp.concatenate([w1 * lane_tile(0, j) + w2 * lane_tile(1, j) for j in range(ROW_SUB)], axis=1)
    is_prompt = i < n_prompt_tiles
    gate2 = jnp.where(is_prompt, gp_ref[...], gs_ref[...])
    y = _rmsnorm(x1_ref[...] + gate2 * moe, gf_ref[...])

    @pl.when(is_prompt)
    def _():
        yp_ref[...] = y

    @pl.when(jnp.logical_not(is_prompt))
    def _():
        ys_ref[...] = y[:n_sample, :]


def _combine(dest, x1, rt, gate2_p, gate2_s, g_final, eo, n_prompt_tiles, n_sample, tiles_per_seq):
    n_all = x1.shape[0]
    last_p = n_prompt_tiles - 1
    return pl.pallas_call(
        functools.partial(_combine_body, n_prompt_tiles, n_sample),
        grid_spec=pltpu.PrefetchScalarGridSpec(
            num_scalar_prefetch=1,
            grid=(n_all // TM,),
            in_specs=[pl.BlockSpec((TM, D_MODEL), lambda i, *_: (i, 0)),
                      pl.BlockSpec((TM, ROUTER_LANES), lambda i, *_: (i, 0)),
                      pl.BlockSpec((None, 1, D_MODEL), lambda i, *_: (jnp.minimum(i, last_p) // tiles_per_seq, 0, 0)),
                      pl.BlockSpec((TM, D_MODEL), lambda i, *_: (0, 0)),
                      pl.BlockSpec((1, D_MODEL), lambda i, *_: (0, 0)),
                      pl.BlockSpec(memory_space=pl.ANY)],
            out_specs=(pl.BlockSpec((TM, D_MODEL), lambda i, *_: (jnp.minimum(i, last_p), 0)),
                       pl.BlockSpec((n_sample, D_MODEL), lambda i, *_: (0, 0))),
            scratch_shapes=[pltpu.VMEM((2, 2, TM * ROW_SUB, LANES), F32), pltpu.SemaphoreType.DMA((2,))],
        ),
        out_shape=(jax.ShapeDtypeStruct((n_prompt_tiles * TM, D_MODEL), F32),
                   jax.ShapeDtypeStruct((n_sample, D_MODEL), F32)),
        compiler_params=pltpu.CompilerParams(dimension_semantics=("arbitrary",), vmem_limit_bytes=VMEM_LIMIT,
                                             disable_bounds_checks=True),
        name="moe_combine",
    )(dest, x1, rt, gate2_p, gate2_s, g_final.reshape(1, -1), eo)


def _routing_tables(rtt, counts, max_tiles):
    cnt = counts[0, :N_EXPERTS].astype(jnp.int32)
    padded = (cnt + EXPERT_TILE - 1) // EXPERT_TILE * EXPERT_TILE
    ends = jnp.cumsum(padded)
    starts = ends - padded
    n_tiles = (ends[-1] // EXPERT_TILE).reshape(1)
    tile_row0 = jnp.arange(max_tiles, dtype=jnp.int32) * EXPERT_TILE
    tile_expert = jnp.minimum(jnp.sum(ends[None, :] <= tile_row0[:, None], axis=1), N_EXPERTS - 1).astype(jnp.int32)
    tail_start = jnp.where(cnt > 0, ends - EXPERT_TILE, -1).astype(jnp.int32)
    eid = rtt[RT_E1:RT_E2 + 1, :].astype(jnp.int32)
    onehot = eid[None] == jnp.arange(N_EXPERTS, dtype=jnp.int32)[:, None, None]
    dest = rtt[RT_R1:RT_R2 + 1, :].astype(jnp.int32) + jnp.sum(jnp.where(onehot, starts[:, None, None], 0), axis=0)
    dest = (jnp.where(eid >= 0, dest, 0) * ROW_SUB).reshape(-1)
    return dest, tail_start, tile_expert, n_tiles


def _sample_front_body(x_ref, mod_ref, g_ref, w_ref, o_ref):
    shift1 = mod_ref[:, 0:D_MODEL]
    scale1 = mod_ref[:, D_MODEL:2 * D_MODEL]
    h = _rmsnorm(x_ref[...], g_ref[...]) * (1.0 + scale1) + shift1
    o_ref[...] = jnp.dot(h, w_ref[...], precision=HIGHEST, preferred_element_type=F32)


def _sample_front(x_s, mod_s, g_mix, w_in):
    nb = x_s.shape[0]
    tn = 256
    return pl.pallas_call(
        _sample_front_body,
        grid=(IN_COLS // tn,),
        in_specs=[pl.BlockSpec((nb, D_MODEL), lambda j: (0, 0)),
                  pl.BlockSpec((nb, 2 * D_MODEL), lambda j: (0, 0)),
                  pl.BlockSpec((1, D_MODEL), lambda j: (0, 0)),
                  pl.BlockSpec((D_MODEL, tn), lambda j: (0, j))],
        out_specs=pl.BlockSpec((nb, tn), lambda j: (0, j)),
        out_shape=jax.ShapeDtypeStruct((nb, IN_COLS), F32),
        compiler_params=pltpu.CompilerParams(dimension_semantics=("arbitrary",), vmem_limit_bytes=VMEM_LIMIT),
        name="sample_front",
    )(x_s, mod_s, g_mix.reshape(1, -1), w_in)


def _sample_attn_body(qkv_ref, c0_ref, c1_ref, c2_ref, oa_ref):
    head = lax.broadcasted_iota(jnp.int32, (HEADS, 1, 1), 0)
    for b in range(qkv_ref.shape[0]):
        outs, lses = [], []
        for g, c_ref in enumerate((c0_ref, c1_ref, c2_ref)):
            window = c_ref.shape[-1]
            def new_rows(part):
                c0 = (part * N_GROUPS + g) * HEADS
                return jnp.stack([qkv_ref[b, :, c0 + h:c0 + h + 1] for h in range(HEADS)], axis=0)

            q = new_rows(0) * (HEAD_DIM ** -0.5)
            k_new = new_rows(1)
            v_new = new_rows(2)
            s = jnp.sum(c_ref[b, 0] * q, axis=1, keepdims=True)
            pos = lax.broadcasted_iota(jnp.int32, (1, 1, window), 2)
            slope = jnp.full((HEADS, 1, 1), _alibi_slope(g, HEADS - 1), F32)
            for h in range(HEADS - 1):
                slope = jnp.where(head == h, _alibi_slope(g, h), slope)
            on_band = (pos & (DILATIONS[g] - 1)) == 0
            s = jnp.where(on_band, s - slope * (window - pos).astype(F32), MASKED)
            s_self = jnp.sum(q * k_new, axis=1, keepdims=True)
            m = jnp.maximum(jnp.max(s, axis=2, keepdims=True), s_self)
            e = jnp.exp(s - m)
            e_self = jnp.exp(s_self - m)
            l = jnp.sum(e, axis=2, keepdims=True) + e_self
            o = jnp.sum(c_ref[b, 1] * e, axis=2, keepdims=True) + e_self * v_new
            outs.append(o / l)
            lses.append(m + jnp.log(l))
        mx = jnp.maximum(jnp.maximum(lses[0], lses[1]), lses[2])
        w = [jnp.exp(x - mx) for x in lses]
        mixed = (w[0] * outs[0] + w[1] * outs[1] + w[2] * outs[2]) / (w[0] + w[1] + w[2])
        for h in range(HEADS):
            oa_ref[b, :, h:h + 1] = mixed[h]


def _sample_attention(qkv_s, cache0, cache1, cache2):
    nb = qkv_s.shape[0]
    bb = 2
    hd = (HEADS, HEAD_DIM)
    caches = [jnp.transpose(c, (0, 1, 3, 4, 5, 2)) for c in (cache0, cache1, cache2)]
    cache_spec = lambda c: pl.BlockSpec((None, bb, 2, *hd, c.shape[-1]), lambda i: (0, i, 0, 0, 0, 0))
    return pl.pallas_call(
        _sample_attn_body,
        grid=(nb // bb,),
        in_specs=[pl.BlockSpec((bb,) + qkv_s.shape[1:], lambda i: (i, 0, 0))] + [cache_spec(c) for c in caches],
        out_specs=pl.BlockSpec((bb, HEAD_DIM, HEADS), lambda i: (i, 0, 0)),
        out_shape=jax.ShapeDtypeStruct((nb, HEAD_DIM, HEADS), F32),
        compiler_params=pltpu.CompilerParams(dimension_semantics=("arbitrary",), vmem_limit_bytes=VMEM_LIMIT),
        name="sample_attention",
    )(qkv_s, *caches)


def _sample_back_body(p_ref, oa_ref, cc_ref, x_ref, mod_ref, cw_ref, wpa_ref, wpb_ref, wo_ref, g_ref,
                      wr_ref, br_ref, cntp_ref, x1_any, h2_any, rt_any, rtt_any,
                      x1_ref, h2_ref, rt_ref, rtt_ref, sconv_ref, cnt_ref):
    del x1_any, h2_any, rt_any, rtt_any
    nb = x_ref.shape[0]
    hdot = functools.partial(jnp.dot, precision=HIGHEST, preferred_element_type=F32)
    bg = p_ref[:, BG_OFF:BG_OFF + CONV_W]
    u = p_ref[:, CG_OFF:CG_OFF + CONV_W] * p_ref[:, UI_OFF:UI_OFF + CONV_W]
    c_old, c_new = cc_ref[:, 0, :], cc_ref[:, 1, :]
    yconv = cw_ref[0:1, :] * c_old + cw_ref[1:2, :] * c_new + cw_ref[2:3, :] * u
    sconv_ref[:, 0, :] = c_new
    sconv_ref[:, 1, :] = u
    sga = _sigmoid(p_ref[:, GA_OFF:GA_OFF + D_MODEL])
    sgb = _sigmoid(p_ref[:, GB_OFF:GB_OFF + D_MODEL])
    merged = sga * hdot(oa_ref[...], wpa_ref[...]) + sgb * hdot(bg * yconv, wpb_ref[...])
    gate1 = mod_ref[:, 2 * D_MODEL:3 * D_MODEL]
    shift2 = mod_ref[:, 3 * D_MODEL:4 * D_MODEL]
    scale2 = mod_ref[:, 4 * D_MODEL:5 * D_MODEL]
    x1 = x_ref[...] + gate1 * hdot(merged, wo_ref[...])
    h2 = _rmsnorm(x1, g_ref[...]) * (1.0 + scale2) + shift2
    rt = _route(hdot(h2, wr_ref[...]) + br_ref[...])
    lane = lax.broadcasted_iota(jnp.int32, (TM, ROUTER_LANES), 1)
    x1_ref[...] = jnp.zeros_like(x1_ref)
    h2_ref[...] = jnp.zeros_like(h2_ref)
    rt_ref[...] = jnp.where((lane == RT_E1) | (lane == RT_E2), -1.0, 0.0)
    x1_ref[0:nb, :] = x1
    h2_ref[0:nb, :] = h2.astype(BF16)
    rt_ref[0:nb, :] = rt
    ranked, total = _with_ranks(rt_ref[...], cntp_ref[...])
    rt_ref[...] = ranked
    rtt_ref[...] = ranked.T[:RT_ROWS, :]
    cnt_ref[...] = total


def _sample_back(proj_s, oa_s, cache_conv, x_s, mod_s, conv_w, w_pa, w_pb, w_o, g_ffn, w_r, b_r, cnt_p,
                 x1, h2, rt, rtt):
    nb = x_s.shape[0]
    last = x1.shape[0] // TM - 1
    args = (proj_s, oa_s, cache_conv, x_s, mod_s, conv_w, w_pa, w_pb, w_o, g_ffn.reshape(1, -1), w_r, b_r, cnt_p)
    full = lambda shape: pl.BlockSpec(shape, lambda i: (0,) * len(shape))
    last_tile = lambda a: pl.BlockSpec((TM, a.shape[1]), lambda i: (last, 0))
    shape_of = lambda a: jax.ShapeDtypeStruct(a.shape, a.dtype)
    return pl.pallas_call(
        _sample_back_body,
        grid=(1,),
        in_specs=[full(a.shape) for a in args] + [pl.BlockSpec(memory_space=pl.ANY)] * 4,
        out_specs=(last_tile(x1), last_tile(h2), last_tile(rt), pl.BlockSpec((RT_ROWS, TM), lambda i: (0, last)),
                   full((nb, 2, CONV_W)), full((1, LANES))),
        out_shape=(shape_of(x1), shape_of(h2), shape_of(rt), shape_of(rtt),
                   jax.ShapeDtypeStruct((nb, 2, CONV_W), F32), jax.ShapeDtypeStruct((1, LANES), F32)),
        input_output_aliases={len(args) + j: j for j in range(4)},
        compiler_params=pltpu.CompilerParams(dimension_semantics=("arbitrary",), vmem_limit_bytes=VMEM_LIMIT),
        name="sample_back",
    )(*args, x1, h2, rt, rtt)


def kernel(x_prompt, x_sample, cache_kv_w128, cache_kv_w512, cache_kv_w2048, cache_conv, c_prompt, c_sample,
           g_mix, w_ada, b_ada, w_in, conv_w, w_pa, w_pb, w_o, g_ffn, w_router_group, b_router_group,
           w_router_expert, b_router_expert, w_gate_e, w_up_e, w_down_e, g_final):
    batch, seq, _ = x_prompt.shape
    nb = x_sample.shape[0]
    assert x_sample.shape[1] == 1 and g_mix.shape[0] == 1, "one layer, one new sample token per sequence"
    assert seq % ATT_TILE == 0 and cache_kv_w128.shape[2] == 128 and cache_kv_w512.shape[2] == 512 \
        and cache_kv_w2048.shape[2] == 2048
    (g_mix, w_ada, b_ada, w_in, conv_w, w_pa, w_pb, w_o, g_ffn, w_rg, b_rg, w_re, b_re, w_gate_e, w_up_e,
     w_down_e) = (a[0] for a in (g_mix, w_ada, b_ada, w_in, conv_w, w_pa, w_pb, w_o, g_ffn, w_router_group,
                                 b_router_group, w_router_expert, b_router_expert, w_gate_e, w_up_e, w_down_e))

    c_all = jnp.concatenate([c_prompt, jnp.zeros((8 - batch, D_MODEL), F32), c_sample], axis=0)
    mod = _modulation(c_all, w_ada, b_ada)
    mod_p = mod[:batch].reshape(batch, 1, 6 * D_MODEL)
    mod_s = mod[8:]

    w_r = jnp.zeros((D_MODEL, ROUTER_LANES), F32)
    w_r = w_r.at[:, :N_EXPERT_GROUPS].set(w_rg).at[:, EXPERT_LANE0:EXPERT_LANE0 + N_EXPERTS].set(w_re)
    b_r = jnp.full((1, ROUTER_LANES), MASKED, F32)
    b_r = b_r.at[0, :N_EXPERT_GROUPS].set(b_rg).at[0, EXPERT_LANE0:EXPERT_LANE0 + N_EXPERTS].set(b_re)
    wr_hi = w_r.astype(BF16)
    wr_lo = (w_r - wr_hi.astype(F32)).astype(BF16)

    x2d = x_prompt.reshape(batch * seq, D_MODEL)
    (a0, a1, a2, oconv, sga, sgb, kv0, kv1, kv2, pconv) = _front(
        x2d, mod_p, g_mix, w_in.astype(BF16), conv_w, batch, seq)
    o_attn = _attention(a0, a1, a2, _band_bias_table(), batch, seq).reshape(batch * seq, GROUP_W)
    x1, h2, rt, rtt, cnt_p = _back(o_attn, oconv, sga, sgb, x2d, mod_p, w_pa.astype(BF16), w_pb.astype(BF16),
                              w_o.astype(BF16), g_ffn, wr_hi, wr_lo, b_r, seq)

    x_s = x_sample.reshape(nb, D_MODEL)
    proj_s = _sample_front(x_s, mod_s[:, :2 * D_MODEL], g_mix, w_in)
    qkv_s = proj_s[:, :3 * ATTN_W].reshape(nb, 3, N_GROUPS, HEADS, HEAD_DIM)
    qkv_t = jnp.transpose(qkv_s.reshape(nb, 3 * N_GROUPS * HEADS, HEAD_DIM), (0, 2, 1))
    oa_s = _sample_attention(qkv_t, cache_kv_w128, cache_kv_w512, cache_kv_w2048)
    oa_s = jnp.transpose(oa_s, (0, 2, 1)).reshape(nb, GROUP_W)
    skv0, skv1, skv2 = (jnp.stack([qkv_s[:, 1, g], qkv_s[:, 2, g]], axis=1) for g in range(N_GROUPS))
    x1, h2, rt, rtt, sconv, counts = _sample_back(proj_s, oa_s, cache_conv[0], x_s, mod_s, conv_w,
                                                  w_pa, w_pb, w_o, g_ffn, w_r, b_r, cnt_p, x1, h2, rt, rtt)

    n_prompt_tiles = batch * seq // TM
    max_tiles = -(-(2 * (batch * seq + nb) + N_EXPERTS * (EXPERT_TILE - 1)) // EXPERT_TILE)
    dest, tail_start, tile_expert, n_tiles = _routing_tables(rtt, counts, max_tiles)
    xs = _dispatch(dest, tail_start, h2, max_tiles * EXPERT_TILE, n_prompt_tiles, nb)
    eo = _experts(tile_expert, n_tiles, xs, w_gate_e, w_up_e, w_down_e)
    gate2_s = jnp.zeros((TM, D_MODEL), F32).at[:nb].set(mod_s[:, 5 * D_MODEL:])
    y_prompt, y_sample = _combine(dest, x1, rt, mod_p[:, :, 5 * D_MODEL:], gate2_s, g_final, eo,
                                  n_prompt_tiles, nb, seq // TM)

    def prompt_state(a):
        a = a.reshape(batch, 2, HEADS, HEAD_DIM, a.shape[-1])
        return jnp.transpose(a, (0, 4, 1, 2, 3))[None]

    sample_state = lambda a: a.reshape(1, nb, 1, 2, HEADS, HEAD_DIM)
    return (y_prompt.reshape(batch, seq, D_MODEL), y_sample.reshape(nb, 1, D_MODEL),
            prompt_state(kv0), prompt_state(kv1), prompt_state(kv2),
            pconv[:, 6:8, :].reshape(1, batch, 2, CONV_W),
            sample_state(skv0), sample_state(skv1), sample_state(skv2),
            sconv.reshape(1, nb, 2, CONV_W))
```
